```python
import math
import jax, jax.numpy as jnp
from jax import lax
import numpy as np


D_MODEL = 1024
BATCH = 2
SEQ = 8192
DEPTH = 1

MEM_LEN = 256
EPS = 1e-6
ROPE_BASE = 10000.0

RET_HEADS = 8
RET_QK_WIDTH = D_MODEL // 2
RET_V_WIDTH = D_MODEL
RET_DK = RET_QK_WIDTH // RET_HEADS
RET_DV = RET_V_WIDTH // RET_HEADS
CHUNK = 128

S5_WIDTH = D_MODEL
S5_GROUP = 16
S5_GROUPS = S5_WIDTH // S5_GROUP
S5_STATE = 64

D_MIX = RET_V_WIDTH + S5_WIDTH
IN_COLS = 2 * RET_QK_WIDTH + 2 * RET_V_WIDTH + 2 * S5_WIDTH
SPLITS = (RET_QK_WIDTH, 2 * RET_QK_WIDTH, 2 * RET_QK_WIDTH + RET_V_WIDTH,
          2 * RET_QK_WIDTH + 2 * RET_V_WIDTH, 2 * RET_QK_WIDTH + 2 * RET_V_WIDTH + S5_WIDTH)

XA_HEADS = 4
XA_DH = D_MODEL // XA_HEADS

kernel_name = 'hybrid_retention_s5_block'


def rms_norm(x, g):
    xf = x.astype(jnp.float32)
    y = xf * lax.rsqrt(jnp.mean(xf * xf, axis=-1, keepdims=True) + EPS)
    return (y * g.astype(jnp.float32)).astype(x.dtype)


def rotary(x, positions):
    half = x.shape[-1] // 2
    inv = ROPE_BASE ** (-jnp.arange(half, dtype=jnp.float32) / half)
    ang = positions.astype(jnp.float32)[:, :, None, None] * inv
    cos, sin = jnp.cos(ang), jnp.sin(ang)
    x1, x2 = x[..., :half], x[..., half:]
    return jnp.concatenate([x1 * cos - x2 * sin, x1 * sin + x2 * cos], axis=-1)


def retention(q, k, v, gn_g):
    B, L, H, DK = q.shape
    DV = v.shape[-1]
    nC = L // CHUNK
    q = q.astype(jnp.float32)
    k = k.astype(jnp.float32) * DK ** -0.5
    v = v.astype(jnp.float32)
    log_g = jnp.log1p(-jnp.exp2(-5.0 - jnp.arange(H, dtype=jnp.float32)))
    qc = q.reshape(B, nC, CHUNK, H, DK)
    kc = k.reshape(B, nC, CHUNK, H, DK)
    vc = v.reshape(B, nC, CHUNK, H, DV)
    j = jnp.arange(CHUNK, dtype=jnp.float32)
    diff = j[:, None] - j[None, :]
    decay = jnp.where(diff[None] >= 0.0,
                      jnp.exp(log_g[:, None, None] * jnp.maximum(diff, 0.0)[None]), 0.0)
    scores = jnp.einsum('bnihd,bnjhd->bnhij', qc, kc) * decay
    inner = jnp.einsum('bnhij,bnjhe->bnihe', scores, vc)
    k_w = jnp.exp(log_g[None, :] * (CHUNK - 1.0 - j)[:, None])
    kv = jnp.einsum('bnjhd,jh,bnjhe->bnhde', kc, k_w, vc)
    chunk_decay = jnp.exp(log_g * CHUNK)[None, :, None, None]

    def step(R, kv_n):
        return chunk_decay * R + kv_n, R

    _, R_prev = lax.scan(step, jnp.zeros((B, H, DK, DV), jnp.float32), jnp.moveaxis(kv, 1, 0))
    R_prev = jnp.moveaxis(R_prev, 0, 1)
    q_w = jnp.exp(log_g[None, :] * (j + 1.0)[:, None])
    cross = jnp.einsum('bnihd,ih,bnhde->bnihe', qc, q_w, R_prev)
    o = (inner + cross).reshape(B, L, H, DV)
    mu = jnp.mean(o, axis=-1, keepdims=True)
    var = jnp.mean(jnp.square(o - mu), axis=-1, keepdims=True)
    o = ((o - mu) * lax.rsqrt(var + EPS)).reshape(B, L, H * DV)
    return o * gn_g.astype(jnp.float32)


def s5_branch(u, a_re, a_im, log_dt, b_re, b_im, c_re, c_im, d, glu_w, glu_b):
    Bsz, L, W = u.shape
    f32 = jnp.float32
    uf = u.astype(f32)
    ug = uf.reshape(Bsz, L, S5_GROUPS, S5_GROUP)
    dt = jnp.exp(log_dt.astype(f32))[:, None]
    ar, ai = a_re.astype(f32), a_im.astype(f32)
    mag = jnp.exp(ar * dt)
    abar_re = mag * jnp.cos(ai * dt)
    abar_im = mag * jnp.sin(ai * dt)
    den = ar * ar + ai * ai
    nr, ni = abar_re - 1.0, abar_im
    f_re = (nr * ar + ni * ai) / den
    f_im = (ni * ar - nr * ai) / den
    br, bi = b_re.astype(f32), b_im.astype(f32)
    bb_re = f_re[..., None] * br - f_im[..., None] * bi
    bb_im = f_re[..., None] * bi + f_im[..., None] * br
    bu_re = jnp.einsum('blgp,gnp->blgn', ug, bb_re)
    bu_im = jnp.einsum('blgp,gnp->blgn', ug, bb_im)
    a_seq_re = jnp.broadcast_to(abar_re[None, None], (1, L, S5_GROUPS, S5_STATE))
    a_seq_im = jnp.broadcast_to(abar_im[None, None], (1, L, S5_GROUPS, S5_STATE))

    def combine(e1, e2):
        a1r, a1i, b1r, b1i = e1
        a2r, a2i, b2r, b2i = e2
        return (a2r * a1r - a2i * a1i,
                a2r * a1i + a2i * a1r,
                a2r * b1r - a2i * b1i + b2r,
                a2r * b1i + a2i * b1r + b2i)

    _, _, xr, xi = lax.associative_scan(combine, (a_seq_re, a_seq_im, bu_re, bu_im), axis=1)
    y = (jnp.einsum('blgn,gpn->blgp', xr, c_re.astype(f32))
         - jnp.einsum('blgn,gpn->blgp', xi, c_im.astype(f32)))
    y = y.reshape(Bsz, L, W) + d.astype(f32) * uf
    y = jax.nn.gelu(y)
    y = y * jax.nn.sigmoid(y @ glu_w.astype(f32) + glu_b.astype(f32))
    return y


def setup_inputs(seed: int = 0) -> dict:
    key = jax.random.key(seed)
    ks = jax.random.split(key, 24)
    f32 = jnp.float32

    def nrm(k, shape, scale):
        return jax.random.normal(k, shape, f32) * scale

    Ld = DEPTH
    G, N, P = S5_GROUPS, S5_STATE, S5_GROUP
    x = nrm(ks[0], (BATCH, SEQ, D_MODEL), 1.0)
    mem = nrm(ks[1], (BATCH, MEM_LEN, D_MODEL), 1.0)
    positions = jnp.broadcast_to(jnp.arange(SEQ, dtype=jnp.int32)[None, :], (BATCH, SEQ))
    norm1_g = 1.0 + nrm(ks[2], (Ld, D_MODEL), 0.02)
    w_in = nrm(ks[3], (Ld, D_MODEL, IN_COLS), D_MODEL ** -0.5)
    ret_gn_g = 1.0 + nrm(ks[4], (Ld, RET_V_WIDTH), 0.02)
    n_idx = jnp.arange(N, dtype=f32)
    s5_a_re = -0.5 + nrm(ks[5], (Ld, G, N), 0.01)
    s5_a_im = math.pi * n_idx[None, None, :] + nrm(ks[6], (Ld, G, N), 0.01)
    s5_log_dt = jax.random.uniform(ks[7], (Ld, G), f32, math.log(1e-3), math.log(1e-1))
    s5_b_re = nrm(ks[8], (Ld, G, N, P), (2.0 * P) ** -0.5)
    s5_b_im = nrm(ks[9], (Ld, G, N, P), (2.0 * P) ** -0.5)
    s5_c_re = nrm(ks[10], (Ld, G, P, N), N ** -0.5)
    s5_c_im = nrm(ks[11], (Ld, G, P, N), N ** -0.5)
    s5_d = nrm(ks[12], (Ld, S5_WIDTH), 0.5)
    s5_glu_w = nrm(ks[13], (Ld, S5_WIDTH, S5_WIDTH), S5_WIDTH ** -0.5)
    s5_glu_b = nrm(ks[14], (Ld, S5_WIDTH), 0.01)
    w_out = nrm(ks[15], (Ld, D_MIX, D_MODEL), D_MIX ** -0.5)
    norm2_g = 1.0 + nrm(ks[16], (Ld, D_MODEL), 0.02)
    norm_mem_g = 1.0 + nrm(ks[17], (Ld, D_MODEL), 0.02)
    xa_wq = nrm(ks[18], (Ld, D_MODEL, D_MODEL), D_MODEL ** -0.5)
    xa_wk = nrm(ks[19], (Ld, D_MODEL, D_MODEL), D_MODEL ** -0.5)
    xa_wv = nrm(ks[20], (Ld, D_MODEL, D_MODEL), D_MODEL ** -0.5)
    xa_wo = nrm(ks[21], (Ld, D_MODEL, D_MODEL), D_MODEL ** -0.5)
    norm_f_g = 1.0 + nrm(ks[22], (D_MODEL,), 0.02)
    return {'x': x, 'mem': mem, 'positions': positions, 'norm1_g': norm1_g, 'w_in': w_in,
            'ret_gn_g': ret_gn_g, 's5_a_re': s5_a_re, 's5_a_im': s5_a_im, 's5_log_dt': s5_log_dt,
            's5_b_re': s5_b_re, 's5_b_im': s5_b_im, 's5_c_re': s5_c_re, 's5_c_im': s5_c_im,
            's5_d': s5_d, 's5_glu_w': s5_glu_w, 's5_glu_b': s5_glu_b, 'w_out': w_out,
            'norm2_g': norm2_g, 'norm_mem_g': norm_mem_g, 'xa_wq': xa_wq, 'xa_wk': xa_wk,
            'xa_wv': xa_wv, 'xa_wo': xa_wo, 'norm_f_g': norm_f_g}


def reference(x, mem, positions, norm1_g, w_in, ret_gn_g, s5_a_re, s5_a_im, s5_log_dt,
              s5_b_re, s5_b_im, s5_c_re, s5_c_im, s5_d, s5_glu_w, s5_glu_b, w_out,
              norm2_g, norm_mem_g, xa_wq, xa_wk, xa_wv, xa_wo, norm_f_g):
    B, L, _ = x.shape
    M = mem.shape[1]
    for l in range(DEPTH):
        h = rms_norm(x, norm1_g[l])
        proj = h @ w_in[l]
        q, k, v, g_ret, u, g_s5 = jnp.split(proj, SPLITS, axis=-1)
        q = rotary(q.reshape(B, L, RET_HEADS, RET_DK), positions)
        k = rotary(k.reshape(B, L, RET_HEADS, RET_DK), positions)
        v = v.reshape(B, L, RET_HEADS, RET_DV)
        ret = retention(q, k, v, ret_gn_g[l]).astype(x.dtype) * jax.nn.silu(g_ret)
        ssm = s5_branch(u, s5_a_re[l], s5_a_im[l], s5_log_dt[l], s5_b_re[l], s5_b_im[l],
                        s5_c_re[l], s5_c_im[l], s5_d[l], s5_glu_w[l],
                        s5_glu_b[l]).astype(x.dtype) * jax.nn.silu(g_s5)
        x = x + jnp.concatenate([ret, ssm], axis=-1) @ w_out[l]
        h2 = rms_norm(x, norm2_g[l])
        m = rms_norm(mem, norm_mem_g[l])
        qa = (h2 @ xa_wq[l]).reshape(B, L, XA_HEADS, XA_DH)
        ka = (m @ xa_wk[l]).reshape(B, M, XA_HEADS, XA_DH)
        va = (m @ xa_wv[l]).reshape(B, M, XA_HEADS, XA_DH)
        s = jnp.einsum('blhd,bmhd->bhlm', qa, ka).astype(jnp.float32) * XA_DH ** -0.5
        p = jax.nn.softmax(s, axis=-1).astype(va.dtype)
        o = jnp.einsum('bhlm,bmhd->blhd', p, va).reshape(B, L, XA_HEADS * XA_DH)
        x = x + o @ xa_wo[l]
    return rms_norm(x, norm_f_g)
```

```python
import functools
import math

import numpy as np
import jax
import jax.numpy as jnp
from jax import lax
from jax.experimental import pallas as pl
from jax.experimental.pallas import tpu as pltpu

F32 = jnp.float32
BF16 = jnp.bfloat16

D_MODEL = 1024
MEM_LEN = 256
EPS = 1e-6
ROPE_BASE = 10000.0

RET_HEADS = 8
RET_QK_WIDTH = 512
RET_V_WIDTH = 1024
RET_DK = 64
RET_DV = 128
CHUNK = 128

S5_WIDTH = 1024
S5_GROUP = 16
S5_GROUPS = 64
S5_STATE = 64
S5_NSTATE = S5_GROUPS * S5_STATE

IN_COLS = 5120
COL_Q, COL_K, COL_V, COL_GR, COL_U, COL_GS = 0, 512, 1024, 2048, 3072, 4096

XA_HEADS = 4
XA_DH = 256

LANES = 128
SUBLANES = 8
VMEM_LIMIT = 48 * 1024 * 1024

LANE_TILES = S5_WIDTH // LANES
TILE_STATES = S5_NSTATE // LANE_TILES

LOG_G = tuple(math.log1p(-(2.0 ** (-5.0 - h))) for h in range(RET_HEADS))

TM_PROJ = 256
TM_RET = 256
TM_S5 = 256
TM_OUT = 256


def _rms(x, g):
    ms = jnp.mean(x * x, axis=-1, keepdims=True)
    return x * lax.rsqrt(ms + EPS) * g


def _silu(g):
    return g * (1.0 / (1.0 + jnp.exp(-g)))


def _dot(a, b):
    return jnp.dot(a, b, preferred_element_type=F32)


def _s5_prep_kernel(ar_ref, ai_ref, ldt_ref, brt_ref, bit_ref, consts_ref, bbr_ref, bbi_ref):
    ar = ar_ref[...]
    ai = ai_ref[...]
    dt = jnp.exp(ldt_ref[...])
    mag = jnp.exp(ar * dt)
    p_re = mag * jnp.cos(ai * dt)
    p_im = mag * jnp.sin(ai * dt)
    den = ar * ar + ai * ai
    nr, ni = p_re - 1.0, p_im
    f_re = (nr * ar + ni * ai) / den
    f_im = (ni * ar - nr * ai) / den
    br = brt_ref[...]
    bi = bit_ref[...]
    bbr_ref[...] = f_re * br - f_im * bi
    bbi_ref[...] = f_re * bi + f_im * br

    pw = [(p_re, p_im)]
    for _ in range(SUBLANES - 1):
        qr, qi = pw[-1]
        pw.append((qr * p_re - qi * p_im, qr * p_im + qi * p_re))
    row = lax.broadcasted_iota(jnp.int32, (SUBLANES, S5_NSTATE), 0)
    for idx, s in enumerate((1, 2, 4)):
        qr, qi = pw[s - 1]
        consts_ref[idx, 0] = jnp.where(row >= s, qr, 0.0)
        consts_ref[idx, 1] = jnp.where(row >= s, qi, 0.0)
    cr = jnp.zeros((SUBLANES, S5_NSTATE), F32)
    ci = jnp.zeros((SUBLANES, S5_NSTATE), F32)
    for r in range(SUBLANES):
        cr = jnp.where(row == r, pw[r][0], cr)
        ci = jnp.where(row == r, pw[r][1], ci)
    consts_ref[3, 0] = cr
    consts_ref[3, 1] = ci


def _s5_prep(a_re, a_im, log_dt, b_re, b_im):
    ar = a_re.reshape(1, S5_NSTATE)
    ai = a_im.reshape(1, S5_NSTATE)
    ldt = jnp.repeat(log_dt, S5_STATE).reshape(1, S5_NSTATE)
    brt = b_re.reshape(S5_NSTATE, S5_GROUP).T
    bit = b_im.reshape(S5_NSTATE, S5_GROUP).T
    return pl.pallas_call(
        _s5_prep_kernel,
        out_shape=(jax.ShapeDtypeStruct((4, 2, SUBLANES, S5_NSTATE), F32),
                   jax.ShapeDtypeStruct((S5_GROUP, S5_NSTATE), F32),
                   jax.ShapeDtypeStruct((S5_GROUP, S5_NSTATE), F32)),
        name="s5_prep",
    )(ar, ai, ldt, brt, bit)


def _mem_kv_kernel(mem_ref, g_ref, wk_ref, wv_ref, k_ref, v_ref):
    m = _rms(mem_ref[...], g_ref[...]).astype(BF16)
    k_ref[...] = _dot(m, wk_ref[...]).astype(BF16)
    v_ref[...] = _dot(m, wv_ref[...]).astype(BF16)


def _mem_kv(mem, g, wk, wv):
    B, M, D = mem.shape
    return pl.pallas_call(
        _mem_kv_kernel,
        grid=(B,),
        in_specs=[pl.BlockSpec((None, M, D), lambda b: (b, 0, 0)),
                  pl.BlockSpec((1, D), lambda b: (0, 0)),
                  pl.BlockSpec((D, D), lambda b: (0, 0)),
                  pl.BlockSpec((D, D), lambda b: (0, 0))],
        out_specs=(pl.BlockSpec((None, M, D), lambda b: (b, 0, 0)),
                   pl.BlockSpec((None, M, D), lambda b: (b, 0, 0))),
        out_shape=(jax.ShapeDtypeStruct((B, M, D), BF16),
                   jax.ShapeDtypeStruct((B, M, D), BF16)),
        compiler_params=pltpu.CompilerParams(
            dimension_semantics=("arbitrary",), vmem_limit_bytes=VMEM_LIMIT),
        name="mem_kv",
    )(mem, g, wk, wv)


def _in_proj_kernel(x_ref, pos_ref, inv_ref, g_ref, w_ref,
                    q_ref, k_ref, v_ref, gr_ref, u_ref, gs_ref):
    h = _rms(x_ref[...], g_ref[...]).astype(BF16)
    ang = pos_ref[...].astype(F32) * inv_ref[...]
    cos = jnp.cos(ang)
    sin = jnp.sin(ang)
    lane = lax.broadcasted_iota(jnp.int32, (1, LANES), 1)
    first_half = (lane % RET_DK) < (RET_DK // 2)
    sin_signed = jnp.where(first_half, -sin, sin)

    def rope(p):
        partner = jnp.where(first_half,
                            pltpu.roll(p, LANES - RET_DK // 2, 1),
                            pltpu.roll(p, RET_DK // 2, 1))
        return p * cos + partner * sin_signed

    pq = _dot(h, w_ref[:, COL_Q:COL_K])
    pk = _dot(h, w_ref[:, COL_K:COL_V])
    for t in range(RET_QK_WIDTH // LANES):
        sl = slice(t * LANES, (t + 1) * LANES)
        q_ref[:, sl] = rope(pq[:, sl]).astype(BF16)
        k_ref[:, sl] = (rope(pk[:, sl]) * (RET_DK ** -0.5)).astype(BF16)
    v_ref[...] = _dot(h, w_ref[:, COL_V:COL_GR]).astype(BF16)
    gr_ref[...] = _dot(h, w_ref[:, COL_GR:COL_U]).astype(BF16)
    u_ref[...] = _dot(h, w_ref[:, COL_U:COL_GS]).astype(BF16)
    gs_ref[...] = _dot(h, w_ref[:, COL_GS:IN_COLS]).astype(BF16)


def _in_proj(x, pos3, inv, g, w):
    B, L, D = x.shape
    tm = TM_PROJ
    row = lambda width: pl.BlockSpec((None, tm, width), lambda b, t: (b, t, 0))
    const = lambda shape: pl.BlockSpec(shape, lambda b, t: (0,) * len(shape))
    out = lambda width: jax.ShapeDtypeStruct((B, L, width), BF16)
    return pl.pallas_call(
        _in_proj_kernel,
        grid=(B, L // tm),
        in_specs=[row(D), row(1), const((1, LANES)), const((1, D)), const((D, IN_COLS))],
        out_specs=(row(RET_QK_WIDTH), row(RET_QK_WIDTH), row(RET_V_WIDTH), row(RET_V_WIDTH),
                   row(S5_WIDTH), row(S5_WIDTH)),
        out_shape=(out(RET_QK_WIDTH), out(RET_QK_WIDTH), out(RET_V_WIDTH), out(RET_V_WIDTH),
                   out(S5_WIDTH), out(S5_WIDTH)),
        compiler_params=pltpu.CompilerParams(
            dimension_semantics=("arbitrary", "arbitrary"), vmem_limit_bytes=VMEM_LIMIT),
        name="in_proj",
    )(x, pos3, inv, g, w)


def _retention_kernel(q_ref, k_ref, v_ref, g_ref, gn_ref, o_ref,
                      state_ref, decay_ref, qw_ref, kw_ref):
    b = pl.program_id(0)
    t = pl.program_id(1)
    pairs = RET_HEADS // 2

    @pl.when((b == 0) & (t == 0))
    def _constants():
        i = lax.broadcasted_iota(jnp.int32, (CHUNK, CHUNK), 0).astype(F32)
        j = lax.broadcasted_iota(jnp.int32, (CHUNK, CHUNK), 1).astype(F32)
        diff = i - j
        for h in range(RET_HEADS):
            decay_ref[h] = jnp.where(diff >= 0.0, jnp.exp(LOG_G[h] * jnp.maximum(diff, 0.0)), 0.0)
            qw_ref[h] = jnp.exp(LOG_G[h] * (i + 1.0))
        for p in range(pairs):
            lg = jnp.where(j < float(RET_DK), LOG_G[2 * p], LOG_G[2 * p + 1])
            kw_ref[p] = jnp.exp(lg * (CHUNK - 1.0 - i))

    @pl.when(t == 0)
    def _reset():
        state_ref[...] = jnp.zeros_like(state_ref)

    lane = lax.broadcasted_iota(jnp.int32, (1, LANES), 1)
    srow = lax.broadcasted_iota(jnp.int32, (LANES, 2 * RET_DV), 0)
    scol = lax.broadcasted_iota(jnp.int32, (LANES, 2 * RET_DV), 1)
    own_block = (srow < RET_DK) == (scol < RET_DV)
    scol1 = lax.broadcasted_iota(jnp.int32, (1, 2 * RET_DV), 1)

    for c in range(TM_RET // CHUNK):
        rows = slice(c * CHUNK, (c + 1) * CHUNK)
        for p in range(pairs):
            qt = q_ref[rows, p * LANES:(p + 1) * LANES]
            kt = k_ref[rows, p * LANES:(p + 1) * LANES]
            vp = v_ref[rows, 2 * p * RET_DV:2 * (p + 1) * RET_DV]
            state = state_ref[p]
            cross = _dot(qt, state.astype(BF16))
            kw = (kt.astype(F32) * kw_ref[p]).astype(BF16)
            upd = lax.dot_general(kw, vp, (((0,), (0,)), ((), ())), preferred_element_type=F32)
            chunk_decay = jnp.where(scol1 < RET_DV,
                                    math.exp(LOG_G[2 * p] * CHUNK), math.exp(LOG_G[2 * p + 1] * CHUNK))
            state_ref[p] = state * chunk_decay + jnp.where(own_block, upd, 0.0)
            for e in range(2):
                h = 2 * p + e
                cols = slice(h * RET_DV, (h + 1) * RET_DV)
                qm = jnp.where((lane < RET_DK) == (e == 0), qt, jnp.zeros_like(qt))
                s = lax.dot_general(qm, kt, (((1,), (1,)), ((), ())), preferred_element_type=F32)
                pm = (s * decay_ref[h]).astype(BF16)
                inner = _dot(pm, vp[:, e * RET_DV:(e + 1) * RET_DV])
                o = inner + qw_ref[h] * cross[:, e * RET_DV:(e + 1) * RET_DV]
                mu = jnp.mean(o, axis=-1, keepdims=True)
                oc = o - mu
                var = jnp.mean(oc * oc, axis=-1, keepdims=True)
                on = oc * lax.rsqrt(var + EPS) * gn_ref[:, cols]
                gate = _silu(g_ref[rows, cols].astype(F32))
                o_ref[rows, cols] = (on * gate).astype(BF16)


def _retention(q, k, v, g, gn):
    B, L, _ = q.shape
    tm = TM_RET
    row = lambda width: pl.BlockSpec((None, tm, width), lambda b, t: (b, t, 0))
    return pl.pallas_call(
        _retention_kernel,
        grid=(B, L // tm),
        in_specs=[row(RET_QK_WIDTH), row(RET_QK_WIDTH), row(RET_V_WIDTH), row(RET_V_WIDTH),
                  pl.BlockSpec((1, RET_V_WIDTH), lambda b, t: (0, 0))],
        out_specs=row(RET_V_WIDTH),
        out_shape=jax.ShapeDtypeStruct((B, L, RET_V_WIDTH), BF16),
        scratch_shapes=[pltpu.VMEM((RET_HEADS // 2, LANES, 2 * RET_DV), F32),
                        pltpu.VMEM((RET_HEADS, CHUNK, CHUNK), F32),
                        pltpu.VMEM((RET_HEADS, CHUNK, RET_DV), F32),
                        pltpu.VMEM((RET_HEADS // 2, CHUNK, LANES), F32)],
        compiler_params=pltpu.CompilerParams(
            dimension_semantics=("arbitrary", "arbitrary"), vmem_limit_bytes=VMEM_LIMIT),
        name="retention",
    )(q, k, v, g, gn)


def _s5_kernel(u_ref, g_ref, bmat_ref, cmat_ref, consts_ref, d_ref, gw_ref, gb_ref, o_ref,
               buf_ref, y_ref, carry_ref):
    t = pl.program_id(1)
    tm = u_ref.shape[0]
    half = TILE_STATES

    @pl.when(t == 0)
    def _reset():
        carry_ref[...] = jnp.zeros_like(carry_ref)

    for j in range(LANE_TILES):
        lanes = slice(j * LANES, (j + 1) * LANES)
        st = slice(j * half, (j + 1) * half)
        uj = u_ref[:, lanes]
        buf_ref[...] = _dot(uj, bmat_ref[j])

        def group(r, carry):
            cr, ci = carry
            rows = pl.ds(pl.multiple_of(r * SUBLANES, SUBLANES), SUBLANES)
            xr = buf_ref[rows, 0:half]
            xi = buf_ref[rows, half:2 * half]
            for idx, s in enumerate((1, 2, 4)):
                mr = consts_ref[idx, 0, :, st]
                mi = consts_ref[idx, 1, :, st]
                sr = pltpu.roll(xr, s, 0)
                si = pltpu.roll(xi, s, 0)
                xr, xi = xr + mr * sr - mi * si, xi + mr * si + mi * sr
            mr = consts_ref[3, 0, :, st]
            mi = consts_ref[3, 1, :, st]
            xr, xi = xr + mr * cr - mi * ci, xi + mr * ci + mi * cr
            buf_ref[rows, 0:half] = xr
            buf_ref[rows, half:2 * half] = xi
            last = SUBLANES - 1
            return (jnp.broadcast_to(xr[last:, :], (SUBLANES, half)),
                    jnp.broadcast_to(xi[last:, :], (SUBLANES, half)))

        cr, ci = lax.fori_loop(0, tm // SUBLANES, group, (carry_ref[j, 0], carry_ref[j, 1]))
        carry_ref[j, 0] = cr
        carry_ref[j, 1] = ci
        y = _dot(buf_ref[...].astype(BF16), cmat_ref[j])
        y = y + d_ref[:, lanes] * uj.astype(F32)
        y_ref[:, lanes] = jax.nn.gelu(y)

    y = y_ref[...]
    z = _dot(y.astype(BF16), gw_ref[...]) + gb_ref[...]
    glu = y * (1.0 / (1.0 + jnp.exp(-z)))
    o_ref[...] = (glu * _silu(g_ref[...].astype(F32))).astype(BF16)


def _s5(u, g, bmat, cmat, consts, d, gw, gb):
    B, L, W = u.shape
    tm = TM_S5
    row = pl.BlockSpec((None, tm, W), lambda b, t: (b, t, 0))
    const = lambda shape: pl.BlockSpec(shape, lambda b, t: (0,) * len(shape))
    return pl.pallas_call(
        _s5_kernel,
        grid=(B, L // tm),
        in_specs=[row, row, const(bmat.shape), const(cmat.shape), const(consts.shape),
                  const((1, W)), const((W, W)), const((1, W))],
        out_specs=row,
        out_shape=jax.ShapeDtypeStruct((B, L, W), BF16),
        scratch_shapes=[pltpu.VMEM((tm, 2 * TILE_STATES), F32),
                        pltpu.VMEM((tm, W), F32),
                        pltpu.VMEM((LANE_TILES, 2, SUBLANES, TILE_STATES), F32)],
        compiler_params=pltpu.CompilerParams(
            dimension_semantics=("arbitrary", "arbitrary"), vmem_limit_bytes=VMEM_LIMIT),
        name="s5",
    )(u, g, bmat, cmat, consts, d, gw, gb)


def _out_xattn_kernel(x_ref, ret_ref, ssm_ref, wout_ref, g2_ref, wq_ref, ka_ref, va_ref, wo_ref,
                      gf_ref, o_ref):
    x1 = (x_ref[...] + _dot(ret_ref[...], wout_ref[0:RET_V_WIDTH, :])
          + _dot(ssm_ref[...], wout_ref[RET_V_WIDTH:RET_V_WIDTH + S5_WIDTH, :]))
    h2 = _rms(x1, g2_ref[...]).astype(BF16)
    qa = (_dot(h2, wq_ref[...]) * (XA_DH ** -0.5)).astype(BF16)
    heads = []
    for h in range(XA_HEADS):
        cols = slice(h * XA_DH, (h + 1) * XA_DH)
        s = lax.dot_general(qa[:, cols], ka_ref[:, cols], (((1,), (1,)), ((), ())),
                            preferred_element_type=F32)
        e = jnp.exp(s - jnp.max(s, axis=-1, keepdims=True))
        l = jnp.sum(e, axis=-1, keepdims=True)
        heads.append((_dot(e.astype(BF16), va_ref[:, cols]) * (1.0 / l)).astype(BF16))
    o = jnp.concatenate(heads, axis=-1)
    x2 = x1 + _dot(o, wo_ref[...])
    o_ref[...] = _rms(x2, gf_ref[...])


def _out_xattn(x, ret, ssm, wout, g2, wq, ka, va, wo, gf):
    B, L, D = x.shape
    tm = TM_OUT
    row = pl.BlockSpec((None, tm, D), lambda b, t: (b, t, 0))
    const = lambda shape: pl.BlockSpec(shape, lambda b, t: (0,) * len(shape))
    mem = pl.BlockSpec((None, MEM_LEN, D), lambda b, t: (b, 0, 0))
    return pl.pallas_call(
        _out_xattn_kernel,
        grid=(B, L // tm),
        in_specs=[row, row, row, const(wout.shape), const((1, D)), const((D, D)), mem, mem,
                  const((D, D)), const((1, D))],
        out_specs=row,
        out_shape=jax.ShapeDtypeStruct((B, L, D), F32),
        compiler_params=pltpu.CompilerParams(
            dimension_semantics=("arbitrary", "arbitrary"), vmem_limit_bytes=VMEM_LIMIT),
        name="out_xattn",
    )(x, ret, ssm, wout, g2, wq, ka, va, wo, gf)


def _rope_inv_lanes():
    half = RET_DK // 2
    inv = ROPE_BASE ** (-np.arange(half, dtype=np.float64) / half)
    return jnp.asarray(np.tile(inv, LANES // half).reshape(1, LANES), F32)


def kernel(x, mem, positions, norm1_g, w_in, ret_gn_g, s5_a_re, s5_a_im, s5_log_dt, s5_b_re, s5_b_im, s5_c_re, s5_c_im, s5_d, s5_glu_w, s5_glu_b, w_out, norm2_g, norm_mem_g, xa_wq, xa_wk, xa_wv, xa_wo, norm_f_g):
    B, L, D = x.shape
    l = 0
    bf = lambda w: w.astype(BF16)
    rowvec = lambda v: v.reshape(1, -1)

    consts, bbr, bbi = _s5_prep(s5_a_re[l], s5_a_im[l], s5_log_dt[l], s5_b_re[l], s5_b_im[l])
    gl = LANE_TILES
    eye = jnp.eye(gl, dtype=F32)

    def b_tiles(bbt):
        t = bbt.reshape(S5_GROUP, LANE_TILES, gl, S5_STATE)
        return jnp.einsum('pjgn,gh->jgphn', t, eye).reshape(LANE_TILES, LANES, TILE_STATES)

    def c_tiles(c):
        t = c.reshape(LANE_TILES, gl, S5_GROUP, S5_STATE)
        return jnp.einsum('jgpn,gh->jgnhp', t, eye).reshape(LANE_TILES, TILE_STATES, LANES)

    bmat = bf(jnp.concatenate([b_tiles(bbr), b_tiles(bbi)], axis=-1))
    cmat = bf(jnp.concatenate([c_tiles(s5_c_re[l]), -c_tiles(s5_c_im[l])], axis=1))

    ka, va = _mem_kv(mem, rowvec(norm_mem_g[l]), bf(xa_wk[l]), bf(xa_wv[l]))
    q, k, v, g_ret, u, g_s5 = _in_proj(x, positions.reshape(B, L, 1), _rope_inv_lanes(),
                                       rowvec(norm1_g[l]), bf(w_in[l]))
    ret = _retention(q, k, v, g_ret, rowvec(ret_gn_g[l]))
    ssm = _s5(u, g_s5, bmat, cmat, consts, rowvec(s5_d[l]), bf(s5_glu_w[l]), rowvec(s5_glu_b[l]))
    return _out_xattn(x, ret, ssm, bf(w_out[l]), rowvec(norm2_g[l]), bf(xa_wq[l]), ka, va,
                      bf(xa_wo[l]), rowvec(norm_f_g))
```

```python
import math

import numpy as np
import jax
import jax.numpy as jnp
from jax import lax
from jax.experimental import pallas as pl
from jax.experimental.pallas import tpu as pltpu

F32 = jnp.float32
BF16 = jnp.bfloat16

D_MODEL = 1024
MEM_LEN = 256
EPS = 1e-6
ROPE_BASE = 10000.0

RET_HEADS = 8
RET_QK_WIDTH = 512
RET_V_WIDTH = 1024
RET_DK = 64
RET_DV = 128
CHUNK = 128

S5_WIDTH = 1024
S5_GROUP = 16
S5_GROUPS = 64
S5_STATE = 64
S5_NSTATE = S5_GROUPS * S5_STATE

IN_COLS = 5120
COL_Q, COL_K, COL_V, COL_GR, COL_U, COL_GS = 0, 512, 1024, 2048, 3072, 4096

XA_HEADS = 4
XA_DH = 256

LANES = 128
SUBLANES = 8
VMEM_LIMIT = 48 * 1024 * 1024

LANE_TILES = S5_WIDTH // LANES
TILE_GROUPS = LANES // S5_GROUP
TILE_STATES = S5_NSTATE // LANE_TILES
S5_TAPS = 4

LOG_G = tuple(math.log1p(-(2.0 ** (-5.0 - h))) for h in range(RET_HEADS))

TM_PROJ = 256
TM_RET = 256
TM_S5 = 2048
TM_OUT = 256


def _rms(x, g):
    ms = jnp.mean(x * x, axis=-1, keepdims=True)
    return x * lax.rsqrt(ms + EPS) * g


def _sigmoid(z):
    return 1.0 / (1.0 + jnp.exp(-z))


def _silu(g):
    return g * _sigmoid(g)


def _dot(a, b):
    return jnp.dot(a, b, preferred_element_type=F32)


def _cmul(ar, ai, br, bi):
    return ar * br - ai * bi, ar * bi + ai * br


def _s5_discretise(ar, ai, ldt):
    dt = jnp.exp(ldt)
    mag = jnp.exp(ar * dt)
    p_re = mag * jnp.cos(ai * dt)
    p_im = mag * jnp.sin(ai * dt)
    den = ar * ar + ai * ai
    nr, ni = p_re - 1.0, p_im
    f_re = (nr * ar + ni * ai) / den
    f_im = (ni * ar - nr * ai) / den
    return p_re, p_im, f_re, f_im


def _powers(p_re, p_im, n):
    pw = [(jnp.ones_like(p_re), jnp.zeros_like(p_im))]
    for _ in range(n):
        pw.append(_cmul(pw[-1][0], pw[-1][1], p_re, p_im))
    return pw


def _s5_prep_kernel(arow_ref, acol_ref, br_ref, bi_ref, cr_ref, ci_ref, wb_ref, wck_ref, consts_ref):
    s = S5_TAPS
    half = TILE_STATES
    p_re, p_im, f_re, f_im = _s5_discretise(arow_ref[0:1, :], arow_ref[1:2, :], arow_ref[2:3, :])
    pw = _powers(p_re, p_im, s * SUBLANES)
    bbr, bbi = _cmul(f_re, f_im, br_ref[...], bi_ref[...])
    for k in range(s):
        wr, wi = _cmul(bbr, bbi, *pw[s - 1 - k])
        wb_ref[k * LANES:(k + 1) * LANES, 0:half] = wr.astype(BF16)
        wb_ref[k * LANES:(k + 1) * LANES, half:2 * half] = wi.astype(BF16)

    row = lax.broadcasted_iota(jnp.int32, (SUBLANES, half), 0)
    for idx, d in enumerate((1, 2, 4)):
        qr, qi = pw[s * d]
        consts_ref[idx, 0] = jnp.where(row >= d, qr, 0.0)
        consts_ref[idx, 1] = jnp.where(row >= d, qi, 0.0)
    cr = jnp.zeros((SUBLANES, half), F32)
    ci = jnp.zeros((SUBLANES, half), F32)
    for r in range(SUBLANES):
        cr = jnp.where(row == r, pw[s * (r + 1)][0], cr)
        ci = jnp.where(row == r, pw[s * (r + 1)][1], ci)
    consts_ref[3, 0] = cr
    consts_ref[3, 1] = ci

    q_re, q_im, _, _ = _s5_discretise(acol_ref[:, 0:1], acol_ref[:, 1:2], acol_ref[:, 2:3])
    qw = _powers(q_re, q_im, s)
    c_re = cr_ref[...]
    c_im = ci_ref[...]
    bb = jnp.concatenate([bbr, bbi], axis=1)
    taps = []
    for d in range(s + 1):
        er, ei = _cmul(c_re, c_im, *qw[d])
        cw = jnp.concatenate([er, -ei], axis=0)
        if d >= 1:
            wck_ref[0:2 * half, (d - 1) * LANES:d * LANES] = cw.astype(BF16)
        if d < s:
            taps.append(jnp.dot(bb, cw, preferred_element_type=F32, precision=lax.Precision.HIGHEST))
    zero = jnp.zeros((LANES, LANES), BF16)
    for m in range(s):
        for i in range(s):
            blk = taps[i - m].astype(BF16) if m <= i else zero
            wck_ref[2 * half + m * LANES:2 * half + (m + 1) * LANES, i * LANES:(i + 1) * LANES] = blk


def _s5_prep(a_re, a_im, log_dt, b_re, b_im, c_re, c_im):
    s = S5_TAPS
    tg = TILE_GROUPS
    eye = jnp.eye(tg, dtype=F32)
    ldt = jnp.broadcast_to(log_dt[:, None], (S5_GROUPS, S5_STATE))
    a3 = jnp.stack([a_re, a_im, ldt]).reshape(3, LANE_TILES, TILE_STATES)
    arow = a3.transpose(1, 0, 2)
    acol = a3.transpose(1, 2, 0)

    def b_tiles(b):
        t = b.reshape(LANE_TILES, tg, S5_STATE, S5_GROUP)
        return jnp.einsum('jgnp,gh->jgphn', t, eye).reshape(LANE_TILES, LANES, TILE_STATES)

    def c_tiles(c):
        t = c.reshape(LANE_TILES, tg, S5_GROUP, S5_STATE)
        return jnp.einsum('jgpn,gh->jgnhp', t, eye).reshape(LANE_TILES, TILE_STATES, LANES)

    tile = lambda *shape: pl.BlockSpec((None,) + shape, lambda j: (j,) + (0,) * len(shape))
    return pl.pallas_call(
        _s5_prep_kernel,
        grid=(LANE_TILES,),
        in_specs=[tile(3, TILE_STATES), tile(TILE_STATES, 3),
                  tile(LANES, TILE_STATES), tile(LANES, TILE_STATES),
                  tile(TILE_STATES, LANES), tile(TILE_STATES, LANES)],
        out_specs=(tile(s * LANES, 2 * TILE_STATES),
                   tile(2 * TILE_STATES + s * LANES, s * LANES),
                   tile(4, 2, SUBLANES, TILE_STATES)),
        out_shape=(jax.ShapeDtypeStruct((LANE_TILES, s * LANES, 2 * TILE_STATES), BF16),
                   jax.ShapeDtypeStruct((LANE_TILES, 2 * TILE_STATES + s * LANES, s * LANES), BF16),
                   jax.ShapeDtypeStruct((LANE_TILES, 4, 2, SUBLANES, TILE_STATES), F32)),
        compiler_params=pltpu.CompilerParams(
            dimension_semantics=("arbitrary",), vmem_limit_bytes=VMEM_LIMIT),
        name="s5_prep",
    )(arow, acol, b_tiles(b_re), b_tiles(b_im), c_tiles(c_re), c_tiles(c_im))


def _mem_kv_kernel(mem_ref, g_ref, wk_ref, wv_ref, k_ref, v_ref):
    m = _rms(mem_ref[...], g_ref[...]).astype(BF16)
    k_ref[...] = _dot(m, wk_ref[...]).astype(BF16)
    v_ref[...] = _dot(m, wv_ref[...]).astype(BF16)


def _mem_kv(mem, g, wk, wv):
    B, M, D = mem.shape
    return pl.pallas_call(
        _mem_kv_kernel,
        grid=(B,),
        in_specs=[pl.BlockSpec((None, M, D), lambda b: (b, 0, 0)),
                  pl.BlockSpec((1, D), lambda b: (0, 0)),
                  pl.BlockSpec((D, D), lambda b: (0, 0)),
                  pl.BlockSpec((D, D), lambda b: (0, 0))],
        out_specs=(pl.BlockSpec((None, M, D), lambda b: (b, 0, 0)),
                   pl.BlockSpec((None, M, D), lambda b: (b, 0, 0))),
        out_shape=(jax.ShapeDtypeStruct((B, M, D), BF16),
                   jax.ShapeDtypeStruct((B, M, D), BF16)),
        compiler_params=pltpu.CompilerParams(
            dimension_semantics=("arbitrary",), vmem_limit_bytes=VMEM_LIMIT),
        name="mem_kv",
    )(mem, g, wk, wv)


def _in_proj_kernel(x_ref, pos_ref, inv_ref, g_ref, w_ref,
                    q_ref, k_ref, v_ref, gr_ref, u_ref, gs_ref):
    h = _rms(x_ref[...], g_ref[...]).astype(BF16)
    ang = pos_ref[...].astype(F32) * inv_ref[...]
    cos = jnp.cos(ang)
    sin = jnp.sin(ang)
    lane = lax.broadcasted_iota(jnp.int32, (1, LANES), 1)
    first_half = (lane % RET_DK) < (RET_DK // 2)
    sin_signed = jnp.where(first_half, -sin, sin)

    def rope(p):
        partner = jnp.where(first_half,
                            pltpu.roll(p, LANES - RET_DK // 2, 1),
                            pltpu.roll(p, RET_DK // 2, 1))
        return p * cos + partner * sin_signed

    pq = _dot(h, w_ref[:, COL_Q:COL_K])
    pk = _dot(h, w_ref[:, COL_K:COL_V])
    for t in range(RET_QK_WIDTH // LANES):
        sl = slice(t * LANES, (t + 1) * LANES)
        q_ref[:, sl] = rope(pq[:, sl]).astype(BF16)
        k_ref[:, sl] = (rope(pk[:, sl]) * (RET_DK ** -0.5)).astype(BF16)
    v_ref[...] = _dot(h, w_ref[:, COL_V:COL_GR]).astype(BF16)
    gr_ref[...] = _dot(h, w_ref[:, COL_GR:COL_U]).astype(BF16)
    pu = _dot(h, w_ref[:, COL_U:COL_GS]).astype(BF16)
    for j in range(LANE_TILES):
        u_ref[j] = pu[:, j * LANES:(j + 1) * LANES]
    gs_ref[...] = _dot(h, w_ref[:, COL_GS:IN_COLS]).astype(BF16)


def _in_proj(x, pos3, inv, g, w):
    B, L, D = x.shape
    tm = TM_PROJ
    row = lambda width: pl.BlockSpec((None, tm, width), lambda b, t: (b, t, 0))
    const = lambda shape: pl.BlockSpec(shape, lambda b, t: (0,) * len(shape))
    out = lambda width: jax.ShapeDtypeStruct((B, L, width), BF16)
    return pl.pallas_call(
        _in_proj_kernel,
        grid=(B, L // tm),
        in_specs=[row(D), row(1), const((1, LANES)), const((1, D)), const((D, IN_COLS))],
        out_specs=(row(RET_QK_WIDTH), row(RET_QK_WIDTH), row(RET_V_WIDTH), row(RET_V_WIDTH),
                   pl.BlockSpec((None, LANE_TILES, tm, LANES), lambda b, t: (b, 0, t, 0)),
                   row(S5_WIDTH)),
        out_shape=(out(RET_QK_WIDTH), out(RET_QK_WIDTH), out(RET_V_WIDTH), out(RET_V_WIDTH),
                   jax.ShapeDtypeStruct((B, LANE_TILES, L, LANES), BF16),
                   out(S5_WIDTH)),
        compiler_params=pltpu.CompilerParams(
            dimension_semantics=("arbitrary", "arbitrary"), vmem_limit_bytes=VMEM_LIMIT),
        name="in_proj",
    )(x, pos3, inv, g, w)


def _retention_kernel(q_ref, k_ref, v_ref, g_ref, gn_ref, o_ref,
                      state_ref, decay_ref, qw_ref, kw_ref):
    b = pl.program_id(0)
    t = pl.program_id(1)
    pairs = RET_HEADS // 2

    @pl.when((b == 0) & (t == 0))
    def _constants():
        i = lax.broadcasted_iota(jnp.int32, (CHUNK, CHUNK), 0).astype(F32)
        j = lax.broadcasted_iota(jnp.int32, (CHUNK, CHUNK), 1).astype(F32)
        diff = i - j
        for h in range(RET_HEADS):
            decay_ref[h] = jnp.where(diff >= 0.0, jnp.exp(LOG_G[h] * jnp.maximum(diff, 0.0)), 0.0)
            qw_ref[h] = jnp.exp(LOG_G[h] * (i + 1.0))
        for p in range(pairs):
            lg = jnp.where(j < float(RET_DK), LOG_G[2 * p], LOG_G[2 * p + 1])
            kw_ref[p] = jnp.exp(lg * (CHUNK - 1.0 - i))

    @pl.when(t == 0)
    def _reset():
        state_ref[...] = jnp.zeros_like(state_ref)

    lane = lax.broadcasted_iota(jnp.int32, (1, LANES), 1)
    srow = lax.broadcasted_iota(jnp.int32, (LANES, 2 * RET_DV), 0)
    scol = lax.broadcasted_iota(jnp.int32, (LANES, 2 * RET_DV), 1)
    own_block = (srow < RET_DK) == (scol < RET_DV)
    scol1 = lax.broadcasted_iota(jnp.int32, (1, 2 * RET_DV), 1)

    for c in range(TM_RET // CHUNK):
        rows = slice(c * CHUNK, (c + 1) * CHUNK)
        for p in range(pairs):
            qt = q_ref[rows, p * LANES:(p + 1) * LANES]
            kt = k_ref[rows, p * LANES:(p + 1) * LANES]
            vp = v_ref[rows, 2 * p * RET_DV:2 * (p + 1) * RET_DV]
            state = state_ref[p]
            cross = _dot(qt, state.astype(BF16))
            kw = (kt.astype(F32) * kw_ref[p]).astype(BF16)
            upd = lax.dot_general(kw, vp, (((0,), (0,)), ((), ())), preferred_element_type=F32)
            chunk_decay = jnp.where(scol1 < RET_DV,
                                    math.exp(LOG_G[2 * p] * CHUNK), math.exp(LOG_G[2 * p + 1] * CHUNK))
            state_ref[p] = state * chunk_decay + jnp.where(own_block, upd, 0.0)
            for e in range(2):
                h = 2 * p + e
                cols = slice(h * RET_DV, (h + 1) * RET_DV)
                qm = jnp.where((lane < RET_DK) == (e == 0), qt, jnp.zeros_like(qt))
                s = lax.dot_general(qm, kt, (((1,), (1,)), ((), ())), preferred_element_type=F32)
                pm = (s * decay_ref[h]).astype(BF16)
                inner = _dot(pm, vp[:, e * RET_DV:(e + 1) * RET_DV])
                o = inner + qw_ref[h] * cross[:, e * RET_DV:(e + 1) * RET_DV]
                mu = jnp.mean(o, axis=-1, keepdims=True)
                oc = o - mu
                var = jnp.mean(oc * oc, axis=-1, keepdims=True)
                on = oc * lax.rsqrt(var + EPS) * gn_ref[:, cols]
                gate = _silu(g_ref[rows, cols].astype(F32))
                o_ref[rows, cols] = (on * gate).astype(BF16)


def _retention(q, k, v, g, gn):
    B, L, _ = q.shape
    tm = TM_RET
    row = lambda width: pl.BlockSpec((None, tm, width), lambda b, t: (b, t, 0))
    return pl.pallas_call(
        _retention_kernel,
        grid=(B, L // tm),
        in_specs=[row(RET_QK_WIDTH), row(RET_QK_WIDTH), row(RET_V_WIDTH), row(RET_V_WIDTH),
                  pl.BlockSpec((1, RET_V_WIDTH), lambda b, t: (0, 0))],
        out_specs=row(RET_V_WIDTH),
        out_shape=jax.ShapeDtypeStruct((B, L, RET_V_WIDTH), BF16),
        scratch_shapes=[pltpu.VMEM((RET_HEADS // 2, LANES, 2 * RET_DV), F32),
                        pltpu.VMEM((RET_HEADS, CHUNK, CHUNK), F32),
                        pltpu.VMEM((RET_HEADS, CHUNK, RET_DV), F32),
                        pltpu.VMEM((RET_HEADS // 2, CHUNK, LANES), F32)],
        compiler_params=pltpu.CompilerParams(
            dimension_semantics=("arbitrary", "arbitrary"), vmem_limit_bytes=VMEM_LIMIT),
        name="retention",
    )(q, k, v, g, gn)


def _s5_scan_kernel(u_ref, wb_ref, wck_ref, consts_ref, d_ref, y_ref, uf_ref, zs_ref, ys_ref):
    t = pl.program_id(2)
    s = S5_TAPS
    tm = u_ref.shape[0]
    nblk = tm // s
    half = TILE_STATES
    top = SUBLANES

    @pl.when(t == 0)
    def _reset():
        zs_ref[0:top, :] = jnp.zeros((top, 2 * half), F32)

    uf_ref[...] = u_ref[...].astype(F32)
    lhs = jnp.concatenate([uf_ref[pl.ds(k, nblk, stride=s), :] for k in range(s)], axis=1).astype(BF16)
    zs_ref[top:top + nblk, :] = _dot(lhs, wb_ref[...])

    def group(r, carry):
        cr, ci = carry
        rows = pl.ds(pl.multiple_of(top + r * SUBLANES, SUBLANES), SUBLANES)
        xr = zs_ref[rows, 0:half]
        xi = zs_ref[rows, half:2 * half]
        for idx, d in enumerate((1, 2, 4)):
            mr = consts_ref[idx, 0]
            mi = consts_ref[idx, 1]
            sr = pltpu.roll(xr, d, 0)
            si = pltpu.roll(xi, d, 0)
            xr, xi = xr + mr * sr - mi * si, xi + mr * si + mi * sr
        mr = consts_ref[3, 0]
        mi = consts_ref[3, 1]
        xr, xi = xr + mr * cr - mi * ci, xi + mr * ci + mi * cr
        zs_ref[rows, 0:half] = xr
        zs_ref[rows, half:2 * half] = xi
        last = SUBLANES - 1
        return (jnp.broadcast_to(xr[last:, :], (SUBLANES, half)),
                jnp.broadcast_to(xi[last:, :], (SUBLANES, half)))

    carry0 = (jnp.broadcast_to(zs_ref[top - 1:top, 0:half], (SUBLANES, half)),
              jnp.broadcast_to(zs_ref[top - 1:top, half:2 * half], (SUBLANES, half)))
    lax.fori_loop(0, nblk // SUBLANES, group, carry0)

    prev = zs_ref[pl.ds(top - 1, nblk), :].astype(BF16)
    zs_ref[top - 1:top, :] = zs_ref[top + nblk - 1:top + nblk, :]
    yall = _dot(jnp.concatenate([prev, lhs], axis=1), wck_ref[...])
    for i in range(s):
        ys_ref[pl.ds(i, nblk, stride=s), :] = yall[:, i * LANES:(i + 1) * LANES]
    y = ys_ref[...] + d_ref[...] * uf_ref[...]
    y_ref[...] = jax.nn.gelu(y).astype(BF16)


def _s5_scan(u, wb, wck, consts, d):
    B, _, L, _ = u.shape
    s = S5_TAPS
    tm = TM_S5
    tok = pl.BlockSpec((None, None, tm, LANES), lambda j, b, t: (b, j, t, 0))
    tile = lambda *shape: pl.BlockSpec((None,) + shape, lambda j, b, t: (j,) + (0,) * len(shape))
    return pl.pallas_call(
        _s5_scan_kernel,
        grid=(LANE_TILES, B, L // tm),
        in_specs=[tok, tile(s * LANES, 2 * TILE_STATES), tile(2 * TILE_STATES + s * LANES, s * LANES),
                  tile(4, 2, SUBLANES, TILE_STATES), tile(1, LANES)],
        out_specs=tok,
        out_shape=jax.ShapeDtypeStruct((B, LANE_TILES, L, LANES), BF16),
        scratch_shapes=[pltpu.VMEM((tm, LANES), F32),
                        pltpu.VMEM((SUBLANES + tm // s, 2 * TILE_STATES), F32),
                        pltpu.VMEM((tm, LANES), F32)],
        compiler_params=pltpu.CompilerParams(
            dimension_semantics=("arbitrary", "arbitrary", "arbitrary"), vmem_limit_bytes=VMEM_LIMIT),
        name="s5_scan",
    )(u, wb, wck, consts, d)


def _out_xattn_kernel(x_ref, ret_ref, y_ref, gs_ref, gw_ref, gb_ref, wout_ref, g2_ref, wq_ref,
                      ka_ref, va_ref, wo_ref, gf_ref, o_ref):
    y = jnp.concatenate([y_ref[j] for j in range(LANE_TILES)], axis=1)
    z = _dot(y, gw_ref[...]) + gb_ref[...]
    ssm = (y.astype(F32) * _sigmoid(z) * _silu(gs_ref[...].astype(F32))).astype(BF16)
    x1 = (x_ref[...] + _dot(ret_ref[...], wout_ref[0:RET_V_WIDTH, :])
          + _dot(ssm, wout_ref[RET_V_WIDTH:RET_V_WIDTH + S5_WIDTH, :]))
    h2 = _rms(x1, g2_ref[...]).astype(BF16)
    qa = (_dot(h2, wq_ref[...]) * (XA_DH ** -0.5)).astype(BF16)
    heads = []
    for h in range(XA_HEADS):
        cols = slice(h * XA_DH, (h + 1) * XA_DH)
        s = lax.dot_general(qa[:, cols], ka_ref[:, cols], (((1,), (1,)), ((), ())),
                            preferred_element_type=F32)
        e = jnp.exp(s - jnp.max(s, axis=-1, keepdims=True))
        l = jnp.sum(e, axis=-1, keepdims=True)
        heads.append((_dot(e.astype(BF16), va_ref[:, cols]) * (1.0 / l)).astype(BF16))
    o = jnp.concatenate(heads, axis=-1)
    x2 = x1 + _dot(o, wo_ref[...])
    o_ref[...] = _rms(x2, gf_ref[...])


def _out_xattn(x, ret, y, gs, gw, gb, wout, g2, wq, ka, va, wo, gf):
    B, L, D = x.shape
    tm = TM_OUT
    row = pl.BlockSpec((None, tm, D), lambda b, t: (b, t, 0))
    const = lambda shape: pl.BlockSpec(shape, lambda b, t: (0,) * len(shape))
    mem = pl.BlockSpec((None, MEM_LEN, D), lambda b, t: (b, 0, 0))
    ytile = pl.BlockSpec((None, LANE_TILES, tm, LANES), lambda b, t: (b, 0, t, 0))
    return pl.pallas_call(
        _out_xattn_kernel,
        grid=(B, L // tm),
        in_specs=[row, row, ytile, row, const((D, D)), const((1, D)), const(wout.shape), const((1, D)),
                  const((D, D)), mem, mem, const((D, D)), const((1, D))],
        out_specs=row,
        out_shape=jax.ShapeDtypeStruct((B, L, D), F32),
        compiler_params=pltpu.CompilerParams(
            dimension_semantics=("arbitrary", "arbitrary"), vmem_limit_bytes=VMEM_LIMIT),
        name="out_xattn",
    )(x, ret, y, gs, gw, gb, wout, g2, wq, ka, va, wo, gf)


def _rope_inv_lanes():
    half = RET_DK // 2
    inv = ROPE_BASE ** (-np.arange(half, dtype=np.float64) / half)
    return jnp.asarray(np.tile(inv, LANES // half).reshape(1, LANES), F32)


def kernel(x, mem, positions, norm1_g, w_in, ret_gn_g, s5_a_re, s5_a_im, s5_log_dt, s5_b_re, s5_b_im, s5_c_re, s5_c_im, s5_d, s5_glu_w, s5_glu_b, w_out, norm2_g, norm_mem_g, xa_wq, xa_wk, xa_wv, xa_wo, norm_f_g):
    B, L, D = x.shape
    l = 0
    bf = lambda w: w.astype(BF16)
    rowvec = lambda v: v.reshape(1, -1)

    wb, wck, consts = _s5_prep(s5_a_re[l], s5_a_im[l], s5_log_dt[l], s5_b_re[l], s5_b_im[l],
                               s5_c_re[l], s5_c_im[l])
    ka, va = _mem_kv(mem, rowvec(norm_mem_g[l]), bf(xa_wk[l]), bf(xa_wv[l]))
    q, k, v, g_ret, u, g_s5 = _in_proj(x, positions.reshape(B, L, 1), _rope_inv_lanes(),
                                       rowvec(norm1_g[l]), bf(w_in[l]))
    ret = _retention(q, k, v, g_ret, rowvec(ret_gn_g[l]))
    y = _s5_scan(u, wb, wck, consts, s5_d[l].reshape(LANE_TILES, 1, LANES))
    return _out_xattn(x, ret, y, g_s5, bf(s5_glu_w[l]), rowvec(s5_glu_b[l]), bf(w_out[l]),
                      rowvec(norm2_g[l]), bf(xa_wq[l]), ka, va, bf(xa_wo[l]), rowvec(norm_f_g))
```

```python
import math

import numpy as np
import jax
import jax.numpy as jnp
from jax import lax
from jax.experimental import pallas as pl
from jax.experimental.pallas import tpu as pltpu

F32 = jnp.float32
BF16 = jnp.bfloat16

D_MODEL = 1024
MEM_LEN = 256
EPS = 1e-6
ROPE_BASE = 10000.0

RET_HEADS = 8
RET_QK_WIDTH = 512
RET_V_WIDTH = 1024
RET_DK = 64
RET_DV = 128
CHUNK = 128

S5_WIDTH = 1024
S5_GROUP = 16
S5_GROUPS = 64
S5_STATE = 64
S5_NSTATE = S5_GROUPS * S5_STATE

IN_COLS = 5120
COL_Q, COL_K, COL_V, COL_GR, COL_U, COL_GS = 0, 512, 1024, 2048, 3072, 4096

XA_HEADS = 4
XA_DH = 256

LANES = 128
SUBLANES = 8
VMEM_LIMIT = 48 * 1024 * 1024

LANE_TILES = S5_WIDTH // LANES
TILE_GROUPS = LANES // S5_GROUP
TILE_STATES = S5_NSTATE // LANE_TILES
S5_TAPS = 4

LOG_G = tuple(math.log1p(-(2.0 ** (-5.0 - h))) for h in range(RET_HEADS))

TM_PROJ = 256
TM_RET = 256
TM_S5 = 2048
TM_OUT = 256


def _rms(x, g):
    ms = jnp.mean(x * x, axis=-1, keepdims=True)
    return x * lax.rsqrt(ms + EPS) * g


def _sigmoid(z):
    return 1.0 / (1.0 + jnp.exp(-z))


def _silu(g):
    return g * _sigmoid(g)


def _dot(a, b):
    return jnp.dot(a, b, preferred_element_type=F32)


def _cmul(ar, ai, br, bi):
    return ar * br - ai * bi, ar * bi + ai * br


def _s5_discretise(ar, ai, ldt):
    dt = jnp.exp(ldt)
    mag = jnp.exp(ar * dt)
    p_re = mag * jnp.cos(ai * dt)
    p_im = mag * jnp.sin(ai * dt)
    den = ar * ar + ai * ai
    nr, ni = p_re - 1.0, p_im
    f_re = (nr * ar + ni * ai) / den
    f_im = (ni * ar - nr * ai) / den
    return p_re, p_im, f_re, f_im


def _powers(p_re, p_im, n):
    pw = [(jnp.ones_like(p_re), jnp.zeros_like(p_im))]
    for _ in range(n):
        pw.append(_cmul(pw[-1][0], pw[-1][1], p_re, p_im))
    return pw


def _block_diag(x):
    tiled = jnp.concatenate([x] * TILE_GROUPS, axis=1)
    r = lax.broadcasted_iota(jnp.int32, tiled.shape, 0)
    c = lax.broadcasted_iota(jnp.int32, tiled.shape, 1)
    return jnp.where(r // S5_GROUP == c // S5_STATE, tiled, 0.0)


def _s5_prep_kernel(a_ref, b_ref, c_ref, wb_ref, wck_ref, consts_ref):
    s = S5_TAPS
    half = TILE_STATES
    p_re, p_im, f_re, f_im = _s5_discretise(a_ref[0:1, :], a_ref[1:2, :], a_ref[2:3, :])
    pw = _powers(p_re, p_im, s)
    bbr, bbi = _cmul(f_re, f_im, _block_diag(b_ref[0]), _block_diag(b_ref[1]))
    for k in range(s):
        wr, wi = _cmul(bbr, bbi, *pw[s - 1 - k])
        wb_ref[k * LANES:(k + 1) * LANES, 0:half] = wr.astype(BF16)
        wb_ref[k * LANES:(k + 1) * LANES, half:2 * half] = wi.astype(BF16)

    shape = (SUBLANES, half)
    qp = _powers(pw[s][0], pw[s][1], SUBLANES)
    row = lax.broadcasted_iota(jnp.int32, shape, 0)
    for idx, d in enumerate((1, 2, 4)):
        consts_ref[idx, 0] = jnp.where(row >= d, qp[d][0], 0.0)
        consts_ref[idx, 1] = jnp.where(row >= d, qp[d][1], 0.0)
    cr = jnp.zeros(shape, F32)
    ci = jnp.zeros(shape, F32)
    for r in range(SUBLANES):
        cr = jnp.where(row == r, qp[r + 1][0], cr)
        ci = jnp.where(row == r, qp[r + 1][1], ci)
    consts_ref[3, 0] = cr
    consts_ref[3, 1] = ci

    ct_re = _block_diag(c_ref[0])
    ct_im = _block_diag(c_ref[1])
    bb = jnp.concatenate([bbr, bbi], axis=1)
    taps = []
    for d in range(s + 1):
        er, ei = _cmul(ct_re, ct_im, *pw[d])
        cwt = jnp.concatenate([er, -ei], axis=1)
        if d >= 1:
            wck_ref[0:2 * half, (d - 1) * LANES:d * LANES] = cwt.T.astype(BF16)
        if d < s:
            taps.append(lax.dot_general(bb, cwt, (((1,), (1,)), ((), ())), preferred_element_type=F32,
                                        precision=lax.Precision.HIGHEST))
    zero = jnp.zeros((LANES, LANES), BF16)
    for m in range(s):
        for i in range(s):
            blk = taps[i - m].astype(BF16) if m <= i else zero
            wck_ref[2 * half + m * LANES:2 * half + (m + 1) * LANES, i * LANES:(i + 1) * LANES] = blk


def _s5_prep(a_re, a_im, log_dt, b_re, b_im, c_re, c_im):
    s = S5_TAPS
    ldt = jnp.broadcast_to(log_dt[:, None], (S5_GROUPS, S5_STATE))
    a3 = jnp.stack([a_re, a_im, ldt]).reshape(3, S5_NSTATE)
    bt = jnp.stack([b_re, b_im]).transpose(0, 1, 3, 2).reshape(2, S5_WIDTH, S5_STATE)
    ct = jnp.stack([c_re, c_im]).reshape(2, S5_WIDTH, S5_STATE)
    tile = lambda *shape: pl.BlockSpec((None,) + shape, lambda j: (j,) + (0,) * len(shape))
    return pl.pallas_call(
        _s5_prep_kernel,
        grid=(LANE_TILES,),
        in_specs=[pl.BlockSpec((3, TILE_STATES), lambda j: (0, j)),
                  pl.BlockSpec((2, LANES, S5_STATE), lambda j: (0, j, 0)),
                  pl.BlockSpec((2, LANES, S5_STATE), lambda j: (0, j, 0))],
        out_specs=(tile(s * LANES, 2 * TILE_STATES),
                   tile(2 * TILE_STATES + s * LANES, s * LANES),
                   tile(4, 2, SUBLANES, TILE_STATES)),
        out_shape=(jax.ShapeDtypeStruct((LANE_TILES, s * LANES, 2 * TILE_STATES), BF16),
                   jax.ShapeDtypeStruct((LANE_TILES, 2 * TILE_STATES + s * LANES, s * LANES), BF16),
                   jax.ShapeDtypeStruct((LANE_TILES, 4, 2, SUBLANES, TILE_STATES), F32)),
        compiler_params=pltpu.CompilerParams(
            dimension_semantics=("arbitrary",), vmem_limit_bytes=VMEM_LIMIT),
        name="s5_prep",
    )(a3, bt, ct)


def _mem_kv_kernel(mem_ref, g_ref, wk_ref, wv_ref, k_ref, v_ref):
    m = _rms(mem_ref[...], g_ref[...]).astype(BF16)
    k_ref[...] = _dot(m, wk_ref[...]).astype(BF16)
    v_ref[...] = _dot(m, wv_ref[...]).astype(BF16)


def _mem_kv(mem, g, wk, wv):
    B, M, D = mem.shape
    return pl.pallas_call(
        _mem_kv_kernel,
        grid=(B,),
        in_specs=[pl.BlockSpec((None, M, D), lambda b: (b, 0, 0)),
                  pl.BlockSpec((1, D), lambda b: (0, 0)),
                  pl.BlockSpec((D, D), lambda b: (0, 0)),
                  pl.BlockSpec((D, D), lambda b: (0, 0))],
        out_specs=(pl.BlockSpec((None, M, D), lambda b: (b, 0, 0)),
                   pl.BlockSpec((None, M, D), lambda b: (b, 0, 0))),
        out_shape=(jax.ShapeDtypeStruct((B, M, D), BF16),
                   jax.ShapeDtypeStruct((B, M, D), BF16)),
        compiler_params=pltpu.CompilerParams(
            dimension_semantics=("arbitrary",), vmem_limit_bytes=VMEM_LIMIT),
        name="mem_kv",
    )(mem, g, wk, wv)


def _in_proj_kernel(x_ref, pos_ref, inv_ref, g_ref, w_ref,
                    q_ref, k_ref, v_ref, gr_ref, u_ref, gs_ref):
    h = _rms(x_ref[...], g_ref[...]).astype(BF16)
    ang = pos_ref[...].astype(F32) * inv_ref[...]
    cos = jnp.cos(ang)
    sin = jnp.sin(ang)
    lane = lax.broadcasted_iota(jnp.int32, (1, LANES), 1)
    first_half = (lane % RET_DK) < (RET_DK // 2)
    sin_signed = jnp.where(first_half, -sin, sin)

    def rope(p):
        partner = jnp.where(first_half,
                            pltpu.roll(p, LANES - RET_DK // 2, 1),
                            pltpu.roll(p, RET_DK // 2, 1))
        return p * cos + partner * sin_signed

    pq = _dot(h, w_ref[:, COL_Q:COL_K])
    pk = _dot(h, w_ref[:, COL_K:COL_V])
    for t in range(RET_QK_WIDTH // LANES):
        sl = slice(t * LANES, (t + 1) * LANES)
        q_ref[:, sl] = rope(pq[:, sl]).astype(BF16)
        k_ref[:, sl] = (rope(pk[:, sl]) * (RET_DK ** -0.5)).astype(BF16)
    v_ref[...] = _dot(h, w_ref[:, COL_V:COL_GR]).astype(BF16)
    gr_ref[...] = _silu(_dot(h, w_ref[:, COL_GR:COL_U])).astype(BF16)
    pu = _dot(h, w_ref[:, COL_U:COL_GS]).astype(BF16)
    for j in range(LANE_TILES):
        u_ref[j] = pu[:, j * LANES:(j + 1) * LANES]
    gs_ref[...] = _silu(_dot(h, w_ref[:, COL_GS:IN_COLS])).astype(BF16)


def _in_proj(x, pos3, inv, g, w):
    B, L, D = x.shape
    tm = TM_PROJ
    row = lambda width: pl.BlockSpec((None, tm, width), lambda b, t: (b, t, 0))
    const = lambda shape: pl.BlockSpec(shape, lambda b, t: (0,) * len(shape))
    out = lambda width: jax.ShapeDtypeStruct((B, L, width), BF16)
    return pl.pallas_call(
        _in_proj_kernel,
        grid=(B, L // tm),
        in_specs=[row(D), row(1), const((1, LANES)), const((1, D)), const((D, IN_COLS))],
        out_specs=(row(RET_QK_WIDTH), row(RET_QK_WIDTH), row(RET_V_WIDTH), row(RET_V_WIDTH),
                   pl.BlockSpec((None, LANE_TILES, tm, LANES), lambda b, t: (b, 0, t, 0)),
                   row(S5_WIDTH)),
        out_shape=(out(RET_QK_WIDTH), out(RET_QK_WIDTH), out(RET_V_WIDTH), out(RET_V_WIDTH),
                   jax.ShapeDtypeStruct((B, LANE_TILES, L, LANES), BF16),
                   out(S5_WIDTH)),
        compiler_params=pltpu.CompilerParams(
            dimension_semantics=("arbitrary", "arbitrary"), vmem_limit_bytes=VMEM_LIMIT),
        name="in_proj",
    )(x, pos3, inv, g, w)


def _retention_kernel(q_ref, k_ref, v_ref, g_ref, gn_ref, o_ref,
                      state_ref, decay_ref, qw_ref, kw_ref):
    b = pl.program_id(0)
    t = pl.program_id(1)
    pairs = RET_HEADS // 2

    @pl.when((b == 0) & (t == 0))
    def _constants():
        i = lax.broadcasted_iota(jnp.int32, (CHUNK, CHUNK), 0).astype(F32)
        j = lax.broadcasted_iota(jnp.int32, (CHUNK, CHUNK), 1).astype(F32)
        diff = i - j
        for h in range(RET_HEADS):
            decay_ref[h] = jnp.where(diff >= 0.0, jnp.exp(LOG_G[h] * jnp.maximum(diff, 0.0)), 0.0)
            qw_ref[h] = jnp.exp(LOG_G[h] * (i + 1.0))
        for p in range(pairs):
            lg = jnp.where(j < float(RET_DK), LOG_G[2 * p], LOG_G[2 * p + 1])
            kw_ref[p] = jnp.exp(lg * (CHUNK - 1.0 - i))

    @pl.when(t == 0)
    def _reset():
        state_ref[...] = jnp.zeros_like(state_ref)

    lane = lax.broadcasted_iota(jnp.int32, (1, LANES), 1)
    srow = lax.broadcasted_iota(jnp.int32, (LANES, 2 * RET_DV), 0)
    scol = lax.broadcasted_iota(jnp.int32, (LANES, 2 * RET_DV), 1)
    own_block = (srow < RET_DK) == (scol < RET_DV)
    scol1 = lax.broadcasted_iota(jnp.int32, (1, 2 * RET_DV), 1)

    for c in range(TM_RET // CHUNK):
        rows = slice(c * CHUNK, (c + 1) * CHUNK)
        for p in range(pairs):
            qt = q_ref[rows, p * LANES:(p + 1) * LANES]
            kt = k_ref[rows, p * LANES:(p + 1) * LANES]
            vp = v_ref[rows, 2 * p * RET_DV:2 * (p + 1) * RET_DV]
            state = state_ref[p]
            cross = _dot(qt, state.astype(BF16))
            kw = (kt.astype(F32) * kw_ref[p]).astype(BF16)
            upd = lax.dot_general(kw, vp, (((0,), (0,)), ((), ())), preferred_element_type=F32)
            chunk_decay = jnp.where(scol1 < RET_DV,
                                    math.exp(LOG_G[2 * p] * CHUNK), math.exp(LOG_G[2 * p + 1] * CHUNK))
            state_ref[p] = state * chunk_decay + jnp.where(own_block, upd, 0.0)
            for e in range(2):
                h = 2 * p + e
                cols = slice(h * RET_DV, (h + 1) * RET_DV)
                qm = jnp.where((lane < RET_DK) == (e == 0), qt, jnp.zeros_like(qt))
                s = lax.dot_general(qm, kt, (((1,), (1,)), ((), ())), preferred_element_type=F32)
                pm = (s * decay_ref[h]).astype(BF16)
                inner = _dot(pm, vp[:, e * RET_DV:(e + 1) * RET_DV])
                o = inner + qw_ref[h] * cross[:, e * RET_DV:(e + 1) * RET_DV]
                mu = jnp.mean(o, axis=-1, keepdims=True)
                oc = o - mu
                var = jnp.mean(oc * oc, axis=-1, keepdims=True)
                on = oc * lax.rsqrt(var + EPS) * gn_ref[:, cols]
                o_ref[rows, cols] = (on * g_ref[rows, cols].astype(F32)).astype(BF16)


def _retention(q, k, v, g, gn):
    B, L, _ = q.shape
    tm = TM_RET
    row = lambda width: pl.BlockSpec((None, tm, width), lambda b, t: (b, t, 0))
    return pl.pallas_call(
        _retention_kernel,
        grid=(B, L // tm),
        in_specs=[row(RET_QK_WIDTH), row(RET_QK_WIDTH), row(RET_V_WIDTH), row(RET_V_WIDTH),
                  pl.BlockSpec((1, RET_V_WIDTH), lambda b, t: (0, 0))],
        out_specs=row(RET_V_WIDTH),
        out_shape=jax.ShapeDtypeStruct((B, L, RET_V_WIDTH), BF16),
        scratch_shapes=[pltpu.VMEM((RET_HEADS // 2, LANES, 2 * RET_DV), F32),
                        pltpu.VMEM((RET_HEADS, CHUNK, CHUNK), F32),
                        pltpu.VMEM((RET_HEADS, CHUNK, RET_DV), F32),
                        pltpu.VMEM((RET_HEADS // 2, CHUNK, LANES), F32)],
        compiler_params=pltpu.CompilerParams(
            dimension_semantics=("arbitrary", "arbitrary"), vmem_limit_bytes=VMEM_LIMIT),
        name="retention",
    )(q, k, v, g, gn)


def _s5_scan_kernel(u_ref, wb_ref, wck_ref, consts_ref, d_ref, y_ref, uf_ref, zs_ref, ys_ref):
    t = pl.program_id(2)
    s = S5_TAPS
    tm = u_ref.shape[0]
    nblk = tm // s
    half = TILE_STATES
    top = SUBLANES

    @pl.when(t == 0)
    def _reset():
        zs_ref[0:top, :] = jnp.zeros((top, 2 * half), F32)

    uf_ref[...] = u_ref[...].astype(F32)
    lhs = jnp.concatenate([uf_ref[pl.ds(k, nblk, stride=s), :] for k in range(s)], axis=1).astype(BF16)
    zs_ref[top:top + nblk, :] = _dot(lhs, wb_ref[...])

    def group(r, carry):
        cr, ci = carry
        rows = pl.ds(pl.multiple_of(top + r * SUBLANES, SUBLANES), SUBLANES)
        xr = zs_ref[rows, 0:half]
        xi = zs_ref[rows, half:2 * half]
        for idx, d in enumerate((1, 2, 4)):
            mr = consts_ref[idx, 0]
            mi = consts_ref[idx, 1]
            sr = pltpu.roll(xr, d, 0)
            si = pltpu.roll(xi, d, 0)
            xr, xi = xr + mr * sr - mi * si, xi + mr * si + mi * sr
        mr = consts_ref[3, 0]
        mi = consts_ref[3, 1]
        xr, xi = xr + mr * cr - mi * ci, xi + mr * ci + mi * cr
        zs_ref[rows, 0:half] = xr
        zs_ref[rows, half:2 * half] = xi
        last = SUBLANES - 1
        return (jnp.broadcast_to(xr[last:, :], (SUBLANES, half)),
                jnp.broadcast_to(xi[last:, :], (SUBLANES, half)))

    carry0 = (jnp.broadcast_to(zs_ref[top - 1:top, 0:half], (SUBLANES, half)),
              jnp.broadcast_to(zs_ref[top - 1:top, half:2 * half], (SUBLANES, half)))
    lax.fori_loop(0, nblk // SUBLANES, group, carry0)

    prev = zs_ref[pl.ds(top - 1, nblk), :].astype(BF16)
    zs_ref[top - 1:top, :] = zs_ref[top + nblk - 1:top + nblk, :]
    yall = _dot(jnp.concatenate([prev, lhs], axis=1), wck_ref[...])
    for i in range(s):
        ys_ref[pl.ds(i, nblk, stride=s), :] = yall[:, i * LANES:(i + 1) * LANES]
    y = ys_ref[...] + d_ref[...] * uf_ref[...]
    y_ref[...] = jax.nn.gelu(y).astype(BF16)


def _s5_scan(u, wb, wck, consts, d):
    B, _, L, _ = u.shape
    s = S5_TAPS
    tm = TM_S5
    tok = pl.BlockSpec((None, None, tm, LANES), lambda j, b, t: (b, j, t, 0))
    tile = lambda *shape: pl.BlockSpec((None,) + shape, lambda j, b, t: (j,) + (0,) * len(shape))
    return pl.pallas_call(
        _s5_scan_kernel,
        grid=(LANE_TILES, B, L // tm),
        in_specs=[tok, tile(s * LANES, 2 * TILE_STATES), tile(2 * TILE_STATES + s * LANES, s * LANES),
                  tile(4, 2, SUBLANES, TILE_STATES), tile(1, LANES)],
        out_specs=tok,
        out_shape=jax.ShapeDtypeStruct((B, LANE_TILES, L, LANES), BF16),
        scratch_shapes=[pltpu.VMEM((tm, LANES), F32),
                        pltpu.VMEM((SUBLANES + tm // s, 2 * TILE_STATES), F32),
                        pltpu.VMEM((tm, LANES), F32)],
        compiler_params=pltpu.CompilerParams(
            dimension_semantics=("arbitrary", "arbitrary", "arbitrary"), vmem_limit_bytes=VMEM_LIMIT),
        name="s5_scan",
    )(u, wb, wck, consts, d)


def _out_xattn_kernel(x_ref, ret_ref, y_ref, gs_ref, gw_ref, gb_ref, wout_ref, g2_ref, wq_ref,
                      ka_ref, va_ref, wo_ref, gf_ref, o_ref):
    y = jnp.concatenate([y_ref[j] for j in range(LANE_TILES)], axis=1)
    z = _dot(y, gw_ref[...]) + gb_ref[...]
    ssm = (y.astype(F32) * _sigmoid(z) * gs_ref[...].astype(F32)).astype(BF16)
    x1 = (x_ref[...] + _dot(ret_ref[...], wout_ref[0:RET_V_WIDTH, :])
          + _dot(ssm, wout_ref[RET_V_WIDTH:RET_V_WIDTH + S5_WIDTH, :]))
    h2 = _rms(x1, g2_ref[...]).astype(BF16)
    qa = (_dot(h2, wq_ref[...]) * (XA_DH ** -0.5)).astype(BF16)
    heads = []
    for h in range(XA_HEADS):
        cols = slice(h * XA_DH, (h + 1) * XA_DH)
        s = lax.dot_general(qa[:, cols], ka_ref[:, cols], (((1,), (1,)), ((), ())),
                            preferred_element_type=F32)
        e = jnp.exp(s - jnp.max(s, axis=-1, keepdims=True))
        l = jnp.sum(e, axis=-1, keepdims=True)
        heads.append((_dot(e.astype(BF16), va_ref[:, cols]) * (1.0 / l)).astype(BF16))
    o = jnp.concatenate(heads, axis=-1)
    x2 = x1 + _dot(o, wo_ref[...])
    o_ref[...] = _rms(x2, gf_ref[...])


def _out_xattn(x, ret, y, gs, gw, gb, wout, g2, wq, ka, va, wo, gf):
    B, L, D = x.shape
    tm = TM_OUT
    row = pl.BlockSpec((None, tm, D), lambda b, t: (b, t, 0))
    const = lambda shape: pl.BlockSpec(shape, lambda b, t: (0,) * len(shape))
    mem = pl.BlockSpec((None, MEM_LEN, D), lambda b, t: (b, 0, 0))
    ytile = pl.BlockSpec((None, LANE_TILES, tm, LANES), lambda b, t: (b, 0, t, 0))
    return pl.pallas_call(
        _out_xattn_kernel,
        grid=(B, L // tm),
        in_specs=[row, row, ytile, row, const((D, D)), const((1, D)), const(wout.shape), const((1, D)),
                  const((D, D)), mem, mem, const((D, D)), const((1, D))],
        out_specs=row,
        out_shape=jax.ShapeDtypeStruct((B, L, D), F32),
        compiler_params=pltpu.CompilerParams(
            dimension_semantics=("arbitrary", "arbitrary"), vmem_limit_bytes=VMEM_LIMIT),
        name="out_xattn",
    )(x, ret, y, gs, gw, gb, wout, g2, wq, ka, va, wo, gf)


def _rope_inv_lanes():
    half = RET_DK // 2
    inv = ROPE_BASE ** (-np.arange(half, dtype=np.float64) / half)
    return jnp.asarray(np.tile(inv, LANES // half).reshape(1, LANES), F32)


def kernel(x, mem, positions, norm1_g, w_in, ret_gn_g, s5_a_re, s5_a_im, s5_log_dt, s5_b_re, s5_b_im, s5_c_re, s5_c_im, s5_d, s5_glu_w, s5_glu_b, w_out, norm2_g, norm_mem_g, xa_wq, xa_wk, xa_wv, xa_wo, norm_f_g):
    B, L, D = x.shape
    l = 0
    bf = lambda w: w.astype(BF16)
    rowvec = lambda v: v.reshape(1, -1)

    wb, wck, consts = _s5_prep(s5_a_re[l], s5_a_im[l], s5_log_dt[l], s5_b_re[l], s5_b_im[l],
                               s5_c_re[l], s5_c_im[l])
    ka, va = _mem_kv(mem, rowvec(norm_mem_g[l]), bf(xa_wk[l]), bf(xa_wv[l]))
    q, k, v, g_ret, u, g_s5 = _in_proj(x, positions.reshape(B, L, 1), _rope_inv_lanes(),
                                       rowvec(norm1_g[l]), bf(w_in[l]))
    ret = _retention(q, k, v, g_ret, rowvec(ret_gn_g[l]))
    y = _s5_scan(u, wb, wck, consts, s5_d[l].reshape(LANE_TILES, 1, LANES))
    return _out_xattn(x, ret, y, g_s5, bf(s5_glu_w[l]), rowvec(s5_glu_b[l]), bf(w_out[l]),
                      rowvec(norm2_g[l]), bf(xa_wq[l]), ka, va, bf(xa_wo[l]), rowvec(norm_f_g))
```

```python
import functools
import math

import numpy as np
import jax
import jax.numpy as jnp
from jax import lax
from jax.experimental import pallas as pl
from jax.experimental.pallas import tpu as pltpu

F32 = jnp.float32
BF16 = jnp.bfloat16

D_MODEL = 1024
MEM_LEN = 256
EPS = 1e-6
ROPE_BASE = 10000.0

RET_HEADS = 8
RET_QK_WIDTH = 512
RET_V_WIDTH = 1024
RET_DK = 64
RET_DV = 128
CHUNK = 128

S5_WIDTH = 1024
S5_GROUP = 16
S5_GROUPS = 64
S5_STATE = 64
S5_NSTATE = S5_GROUPS * S5_STATE

IN_COLS = 5120
COL_Q, COL_K, COL_V, COL_GR, COL_U, COL_GS = 0, 512, 1024, 2048, 3072, 4096

XA_HEADS = 4
XA_DH = 256

LANES = 128
SUBLANES = 8
VMEM_LIMIT = 48 * 1024 * 1024

LANE_TILES = S5_WIDTH // LANES
TILE_GROUPS = LANES // S5_GROUP
TILE_STATES = S5_NSTATE // LANE_TILES
S5_TAPS = 4

LOG_G = tuple(math.log1p(-(2.0 ** (-5.0 - h))) for h in range(RET_HEADS))

TM_PROJ = 256
TM_S5 = 2048
TM_OUT = 256


def _rms(x, g):
    ms = jnp.mean(x * x, axis=-1, keepdims=True)
    return x * lax.rsqrt(ms + EPS) * g


def _sigmoid(z):
    return 1.0 / (1.0 + jnp.exp(-z))


def _silu(g):
    return g * _sigmoid(g)


def _dot(a, b):
    return jnp.dot(a, b, preferred_element_type=F32)


def _cmul(ar, ai, br, bi):
    return ar * br - ai * bi, ar * bi + ai * br


def _s5_discretise(ar, ai, ldt):
    dt = jnp.exp(ldt)
    mag = jnp.exp(ar * dt)
    p_re = mag * jnp.cos(ai * dt)
    p_im = mag * jnp.sin(ai * dt)
    den = ar * ar + ai * ai
    nr, ni = p_re - 1.0, p_im
    f_re = (nr * ar + ni * ai) / den
    f_im = (ni * ar - nr * ai) / den
    return p_re, p_im, f_re, f_im


def _powers(p_re, p_im, n):
    pw = [(jnp.ones_like(p_re), jnp.zeros_like(p_im))]
    for _ in range(n):
        pw.append(_cmul(pw[-1][0], pw[-1][1], p_re, p_im))
    return pw


def _block_diag(x):
    tiled = jnp.concatenate([x] * TILE_GROUPS, axis=1)
    r = lax.broadcasted_iota(jnp.int32, tiled.shape, 0)
    c = lax.broadcasted_iota(jnp.int32, tiled.shape, 1)
    return jnp.where(r // S5_GROUP == c // S5_STATE, tiled, 0.0)


def _s5_prep_kernel(a_ref, b_ref, c_ref, wb_ref, wck_ref, consts_ref):
    s = S5_TAPS
    half = TILE_STATES
    p_re, p_im, f_re, f_im = _s5_discretise(a_ref[0:1, :], a_ref[1:2, :], a_ref[2:3, :])
    pw = _powers(p_re, p_im, s)
    bbr, bbi = _cmul(f_re, f_im, _block_diag(b_ref[0]), _block_diag(b_ref[1]))
    for k in range(s):
        wr, wi = _cmul(bbr, bbi, *pw[s - 1 - k])
        wb_ref[k * LANES:(k + 1) * LANES, 0:half] = wr.astype(BF16)
        wb_ref[k * LANES:(k + 1) * LANES, half:2 * half] = wi.astype(BF16)

    shape = (SUBLANES, half)
    qp = _powers(pw[s][0], pw[s][1], SUBLANES)
    row = lax.broadcasted_iota(jnp.int32, shape, 0)
    for idx, d in enumerate((1, 2, 4)):
        consts_ref[idx, 0] = jnp.where(row >= d, qp[d][0], 0.0)
        consts_ref[idx, 1] = jnp.where(row >= d, qp[d][1], 0.0)
    cr = jnp.zeros(shape, F32)
    ci = jnp.zeros(shape, F32)
    for r in range(SUBLANES):
        cr = jnp.where(row == r, qp[r + 1][0], cr)
        ci = jnp.where(row == r, qp[r + 1][1], ci)
    consts_ref[3, 0] = cr
    consts_ref[3, 1] = ci

    ct_re = _block_diag(c_ref[0])
    ct_im = _block_diag(c_ref[1])
    bb = jnp.concatenate([bbr, bbi], axis=1)
    taps = []
    for d in range(s + 1):
        er, ei = _cmul(ct_re, ct_im, *pw[d])
        cwt = jnp.concatenate([er, -ei], axis=1)
        if d >= 1:
            wck_ref[0:2 * half, (d - 1) * LANES:d * LANES] = cwt.T.astype(BF16)
        if d < s:
            taps.append(lax.dot_general(bb, cwt, (((1,), (1,)), ((), ())), preferred_element_type=F32,
                                        precision=lax.Precision.HIGHEST))
    zero = jnp.zeros((LANES, LANES), BF16)
    for m in range(s):
        for i in range(s):
            blk = taps[i - m].astype(BF16) if m <= i else zero
            wck_ref[2 * half + m * LANES:2 * half + (m + 1) * LANES, i * LANES:(i + 1) * LANES] = blk


def _s5_prep(a_re, a_im, log_dt, b_re, b_im, c_re, c_im):
    s = S5_TAPS
    ldt = jnp.broadcast_to(log_dt[:, None], (S5_GROUPS, S5_STATE))
    a3 = jnp.stack([a_re, a_im, ldt]).reshape(3, S5_NSTATE)
    bt = jnp.stack([b_re, b_im]).transpose(0, 1, 3, 2).reshape(2, S5_WIDTH, S5_STATE)
    ct = jnp.stack([c_re, c_im]).reshape(2, S5_WIDTH, S5_STATE)
    tile = lambda *shape: pl.BlockSpec((None,) + shape, lambda j: (j,) + (0,) * len(shape))
    return pl.pallas_call(
        _s5_prep_kernel,
        grid=(LANE_TILES,),
        in_specs=[pl.BlockSpec((3, TILE_STATES), lambda j: (0, j)),
                  pl.BlockSpec((2, LANES, S5_STATE), lambda j: (0, j, 0)),
                  pl.BlockSpec((2, LANES, S5_STATE), lambda j: (0, j, 0))],
        out_specs=(tile(s * LANES, 2 * TILE_STATES),
                   tile(2 * TILE_STATES + s * LANES, s * LANES),
                   tile(4, 2, SUBLANES, TILE_STATES)),
        out_shape=(jax.ShapeDtypeStruct((LANE_TILES, s * LANES, 2 * TILE_STATES), BF16),
                   jax.ShapeDtypeStruct((LANE_TILES, 2 * TILE_STATES + s * LANES, s * LANES), BF16),
                   jax.ShapeDtypeStruct((LANE_TILES, 4, 2, SUBLANES, TILE_STATES), F32)),
        compiler_params=pltpu.CompilerParams(
            dimension_semantics=("arbitrary",), vmem_limit_bytes=VMEM_LIMIT),
        name="s5_prep",
    )(a3, bt, ct)


def _mem_kv_kernel(mem_ref, g_ref, wk_ref, wv_ref, k_ref, v_ref):
    m = _rms(mem_ref[...], g_ref[...]).astype(BF16)
    k_ref[...] = _dot(m, wk_ref[...]).astype(BF16)
    v_ref[...] = _dot(m, wv_ref[...]).astype(BF16)


def _mem_kv(mem, g, wk, wv):
    B, M, D = mem.shape
    return pl.pallas_call(
        _mem_kv_kernel,
        grid=(B,),
        in_specs=[pl.BlockSpec((None, M, D), lambda b: (b, 0, 0)),
                  pl.BlockSpec((1, D), lambda b: (0, 0)),
                  pl.BlockSpec((D, D), lambda b: (0, 0)),
                  pl.BlockSpec((D, D), lambda b: (0, 0))],
        out_specs=(pl.BlockSpec((None, M, D), lambda b: (b, 0, 0)),
                   pl.BlockSpec((None, M, D), lambda b: (b, 0, 0))),
        out_shape=(jax.ShapeDtypeStruct((B, M, D), BF16),
                   jax.ShapeDtypeStruct((B, M, D), BF16)),
        compiler_params=pltpu.CompilerParams(
            dimension_semantics=("arbitrary",), vmem_limit_bytes=VMEM_LIMIT),
        name="mem_kv",
    )(mem, g, wk, wv)


def _proj_ret_kernel(x_ref, pos_ref, inv_ref, g_ref, w_ref, gn_ref, u_ref, gs_ref, ret_ref,
                     cur_ref, prev_ref, state_ref, decay_ref, qw_ref, kw_ref, *, tiles_per_seq):
    t = pl.program_id(0)

    @pl.when(t == 0)
    def _init():
        _retention_constants(decay_ref, qw_ref, kw_ref)
        cur_ref[...] = jnp.zeros_like(cur_ref)

    @pl.when((t == 0) | (t % tiles_per_seq == 1))
    def _reset():
        state_ref[...] = jnp.zeros_like(state_ref)

    prev_ref[...] = cur_ref[...]

    h = _rms(x_ref[...], g_ref[...]).astype(BF16)
    ang = pos_ref[...].astype(F32) * inv_ref[...]
    cos = jnp.cos(ang)
    sin = jnp.sin(ang)
    lane = lax.broadcasted_iota(jnp.int32, (1, LANES), 1)
    first_half = (lane % RET_DK) < (RET_DK // 2)
    sin_signed = jnp.where(first_half, -sin, sin)

    def rope(p):
        partner = jnp.where(first_half,
                            pltpu.roll(p, LANES - RET_DK // 2, 1),
                            pltpu.roll(p, RET_DK // 2, 1))
        return p * cos + partner * sin_signed

    def proj_q():
        pq = _dot(h, w_ref[:, COL_Q:COL_K])
        for i in range(RET_QK_WIDTH // LANES):
            sl = slice(i * LANES, (i + 1) * LANES)
            cur_ref[:, COL_Q + i * LANES:COL_Q + (i + 1) * LANES] = rope(pq[:, sl]).astype(BF16)

    def proj_k():
        pk = _dot(h, w_ref[:, COL_K:COL_V])
        for i in range(RET_QK_WIDTH // LANES):
            sl = slice(i * LANES, (i + 1) * LANES)
            cur_ref[:, COL_K + i * LANES:COL_K + (i + 1) * LANES] = (
                rope(pk[:, sl]) * (RET_DK ** -0.5)).astype(BF16)

    def proj_v():
        cur_ref[:, COL_V:COL_GR] = _dot(h, w_ref[:, COL_V:COL_GR]).astype(BF16)

    def proj_gate():
        cur_ref[:, COL_GR:COL_U] = _silu(_dot(h, w_ref[:, COL_GR:COL_U])).astype(BF16)

    def proj_u():
        pu = _dot(h, w_ref[:, COL_U:COL_GS]).astype(BF16)
        for j in range(LANE_TILES):
            u_ref[j] = pu[:, j * LANES:(j + 1) * LANES]

    def proj_gs():
        gs_ref[...] = _silu(_dot(h, w_ref[:, COL_GS:IN_COLS])).astype(BF16)

    ret_items = _retention_items(prev_ref, gn_ref, ret_ref, state_ref, decay_ref, qw_ref, kw_ref)
    proj_items = [proj_q, proj_k, proj_v, proj_gate, proj_u, proj_gs]
    per = -(-len(ret_items) // len(proj_items))
    for i, item in enumerate(proj_items):
        for r in ret_items[i * per:(i + 1) * per]:
            r()
        item()


def _proj_ret(x, pos3, inv, g, w, gn):
    B, L, D = x.shape
    tm = TM_PROJ
    nt = L // tm
    n = B * nt
    cur = lambda t: jnp.minimum(t, n - 1)
    lag = lambda t: jnp.maximum(t - 1, 0)
    row = lambda width, tile: pl.BlockSpec((None, tm, width), lambda t: (tile(t) // nt, tile(t) % nt, 0))
    const = lambda shape: pl.BlockSpec(shape, lambda t: (0,) * len(shape))
    return pl.pallas_call(
        functools.partial(_proj_ret_kernel, tiles_per_seq=nt),
        grid=(n + 1,),
        in_specs=[row(D, cur), row(1, cur), const((1, LANES)), const((1, D)), const((D, IN_COLS)),
                  const((1, RET_V_WIDTH))],
        out_specs=(pl.BlockSpec((None, LANE_TILES, tm, LANES), lambda t: (cur(t) // nt, 0, cur(t) % nt, 0)),
                   row(S5_WIDTH, cur), row(RET_V_WIDTH, lag)),
        out_shape=(jax.ShapeDtypeStruct((B, LANE_TILES, L, LANES), BF16),
                   jax.ShapeDtypeStruct((B, L, S5_WIDTH), BF16),
                   jax.ShapeDtypeStruct((B, L, RET_V_WIDTH), BF16)),
        scratch_shapes=[pltpu.VMEM((tm, COL_U), BF16),
                        pltpu.VMEM((tm, COL_U), BF16),
                        pltpu.VMEM((RET_HEADS // 2, LANES, 2 * RET_DV), F32),
                        pltpu.VMEM((RET_HEADS, CHUNK, CHUNK), F32),
                        pltpu.VMEM((RET_HEADS, CHUNK, RET_DV), F32),
                        pltpu.VMEM((RET_HEADS // 2, CHUNK, LANES), F32)],
        compiler_params=pltpu.CompilerParams(
            dimension_semantics=("arbitrary",), vmem_limit_bytes=VMEM_LIMIT),
        name="proj_ret",
    )(x, pos3, inv, g, w, gn)


def _retention_constants(decay_ref, qw_ref, kw_ref):
    i = lax.broadcasted_iota(jnp.int32, (CHUNK, CHUNK), 0).astype(F32)
    j = lax.broadcasted_iota(jnp.int32, (CHUNK, CHUNK), 1).astype(F32)
    diff = i - j
    for h in range(RET_HEADS):
        decay_ref[h] = jnp.where(diff >= 0.0, jnp.exp(LOG_G[h] * jnp.maximum(diff, 0.0)), 0.0)
        qw_ref[h] = jnp.exp(LOG_G[h] * (i + 1.0))
    for p in range(RET_HEADS // 2):
        lg = jnp.where(j < float(RET_DK), LOG_G[2 * p], LOG_G[2 * p + 1])
        kw_ref[p] = jnp.exp(lg * (CHUNK - 1.0 - i))


def _retention_items(src_ref, gn_ref, o_ref, state_ref, decay_ref, qw_ref, kw_ref):
    pairs = RET_HEADS // 2
    lane = lax.broadcasted_iota(jnp.int32, (1, LANES), 1)
    srow = lax.broadcasted_iota(jnp.int32, (LANES, 2 * RET_DV), 0)
    scol = lax.broadcasted_iota(jnp.int32, (LANES, 2 * RET_DV), 1)
    own_block = (srow < RET_DK) == (scol < RET_DV)
    scol1 = lax.broadcasted_iota(jnp.int32, (1, 2 * RET_DV), 1)

    def pair_step(c, p):
        rows = slice(c * CHUNK, (c + 1) * CHUNK)
        if True:
            qt = src_ref[rows, COL_Q + p * LANES:COL_Q + (p + 1) * LANES]
            kt = src_ref[rows, COL_K + p * LANES:COL_K + (p + 1) * LANES]
            vp = src_ref[rows, COL_V + 2 * p * RET_DV:COL_V + 2 * (p + 1) * RET_DV]
            state = state_ref[p]
            cross = _dot(qt, state.astype(BF16))
            kw = (kt.astype(F32) * kw_ref[p]).astype(BF16)
            upd = lax.dot_general(kw, vp, (((0,), (0,)), ((), ())), preferred_element_type=F32)
            chunk_decay = jnp.where(scol1 < RET_DV,
                                    math.exp(LOG_G[2 * p] * CHUNK), math.exp(LOG_G[2 * p + 1] * CHUNK))
            state_ref[p] = state * chunk_decay + jnp.where(own_block, upd, 0.0)
            for e in range(2):
                h = 2 * p + e
                cols = slice(h * RET_DV, (h + 1) * RET_DV)
                qm = jnp.where((lane < RET_DK) == (e == 0), qt, jnp.zeros_like(qt))
                s = lax.dot_general(qm, kt, (((1,), (1,)), ((), ())), preferred_element_type=F32)
                pm = (s * decay_ref[h]).astype(BF16)
                inner = _dot(pm, vp[:, e * RET_DV:(e + 1) * RET_DV])
                o = inner + qw_ref[h] * cross[:, e * RET_DV:(e + 1) * RET_DV]
                mu = jnp.mean(o, axis=-1, keepdims=True)
                oc = o - mu
                var = jnp.mean(oc * oc, axis=-1, keepdims=True)
                on = oc * lax.rsqrt(var + EPS) * gn_ref[:, cols]
                gate = src_ref[rows, COL_GR + h * RET_DV:COL_GR + (h + 1) * RET_DV].astype(F32)
                o_ref[rows, cols] = (on * gate).astype(BF16)

    return [functools.partial(pair_step, c, p)
            for c in range(src_ref.shape[0] // CHUNK) for p in range(pairs)]


def _s5_scan_kernel(u_ref, wb_ref, wck_ref, consts_ref, d_ref, y_ref, uf_ref, zs_ref, ys_ref):
    t = pl.program_id(2)
    s = S5_TAPS
    tm = u_ref.shape[0]
    nblk = tm // s
    half = TILE_STATES
    top = SUBLANES

    @pl.when(t == 0)
    def _reset():
        zs_ref[0:top, :] = jnp.zeros((top, 2 * half), F32)

    uf_ref[...] = u_ref[...].astype(F32)
    lhs = jnp.concatenate([uf_ref[pl.ds(k, nblk, stride=s), :] for k in range(s)], axis=1).astype(BF16)
    zs_ref[top:top + nblk, :] = _dot(lhs, wb_ref[...])

    def group(r, carry):
        cr, ci = carry
        rows = pl.ds(pl.multiple_of(top + r * SUBLANES, SUBLANES), SUBLANES)
        xr = zs_ref[rows, 0:half]
        xi = zs_ref[rows, half:2 * half]
        for idx, d in enumerate((1, 2, 4)):
            mr = consts_ref[idx, 0]
            mi = consts_ref[idx, 1]
            sr = pltpu.roll(xr, d, 0)
            si = pltpu.roll(xi, d, 0)
            xr, xi = xr + mr * sr - mi * si, xi + mr * si + mi * sr
        mr = consts_ref[3, 0]
        mi = consts_ref[3, 1]
        xr, xi = xr + mr * cr - mi * ci, xi + mr * ci + mi * cr
        zs_ref[rows, 0:half] = xr
        zs_ref[rows, half:2 * half] = xi
        last = SUBLANES - 1
        return (jnp.broadcast_to(xr[last:, :], (SUBLANES, half)),
                jnp.broadcast_to(xi[last:, :], (SUBLANES, half)))

    carry0 = (jnp.broadcast_to(zs_ref[top - 1:top, 0:half], (SUBLANES, half)),
              jnp.broadcast_to(zs_ref[top - 1:top, half:2 * half], (SUBLANES, half)))
    lax.fori_loop(0, nblk // SUBLANES, group, carry0)

    prev = zs_ref[pl.ds(top - 1, nblk), :].astype(BF16)
    zs_ref[top - 1:top, :] = zs_ref[top + nblk - 1:top + nblk, :]
    yall = _dot(jnp.concatenate([prev, lhs], axis=1), wck_ref[...])
    for i in range(s):
        ys_ref[pl.ds(i, nblk, stride=s), :] = yall[:, i * LANES:(i + 1) * LANES]
    y = ys_ref[...] + d_ref[...] * uf_ref[...]
    y_ref[...] = jax.nn.gelu(y).astype(BF16)


def _s5_scan(u, wb, wck, consts, d):
    B, _, L, _ = u.shape
    s = S5_TAPS
    tm = TM_S5
    tok = pl.BlockSpec((None, None, tm, LANES), lambda j, b, t: (b, j, t, 0))
    tile = lambda *shape: pl.BlockSpec((None,) + shape, lambda j, b, t: (j,) + (0,) * len(shape))
    return pl.pallas_call(
        _s5_scan_kernel,
        grid=(LANE_TILES, B, L // tm),
        in_specs=[tok, tile(s * LANES, 2 * TILE_STATES), tile(2 * TILE_STATES + s * LANES, s * LANES),
                  tile(4, 2, SUBLANES, TILE_STATES), tile(1, LANES)],
        out_specs=tok,
        out_shape=jax.ShapeDtypeStruct((B, LANE_TILES, L, LANES), BF16),
        scratch_shapes=[pltpu.VMEM((tm, LANES), F32),
                        pltpu.VMEM((SUBLANES + tm // s, 2 * TILE_STATES), F32),
                        pltpu.VMEM((tm, LANES), F32)],
        compiler_params=pltpu.CompilerParams(
            dimension_semantics=("arbitrary", "arbitrary", "arbitrary"), vmem_limit_bytes=VMEM_LIMIT),
        name="s5_scan",
    )(u, wb, wck, consts, d)


def _out_xattn_kernel(x_ref, ret_ref, y_ref, gs_ref, gw_ref, gb_ref, wout_ref, g2_ref, wq_ref,
                      ka_ref, va_ref, wo_ref, gf_ref, o_ref):
    y = jnp.concatenate([y_ref[j] for j in range(LANE_TILES)], axis=1)
    z = _dot(y, gw_ref[...]) + gb_ref[...]
    ssm = (y.astype(F32) * _sigmoid(z) * gs_ref[...].astype(F32)).astype(BF16)
    x1 = (x_ref[...] + _dot(ret_ref[...], wout_ref[0:RET_V_WIDTH, :])
          + _dot(ssm, wout_ref[RET_V_WIDTH:RET_V_WIDTH + S5_WIDTH, :]))
    h2 = _rms(x1, g2_ref[...]).astype(BF16)
    qa = (_dot(h2, wq_ref[...]) * (XA_DH ** -0.5)).astype(BF16)
    heads = []
    for h in range(XA_HEADS):
        cols = slice(h * XA_DH, (h + 1) * XA_DH)
        s = lax.dot_general(qa[:, cols], ka_ref[:, cols], (((1,), (1,)), ((), ())),
                            preferred_element_type=F32)
        e = jnp.exp(s - jnp.max(s, axis=-1, keepdims=True))
        l = jnp.sum(e, axis=-1, keepdims=True)
        heads.append((_dot(e.astype(BF16), va_ref[:, cols]) * (1.0 / l)).astype(BF16))
    o = jnp.concatenate(heads, axis=-1)
    x2 = x1 + _dot(o, wo_ref[...])
    o_ref[...] = _rms(x2, gf_ref[...])


def _out_xattn(x, ret, y, gs, gw, gb, wout, g2, wq, ka, va, wo, gf):
    B, L, D = x.shape
    tm = TM_OUT
    row = pl.BlockSpec((None, tm, D), lambda b, t: (b, t, 0))
    const = lambda shape: pl.BlockSpec(shape, lambda b, t: (0,) * len(shape))
    mem = pl.BlockSpec((None, MEM_LEN, D), lambda b, t: (b, 0, 0))
    ytile = pl.BlockSpec((None, LANE_TILES, tm, LANES), lambda b, t: (b, 0, t, 0))
    return pl.pallas_call(
        _out_xattn_kernel,
        grid=(B, L // tm),
        in_specs=[row, row, ytile, row, const((D, D)), const((1, D)), const(wout.shape), const((1, D)),
                  const((D, D)), mem, mem, const((D, D)), const((1, D))],
        out_specs=row,
        out_shape=jax.ShapeDtypeStruct((B, L, D), F32),
        compiler_params=pltpu.CompilerParams(
            dimension_semantics=("arbitrary", "arbitrary"), vmem_limit_bytes=VMEM_LIMIT),
        name="out_xattn",
    )(x, ret, y, gs, gw, gb, wout, g2, wq, ka, va, wo, gf)


def _rope_inv_lanes():
    half = RET_DK // 2
    inv = ROPE_BASE ** (-np.arange(half, dtype=np.float64) / half)
    return jnp.asarray(np.tile(inv, LANES // half).reshape(1, LANES), F32)


def kernel(x, mem, positions, norm1_g, w_in, ret_gn_g, s5_a_re, s5_a_im, s5_log_dt, s5_b_re, s5_b_im, s5_c_re, s5_c_im, s5_d, s5_glu_w, s5_glu_b, w_out, norm2_g, norm_mem_g, xa_wq, xa_wk, xa_wv, xa_wo, norm_f_g):
    B, L, D = x.shape
    l = 0
    bf = lambda w: w.astype(BF16)
    rowvec = lambda v: v.reshape(1, -1)

    wb, wck, consts = _s5_prep(s5_a_re[l], s5_a_im[l], s5_log_dt[l], s5_b_re[l], s5_b_im[l],
                               s5_c_re[l], s5_c_im[l])
    ka, va = _mem_kv(mem, rowvec(norm_mem_g[l]), bf(xa_wk[l]), bf(xa_wv[l]))
    u, g_s5, ret = _proj_ret(x, positions.reshape(B, L, 1), _rope_inv_lanes(),
                             rowvec(norm1_g[l]), bf(w_in[l]), rowvec(ret_gn_g[l]))
    y = _s5_scan(u, wb, wck, consts, s5_d[l].reshape(LANE_TILES, 1, LANES))
    return _out_xattn(x, ret, y, g_s5, bf(s5_glu_w[l]), rowvec(s5_glu_b[l]), bf(w_out[l]),
                      rowvec(norm2_g[l]), bf(xa_wq[l]), ka, va, bf(xa_wo[l]), rowvec(norm_f_g))
```

```python
import functools
import math

import numpy as np
import jax
import jax.numpy as jnp
from jax import lax
from jax.experimental import pallas as pl
from jax.experimental.pallas import tpu as pltpu

F32 = jnp.float32
BF16 = jnp.bfloat16

D_MODEL = 1024
MEM_LEN = 256
EPS = 1e-6
ROPE_BASE = 10000.0

RET_HEADS = 8
RET_QK_WIDTH = 512
RET_V_WIDTH = 1024
RET_DK = 64
RET_DV = 128
CHUNK = 128

S5_WIDTH = 1024
S5_GROUP = 16
S5_GROUPS = 64
S5_STATE = 64
S5_NSTATE = S5_GROUPS * S5_STATE

IN_COLS = 5120
COL_Q, COL_K, COL_V, COL_GR, COL_U, COL_GS = 0, 512, 1024, 2048, 3072, 4096

XA_HEADS = 4
XA_DH = 256

LANES = 128
SUBLANES = 8
VMEM_LIMIT = 48 * 1024 * 1024

LANE_TILES = S5_WIDTH // LANES
TILE_GROUPS = LANES // S5_GROUP
TILE_STATES = S5_NSTATE // LANE_TILES
S5_TAPS = 4

LOG_G = tuple(math.log1p(-(2.0 ** (-5.0 - h))) for h in range(RET_HEADS))

TM_PROJ = 512
TM_S5 = 2048
TM_OUT = 512


def _rms(x, g):
    ms = jnp.mean(x * x, axis=-1, keepdims=True)
    return x * lax.rsqrt(ms + EPS) * g


def _sigmoid(z):
    return 1.0 / (1.0 + jnp.exp(-z))


def _silu(g):
    return g * _sigmoid(g)


def _dot(a, b):
    return jnp.dot(a, b, preferred_element_type=F32)


def _cmul(ar, ai, br, bi):
    return ar * br - ai * bi, ar * bi + ai * br


def _s5_discretise(ar, ai, ldt):
    dt = jnp.exp(ldt)
    mag = jnp.exp(ar * dt)
    p_re = mag * jnp.cos(ai * dt)
    p_im = mag * jnp.sin(ai * dt)
    den = ar * ar + ai * ai
    nr, ni = p_re - 1.0, p_im
    f_re = (nr * ar + ni * ai) / den
    f_im = (ni * ar - nr * ai) / den
    return p_re, p_im, f_re, f_im


def _powers(p_re, p_im, n):
    pw = [(jnp.ones_like(p_re), jnp.zeros_like(p_im))]
    for _ in range(n):
        pw.append(_cmul(pw[-1][0], pw[-1][1], p_re, p_im))
    return pw


def _block_diag(x):
    tiled = jnp.concatenate([x] * TILE_GROUPS, axis=1)
    r = lax.broadcasted_iota(jnp.int32, tiled.shape, 0)
    c = lax.broadcasted_iota(jnp.int32, tiled.shape, 1)
    return jnp.where(r // S5_GROUP == c // S5_STATE, tiled, 0.0)


def _s5_prep_kernel(a_ref, b_ref, c_ref, wb_ref, wck_ref, consts_ref):
    s = S5_TAPS
    half = TILE_STATES
    p_re, p_im, f_re, f_im = _s5_discretise(a_ref[0:1, :], a_ref[1:2, :], a_ref[2:3, :])
    pw = _powers(p_re, p_im, s)
    bbr, bbi = _cmul(f_re, f_im, _block_diag(b_ref[0]), _block_diag(b_ref[1]))
    for k in range(s):
        wr, wi = _cmul(bbr, bbi, *pw[s - 1 - k])
        wb_ref[k * LANES:(k + 1) * LANES, 0:half] = wr.astype(BF16)
        wb_ref[k * LANES:(k + 1) * LANES, half:2 * half] = wi.astype(BF16)

    shape = (SUBLANES, half)
    qp = _powers(pw[s][0], pw[s][1], SUBLANES)
    row = lax.broadcasted_iota(jnp.int32, shape, 0)
    for idx, d in enumerate((1, 2, 4)):
        consts_ref[idx, 0] = jnp.where(row >= d, qp[d][0], 0.0)
        consts_ref[idx, 1] = jnp.where(row >= d, qp[d][1], 0.0)
    cr = jnp.zeros(shape, F32)
    ci = jnp.zeros(shape, F32)
    for r in range(SUBLANES):
        cr = jnp.where(row == r, qp[r + 1][0], cr)
        ci = jnp.where(row == r, qp[r + 1][1], ci)
    consts_ref[3, 0] = cr
    consts_ref[3, 1] = ci

    ct_re = _block_diag(c_ref[0])
    ct_im = _block_diag(c_ref[1])
    bb = jnp.concatenate([bbr, bbi], axis=1)
    taps = []
    for d in range(s + 1):
        er, ei = _cmul(ct_re, ct_im, *pw[d])
        cwt = jnp.concatenate([er, -ei], axis=1)
        if d >= 1:
            wck_ref[0:2 * half, (d - 1) * LANES:d * LANES] = cwt.T.astype(BF16)
        if d < s:
            taps.append(lax.dot_general(bb, cwt, (((1,), (1,)), ((), ())), preferred_element_type=F32,
                                        precision=lax.Precision.HIGHEST))
    zero = jnp.zeros((LANES, LANES), BF16)
    for m in range(s):
        for i in range(s):
            blk = taps[i - m].astype(BF16) if m <= i else zero
            wck_ref[2 * half + m * LANES:2 * half + (m + 1) * LANES, i * LANES:(i + 1) * LANES] = blk


def _s5_prep(a_re, a_im, log_dt, b_re, b_im, c_re, c_im):
    s = S5_TAPS
    ldt = jnp.broadcast_to(log_dt[:, None], (S5_GROUPS, S5_STATE))
    a3 = jnp.stack([a_re, a_im, ldt]).reshape(3, S5_NSTATE)
    bt = jnp.stack([b_re, b_im]).transpose(0, 1, 3, 2).reshape(2, S5_WIDTH, S5_STATE)
    ct = jnp.stack([c_re, c_im]).reshape(2, S5_WIDTH, S5_STATE)
    tile = lambda *shape: pl.BlockSpec((None,) + shape, lambda j: (j,) + (0,) * len(shape))
    return pl.pallas_call(
        _s5_prep_kernel,
        grid=(LANE_TILES,),
        in_specs=[pl.BlockSpec((3, TILE_STATES), lambda j: (0, j)),
                  pl.BlockSpec((2, LANES, S5_STATE), lambda j: (0, j, 0)),
                  pl.BlockSpec((2, LANES, S5_STATE), lambda j: (0, j, 0))],
        out_specs=(tile(s * LANES, 2 * TILE_STATES),
                   tile(2 * TILE_STATES + s * LANES, s * LANES),
                   tile(4, 2, SUBLANES, TILE_STATES)),
        out_shape=(jax.ShapeDtypeStruct((LANE_TILES, s * LANES, 2 * TILE_STATES), BF16),
                   jax.ShapeDtypeStruct((LANE_TILES, 2 * TILE_STATES + s * LANES, s * LANES), BF16),
                   jax.ShapeDtypeStruct((LANE_TILES, 4, 2, SUBLANES, TILE_STATES), F32)),
        compiler_params=pltpu.CompilerParams(
            dimension_semantics=("arbitrary",), vmem_limit_bytes=VMEM_LIMIT),
        name="s5_prep",
    )(a3, bt, ct)


def _mem_kv_kernel(mem_ref, g_ref, wk_ref, wv_ref, k_ref, v_ref):
    m = _rms(mem_ref[...], g_ref[...]).astype(BF16)
    k_ref[...] = _dot(m, wk_ref[...]).astype(BF16)
    v_ref[...] = _dot(m, wv_ref[...]).astype(BF16)


def _mem_kv(mem, g, wk, wv):
    B, M, D = mem.shape
    return pl.pallas_call(
        _mem_kv_kernel,
        grid=(B,),
        in_specs=[pl.BlockSpec((None, M, D), lambda b: (b, 0, 0)),
                  pl.BlockSpec((1, D), lambda b: (0, 0)),
                  pl.BlockSpec((D, D), lambda b: (0, 0)),
                  pl.BlockSpec((D, D), lambda b: (0, 0))],
        out_specs=(pl.BlockSpec((None, M, D), lambda b: (b, 0, 0)),
                   pl.BlockSpec((None, M, D), lambda b: (b, 0, 0))),
        out_shape=(jax.ShapeDtypeStruct((B, M, D), BF16),
                   jax.ShapeDtypeStruct((B, M, D), BF16)),
        compiler_params=pltpu.CompilerParams(
            dimension_semantics=("arbitrary",), vmem_limit_bytes=VMEM_LIMIT),
        name="mem_kv",
    )(mem, g, wk, wv)


def _proj_ret_kernel(x_ref, pos_ref, inv_ref, g_ref, w_ref, gn_ref, u_ref, gs_ref, ret_ref,
                     cur_ref, prev_ref, state_ref, decay_ref, qw_ref, kw_ref, *, tiles_per_seq):
    t = pl.program_id(0)

    @pl.when(t == 0)
    def _init():
        _retention_constants(decay_ref, qw_ref, kw_ref)
        cur_ref[...] = jnp.zeros_like(cur_ref)

    @pl.when((t == 0) | (t % tiles_per_seq == 1))
    def _reset():
        state_ref[...] = jnp.zeros_like(state_ref)

    prev_ref[...] = cur_ref[...]

    h = _rms(x_ref[...], g_ref[...]).astype(BF16)
    ang = pos_ref[...].astype(F32) * inv_ref[...]
    cos = jnp.cos(ang)
    sin = jnp.sin(ang)
    lane = lax.broadcasted_iota(jnp.int32, (1, LANES), 1)
    first_half = (lane % RET_DK) < (RET_DK // 2)
    sin_signed = jnp.where(first_half, -sin, sin)

    def rope(p):
        partner = jnp.where(first_half,
                            pltpu.roll(p, LANES - RET_DK // 2, 1),
                            pltpu.roll(p, RET_DK // 2, 1))
        return p * cos + partner * sin_signed

    def proj_q():
        pq = _dot(h, w_ref[:, COL_Q:COL_K])
        for i in range(RET_QK_WIDTH // LANES):
            sl = slice(i * LANES, (i + 1) * LANES)
            cur_ref[:, COL_Q + i * LANES:COL_Q + (i + 1) * LANES] = rope(pq[:, sl]).astype(BF16)

    def proj_k():
        pk = _dot(h, w_ref[:, COL_K:COL_V])
        for i in range(RET_QK_WIDTH // LANES):
            sl = slice(i * LANES, (i + 1) * LANES)
            cur_ref[:, COL_K + i * LANES:COL_K + (i + 1) * LANES] = (
                rope(pk[:, sl]) * (RET_DK ** -0.5)).astype(BF16)

    def proj_v():
        cur_ref[:, COL_V:COL_GR] = _dot(h, w_ref[:, COL_V:COL_GR]).astype(BF16)

    def proj_gate():
        cur_ref[:, COL_GR:COL_U] = _silu(_dot(h, w_ref[:, COL_GR:COL_U])).astype(BF16)

    def proj_u():
        pu = _dot(h, w_ref[:, COL_U:COL_GS]).astype(BF16)
        for j in range(LANE_TILES):
            u_ref[j] = pu[:, j * LANES:(j + 1) * LANES]

    def proj_gs():
        gs_ref[...] = _silu(_dot(h, w_ref[:, COL_GS:IN_COLS])).astype(BF16)

    ret_items = _retention_items(prev_ref, gn_ref, ret_ref, state_ref, decay_ref, qw_ref, kw_ref)
    proj_items = [proj_q, proj_k, proj_v, proj_gate, proj_u, proj_gs]
    per = -(-len(ret_items) // len(proj_items))
    for i, item in enumerate(proj_items):
        for r in ret_items[i * per:(i + 1) * per]:
            r()
        item()


def _proj_ret(x, pos3, inv, g, w, gn):
    B, L, D = x.shape
    tm = TM_PROJ
    nt = L // tm
    n = B * nt
    cur = lambda t: jnp.minimum(t, n - 1)
    lag = lambda t: jnp.maximum(t - 1, 0)
    row = lambda width, tile: pl.BlockSpec((None, tm, width), lambda t: (tile(t) // nt, tile(t) % nt, 0))
    const = lambda shape: pl.BlockSpec(shape, lambda t: (0,) * len(shape))
    return pl.pallas_call(
        functools.partial(_proj_ret_kernel, tiles_per_seq=nt),
        grid=(n + 1,),
        in_specs=[row(D, cur), row(1, cur), const((1, LANES)), const((1, D)), const((D, IN_COLS)),
                  const((1, RET_V_WIDTH))],
        out_specs=(pl.BlockSpec((None, LANE_TILES, tm, LANES), lambda t: (cur(t) // nt, 0, cur(t) % nt, 0)),
                   row(S5_WIDTH, cur), row(RET_V_WIDTH, lag)),
        out_shape=(jax.ShapeDtypeStruct((B, LANE_TILES, L, LANES), BF16),
                   jax.ShapeDtypeStruct((B, L, S5_WIDTH), BF16),
                   jax.ShapeDtypeStruct((B, L, RET_V_WIDTH), BF16)),
        scratch_shapes=[pltpu.VMEM((tm, COL_U), BF16),
                        pltpu.VMEM((tm, COL_U), BF16),
                        pltpu.VMEM((RET_HEADS // 2, LANES, 2 * RET_DV), F32),
                        pltpu.VMEM((RET_HEADS, CHUNK, CHUNK), F32),
                        pltpu.VMEM((RET_HEADS, CHUNK, RET_DV), F32),
                        pltpu.VMEM((RET_HEADS // 2, CHUNK, LANES), F32)],
        compiler_params=pltpu.CompilerParams(
            dimension_semantics=("arbitrary",), vmem_limit_bytes=VMEM_LIMIT),
        name="proj_ret",
    )(x, pos3, inv, g, w, gn)


def _retention_constants(decay_ref, qw_ref, kw_ref):
    i = lax.broadcasted_iota(jnp.int32, (CHUNK, CHUNK), 0).astype(F32)
    j = lax.broadcasted_iota(jnp.int32, (CHUNK, CHUNK), 1).astype(F32)
    diff = i - j
    for h in range(RET_HEADS):
        decay_ref[h] = jnp.where(diff >= 0.0, jnp.exp(LOG_G[h] * jnp.maximum(diff, 0.0)), 0.0)
        qw_ref[h] = jnp.exp(LOG_G[h] * (i + 1.0))
    for p in range(RET_HEADS // 2):
        lg = jnp.where(j < float(RET_DK), LOG_G[2 * p], LOG_G[2 * p + 1])
        kw_ref[p] = jnp.exp(lg * (CHUNK - 1.0 - i))


def _retention_items(src_ref, gn_ref, o_ref, state_ref, decay_ref, qw_ref, kw_ref):
    pairs = RET_HEADS // 2
    lane = lax.broadcasted_iota(jnp.int32, (1, LANES), 1)
    srow = lax.broadcasted_iota(jnp.int32, (LANES, 2 * RET_DV), 0)
    scol = lax.broadcasted_iota(jnp.int32, (LANES, 2 * RET_DV), 1)
    own_block = (srow < RET_DK) == (scol < RET_DV)
    scol1 = lax.broadcasted_iota(jnp.int32, (1, 2 * RET_DV), 1)

    def pair_step(c, p):
        rows = slice(c * CHUNK, (c + 1) * CHUNK)
        if True:
            qt = src_ref[rows, COL_Q + p * LANES:COL_Q + (p + 1) * LANES]
            kt = src_ref[rows, COL_K + p * LANES:COL_K + (p + 1) * LANES]
            vp = src_ref[rows, COL_V + 2 * p * RET_DV:COL_V + 2 * (p + 1) * RET_DV]
            state = state_ref[p]
            cross = _dot(qt, state.astype(BF16))
            kw = (kt.astype(F32) * kw_ref[p]).astype(BF16)
            upd = lax.dot_general(kw, vp, (((0,), (0,)), ((), ())), preferred_element_type=F32)
            chunk_decay = jnp.where(scol1 < RET_DV,
                                    math.exp(LOG_G[2 * p] * CHUNK), math.exp(LOG_G[2 * p + 1] * CHUNK))
            state_ref[p] = state * chunk_decay + jnp.where(own_block, upd, 0.0)
            for e in range(2):
                h = 2 * p + e
                cols = slice(h * RET_DV, (h + 1) * RET_DV)
                qm = jnp.where((lane < RET_DK) == (e == 0), qt, jnp.zeros_like(qt))
                s = lax.dot_general(qm, kt, (((1,), (1,)), ((), ())), preferred_element_type=F32)
                pm = (s * decay_ref[h]).astype(BF16)
                inner = _dot(pm, vp[:, e * RET_DV:(e + 1) * RET_DV])
                o = inner + qw_ref[h] * cross[:, e * RET_DV:(e + 1) * RET_DV]
                mu = jnp.mean(o, axis=-1, keepdims=True)
                oc = o - mu
                var = jnp.mean(oc * oc, axis=-1, keepdims=True)
                on = oc * lax.rsqrt(var + EPS) * gn_ref[:, cols]
                gate = src_ref[rows, COL_GR + h * RET_DV:COL_GR + (h + 1) * RET_DV].astype(F32)
                o_ref[rows, cols] = (on * gate).astype(BF16)

    return [functools.partial(pair_step, c, p)
            for c in range(src_ref.shape[0] // CHUNK) for p in range(pairs)]


def _s5_scan_kernel(u_ref, wb_ref, wck_ref, consts_ref, d_ref, y_ref, uf_ref, zs_ref, ys_ref):
    t = pl.program_id(2)
    s = S5_TAPS
    tm = u_ref.shape[0]
    nblk = tm // s
    half = TILE_STATES
    top = SUBLANES

    @pl.when(t == 0)
    def _reset():
        zs_ref[0:top, :] = jnp.zeros((top, 2 * half), F32)

    uf_ref[...] = u_ref[...].astype(F32)
    lhs = jnp.concatenate([uf_ref[pl.ds(k, nblk, stride=s), :] for k in range(s)], axis=1).astype(BF16)
    zs_ref[top:top + nblk, :] = _dot(lhs, wb_ref[...])

    def group(r, carry):
        cr, ci = carry
        rows = pl.ds(pl.multiple_of(top + r * SUBLANES, SUBLANES), SUBLANES)
        xr = zs_ref[rows, 0:half]
        xi = zs_ref[rows, half:2 * half]
        for idx, d in enumerate((1, 2, 4)):
            mr = consts_ref[idx, 0]
            mi = consts_ref[idx, 1]
            sr = pltpu.roll(xr, d, 0)
            si = pltpu.roll(xi, d, 0)
            xr, xi = xr + mr * sr - mi * si, xi + mr * si + mi * sr
        mr = consts_ref[3, 0]
        mi = consts_ref[3, 1]
        xr, xi = xr + mr * cr - mi * ci, xi + mr * ci + mi * cr
        zs_ref[rows, 0:half] = xr
        zs_ref[rows, half:2 * half] = xi
        last = SUBLANES - 1
        return (jnp.broadcast_to(xr[last:, :], (SUBLANES, half)),
                jnp.broadcast_to(xi[last:, :], (SUBLANES, half)))

    carry0 = (jnp.broadcast_to(zs_ref[top - 1:top, 0:half], (SUBLANES, half)),
              jnp.broadcast_to(zs_ref[top - 1:top, half:2 * half], (SUBLANES, half)))
    lax.fori_loop(0, nblk // SUBLANES, group, carry0)

    prev = zs_ref[pl.ds(top - 1, nblk), :].astype(BF16)
    zs_ref[top - 1:top, :] = zs_ref[top + nblk - 1:top + nblk, :]
    yall = _dot(jnp.concatenate([prev, lhs], axis=1), wck_ref[...])
    for i in range(s):
        ys_ref[pl.ds(i, nblk, stride=s), :] = yall[:, i * LANES:(i + 1) * LANES]
    y = ys_ref[...] + d_ref[...] * uf_ref[...]
    y_ref[...] = jax.nn.gelu(y).astype(BF16)


def _s5_scan(u, wb, wck, consts, d):
    B, _, L, _ = u.shape
    s = S5_TAPS
    tm = TM_S5
    tok = pl.BlockSpec((None, None, tm, LANES), lambda j, b, t: (b, j, t, 0))
    tile = lambda *shape: pl.BlockSpec((None,) + shape, lambda j, b, t: (j,) + (0,) * len(shape))
    return pl.pallas_call(
        _s5_scan_kernel,
        grid=(LANE_TILES, B, L // tm),
        in_specs=[tok, tile(s * LANES, 2 * TILE_STATES), tile(2 * TILE_STATES + s * LANES, s * LANES),
                  tile(4, 2, SUBLANES, TILE_STATES), tile(1, LANES)],
        out_specs=tok,
        out_shape=jax.ShapeDtypeStruct((B, LANE_TILES, L, LANES), BF16),
        scratch_shapes=[pltpu.VMEM((tm, LANES), F32),
                        pltpu.VMEM((SUBLANES + tm // s, 2 * TILE_STATES), F32),
                        pltpu.VMEM((tm, LANES), F32)],
        compiler_params=pltpu.CompilerParams(
            dimension_semantics=("arbitrary", "arbitrary", "arbitrary"), vmem_limit_bytes=VMEM_LIMIT),
        name="s5_scan",
    )(u, wb, wck, consts, d)


def _out_xattn_kernel(x_ref, ret_ref, y_ref, gs_ref, gw_ref, gb_ref, wout_ref, g2_ref, wq_ref,
                      ka_ref, va_ref, wo_ref, gf_ref, o_ref):
    y = jnp.concatenate([y_ref[j] for j in range(LANE_TILES)], axis=1)
    z = _dot(y, gw_ref[...]) + gb_ref[...]
    ssm = (y.astype(F32) * _sigmoid(z) * gs_ref[...].astype(F32)).astype(BF16)
    x1 = (x_ref[...] + _dot(ret_ref[...], wout_ref[0:RET_V_WIDTH, :])
          + _dot(ssm, wout_ref[RET_V_WIDTH:RET_V_WIDTH + S5_WIDTH, :]))
    h2 = _rms(x1, g2_ref[...]).astype(BF16)
    qa = (_dot(h2, wq_ref[...]) * (XA_DH ** -0.5)).astype(BF16)
    heads = []
    for h in range(XA_HEADS):
        cols = slice(h * XA_DH, (h + 1) * XA_DH)
        s = lax.dot_general(qa[:, cols], ka_ref[:, cols], (((1,), (1,)), ((), ())),
                            preferred_element_type=F32)
        e = jnp.exp(s - jnp.max(s, axis=-1, keepdims=True))
        l = jnp.sum(e, axis=-1, keepdims=True)
        heads.append((_dot(e.astype(BF16), va_ref[:, cols]) * (1.0 / l)).astype(BF16))
    o = jnp.concatenate(heads, axis=-1)
    x2 = x1 + _dot(o, wo_ref[...])
    o_ref[...] = _rms(x2, gf_ref[...])


def _out_xattn(x, ret, y, gs, gw, gb, wout, g2, wq, ka, va, wo, gf):
    B, L, D = x.shape
    tm = TM_OUT
    row = pl.BlockSpec((None, tm, D), lambda b, t: (b, t, 0))
    const = lambda shape: pl.BlockSpec(shape, lambda b, t: (0,) * len(shape))
    mem = pl.BlockSpec((None, MEM_LEN, D), lambda b, t: (b, 0, 0))
    ytile = pl.BlockSpec((None, LANE_TILES, tm, LANES), lambda b, t: (b, 0, t, 0))
    return pl.pallas_call(
        _out_xattn_kernel,
        grid=(B, L // tm),
        in_specs=[row, row, ytile, row, const((D, D)), const((1, D)), const(wout.shape), const((1, D)),
                  const((D, D)), mem, mem, const((D, D)), const((1, D))],
        out_specs=row,
        out_shape=jax.ShapeDtypeStruct((B, L, D), F32),
        compiler_params=pltpu.CompilerParams(
            dimension_semantics=("arbitrary", "arbitrary"), vmem_limit_bytes=VMEM_LIMIT),
        name="out_xattn",
    )(x, ret, y, gs, gw, gb, wout, g2, wq, ka, va, wo, gf)


def _rope_inv_lanes():
    half = RET_DK // 2
    inv = ROPE_BASE ** (-np.arange(half, dtype=np.float64) / half)
    return jnp.asarray(np.tile(inv, LANES // half).reshape(1, LANES), F32)


def kernel(x, mem, positions, norm1_g, w_in, ret_gn_g, s5_a_re, s5_a_im, s5_log_dt, s5_b_re, s5_b_im, s5_c_re, s5_c_im, s5_d, s5_glu_w, s5_glu_b, w_out, norm2_g, norm_mem_g, xa_wq, xa_wk, xa_wv, xa_wo, norm_f_g):
    B, L, D = x.shape
    l = 0
    bf = lambda w: w.astype(BF16)
    rowvec = lambda v: v.reshape(1, -1)

    wb, wck, consts = _s5_prep(s5_a_re[l], s5_a_im[l], s5_log_dt[l], s5_b_re[l], s5_b_im[l],
                               s5_c_re[l], s5_c_im[l])
    ka, va = _mem_kv(mem, rowvec(norm_mem_g[l]), bf(xa_wk[l]), bf(xa_wv[l]))
    u, g_s5, ret = _proj_ret(x, positions.reshape(B, L, 1), _rope_inv_lanes(),
                             rowvec(norm1_g[l]), bf(w_in[l]), rowvec(ret_gn_g[l]))
    y = _s5_scan(u, wb, wck, consts, s5_d[l].reshape(LANE_TILES, 1, LANES))
    return _out_xattn(x, ret, y, g_s5, bf(s5_glu_w[l]), rowvec(s5_glu_b[l]), bf(w_out[l]),
                      rowvec(norm2_g[l]), bf(xa_wq[l]), ka, va, bf(xa_wo[l]), rowvec(norm_f_g))
```

```python
import functools
import math

import numpy as np
import jax
import jax.numpy as jnp
from jax import lax
from jax.experimental import pallas as pl
from jax.experimental.pallas import tpu as pltpu

F32 = jnp.float32
BF16 = jnp.bfloat16

D_MODEL = 1024
MEM_LEN = 256
EPS = 1e-6
ROPE_BASE = 10000.0

RET_HEADS = 8
RET_QK_WIDTH = 512
RET_V_WIDTH = 1024
RET_DK = 64
RET_DV = 128
CHUNK = 128

S5_WIDTH = 1024
S5_GROUP = 16
S5_GROUPS = 64
S5_STATE = 64
S5_NSTATE = S5_GROUPS * S5_STATE

IN_COLS = 5120
COL_Q, COL_K, COL_V, COL_GR, COL_U, COL_GS = 0, 512, 1024, 2048, 3072, 4096

XA_HEADS = 4
XA_DH = 256

LANES = 128
SUBLANES = 8
VMEM_LIMIT = 48 * 1024 * 1024

LANE_TILES = S5_WIDTH // LANES
TILE_GROUPS = LANES // S5_GROUP
TILE_STATES = S5_NSTATE // LANE_TILES
S5_TAPS = 4
S5_SCAN_LANES = 2 * LANES

LOG_G = tuple(math.log1p(-(2.0 ** (-5.0 - h))) for h in range(RET_HEADS))

TM_PROJ = 512
TM_S5 = 2048
TM_OUT = 512


def _rms(x, g):
    ms = jnp.mean(x * x, axis=-1, keepdims=True)
    return x * lax.rsqrt(ms + EPS) * g


def _sigmoid(z):
    return 1.0 / (1.0 + jnp.exp(-z))


def _silu(g):
    return g * _sigmoid(g)


def _dot(a, b):
    return jnp.dot(a, b, preferred_element_type=F32)


def _cmul(ar, ai, br, bi):
    return ar * br - ai * bi, ar * bi + ai * br


def _s5_discretise(ar, ai, ldt):
    dt = jnp.exp(ldt)
    mag = jnp.exp(ar * dt)
    p_re = mag * jnp.cos(ai * dt)
    p_im = mag * jnp.sin(ai * dt)
    den = ar * ar + ai * ai
    nr, ni = p_re - 1.0, p_im
    f_re = (nr * ar + ni * ai) / den
    f_im = (ni * ar - nr * ai) / den
    return p_re, p_im, f_re, f_im


def _powers(p_re, p_im, n):
    pw = [(jnp.ones_like(p_re), jnp.zeros_like(p_im))]
    for _ in range(n):
        pw.append(_cmul(pw[-1][0], pw[-1][1], p_re, p_im))
    return pw


def _block_diag(x):
    tiled = jnp.concatenate([x] * TILE_GROUPS, axis=1)
    r = lax.broadcasted_iota(jnp.int32, tiled.shape, 0)
    c = lax.broadcasted_iota(jnp.int32, tiled.shape, 1)
    return jnp.where(r // S5_GROUP == c // S5_STATE, tiled, 0.0)


def _s5_prep_kernel(a_ref, b_ref, c_ref, wb_ref, wck_ref, consts_ref):
    s = S5_TAPS
    half = TILE_STATES
    p_re, p_im, f_re, f_im = _s5_discretise(a_ref[0:1, :], a_ref[1:2, :], a_ref[2:3, :])
    pw = _powers(p_re, p_im, s)
    bbr, bbi = _cmul(f_re, f_im, _block_diag(b_ref[0]), _block_diag(b_ref[1]))
    for k in range(s):
        wr, wi = _cmul(bbr, bbi, *pw[s - 1 - k])
        wb_ref[k * LANES:(k + 1) * LANES, 0:half] = wr.astype(BF16)
        wb_ref[k * LANES:(k + 1) * LANES, half:2 * half] = wi.astype(BF16)

    shape = (SUBLANES, half)
    qp = _powers(pw[s][0], pw[s][1], SUBLANES)
    row = lax.broadcasted_iota(jnp.int32, shape, 0)
    for idx, d in enumerate((1, 2, 4)):
        consts_ref[idx, 0] = jnp.where(row >= d, qp[d][0], 0.0)
        consts_ref[idx, 1] = jnp.where(row >= d, qp[d][1], 0.0)
    cr = jnp.zeros(shape, F32)
    ci = jnp.zeros(shape, F32)
    for r in range(SUBLANES):
        cr = jnp.where(row == r, qp[r + 1][0], cr)
        ci = jnp.where(row == r, qp[r + 1][1], ci)
    consts_ref[3, 0] = cr
    consts_ref[3, 1] = ci

    ct_re = _block_diag(c_ref[0])
    ct_im = _block_diag(c_ref[1])
    bb = jnp.concatenate([bbr, bbi], axis=1)
    taps = []
    for d in range(s + 1):
        er, ei = _cmul(ct_re, ct_im, *pw[d])
        cwt = jnp.concatenate([er, -ei], axis=1)
        if d >= 1:
            wck_ref[0:2 * half, (d - 1) * LANES:d * LANES] = cwt.T.astype(BF16)
        if d < s:
            taps.append(lax.dot_general(bb, cwt, (((1,), (1,)), ((), ())), preferred_element_type=F32,
                                        precision=lax.Precision.HIGHEST))
    zero = jnp.zeros((LANES, LANES), BF16)
    for m in range(s):
        for i in range(s):
            blk = taps[i - m].astype(BF16) if m <= i else zero
            wck_ref[2 * half + m * LANES:2 * half + (m + 1) * LANES, i * LANES:(i + 1) * LANES] = blk


def _s5_prep(a_re, a_im, log_dt, b_re, b_im, c_re, c_im):
    s = S5_TAPS
    ldt = jnp.broadcast_to(log_dt[:, None], (S5_GROUPS, S5_STATE))
    a3 = jnp.stack([a_re, a_im, ldt]).reshape(3, S5_NSTATE)
    bt = jnp.stack([b_re, b_im]).transpose(0, 1, 3, 2).reshape(2, S5_WIDTH, S5_STATE)
    ct = jnp.stack([c_re, c_im]).reshape(2, S5_WIDTH, S5_STATE)
    tile = lambda *shape: pl.BlockSpec((None,) + shape, lambda j: (j,) + (0,) * len(shape))
    return pl.pallas_call(
        _s5_prep_kernel,
        grid=(LANE_TILES,),
        in_specs=[pl.BlockSpec((3, TILE_STATES), lambda j: (0, j)),
                  pl.BlockSpec((2, LANES, S5_STATE), lambda j: (0, j, 0)),
                  pl.BlockSpec((2, LANES, S5_STATE), lambda j: (0, j, 0))],
        out_specs=(tile(s * LANES, 2 * TILE_STATES),
                   tile(2 * TILE_STATES + s * LANES, s * LANES),
                   tile(4, 2, SUBLANES, TILE_STATES)),
        out_shape=(jax.ShapeDtypeStruct((LANE_TILES, s * LANES, 2 * TILE_STATES), BF16),
                   jax.ShapeDtypeStruct((LANE_TILES, 2 * TILE_STATES + s * LANES, s * LANES), BF16),
                   jax.ShapeDtypeStruct((LANE_TILES, 4, 2, SUBLANES, TILE_STATES), F32)),
        compiler_params=pltpu.CompilerParams(
            dimension_semantics=("arbitrary",), vmem_limit_bytes=VMEM_LIMIT),
        name="s5_prep",
    )(a3, bt, ct)


def _mem_kv_kernel(mem_ref, g_ref, wk_ref, wv_ref, k_ref, v_ref):
    m = _rms(mem_ref[...], g_ref[...]).astype(BF16)
    k_ref[...] = _dot(m, wk_ref[...]).astype(BF16)
    v_ref[...] = _dot(m, wv_ref[...]).astype(BF16)


def _mem_kv(mem, g, wk, wv):
    B, M, D = mem.shape
    return pl.pallas_call(
        _mem_kv_kernel,
        grid=(B,),
        in_specs=[pl.BlockSpec((None, M, D), lambda b: (b, 0, 0)),
                  pl.BlockSpec((1, D), lambda b: (0, 0)),
                  pl.BlockSpec((D, D), lambda b: (0, 0)),
                  pl.BlockSpec((D, D), lambda b: (0, 0))],
        out_specs=(pl.BlockSpec((None, M, D), lambda b: (b, 0, 0)),
                   pl.BlockSpec((None, M, D), lambda b: (b, 0, 0))),
        out_shape=(jax.ShapeDtypeStruct((B, M, D), BF16),
                   jax.ShapeDtypeStruct((B, M, D), BF16)),
        compiler_params=pltpu.CompilerParams(
            dimension_semantics=("arbitrary",), vmem_limit_bytes=VMEM_LIMIT),
        name="mem_kv",
    )(mem, g, wk, wv)


def _proj_ret_kernel(x_ref, pos_ref, inv_ref, g_ref, w_ref, gn_ref, u_ref, gs_ref, ret_ref,
                     cur_ref, prev_ref, state_ref, decay_ref, qw_ref, kw_ref, *, tiles_per_seq):
    t = pl.program_id(0)

    @pl.when(t == 0)
    def _init():
        _retention_constants(decay_ref, qw_ref, kw_ref)
        cur_ref[...] = jnp.zeros_like(cur_ref)

    @pl.when((t == 0) | (t % tiles_per_seq == 1))
    def _reset():
        state_ref[...] = jnp.zeros_like(state_ref)

    prev_ref[...] = cur_ref[...]

    h = _rms(x_ref[...], g_ref[...]).astype(BF16)
    ang = pos_ref[...].astype(F32) * inv_ref[...]
    cos = jnp.cos(ang)
    sin = jnp.sin(ang)
    lane = lax.broadcasted_iota(jnp.int32, (1, LANES), 1)
    first_half = (lane % RET_DK) < (RET_DK // 2)
    sin_signed = jnp.where(first_half, -sin, sin)

    def rope(p):
        partner = jnp.where(first_half,
                            pltpu.roll(p, LANES - RET_DK // 2, 1),
                            pltpu.roll(p, RET_DK // 2, 1))
        return p * cos + partner * sin_signed

    def proj_q():
        pq = _dot(h, w_ref[:, COL_Q:COL_K])
        for i in range(RET_QK_WIDTH // LANES):
            sl = slice(i * LANES, (i + 1) * LANES)
            cur_ref[:, COL_Q + i * LANES:COL_Q + (i + 1) * LANES] = rope(pq[:, sl]).astype(BF16)

    def proj_k():
        pk = _dot(h, w_ref[:, COL_K:COL_V])
        for i in range(RET_QK_WIDTH // LANES):
            sl = slice(i * LANES, (i + 1) * LANES)
            cur_ref[:, COL_K + i * LANES:COL_K + (i + 1) * LANES] = (
                rope(pk[:, sl]) * (RET_DK ** -0.5)).astype(BF16)

    def proj_v():
        cur_ref[:, COL_V:COL_GR] = _dot(h, w_ref[:, COL_V:COL_GR]).astype(BF16)

    def proj_gate():
        cur_ref[:, COL_GR:COL_U] = _silu(_dot(h, w_ref[:, COL_GR:COL_U])).astype(BF16)

    def proj_u():
        pu = _dot(h, w_ref[:, COL_U:COL_GS]).astype(BF16)
        for j in range(LANE_TILES):
            u_ref[j] = pu[:, j * LANES:(j + 1) * LANES]

    def proj_gs():
        gs_ref[...] = _silu(_dot(h, w_ref[:, COL_GS:IN_COLS])).astype(BF16)

    ret_items = _retention_items(prev_ref, gn_ref, ret_ref, state_ref, decay_ref, qw_ref, kw_ref)
    proj_items = [proj_q, proj_k, proj_v, proj_gate, proj_u, proj_gs]
    per = -(-len(ret_items) // len(proj_items))
    for i, item in enumerate(proj_items):
        for r in ret_items[i * per:(i + 1) * per]:
            r()
        item()


def _proj_ret(x, pos3, inv, g, w, gn):
    B, L, D = x.shape
    tm = TM_PROJ
    nt = L // tm
    n = B * nt
    cur = lambda t: jnp.minimum(t, n - 1)
    lag = lambda t: jnp.maximum(t - 1, 0)
    row = lambda width, tile: pl.BlockSpec((None, tm, width), lambda t: (tile(t) // nt, tile(t) % nt, 0))
    const = lambda shape: pl.BlockSpec(shape, lambda t: (0,) * len(shape))
    return pl.pallas_call(
        functools.partial(_proj_ret_kernel, tiles_per_seq=nt),
        grid=(n + 1,),
        in_specs=[row(D, cur), row(1, cur), const((1, LANES)), const((1, D)), const((D, IN_COLS)),
                  const((1, RET_V_WIDTH))],
        out_specs=(pl.BlockSpec((None, LANE_TILES, tm, LANES), lambda t: (cur(t) // nt, 0, cur(t) % nt, 0)),
                   row(S5_WIDTH, cur), row(RET_V_WIDTH, lag)),
        out_shape=(jax.ShapeDtypeStruct((B, LANE_TILES, L, LANES), BF16),
                   jax.ShapeDtypeStruct((B, L, S5_WIDTH), BF16),
                   jax.ShapeDtypeStruct((B, L, RET_V_WIDTH), BF16)),
        scratch_shapes=[pltpu.VMEM((tm, COL_U), BF16),
                        pltpu.VMEM((tm, COL_U), BF16),
                        pltpu.VMEM((RET_HEADS // 2, LANES, 2 * RET_DV), F32),
                        pltpu.VMEM((RET_HEADS, CHUNK, CHUNK), F32),
                        pltpu.VMEM((RET_HEADS, CHUNK, RET_DV), F32),
                        pltpu.VMEM((RET_HEADS // 2, CHUNK, LANES), F32)],
        compiler_params=pltpu.CompilerParams(
            dimension_semantics=("arbitrary",), vmem_limit_bytes=VMEM_LIMIT),
        name="proj_ret",
    )(x, pos3, inv, g, w, gn)


def _retention_constants(decay_ref, qw_ref, kw_ref):
    i = lax.broadcasted_iota(jnp.int32, (CHUNK, CHUNK), 0).astype(F32)
    j = lax.broadcasted_iota(jnp.int32, (CHUNK, CHUNK), 1).astype(F32)
    diff = i - j
    for h in range(RET_HEADS):
        decay_ref[h] = jnp.where(diff >= 0.0, jnp.exp(LOG_G[h] * jnp.maximum(diff, 0.0)), 0.0)
        qw_ref[h] = jnp.exp(LOG_G[h] * (i + 1.0))
    for p in range(RET_HEADS // 2):
        lg = jnp.where(j < float(RET_DK), LOG_G[2 * p], LOG_G[2 * p + 1])
        kw_ref[p] = jnp.exp(lg * (CHUNK - 1.0 - i))


def _retention_items(src_ref, gn_ref, o_ref, state_ref, decay_ref, qw_ref, kw_ref):
    pairs = RET_HEADS // 2
    lane = lax.broadcasted_iota(jnp.int32, (1, LANES), 1)
    srow = lax.broadcasted_iota(jnp.int32, (LANES, 2 * RET_DV), 0)
    scol = lax.broadcasted_iota(jnp.int32, (LANES, 2 * RET_DV), 1)
    own_block = (srow < RET_DK) == (scol < RET_DV)
    scol1 = lax.broadcasted_iota(jnp.int32, (1, 2 * RET_DV), 1)

    def pair_step(c, p):
        rows = slice(c * CHUNK, (c + 1) * CHUNK)
        if True:
            qt = src_ref[rows, COL_Q + p * LANES:COL_Q + (p + 1) * LANES]
            kt = src_ref[rows, COL_K + p * LANES:COL_K + (p + 1) * LANES]
            vp = src_ref[rows, COL_V + 2 * p * RET_DV:COL_V + 2 * (p + 1) * RET_DV]
            state = state_ref[p]
            cross = _dot(qt, state.astype(BF16))
            kw = (kt.astype(F32) * kw_ref[p]).astype(BF16)
            upd = lax.dot_general(kw, vp, (((0,), (0,)), ((), ())), preferred_element_type=F32)
            chunk_decay = jnp.where(scol1 < RET_DV,
                                    math.exp(LOG_G[2 * p] * CHUNK), math.exp(LOG_G[2 * p + 1] * CHUNK))
            state_ref[p] = state * chunk_decay + jnp.where(own_block, upd, 0.0)
            for e in range(2):
                h = 2 * p + e
                cols = slice(h * RET_DV, (h + 1) * RET_DV)
                qm = jnp.where((lane < RET_DK) == (e == 0), qt, jnp.zeros_like(qt))
                s = lax.dot_general(qm, kt, (((1,), (1,)), ((), ())), preferred_element_type=F32)
                pm = (s * decay_ref[h]).astype(BF16)
                inner = _dot(pm, vp[:, e * RET_DV:(e + 1) * RET_DV])
                o = inner + qw_ref[h] * cross[:, e * RET_DV:(e + 1) * RET_DV]
                mu = jnp.mean(o, axis=-1, keepdims=True)
                oc = o - mu
                var = jnp.mean(oc * oc, axis=-1, keepdims=True)
                on = oc * lax.rsqrt(var + EPS) * gn_ref[:, cols]
                gate = src_ref[rows, COL_GR + h * RET_DV:COL_GR + (h + 1) * RET_DV].astype(F32)
                o_ref[rows, cols] = (on * gate).astype(BF16)

    return [functools.partial(pair_step, c, p)
            for c in range(src_ref.shape[0] // CHUNK) for p in range(pairs)]


def _s5_scan_kernel(u_ref, wb_ref, consts_ref, wck_ref, d_ref, y_ref, *scratch, tiles_per_seq):
    uf_refs, lhs_refs, st_refs = scratch[0:3], scratch[3:6], scratch[6:9]
    sb_ref, ys_ref, carry_ref = scratch[9:]
    n = pl.program_id(0)
    s = S5_TAPS
    tm = u_ref.shape[0]
    nblk = tm // s
    half = TILE_STATES

    @pl.when(n == 0)
    def _init():
        for ref in scratch[0:9]:
            ref[...] = jnp.zeros_like(ref)

    @pl.when((n == 0) | ((n - 1) % tiles_per_seq == 0))
    def _reset():
        carry_ref[...] = jnp.zeros_like(carry_ref)

    ncol = 2 * LANES

    def increments(slot):
        uf_ref, lhs_ref, st_ref = uf_refs[slot], lhs_refs[slot], st_refs[slot]

        def piece(c):
            if c == 0:
                uf_ref[...] = u_ref[...].astype(F32)
                lhs_ref[...] = jnp.concatenate(
                    [uf_ref[pl.ds(k, nblk, stride=s), :] for k in range(s)], axis=1).astype(BF16)
            cols = slice(c * ncol, (c + 1) * ncol)
            st_ref[:, cols] = _dot(lhs_ref[...], wb_ref[:, cols])

        return [functools.partial(piece, c) for c in range(2 * half // ncol)]

    def scan(slot, row_pieces):
        st_ref = st_refs[slot]
        lw = S5_SCAN_LANES
        row = lax.broadcasted_iota(jnp.int32, (SUBLANES, lw), 0)
        last = SUBLANES - 1
        groups = nblk // SUBLANES
        per = -(-groups // row_pieces)

        def piece(q, i, carry):
            re = slice(q * lw, (q + 1) * lw)
            im = slice(half + q * lw, half + (q + 1) * lw)
            if i == 0:
                carry[:] = [carry_ref[:, re], carry_ref[:, im]]
            cr, ci = carry
            for r in range(i * per, min((i + 1) * per, groups)):
                rows = slice(r * SUBLANES, (r + 1) * SUBLANES)
                xr = st_ref[rows, re]
                xi = st_ref[rows, im]
                for idx, d in enumerate((1, 2, 4)):
                    pr, pi = _cmul(consts_ref[idx, 0, :, re], consts_ref[idx, 1, :, re],
                                   pltpu.roll(xr, d, 0), pltpu.roll(xi, d, 0))
                    xr, xi = xr + pr, xi + pi
                pr, pi = _cmul(consts_ref[3, 0, :, re], consts_ref[3, 1, :, re], cr, ci)
                xr, xi = xr + pr, xi + pi
                st_ref[rows, re] = jnp.where(row == 0, cr, pltpu.roll(xr, 1, 0))
                st_ref[rows, im] = jnp.where(row == 0, ci, pltpu.roll(xi, 1, 0))
                cr = jnp.broadcast_to(xr[last:, :], (SUBLANES, lw))
                ci = jnp.broadcast_to(xi[last:, :], (SUBLANES, lw))
            carry[:] = [cr, ci]
            if i == row_pieces - 1:
                carry_ref[:, re] = cr
                carry_ref[:, im] = ci

        pieces = []
        for q in range(half // lw):
            carry = [None, None]
            pieces += [functools.partial(piece, q, i, carry) for i in range(row_pieces)]
        return pieces

    def outputs(slot):
        taps_per_piece = ncol // LANES

        def cast():
            sb_ref[...] = st_refs[slot][...].astype(BF16)

        def piece(c):
            lhs = jnp.concatenate([sb_ref[...], lhs_refs[slot][...]], axis=1)
            yall = _dot(lhs, wck_ref[:, c * ncol:(c + 1) * ncol])
            for i in range(taps_per_piece):
                tap = c * taps_per_piece + i
                ys_ref[pl.ds(tap, nblk, stride=s), :] = yall[:, i * LANES:(i + 1) * LANES]

        def finish():
            y = ys_ref[...] + d_ref[...] * uf_refs[slot][...]
            y_ref[...] = jax.nn.gelu(y).astype(BF16)

        return [cast] + [functools.partial(piece, c) for c in range(s * LANES // ncol)] + [finish]

    for phase in range(3):
        @pl.when(n % 3 == phase)
        def _steps(phase=phase):
            mxu = increments(phase) + outputs((phase + 1) % 3)
            vpu = scan((phase + 2) % 3, 3)
            for i, piece in enumerate(mxu):
                piece()
                if i < len(vpu):
                    vpu[i]()
            for piece in vpu[len(mxu):]:
                piece()


def _s5_scan(u, wb, wck, consts, d):
    B, _, L, _ = u.shape
    s = S5_TAPS
    tm = TM_S5
    nt = L // tm
    ntiles = LANE_TILES * B * nt
    tile_of = lambda n, lag: jnp.clip(n - lag, 0, ntiles - 1)
    lane_tile = lambda n, lag: tile_of(n, lag) // (B * nt)

    def tok(lag):
        def index(n):
            i = tile_of(n, lag)
            return ((i // nt) % B, i // (B * nt), i % nt, 0)
        return pl.BlockSpec((None, None, tm, LANES), index)

    def par(lag, *shape):
        return pl.BlockSpec((None,) + shape, lambda n: (lane_tile(n, lag),) + (0,) * len(shape))

    nblk = tm // s
    return pl.pallas_call(
        functools.partial(_s5_scan_kernel, tiles_per_seq=nt),
        grid=(ntiles + 2,),
        in_specs=[tok(0), par(0, s * LANES, 2 * TILE_STATES), par(1, 4, 2, SUBLANES, TILE_STATES),
                  par(2, 2 * TILE_STATES + s * LANES, s * LANES), par(2, 1, LANES)],
        out_specs=tok(2),
        out_shape=jax.ShapeDtypeStruct((B, LANE_TILES, L, LANES), BF16),
        scratch_shapes=([pltpu.VMEM((tm, LANES), F32)] * 3
                        + [pltpu.VMEM((nblk, s * LANES), BF16)] * 3
                        + [pltpu.VMEM((nblk, 2 * TILE_STATES), F32)] * 3
                        + [pltpu.VMEM((nblk, 2 * TILE_STATES), BF16), pltpu.VMEM((tm, LANES), F32),
                           pltpu.VMEM((SUBLANES, 2 * TILE_STATES), F32)]),
        compiler_params=pltpu.CompilerParams(
            dimension_semantics=("arbitrary",), vmem_limit_bytes=VMEM_LIMIT),
        name="s5_scan",
    )(u, wb, consts, wck, d)


def _out_xattn_kernel(x_ref, ret_ref, y_ref, gs_ref, gw_ref, gb_ref, wout_ref, g2_ref, wq_ref,
                      ka_ref, va_ref, wo_ref, gf_ref, o_ref):
    y = jnp.concatenate([y_ref[j] for j in range(LANE_TILES)], axis=1)
    z = _dot(y, gw_ref[...]) + gb_ref[...]
    ssm = (y.astype(F32) * _sigmoid(z) * gs_ref[...].astype(F32)).astype(BF16)
    x1 = (x_ref[...] + _dot(ret_ref[...], wout_ref[0:RET_V_WIDTH, :])
          + _dot(ssm, wout_ref[RET_V_WIDTH:RET_V_WIDTH + S5_WIDTH, :]))
    h2 = _rms(x1, g2_ref[...]).astype(BF16)
    qa = (_dot(h2, wq_ref[...]) * (XA_DH ** -0.5)).astype(BF16)
    heads = []
    for h in range(XA_HEADS):
        cols = slice(h * XA_DH, (h + 1) * XA_DH)
        s = lax.dot_general(qa[:, cols], ka_ref[:, cols], (((1,), (1,)), ((), ())),
                            preferred_element_type=F32)
        e = jnp.exp(s - jnp.max(s, axis=-1, keepdims=True))
        l = jnp.sum(e, axis=-1, keepdims=True)
        heads.append((_dot(e.astype(BF16), va_ref[:, cols]) * (1.0 / l)).astype(BF16))
    o = jnp.concatenate(heads, axis=-1)
    x2 = x1 + _dot(o, wo_ref[...])
    o_ref[...] = _rms(x2, gf_ref[...])


def _out_xattn(x, ret, y, gs, gw, gb, wout, g2, wq, ka, va, wo, gf):
    B, L, D = x.shape
    tm = TM_OUT
    row = pl.BlockSpec((None, tm, D), lambda b, t: (b, t, 0))
    const = lambda shape: pl.BlockSpec(shape, lambda b, t: (0,) * len(shape))
    mem = pl.BlockSpec((None, MEM_LEN, D), lambda b, t: (b, 0, 0))
    ytile = pl.BlockSpec((None, LANE_TILES, tm, LANES), lambda b, t: (b, 0, t, 0))
    return pl.pallas_call(
        _out_xattn_kernel,
        grid=(B, L // tm),
        in_specs=[row, row, ytile, row, const((D, D)), const((1, D)), const(wout.shape), const((1, D)),
                  const((D, D)), mem, mem, const((D, D)), const((1, D))],
        out_specs=row,
        out_shape=jax.ShapeDtypeStruct((B, L, D), F32),
        compiler_params=pltpu.CompilerParams(
            dimension_semantics=("arbitrary", "arbitrary"), vmem_limit_bytes=VMEM_LIMIT),
        name="out_xattn",
    )(x, ret, y, gs, gw, gb, wout, g2, wq, ka, va, wo, gf)


def _rope_inv_lanes():
    half = RET_DK // 2
    inv = ROPE_BASE ** (-np.arange(half, dtype=np.float64) / half)
    return jnp.asarray(np.tile(inv, LANES // half).reshape(1, LANES), F32)


def kernel(x, mem, positions, norm1_g, w_in, ret_gn_g, s5_a_re, s5_a_im, s5_log_dt, s5_b_re, s5_b_im, s5_c_re, s5_c_im, s5_d, s5_glu_w, s5_glu_b, w_out, norm2_g, norm_mem_g, xa_wq, xa_wk, xa_wv, xa_wo, norm_f_g):
    B, L, D = x.shape
    l = 0
    bf = lambda w: w.astype(BF16)
    rowvec = lambda v: v.reshape(1, -1)

    wb, wck, consts = _s5_prep(s5_a_re[l], s5_a_im[l], s5_log_dt[l], s5_b_re[l], s5_b_im[l],
                               s5_c_re[l], s5_c_im[l])
    ka, va = _mem_kv(mem, rowvec(norm_mem_g[l]), bf(xa_wk[l]), bf(xa_wv[l]))
    u, g_s5, ret = _proj_ret(x, positions.reshape(B, L, 1), _rope_inv_lanes(),
                             rowvec(norm1_g[l]), bf(w_in[l]), rowvec(ret_gn_g[l]))
    y = _s5_scan(u, wb, wck, consts, s5_d[l].reshape(LANE_TILES, 1, LANES))
    return _out_xattn(x, ret, y, g_s5, bf(s5_glu_w[l]), rowvec(s5_glu_b[l]), bf(w_out[l]),
                      rowvec(norm2_g[l]), bf(xa_wq[l]), ka, va, bf(xa_wo[l]), rowvec(norm_f_g))
```

```python
import functools
import math

import numpy as np
import jax
import jax.numpy as jnp
from jax import lax
from jax.experimental import pallas as pl
from jax.experimental.pallas import tpu as pltpu

F32 = jnp.float32
BF16 = jnp.bfloat16

D_MODEL = 1024
MEM_LEN = 256
EPS = 1e-6
ROPE_BASE = 10000.0

RET_HEADS = 8
RET_QK_WIDTH = 512
RET_V_WIDTH = 1024
RET_DK = 64
RET_DV = 128
CHUNK = 128

S5_WIDTH = 1024
S5_GROUP = 16
S5_GROUPS = 64
S5_STATE = 64
S5_NSTATE = S5_GROUPS * S5_STATE

IN_COLS = 5120
COL_Q, COL_K, COL_V, COL_GR, COL_U, COL_GS = 0, 512, 1024, 2048, 3072, 4096

XA_HEADS = 4
XA_DH = 256

LANES = 128
SUBLANES = 8
VMEM_LIMIT = 48 * 1024 * 1024

LANE_TILES = S5_WIDTH // LANES
TILE_GROUPS = LANES // S5_GROUP
TILE_STATES = S5_NSTATE // LANE_TILES
S5_TAPS = 4
S5_SCAN_LANES = 2 * LANES

LOG_G = tuple(math.log1p(-(2.0 ** (-5.0 - h))) for h in range(RET_HEADS))

TM_PROJ = 512
TM_S5 = 2048
TM_OUT = 512


def _rms(x, g):
    ms = jnp.mean(x * x, axis=-1, keepdims=True)
    return x * lax.rsqrt(ms + EPS) * g


def _sigmoid(z):
    return 1.0 / (1.0 + jnp.exp(-z))


def _silu(g):
    return g * _sigmoid(g)


def _dot(a, b):
    return jnp.dot(a, b, preferred_element_type=F32)


def _cmul(ar, ai, br, bi):
    return ar * br - ai * bi, ar * bi + ai * br


def _s5_discretise(ar, ai, ldt):
    dt = jnp.exp(ldt)
    mag = jnp.exp(ar * dt)
    p_re = mag * jnp.cos(ai * dt)
    p_im = mag * jnp.sin(ai * dt)
    den = ar * ar + ai * ai
    nr, ni = p_re - 1.0, p_im
    f_re = (nr * ar + ni * ai) / den
    f_im = (ni * ar - nr * ai) / den
    return p_re, p_im, f_re, f_im


def _powers(p_re, p_im, n):
    pw = [(jnp.ones_like(p_re), jnp.zeros_like(p_im))]
    for _ in range(n):
        pw.append(_cmul(pw[-1][0], pw[-1][1], p_re, p_im))
    return pw


def _block_diag(x):
    tiled = jnp.concatenate([x] * TILE_GROUPS, axis=1)
    r = lax.broadcasted_iota(jnp.int32, tiled.shape, 0)
    c = lax.broadcasted_iota(jnp.int32, tiled.shape, 1)
    return jnp.where(r // S5_GROUP == c // S5_STATE, tiled, 0.0)


def _s5_prep_kernel(a_ref, b_ref, c_ref, wb_ref, wck_ref, consts_ref):
    s = S5_TAPS
    half = TILE_STATES
    p_re, p_im, f_re, f_im = _s5_discretise(a_ref[0:1, :], a_ref[1:2, :], a_ref[2:3, :])
    pw = _powers(p_re, p_im, s)
    bbr, bbi = _cmul(f_re, f_im, _block_diag(b_ref[0]), _block_diag(b_ref[1]))
    for k in range(s):
        wr, wi = _cmul(bbr, bbi, *pw[s - 1 - k])
        wb_ref[k * LANES:(k + 1) * LANES, 0:half] = wr.astype(BF16)
        wb_ref[k * LANES:(k + 1) * LANES, half:2 * half] = wi.astype(BF16)

    shape = (SUBLANES, half)
    qp = _powers(pw[s][0], pw[s][1], SUBLANES)
    row = lax.broadcasted_iota(jnp.int32, shape, 0)
    for idx, d in enumerate((1, 2, 4)):
        consts_ref[idx, 0] = jnp.where(row >= d, qp[d][0], 0.0)
        consts_ref[idx, 1] = jnp.where(row >= d, qp[d][1], 0.0)
    cr = jnp.zeros(shape, F32)
    ci = jnp.zeros(shape, F32)
    for r in range(SUBLANES):
        cr = jnp.where(row == r, qp[r + 1][0], cr)
        ci = jnp.where(row == r, qp[r + 1][1], ci)
    consts_ref[3, 0] = cr
    consts_ref[3, 1] = ci

    ct_re = _block_diag(c_ref[0])
    ct_im = _block_diag(c_ref[1])
    bb = jnp.concatenate([bbr, bbi], axis=1)
    taps = []
    for d in range(s + 1):
        er, ei = _cmul(ct_re, ct_im, *pw[d])
        cwt = jnp.concatenate([er, -ei], axis=1)
        if d >= 1:
            wck_ref[0:2 * half, (d - 1) * LANES:d * LANES] = cwt.T.astype(BF16)
        if d < s:
            taps.append(lax.dot_general(bb, cwt, (((1,), (1,)), ((), ())), preferred_element_type=F32,
                                        precision=lax.Precision.HIGHEST))
    zero = jnp.zeros((LANES, LANES), BF16)
    for m in range(s):
        for i in range(s):
            blk = taps[i - m].astype(BF16) if m <= i else zero
            wck_ref[2 * half + m * LANES:2 * half + (m + 1) * LANES, i * LANES:(i + 1) * LANES] = blk


def _s5_prep(a_re, a_im, log_dt, b_re, b_im, c_re, c_im):
    s = S5_TAPS
    ldt = jnp.broadcast_to(log_dt[:, None], (S5_GROUPS, S5_STATE))
    a3 = jnp.stack([a_re, a_im, ldt]).reshape(3, S5_NSTATE)
    bt = jnp.stack([b_re, b_im]).transpose(0, 1, 3, 2).reshape(2, S5_WIDTH, S5_STATE)
    ct = jnp.stack([c_re, c_im]).reshape(2, S5_WIDTH, S5_STATE)
    tile = lambda *shape: pl.BlockSpec((None,) + shape, lambda j: (j,) + (0,) * len(shape))
    return pl.pallas_call(
        _s5_prep_kernel,
        grid=(LANE_TILES,),
        in_specs=[pl.BlockSpec((3, TILE_STATES), lambda j: (0, j)),
                  pl.BlockSpec((2, LANES, S5_STATE), lambda j: (0, j, 0)),
                  pl.BlockSpec((2, LANES, S5_STATE), lambda j: (0, j, 0))],
        out_specs=(tile(s * LANES, 2 * TILE_STATES),
                   tile(2 * TILE_STATES + s * LANES, s * LANES),
                   tile(4, 2, SUBLANES, TILE_STATES)),
        out_shape=(jax.ShapeDtypeStruct((LANE_TILES, s * LANES, 2 * TILE_STATES), BF16),
                   jax.ShapeDtypeStruct((LANE_TILES, 2 * TILE_STATES + s * LANES, s * LANES), BF16),
                   jax.ShapeDtypeStruct((LANE_TILES, 4, 2, SUBLANES, TILE_STATES), F32)),
        compiler_params=pltpu.CompilerParams(
            dimension_semantics=("arbitrary",), vmem_limit_bytes=VMEM_LIMIT),
        name="s5_prep",
    )(a3, bt, ct)


def _mem_kv_kernel(mem_ref, g_ref, wk_ref, wv_ref, k_ref, v_ref):
    m = _rms(mem_ref[...], g_ref[...]).astype(BF16)
    k_ref[...] = _dot(m, wk_ref[...]).astype(BF16)
    v_ref[...] = _dot(m, wv_ref[...]).astype(BF16)


def _mem_kv(mem, g, wk, wv):
    B, M, D = mem.shape
    return pl.pallas_call(
        _mem_kv_kernel,
        grid=(B,),
        in_specs=[pl.BlockSpec((None, M, D), lambda b: (b, 0, 0)),
                  pl.BlockSpec((1, D), lambda b: (0, 0)),
                  pl.BlockSpec((D, D), lambda b: (0, 0)),
                  pl.BlockSpec((D, D), lambda b: (0, 0))],
        out_specs=(pl.BlockSpec((None, M, D), lambda b: (b, 0, 0)),
                   pl.BlockSpec((None, M, D), lambda b: (b, 0, 0))),
        out_shape=(jax.ShapeDtypeStruct((B, M, D), BF16),
                   jax.ShapeDtypeStruct((B, M, D), BF16)),
        compiler_params=pltpu.CompilerParams(
            dimension_semantics=("arbitrary",), vmem_limit_bytes=VMEM_LIMIT),
        name="mem_kv",
    )(mem, g, wk, wv)


def _proj_ret_kernel(x_ref, pos_ref, inv_ref, g_ref, w_ref, gn_ref, u_ref, gs_ref, ret_ref,
                     cur_ref, prev_ref, state_ref, decay_ref, qw_ref, kw_ref, *, tiles_per_seq):
    t = pl.program_id(0)

    @pl.when(t == 0)
    def _init():
        _retention_constants(decay_ref, qw_ref, kw_ref)
        cur_ref[...] = jnp.zeros_like(cur_ref)

    @pl.when((t == 0) | (t % tiles_per_seq == 1))
    def _reset():
        state_ref[...] = jnp.zeros_like(state_ref)

    prev_ref[...] = cur_ref[...]

    h = _rms(x_ref[...], g_ref[...]).astype(BF16)
    ang = pos_ref[...].astype(F32) * inv_ref[...]
    cos = jnp.cos(ang)
    sin = jnp.sin(ang)
    lane = lax.broadcasted_iota(jnp.int32, (1, LANES), 1)
    first_half = (lane % RET_DK) < (RET_DK // 2)
    sin_signed = jnp.where(first_half, -sin, sin)

    def rope(p):
        partner = jnp.where(first_half,
                            pltpu.roll(p, LANES - RET_DK // 2, 1),
                            pltpu.roll(p, RET_DK // 2, 1))
        return p * cos + partner * sin_signed

    def proj_q():
        pq = _dot(h, w_ref[:, COL_Q:COL_K])
        for i in range(RET_QK_WIDTH // LANES):
            sl = slice(i * LANES, (i + 1) * LANES)
            cur_ref[:, COL_Q + i * LANES:COL_Q + (i + 1) * LANES] = rope(pq[:, sl]).astype(BF16)

    def proj_k():
        pk = _dot(h, w_ref[:, COL_K:COL_V])
        for i in range(RET_QK_WIDTH // LANES):
            sl = slice(i * LANES, (i + 1) * LANES)
            cur_ref[:, COL_K + i * LANES:COL_K + (i + 1) * LANES] = (
                rope(pk[:, sl]) * (RET_DK ** -0.5)).astype(BF16)

    def proj_v():
        cur_ref[:, COL_V:COL_GR] = _dot(h, w_ref[:, COL_V:COL_GR]).astype(BF16)

    def proj_gate():
        cur_ref[:, COL_GR:COL_U] = _silu(_dot(h, w_ref[:, COL_GR:COL_U])).astype(BF16)

    def proj_u():
        pu = _dot(h, w_ref[:, COL_U:COL_GS]).astype(BF16)
        for j in range(LANE_TILES):
            u_ref[j] = pu[:, j * LANES:(j + 1) * LANES]

    def proj_gs():
        gs_ref[...] = _silu(_dot(h, w_ref[:, COL_GS:IN_COLS])).astype(BF16)

    ret_items = _retention_items(prev_ref, gn_ref, ret_ref, state_ref, decay_ref, qw_ref, kw_ref)
    proj_items = [proj_q, proj_k, proj_v, proj_gate, proj_u, proj_gs]
    per = -(-len(ret_items) // len(proj_items))
    for i, item in enumerate(proj_items):
        for r in ret_items[i * per:(i + 1) * per]:
            r()
        item()


def _proj_ret(x, pos3, inv, g, w, gn):
    B, L, D = x.shape
    tm = TM_PROJ
    nt = L // tm
    n = B * nt
    cur = lambda t: jnp.minimum(t, n - 1)
    lag = lambda t: jnp.maximum(t - 1, 0)
    row = lambda width, tile: pl.BlockSpec((None, tm, width), lambda t: (tile(t) // nt, tile(t) % nt, 0))
    const = lambda shape: pl.BlockSpec(shape, lambda t: (0,) * len(shape))
    return pl.pallas_call(
        functools.partial(_proj_ret_kernel, tiles_per_seq=nt),
        grid=(n + 1,),
        in_specs=[row(D, cur), row(1, cur), const((1, LANES)), const((1, D)), const((D, IN_COLS)),
                  const((1, RET_V_WIDTH))],
        out_specs=(pl.BlockSpec((None, LANE_TILES, tm, LANES), lambda t: (cur(t) // nt, 0, cur(t) % nt, 0)),
                   row(S5_WIDTH, cur), row(RET_V_WIDTH, lag)),
        out_shape=(jax.ShapeDtypeStruct((B, LANE_TILES, L, LANES), BF16),
                   jax.ShapeDtypeStruct((B, L, S5_WIDTH), BF16),
                   jax.ShapeDtypeStruct((B, L, RET_V_WIDTH), BF16)),
        scratch_shapes=[pltpu.VMEM((tm, COL_U), BF16),
                        pltpu.VMEM((tm, COL_U), BF16),
                        pltpu.VMEM((RET_HEADS // 2, LANES, 2 * RET_DV), F32),
                        pltpu.VMEM((RET_HEADS // 2, CHUNK, 2 * CHUNK), F32),
                        pltpu.VMEM((RET_HEADS // 2, CHUNK, 2 * RET_DV), F32),
                        pltpu.VMEM((RET_HEADS // 2, CHUNK, LANES), F32)],
        compiler_params=pltpu.CompilerParams(
            dimension_semantics=("arbitrary",), vmem_limit_bytes=VMEM_LIMIT),
        name="proj_ret",
    )(x, pos3, inv, g, w, gn)


def _retention_constants(decay_ref, qw_ref, kw_ref):
    i = lax.broadcasted_iota(jnp.int32, (CHUNK, 2 * CHUNK), 0).astype(F32)
    c = lax.broadcasted_iota(jnp.int32, (CHUNK, 2 * CHUNK), 1)
    j = (c % CHUNK).astype(F32)
    diff = i - j
    lane = lax.broadcasted_iota(jnp.int32, (CHUNK, LANES), 1)
    ik = lax.broadcasted_iota(jnp.int32, (CHUNK, LANES), 0).astype(F32)
    for p in range(RET_HEADS // 2):
        lg = jnp.where(c < CHUNK, LOG_G[2 * p], LOG_G[2 * p + 1])
        decay_ref[p] = jnp.where(diff >= 0.0, jnp.exp(lg * jnp.maximum(diff, 0.0)), 0.0)
        qw_ref[p] = jnp.exp(lg * (i + 1.0))
        lgk = jnp.where(lane < RET_DK, LOG_G[2 * p], LOG_G[2 * p + 1])
        kw_ref[p] = jnp.exp(lgk * (CHUNK - 1.0 - ik))


def _retention_items(src_ref, gn_ref, o_ref, state_ref, decay_ref, qw_ref, kw_ref):
    pairs = RET_HEADS // 2
    low = lax.broadcasted_iota(jnp.int32, (1, LANES), 1) < RET_DK
    srow = lax.broadcasted_iota(jnp.int32, (LANES, 2 * RET_DV), 0)
    scol = lax.broadcasted_iota(jnp.int32, (LANES, 2 * RET_DV), 1)
    own_block = (srow < RET_DK) == (scol < RET_DV)
    scol1 = lax.broadcasted_iota(jnp.int32, (1, 2 * RET_DV), 1)
    zero_v = jnp.zeros((CHUNK, RET_DV), BF16)

    def pair_step(c, p):
        rows = slice(c * CHUNK, (c + 1) * CHUNK)
        qt = src_ref[rows, COL_Q + p * LANES:COL_Q + (p + 1) * LANES]
        kt = src_ref[rows, COL_K + p * LANES:COL_K + (p + 1) * LANES]
        vp = src_ref[rows, COL_V + 2 * p * RET_DV:COL_V + 2 * (p + 1) * RET_DV]
        state = state_ref[p]
        cross = _dot(qt, state.astype(BF16))
        kw = (kt.astype(F32) * kw_ref[p]).astype(BF16)
        upd = lax.dot_general(kw, vp, (((0,), (0,)), ((), ())), preferred_element_type=F32)
        chunk_decay = jnp.where(scol1 < RET_DV,
                                math.exp(LOG_G[2 * p] * CHUNK), math.exp(LOG_G[2 * p + 1] * CHUNK))
        state_ref[p] = state * chunk_decay + jnp.where(own_block, upd, 0.0)

        zero_k = jnp.zeros_like(kt)
        k_rows = jnp.concatenate([jnp.where(low, kt, zero_k), jnp.where(low, zero_k, kt)], axis=0)
        s = lax.dot_general(qt, k_rows, (((1,), (1,)), ((), ())), preferred_element_type=F32)
        pm = (s * decay_ref[p]).astype(BF16)
        v_diag = jnp.concatenate(
            [jnp.concatenate([vp[:, 0:RET_DV], zero_v], axis=1),
             jnp.concatenate([zero_v, vp[:, RET_DV:2 * RET_DV]], axis=1)], axis=0)
        o2 = _dot(pm, v_diag) + qw_ref[p] * cross
        for e in range(2):
            h = 2 * p + e
            cols = slice(h * RET_DV, (h + 1) * RET_DV)
            o = o2[:, e * RET_DV:(e + 1) * RET_DV]
            mu = jnp.mean(o, axis=-1, keepdims=True)
            oc = o - mu
            var = jnp.mean(oc * oc, axis=-1, keepdims=True)
            on = oc * lax.rsqrt(var + EPS) * gn_ref[:, cols]
            gate = src_ref[rows, COL_GR + h * RET_DV:COL_GR + (h + 1) * RET_DV].astype(F32)
            o_ref[rows, cols] = (on * gate).astype(BF16)

    return [functools.partial(pair_step, c, p)
            for c in range(src_ref.shape[0] // CHUNK) for p in range(pairs)]


def _s5_scan_kernel(u_ref, wb_ref, consts_ref, wck_ref, d_ref, y_ref, *scratch, tiles_per_seq):
    uf_refs, lhs_refs, st_refs = scratch[0:3], scratch[3:6], scratch[6:9]
    sb_ref, ys_ref, carry_ref = scratch[9:]
    n = pl.program_id(0)
    s = S5_TAPS
    tm = u_ref.shape[0]
    nblk = tm // s
    half = TILE_STATES

    @pl.when(n == 0)
    def _init():
        for ref in scratch[0:9]:
            ref[...] = jnp.zeros_like(ref)

    @pl.when((n == 0) | ((n - 1) % tiles_per_seq == 0))
    def _reset():
        carry_ref[...] = jnp.zeros_like(carry_ref)

    ncol = 2 * LANES

    def increments(slot):
        uf_ref, lhs_ref, st_ref = uf_refs[slot], lhs_refs[slot], st_refs[slot]

        def piece(c):
            if c == 0:
                uf_ref[...] = u_ref[...].astype(F32)
                lhs_ref[...] = jnp.concatenate(
                    [uf_ref[pl.ds(k, nblk, stride=s), :] for k in range(s)], axis=1).astype(BF16)
            cols = slice(c * ncol, (c + 1) * ncol)
            st_ref[:, cols] = _dot(lhs_ref[...], wb_ref[:, cols])

        return [functools.partial(piece, c) for c in range(2 * half // ncol)]

    def scan(slot, row_pieces):
        st_ref = st_refs[slot]
        lw = S5_SCAN_LANES
        row = lax.broadcasted_iota(jnp.int32, (SUBLANES, lw), 0)
        last = SUBLANES - 1
        groups = nblk // SUBLANES
        per = -(-groups // row_pieces)

        def piece(q, i, carry):
            re = slice(q * lw, (q + 1) * lw)
            im = slice(half + q * lw, half + (q + 1) * lw)
            if i == 0:
                carry[:] = [carry_ref[:, re], carry_ref[:, im]]
            cr, ci = carry
            for r in range(i * per, min((i + 1) * per, groups)):
                rows = slice(r * SUBLANES, (r + 1) * SUBLANES)
                xr = st_ref[rows, re]
                xi = st_ref[rows, im]
                for idx, d in enumerate((1, 2, 4)):
                    pr, pi = _cmul(consts_ref[idx, 0, :, re], consts_ref[idx, 1, :, re],
                                   pltpu.roll(xr, d, 0), pltpu.roll(xi, d, 0))
                    xr, xi = xr + pr, xi + pi
                pr, pi = _cmul(consts_ref[3, 0, :, re], consts_ref[3, 1, :, re], cr, ci)
                xr, xi = xr + pr, xi + pi
                st_ref[rows, re] = jnp.where(row == 0, cr, pltpu.roll(xr, 1, 0))
                st_ref[rows, im] = jnp.where(row == 0, ci, pltpu.roll(xi, 1, 0))
                cr = jnp.broadcast_to(xr[last:, :], (SUBLANES, lw))
                ci = jnp.broadcast_to(xi[last:, :], (SUBLANES, lw))
            carry[:] = [cr, ci]
            if i == row_pieces - 1:
                carry_ref[:, re] = cr
                carry_ref[:, im] = ci

        pieces = []
        for q in range(half // lw):
            carry = [None, None]
            pieces += [functools.partial(piece, q, i, carry) for i in range(row_pieces)]
        return pieces

    def outputs(slot):
        taps_per_piece = ncol // LANES

        def cast():
            sb_ref[...] = st_refs[slot][...].astype(BF16)

        def piece(c):
            lhs = jnp.concatenate([sb_ref[...], lhs_refs[slot][...]], axis=1)
            yall = _dot(lhs, wck_ref[:, c * ncol:(c + 1) * ncol])
            for i in range(taps_per_piece):
                tap = c * taps_per_piece + i
                ys_ref[pl.ds(tap, nblk, stride=s), :] = yall[:, i * LANES:(i + 1) * LANES]

        def finish():
            y = ys_ref[...] + d_ref[...] * uf_refs[slot][...]
            y_ref[...] = jax.nn.gelu(y).astype(BF16)

        return [cast] + [functools.partial(piece, c) for c in range(s * LANES // ncol)] + [finish]

    for phase in range(3):
        @pl.when(n % 3 == phase)
        def _steps(phase=phase):
            mxu = increments(phase) + outputs((phase + 1) % 3)
            vpu = scan((phase + 2) % 3, 3)
            for i, piece in enumerate(mxu):
                piece()
                if i < len(vpu):
                    vpu[i]()
            for piece in vpu[len(mxu):]:
                piece()


def _s5_scan(u, wb, wck, consts, d):
    B, _, L, _ = u.shape
    s = S5_TAPS
    tm = TM_S5
    nt = L // tm
    ntiles = LANE_TILES * B * nt
    tile_of = lambda n, lag: jnp.clip(n - lag, 0, ntiles - 1)
    lane_tile = lambda n, lag: tile_of(n, lag) // (B * nt)

    def tok(lag):
        def index(n):
            i = tile_of(n, lag)
            return ((i // nt) % B, i // (B * nt), i % nt, 0)
        return pl.BlockSpec((None, None, tm, LANES), index)

    def par(lag, *shape):
        return pl.BlockSpec((None,) + shape, lambda n: (lane_tile(n, lag),) + (0,) * len(shape))

    nblk = tm // s
    return pl.pallas_call(
        functools.partial(_s5_scan_kernel, tiles_per_seq=nt),
        grid=(ntiles + 2,),
        in_specs=[tok(0), par(0, s * LANES, 2 * TILE_STATES), par(1, 4, 2, SUBLANES, TILE_STATES),
                  par(2, 2 * TILE_STATES + s * LANES, s * LANES), par(2, 1, LANES)],
        out_specs=tok(2),
        out_shape=jax.ShapeDtypeStruct((B, LANE_TILES, L, LANES), BF16),
        scratch_shapes=([pltpu.VMEM((tm, LANES), F32)] * 3
                        + [pltpu.VMEM((nblk, s * LANES), BF16)] * 3
                        + [pltpu.VMEM((nblk, 2 * TILE_STATES), F32)] * 3
                        + [pltpu.VMEM((nblk, 2 * TILE_STATES), BF16), pltpu.VMEM((tm, LANES), F32),
                           pltpu.VMEM((SUBLANES, 2 * TILE_STATES), F32)]),
        compiler_params=pltpu.CompilerParams(
            dimension_semantics=("arbitrary",), vmem_limit_bytes=VMEM_LIMIT),
        name="s5_scan",
    )(u, wb, consts, wck, d)


def _out_xattn_kernel(x_ref, ret_ref, y_ref, gs_ref, gw_ref, gb_ref, wout_ref, g2_ref, wq_ref,
                      ka_ref, va_ref, wo_ref, gf_ref, o_ref):
    y = jnp.concatenate([y_ref[j] for j in range(LANE_TILES)], axis=1)
    z = _dot(y, gw_ref[...]) + gb_ref[...]
    ssm = (y.astype(F32) * _sigmoid(z) * gs_ref[...].astype(F32)).astype(BF16)
    x1 = (x_ref[...] + _dot(ret_ref[...], wout_ref[0:RET_V_WIDTH, :])
          + _dot(ssm, wout_ref[RET_V_WIDTH:RET_V_WIDTH + S5_WIDTH, :]))
    h2 = _rms(x1, g2_ref[...]).astype(BF16)
    qa = (_dot(h2, wq_ref[...]) * (XA_DH ** -0.5)).astype(BF16)
    heads = []
    for h in range(XA_HEADS):
        cols = slice(h * XA_DH, (h + 1) * XA_DH)
        s = lax.dot_general(qa[:, cols], ka_ref[:, cols], (((1,), (1,)), ((), ())),
                            preferred_element_type=F32)
        e = jnp.exp(s - jnp.max(s, axis=-1, keepdims=True))
        l = jnp.sum(e, axis=-1, keepdims=True)
        heads.append((_dot(e.astype(BF16), va_ref[:, cols]) * (1.0 / l)).astype(BF16))
    o = jnp.concatenate(heads, axis=-1)
    x2 = x1 + _dot(o, wo_ref[...])
    o_ref[...] = _rms(x2, gf_ref[...])


def _out_xattn(x, ret, y, gs, gw, gb, wout, g2, wq, ka, va, wo, gf):
    B, L, D = x.shape
    tm = TM_OUT
    row = pl.BlockSpec((None, tm, D), lambda b, t: (b, t, 0))
    const = lambda shape: pl.BlockSpec(shape, lambda b, t: (0,) * len(shape))
    mem = pl.BlockSpec((None, MEM_LEN, D), lambda b, t: (b, 0, 0))
    ytile = pl.BlockSpec((None, LANE_TILES, tm, LANES), lambda b, t: (b, 0, t, 0))
    return pl.pallas_call(
        _out_xattn_kernel,
        grid=(B, L // tm),
        in_specs=[row, row, ytile, row, const((D, D)), const((1, D)), const(wout.shape), const((1, D)),
                  const((D, D)), mem, mem, const((D, D)), const((1, D))],
        out_specs=row,
        out_shape=jax.ShapeDtypeStruct((B, L, D), F32),
        compiler_params=pltpu.CompilerParams(
            dimension_semantics=("arbitrary", "arbitrary"), vmem_limit_bytes=VMEM_LIMIT),
        name="out_xattn",
    )(x, ret, y, gs, gw, gb, wout, g2, wq, ka, va, wo, gf)


def _rope_inv_lanes():
    half = RET_DK // 2
    inv = ROPE_BASE ** (-np.arange(half, dtype=np.float64) / half)
    return jnp.asarray(np.tile(inv, LANES // half).reshape(1, LANES), F32)


def kernel(x, mem, positions, norm1_g, w_in, ret_gn_g, s5_a_re, s5_a_im, s5_log_dt, s5_b_re, s5_b_im, s5_c_re, s5_c_im, s5_d, s5_glu_w, s5_glu_b, w_out, norm2_g, norm_mem_g, xa_wq, xa_wk, xa_wv, xa_wo, norm_f_g):
    B, L, D = x.shape
    l = 0
    bf = lambda w: w.astype(BF16)
    rowvec = lambda v: v.reshape(1, -1)

    wb, wck, consts = _s5_prep(s5_a_re[l], s5_a_im[l], s5_log_dt[l], s5_b_re[l], s5_b_im[l],
                               s5_c_re[l], s5_c_im[l])
    ka, va = _mem_kv(mem, rowvec(norm_mem_g[l]), bf(xa_wk[l]), bf(xa_wv[l]))
    u, g_s5, ret = _proj_ret(x, positions.reshape(B, L, 1), _rope_inv_lanes(),
                             rowvec(norm1_g[l]), bf(w_in[l]), rowvec(ret_gn_g[l]))
    y = _s5_scan(u, wb, wck, consts, s5_d[l].reshape(LANE_TILES, 1, LANES))
    return _out_xattn(x, ret, y, g_s5, bf(s5_glu_w[l]), rowvec(s5_glu_b[l]), bf(w_out[l]),
                      rowvec(norm2_g[l]), bf(xa_wq[l]), ka, va, bf(xa_wo[l]), rowvec(norm_f_g))
```

```python
import functools
import math

import numpy as np
import jax
import jax.numpy as jnp
from jax import lax
from jax.experimental import pallas as pl
from jax.experimental.pallas import tpu as pltpu

F32 = jnp.float32
BF16 = jnp.bfloat16

D_MODEL = 1024
MEM_LEN = 256
EPS = 1e-6
ROPE_BASE = 10000.0

RET_HEADS = 8
RET_QK_WIDTH = 512
RET_V_WIDTH = 1024
RET_DK = 64
RET_DV = 128
CHUNK = 128

S5_WIDTH = 1024
S5_GROUP = 16
S5_GROUPS = 64
S5_STATE = 64
S5_NSTATE = S5_GROUPS * S5_STATE

IN_COLS = 5120
COL_Q, COL_K, COL_V, COL_GR, COL_U, COL_GS = 0, 512, 1024, 2048, 3072, 4096

XA_HEADS = 4
XA_DH = 256

LANES = 128
SUBLANES = 8
VMEM_LIMIT = 48 * 1024 * 1024

LANE_TILES = S5_WIDTH // LANES
TILE_GROUPS = LANES // S5_GROUP
TILE_STATES = S5_NSTATE // LANE_TILES
S5_TAPS = 4
S5_SCAN_LANES = 2 * LANES

LOG_G = tuple(math.log1p(-(2.0 ** (-5.0 - h))) for h in range(RET_HEADS))

TM_PROJ = 512
TM_S5 = 2048
TM_OUT = 512


def _rms(x, g):
    ms = jnp.mean(x * x, axis=-1, keepdims=True)
    return x * lax.rsqrt(ms + EPS) * g


def _sigmoid(z):
    return 1.0 / (1.0 + jnp.exp(-z))


def _silu(g):
    return g * _sigmoid(g)


def _dot(a, b):
    return jnp.dot(a, b, preferred_element_type=F32)


def _cmul(ar, ai, br, bi):
    return ar * br - ai * bi, ar * bi + ai * br


def _s5_discretise(ar, ai, ldt):
    dt = jnp.exp(ldt)
    mag = jnp.exp(ar * dt)
    p_re = mag * jnp.cos(ai * dt)
    p_im = mag * jnp.sin(ai * dt)
    den = ar * ar + ai * ai
    nr, ni = p_re - 1.0, p_im
    f_re = (nr * ar + ni * ai) / den
    f_im = (ni * ar - nr * ai) / den
    return p_re, p_im, f_re, f_im


def _powers(p_re, p_im, n):
    pw = [(jnp.ones_like(p_re), jnp.zeros_like(p_im))]
    for _ in range(n):
        pw.append(_cmul(pw[-1][0], pw[-1][1], p_re, p_im))
    return pw


def _block_diag(x):
    tiled = jnp.concatenate([x] * TILE_GROUPS, axis=1)
    r = lax.broadcasted_iota(jnp.int32, tiled.shape, 0)
    c = lax.broadcasted_iota(jnp.int32, tiled.shape, 1)
    return jnp.where(r // S5_GROUP == c // S5_STATE, tiled, 0.0)


def _s5_prep_kernel(a_ref, b_ref, c_ref, wb_ref, wck_ref, consts_ref):
    s = S5_TAPS
    half = TILE_STATES
    p_re, p_im, f_re, f_im = _s5_discretise(a_ref[0:1, :], a_ref[1:2, :], a_ref[2:3, :])
    pw = _powers(p_re, p_im, s)
    bbr, bbi = _cmul(f_re, f_im, _block_diag(b_ref[0]), _block_diag(b_ref[1]))
    for k in range(s):
        wr, wi = _cmul(bbr, bbi, *pw[s - 1 - k])
        wb_ref[k * LANES:(k + 1) * LANES, 0:half] = wr.astype(BF16)
        wb_ref[k * LANES:(k + 1) * LANES, half:2 * half] = wi.astype(BF16)

    shape = (SUBLANES, half)
    qp = _powers(pw[s][0], pw[s][1], SUBLANES)
    row = lax.broadcasted_iota(jnp.int32, shape, 0)
    for idx, d in enumerate((1, 2, 4)):
        consts_ref[idx, 0] = jnp.where(row >= d, qp[d][0], 0.0)
        consts_ref[idx, 1] = jnp.where(row >= d, qp[d][1], 0.0)
    cr = jnp.zeros(shape, F32)
    ci = jnp.zeros(shape, F32)
    for r in range(SUBLANES):
        cr = jnp.where(row == r, qp[r + 1][0], cr)
        ci = jnp.where(row == r, qp[r + 1][1], ci)
    consts_ref[3, 0] = cr
    consts_ref[3, 1] = ci

    ct_re = _block_diag(c_ref[0])
    ct_im = _block_diag(c_ref[1])
    bb = jnp.concatenate([bbr, bbi], axis=1)
    taps = []
    for d in range(s + 1):
        er, ei = _cmul(ct_re, ct_im, *pw[d])
        cwt = jnp.concatenate([er, -ei], axis=1)
        if d >= 1:
            wck_ref[0:2 * half, (d - 1) * LANES:d * LANES] = cwt.T.astype(BF16)
        if d < s:
            taps.append(lax.dot_general(bb, cwt, (((1,), (1,)), ((), ())), preferred_element_type=F32,
                                        precision=lax.Precision.HIGHEST))
    zero = jnp.zeros((LANES, LANES), BF16)
    for m in range(s):
        for i in range(s):
            blk = taps[i - m].astype(BF16) if m <= i else zero
            wck_ref[2 * half + m * LANES:2 * half + (m + 1) * LANES, i * LANES:(i + 1) * LANES] = blk


def _s5_prep(a_re, a_im, log_dt, b_re, b_im, c_re, c_im):
    s = S5_TAPS
    ldt = jnp.broadcast_to(log_dt[:, None], (S5_GROUPS, S5_STATE))
    a3 = jnp.stack([a_re, a_im, ldt]).reshape(3, S5_NSTATE)
    bt = jnp.stack([b_re, b_im]).transpose(0, 1, 3, 2).reshape(2, S5_WIDTH, S5_STATE)
    ct = jnp.stack([c_re, c_im]).reshape(2, S5_WIDTH, S5_STATE)
    tile = lambda *shape: pl.BlockSpec((None,) + shape, lambda j: (j,) + (0,) * len(shape))
    return pl.pallas_call(
        _s5_prep_kernel,
        grid=(LANE_TILES,),
        in_specs=[pl.BlockSpec((3, TILE_STATES), lambda j: (0, j)),
                  pl.BlockSpec((2, LANES, S5_STATE), lambda j: (0, j, 0)),
                  pl.BlockSpec((2, LANES, S5_STATE), lambda j: (0, j, 0))],
        out_specs=(tile(s * LANES, 2 * TILE_STATES),
                   tile(2 * TILE_STATES + s * LANES, s * LANES),
                   tile(4, 2, SUBLANES, TILE_STATES)),
        out_shape=(jax.ShapeDtypeStruct((LANE_TILES, s * LANES, 2 * TILE_STATES), BF16),
                   jax.ShapeDtypeStruct((LANE_TILES, 2 * TILE_STATES + s * LANES, s * LANES), BF16),
                   jax.ShapeDtypeStruct((LANE_TILES, 4, 2, SUBLANES, TILE_STATES), F32)),
        compiler_params=pltpu.CompilerParams(
            dimension_semantics=("arbitrary",), vmem_limit_bytes=VMEM_LIMIT),
        name="s5_prep",
    )(a3, bt, ct)


def _mem_kv_kernel(mem_ref, g_ref, wk_ref, wv_ref, k_ref, v_ref):
    m = _rms(mem_ref[...], g_ref[...]).astype(BF16)
    k_ref[...] = _dot(m, wk_ref[...]).astype(BF16)
    v_ref[...] = _dot(m, wv_ref[...]).astype(BF16)


def _mem_kv(mem, g, wk, wv):
    B, M, D = mem.shape
    return pl.pallas_call(
        _mem_kv_kernel,
        grid=(B,),
        in_specs=[pl.BlockSpec((None, M, D), lambda b: (b, 0, 0)),
                  pl.BlockSpec((1, D), lambda b: (0, 0)),
                  pl.BlockSpec((D, D), lambda b: (0, 0)),
                  pl.BlockSpec((D, D), lambda b: (0, 0))],
        out_specs=(pl.BlockSpec((None, M, D), lambda b: (b, 0, 0)),
                   pl.BlockSpec((None, M, D), lambda b: (b, 0, 0))),
        out_shape=(jax.ShapeDtypeStruct((B, M, D), BF16),
                   jax.ShapeDtypeStruct((B, M, D), BF16)),
        compiler_params=pltpu.CompilerParams(
            dimension_semantics=("arbitrary",), vmem_limit_bytes=VMEM_LIMIT),
        name="mem_kv",
    )(mem, g, wk, wv)


def _proj_ret_kernel(x_ref, pos_ref, inv_ref, g_ref, w_ref, gn_ref, u_ref, gs_ref, ret_ref,
                     cur_ref, prev_ref, state_ref, decay_ref, qw_ref, kw_ref, *, tiles_per_seq):
    t = pl.program_id(0)

    @pl.when(t == 0)
    def _init():
        _retention_constants(decay_ref, qw_ref, kw_ref)
        cur_ref[...] = jnp.zeros_like(cur_ref)

    @pl.when((t == 0) | (t % tiles_per_seq == 1))
    def _reset():
        state_ref[...] = jnp.zeros_like(state_ref)

    prev_ref[...] = cur_ref[...]

    h = _rms(x_ref[...], g_ref[...]).astype(BF16)
    ang = pos_ref[...].astype(F32) * inv_ref[...]
    cos = jnp.cos(ang)
    sin = jnp.sin(ang)
    lane = lax.broadcasted_iota(jnp.int32, (1, LANES), 1)
    first_half = (lane % RET_DK) < (RET_DK // 2)
    sin_signed = jnp.where(first_half, -sin, sin)

    def rope(p):
        partner = jnp.where(first_half,
                            pltpu.roll(p, LANES - RET_DK // 2, 1),
                            pltpu.roll(p, RET_DK // 2, 1))
        return p * cos + partner * sin_signed

    def proj_q():
        pq = _dot(h, w_ref[:, COL_Q:COL_K])
        for i in range(RET_QK_WIDTH // LANES):
            sl = slice(i * LANES, (i + 1) * LANES)
            cur_ref[:, COL_Q + i * LANES:COL_Q + (i + 1) * LANES] = rope(pq[:, sl]).astype(BF16)

    def proj_k():
        pk = _dot(h, w_ref[:, COL_K:COL_V])
        for i in range(RET_QK_WIDTH // LANES):
            sl = slice(i * LANES, (i + 1) * LANES)
            cur_ref[:, COL_K + i * LANES:COL_K + (i + 1) * LANES] = (
                rope(pk[:, sl]) * (RET_DK ** -0.5)).astype(BF16)

    def proj_v():
        cur_ref[:, COL_V:COL_GR] = _dot(h, w_ref[:, COL_V:COL_GR]).astype(BF16)

    def proj_gate():
        cur_ref[:, COL_GR:COL_U] = _silu(_dot(h, w_ref[:, COL_GR:COL_U])).astype(BF16)

    def proj_u():
        pu = _dot(h, w_ref[:, COL_U:COL_GS]).astype(BF16)
        for j in range(LANE_TILES):
            u_ref[j] = pu[:, j * LANES:(j + 1) * LANES]

    def proj_gs():
        gs_ref[...] = _silu(_dot(h, w_ref[:, COL_GS:IN_COLS])).astype(BF16)

    ret_items = _retention_items(prev_ref, gn_ref, ret_ref, state_ref, decay_ref, qw_ref, kw_ref)
    proj_items = [proj_q, proj_k, proj_v, proj_gate, proj_u, proj_gs]
    per = -(-len(ret_items) // len(proj_items))
    for i, item in enumerate(proj_items):
        for r in ret_items[i * per:(i + 1) * per]:
            r()
        item()


def _proj_ret(x, pos3, inv, g, w, gn):
    B, L, D = x.shape
    tm = TM_PROJ
    nt = L // tm
    n = B * nt
    cur = lambda t: jnp.minimum(t, n - 1)
    lag = lambda t: jnp.maximum(t - 1, 0)
    row = lambda width, tile: pl.BlockSpec((None, tm, width), lambda t: (tile(t) // nt, tile(t) % nt, 0))
    const = lambda shape: pl.BlockSpec(shape, lambda t: (0,) * len(shape))
    return pl.pallas_call(
        functools.partial(_proj_ret_kernel, tiles_per_seq=nt),
        grid=(n + 1,),
        in_specs=[row(D, cur), row(1, cur), const((1, LANES)), const((1, D)), const((D, IN_COLS)),
                  const((1, RET_V_WIDTH))],
        out_specs=(pl.BlockSpec((None, LANE_TILES, tm, LANES), lambda t: (cur(t) // nt, 0, cur(t) % nt, 0)),
                   row(S5_WIDTH, cur), row(RET_V_WIDTH, lag)),
        out_shape=(jax.ShapeDtypeStruct((B, LANE_TILES, L, LANES), BF16),
                   jax.ShapeDtypeStruct((B, L, S5_WIDTH), BF16),
                   jax.ShapeDtypeStruct((B, L, RET_V_WIDTH), BF16)),
        scratch_shapes=[pltpu.VMEM((tm, COL_U), BF16),
                        pltpu.VMEM((tm, COL_U), BF16),
                        pltpu.VMEM((RET_HEADS // 2, LANES, 2 * RET_DV), F32),
                        pltpu.VMEM((RET_HEADS // 2, CHUNK, 2 * CHUNK), F32),
                        pltpu.VMEM((RET_HEADS // 2, CHUNK, 2 * RET_DV), F32),
                        pltpu.VMEM((RET_HEADS // 2, CHUNK, LANES), F32)],
        compiler_params=pltpu.CompilerParams(
            dimension_semantics=("arbitrary",), vmem_limit_bytes=VMEM_LIMIT),
        name="proj_ret",
    )(x, pos3, inv, g, w, gn)


def _retention_constants(decay_ref, qw_ref, kw_ref):
    i = lax.broadcasted_iota(jnp.int32, (CHUNK, 2 * CHUNK), 0).astype(F32)
    c = lax.broadcasted_iota(jnp.int32, (CHUNK, 2 * CHUNK), 1)
    j = (c % CHUNK).astype(F32)
    diff = i - j
    lane = lax.broadcasted_iota(jnp.int32, (CHUNK, LANES), 1)
    ik = lax.broadcasted_iota(jnp.int32, (CHUNK, LANES), 0).astype(F32)
    for p in range(RET_HEADS // 2):
        lg = jnp.where(c < CHUNK, LOG_G[2 * p], LOG_G[2 * p + 1])
        decay_ref[p] = jnp.where(diff >= 0.0, jnp.exp(lg * jnp.maximum(diff, 0.0)), 0.0)
        qw_ref[p] = jnp.exp(lg * (i + 1.0))
        lgk = jnp.where(lane < RET_DK, LOG_G[2 * p], LOG_G[2 * p + 1])
        kw_ref[p] = jnp.exp(lgk * (CHUNK - 1.0 - ik))


def _retention_items(src_ref, gn_ref, o_ref, state_ref, decay_ref, qw_ref, kw_ref):
    pairs = RET_HEADS // 2
    low = lax.broadcasted_iota(jnp.int32, (1, LANES), 1) < RET_DK
    srow = lax.broadcasted_iota(jnp.int32, (LANES, 2 * RET_DV), 0)
    scol = lax.broadcasted_iota(jnp.int32, (LANES, 2 * RET_DV), 1)
    own_block = (srow < RET_DK) == (scol < RET_DV)
    scol1 = lax.broadcasted_iota(jnp.int32, (1, 2 * RET_DV), 1)
    zero_v = jnp.zeros((CHUNK, RET_DV), BF16)

    def pair_step(c, p):
        rows = slice(c * CHUNK, (c + 1) * CHUNK)
        qt = src_ref[rows, COL_Q + p * LANES:COL_Q + (p + 1) * LANES]
        kt = src_ref[rows, COL_K + p * LANES:COL_K + (p + 1) * LANES]
        vp = src_ref[rows, COL_V + 2 * p * RET_DV:COL_V + 2 * (p + 1) * RET_DV]
        state = state_ref[p]
        cross = _dot(qt, state.astype(BF16))
        kw = (kt.astype(F32) * kw_ref[p]).astype(BF16)
        upd = lax.dot_general(kw, vp, (((0,), (0,)), ((), ())), preferred_element_type=F32)
        chunk_decay = jnp.where(scol1 < RET_DV,
                                math.exp(LOG_G[2 * p] * CHUNK), math.exp(LOG_G[2 * p + 1] * CHUNK))
        state_ref[p] = state * chunk_decay + jnp.where(own_block, upd, 0.0)

        zero_k = jnp.zeros_like(kt)
        k_rows = jnp.concatenate([jnp.where(low, kt, zero_k), jnp.where(low, zero_k, kt)], axis=0)
        s = lax.dot_general(qt, k_rows, (((1,), (1,)), ((), ())), preferred_element_type=F32)
        pm = (s * decay_ref[p]).astype(BF16)
        v_diag = jnp.concatenate(
            [jnp.concatenate([vp[:, 0:RET_DV], zero_v], axis=1),
             jnp.concatenate([zero_v, vp[:, RET_DV:2 * RET_DV]], axis=1)], axis=0)
        o2 = _dot(pm, v_diag) + qw_ref[p] * cross
        for e in range(2):
            h = 2 * p + e
            cols = slice(h * RET_DV, (h + 1) * RET_DV)
            o = o2[:, e * RET_DV:(e + 1) * RET_DV]
            mu = jnp.mean(o, axis=-1, keepdims=True)
            oc = o - mu
            var = jnp.mean(oc * oc, axis=-1, keepdims=True)
            on = oc * lax.rsqrt(var + EPS) * gn_ref[:, cols]
            gate = src_ref[rows, COL_GR + h * RET_DV:COL_GR + (h + 1) * RET_DV].astype(F32)
            o_ref[rows, cols] = (on * gate).astype(BF16)

    return [functools.partial(pair_step, c, p)
            for c in range(src_ref.shape[0] // CHUNK) for p in range(pairs)]


S5_SLOTS = 4


def _s5_scan_kernel(u0_ref, u_ref, wb_ref, consts_ref, wck_ref, d_ref, y_ref, *scratch, tiles_per_seq):
    ns = S5_SLOTS
    uf_refs, lhs_refs, st_refs, sb_refs, ys_refs = (scratch[i * ns:(i + 1) * ns] for i in range(5))
    carry_ref = scratch[5 * ns]
    n = pl.program_id(0)
    s = S5_TAPS
    tm = u_ref.shape[0]
    nblk = tm // s
    half = TILE_STATES
    ncol = 2 * LANES

    nchunk = 8
    cblk = nblk // nchunk

    def load(src_ref, slot):
        uf_ref = uf_refs[slot]

        def chunk(c):
            tok = slice(c * cblk * s, (c + 1) * cblk * s)
            uf_ref[tok, :] = src_ref[tok, :].astype(F32)
            lhs_refs[slot][c * cblk:(c + 1) * cblk, :] = jnp.concatenate(
                [uf_ref[pl.ds(c * cblk * s + k, cblk, stride=s), :] for k in range(s)], axis=1).astype(BF16)

        return [functools.partial(chunk, c) for c in range(nchunk)]

    def gelu(slot):
        def chunk(c):
            tok = slice(c * cblk * s, (c + 1) * cblk * s)
            y_ref[tok, :] = jax.nn.gelu(ys_refs[slot][tok, :]).astype(BF16)

        return [functools.partial(chunk, c) for c in range(nchunk)]

    @pl.when(n == 0)
    def _init():
        for ref in scratch[0:5 * ns]:
            ref[...] = jnp.zeros_like(ref)
        for chunk in load(u0_ref, 0):
            chunk()

    @pl.when((n == 0) | ((n - 1) % tiles_per_seq == 0))
    def _reset():
        carry_ref[...] = jnp.zeros_like(carry_ref)

    def increments(slot):
        def piece(c):
            cols = slice(c * ncol, (c + 1) * ncol)
            st_refs[slot][:, cols] = _dot(lhs_refs[slot][...], wb_ref[:, cols])

        return [functools.partial(piece, c) for c in range(2 * half // ncol)]

    def scan(slot, row_pieces):
        st_ref, sb_ref = st_refs[slot], sb_refs[slot]
        lw = S5_SCAN_LANES
        row = lax.broadcasted_iota(jnp.int32, (SUBLANES, lw), 0)
        last = SUBLANES - 1
        groups = nblk // SUBLANES
        per = -(-groups // (2 * row_pieces)) * 2

        def piece(q, i, carry):
            re = slice(q * lw, (q + 1) * lw)
            im = slice(half + q * lw, half + (q + 1) * lw)
            if i == 0:
                carry[:] = [carry_ref[:, re], carry_ref[:, im]]
            cr, ci = carry
            for r2 in range(i * per, min((i + 1) * per, groups), 2):
                enter_re, enter_im = [], []
                for r in (r2, r2 + 1):
                    rows = slice(r * SUBLANES, (r + 1) * SUBLANES)
                    xr = st_ref[rows, re]
                    xi = st_ref[rows, im]
                    for idx, d in enumerate((1, 2, 4)):
                        pr, pi = _cmul(consts_ref[idx, 0, :, re], consts_ref[idx, 1, :, re],
                                       pltpu.roll(xr, d, 0), pltpu.roll(xi, d, 0))
                        xr, xi = xr + pr, xi + pi
                    pr, pi = _cmul(consts_ref[3, 0, :, re], consts_ref[3, 1, :, re], cr, ci)
                    xr, xi = xr + pr, xi + pi
                    enter_re.append(jnp.where(row == 0, cr, pltpu.roll(xr, 1, 0)))
                    enter_im.append(jnp.where(row == 0, ci, pltpu.roll(xi, 1, 0)))
                    cr = jnp.broadcast_to(xr[last:, :], (SUBLANES, lw))
                    ci = jnp.broadcast_to(xi[last:, :], (SUBLANES, lw))
                rows2 = slice(r2 * SUBLANES, (r2 + 2) * SUBLANES)
                sb_ref[rows2, re] = jnp.concatenate(enter_re, axis=0).astype(BF16)
                sb_ref[rows2, im] = jnp.concatenate(enter_im, axis=0).astype(BF16)
            carry[:] = [cr, ci]
            if i == row_pieces - 1:
                carry_ref[:, re] = cr
                carry_ref[:, im] = ci

        pieces = []
        for q in range(half // lw):
            carry = [None, None]
            pieces += [functools.partial(piece, q, i, carry) for i in range(row_pieces)]
        return pieces

    def outputs(slot):
        taps_per_piece = ncol // LANES
        ys_ref = ys_refs[slot]

        def piece(c):
            lhs = jnp.concatenate([sb_refs[slot][...], lhs_refs[slot][...]], axis=1)
            yall = _dot(lhs, wck_ref[:, c * ncol:(c + 1) * ncol])
            for i in range(taps_per_piece):
                tap = c * taps_per_piece + i
                rows = pl.ds(tap, nblk, stride=s)
                ys_ref[rows, :] = yall[:, i * LANES:(i + 1) * LANES] + d_ref[...] * uf_refs[slot][rows, :]

        return [functools.partial(piece, c) for c in range(s * LANES // ncol)]

    for phase in range(ns):
        @pl.when(n % ns == phase)
        def _steps(phase=phase):
            mxu = increments(phase) + outputs((phase - 2) % ns)
            vpu = scan((phase - 1) % ns, 3)
            extra = [c for pair in zip(load(u_ref, (phase + 1) % ns), gelu((phase - 3) % ns)) for c in pair]
            rounds = max(len(mxu), len(vpu))
            per = -(-len(extra) // rounds)
            for i in range(rounds):
                for piece in mxu[i:i + 1] + vpu[i:i + 1] + extra[i * per:(i + 1) * per]:
                    piece()


def _s5_scan(u, wb, wck, consts, d):
    B, _, L, _ = u.shape
    s = S5_TAPS
    tm = TM_S5
    nt = L // tm
    ntiles = LANE_TILES * B * nt
    tile_of = lambda n, lag: jnp.clip(n - lag, 0, ntiles - 1)
    lane_tile = lambda n, lag: tile_of(n, lag) // (B * nt)

    def tok(lag):
        def index(n):
            i = tile_of(n, lag)
            return ((i // nt) % B, i // (B * nt), i % nt, 0)
        return pl.BlockSpec((None, None, tm, LANES), index)

    def par(lag, *shape):
        return pl.BlockSpec((None,) + shape, lambda n: (lane_tile(n, lag),) + (0,) * len(shape))

    nblk = tm // s
    ns = S5_SLOTS
    return pl.pallas_call(
        functools.partial(_s5_scan_kernel, tiles_per_seq=nt),
        grid=(ntiles + 3,),
        in_specs=[pl.BlockSpec((None, None, tm, LANES), lambda n: (0, 0, 0, 0)), tok(-1),
                  par(0, s * LANES, 2 * TILE_STATES), par(1, 4, 2, SUBLANES, TILE_STATES),
                  par(2, 2 * TILE_STATES + s * LANES, s * LANES), par(2, 1, LANES)],
        out_specs=tok(3),
        out_shape=jax.ShapeDtypeStruct((B, LANE_TILES, L, LANES), BF16),
        scratch_shapes=([pltpu.VMEM((tm, LANES), F32)] * ns
                        + [pltpu.VMEM((nblk, s * LANES), BF16)] * ns
                        + [pltpu.VMEM((nblk, 2 * TILE_STATES), F32)] * ns
                        + [pltpu.VMEM((nblk, 2 * TILE_STATES), BF16)] * ns
                        + [pltpu.VMEM((tm, LANES), F32)] * ns
                        + [pltpu.VMEM((SUBLANES, 2 * TILE_STATES), F32)]),
        compiler_params=pltpu.CompilerParams(
            dimension_semantics=("arbitrary",), vmem_limit_bytes=VMEM_LIMIT),
        name="s5_scan",
    )(u, u, wb, consts, wck, d)


def _out_xattn_kernel(x_ref, ret_ref, y_ref, gs_ref, gw_ref, gb_ref, wout_ref, g2_ref, wq_ref,
                      ka_ref, va_ref, wo_ref, gf_ref, o_ref):
    y = jnp.concatenate([y_ref[j] for j in range(LANE_TILES)], axis=1)
    z = _dot(y, gw_ref[...]) + gb_ref[...]
    ssm = (y.astype(F32) * _sigmoid(z) * gs_ref[...].astype(F32)).astype(BF16)
    x1 = (x_ref[...] + _dot(ret_ref[...], wout_ref[0:RET_V_WIDTH, :])
          + _dot(ssm, wout_ref[RET_V_WIDTH:RET_V_WIDTH + S5_WIDTH, :]))
    h2 = _rms(x1, g2_ref[...]).astype(BF16)
    qa = (_dot(h2, wq_ref[...]) * (XA_DH ** -0.5)).astype(BF16)
    heads = []
    for h in range(XA_HEADS):
        cols = slice(h * XA_DH, (h + 1) * XA_DH)
        s = lax.dot_general(qa[:, cols], ka_ref[:, cols], (((1,), (1,)), ((), ())),
                            preferred_element_type=F32)
        e = jnp.exp(s - jnp.max(s, axis=-1, keepdims=True))
        l = jnp.sum(e, axis=-1, keepdims=True)
        heads.append((_dot(e.astype(BF16), va_ref[:, cols]) * (1.0 / l)).astype(BF16))
    o = jnp.concatenate(heads, axis=-1)
    x2 = x1 + _dot(o, wo_ref[...])
    o_ref[...] = _rms(x2, gf_ref[...])


def _out_xattn(x, ret, y, gs, gw, gb, wout, g2, wq, ka, va, wo, gf):
    B, L, D = x.shape
    tm = TM_OUT
    row = pl.BlockSpec((None, tm, D), lambda b, t: (b, t, 0))
    const = lambda shape: pl.BlockSpec(shape, lambda b, t: (0,) * len(shape))
    mem = pl.BlockSpec((None, MEM_LEN, D), lambda b, t: (b, 0, 0))
    ytile = pl.BlockSpec((None, LANE_TILES, tm, LANES), lambda b, t: (b, 0, t, 0))
    return pl.pallas_call(
        _out_xattn_kernel,
        grid=(B, L // tm),
        in_specs=[row, row, ytile, row, const((D, D)), const((1, D)), const(wout.shape), const((1, D)),
                  const((D, D)), mem, mem, const((D, D)), const((1, D))],
        out_specs=row,
        out_shape=jax.ShapeDtypeStruct((B, L, D), F32),
        compiler_params=pltpu.CompilerParams(
            dimension_semantics=("arbitrary", "arbitrary"), vmem_limit_bytes=VMEM_LIMIT),
        name="out_xattn",
    )(x, ret, y, gs, gw, gb, wout, g2, wq, ka, va, wo, gf)


def _rope_inv_lanes():
    half = RET_DK // 2
    inv = ROPE_BASE ** (-np.arange(half, dtype=np.float64) / half)
    return jnp.asarray(np.tile(inv, LANES // half).reshape(1, LANES), F32)


def kernel(x, mem, positions, norm1_g, w_in, ret_gn_g, s5_a_re, s5_a_im, s5_log_dt, s5_b_re, s5_b_im, s5_c_re, s5_c_im, s5_d, s5_glu_w, s5_glu_b, w_out, norm2_g, norm_mem_g, xa_wq, xa_wk, xa_wv, xa_wo, norm_f_g):
    B, L, D = x.shape
    l = 0
    bf = lambda w: w.astype(BF16)
    rowvec = lambda v: v.reshape(1, -1)

    wb, wck, consts = _s5_prep(s5_a_re[l], s5_a_im[l], s5_log_dt[l], s5_b_re[l], s5_b_im[l],
                               s5_c_re[l], s5_c_im[l])
    ka, va = _mem_kv(mem, rowvec(norm_mem_g[l]), bf(xa_wk[l]), bf(xa_wv[l]))
    u, g_s5, ret = _proj_ret(x, positions.reshape(B, L, 1), _rope_inv_lanes(),
                             rowvec(norm1_g[l]), bf(w_in[l]), rowvec(ret_gn_g[l]))
    y = _s5_scan(u, wb, wck, consts, s5_d[l].reshape(LANE_TILES, 1, LANES))
    return _out_xattn(x, ret, y, g_s5, bf(s5_glu_w[l]), rowvec(s5_glu_b[l]), bf(w_out[l]),
                      rowvec(norm2_g[l]), bf(xa_wq[l]), ka, va, bf(xa_wo[l]), rowvec(norm_f_g))
```

```python
import functools
import math

import numpy as np
import jax
import jax.numpy as jnp
from jax import lax
from jax.experimental import pallas as pl
from jax.experimental.pallas import tpu as pltpu

F32 = jnp.float32
BF16 = jnp.bfloat16

D_MODEL = 1024
MEM_LEN = 256
EPS = 1e-6
ROPE_BASE = 10000.0

RET_HEADS = 8
RET_QK_WIDTH = 512
RET_V_WIDTH = 1024
RET_DK = 64
RET_DV = 128
CHUNK = 128

S5_WIDTH = 1024
S5_GROUP = 16
S5_GROUPS = 64
S5_STATE = 64
S5_NSTATE = S5_GROUPS * S5_STATE

IN_COLS = 5120
COL_Q, COL_K, COL_V, COL_GR, COL_U, COL_GS = 0, 512, 1024, 2048, 3072, 4096

XA_HEADS = 4
XA_DH = 256

LANES = 128
SUBLANES = 8
VMEM_LIMIT = 48 * 1024 * 1024

LANE_TILES = S5_WIDTH // LANES
TILE_GROUPS = LANES // S5_GROUP
TILE_STATES = S5_NSTATE // LANE_TILES
S5_TAPS = 4
S5_SCAN_LANES = 2 * LANES
S5_RUN = 4

LOG_G = tuple(math.log1p(-(2.0 ** (-5.0 - h))) for h in range(RET_HEADS))

TM_PROJ = 512
TM_S5 = 2048
TM_OUT = 512


def _rms(x, g):
    ms = jnp.mean(x * x, axis=-1, keepdims=True)
    return x * lax.rsqrt(ms + EPS) * g


def _sigmoid(z):
    return 1.0 / (1.0 + jnp.exp(-z))


def _silu(g):
    return g * _sigmoid(g)


def _dot(a, b):
    return jnp.dot(a, b, preferred_element_type=F32)


def _cmul(ar, ai, br, bi):
    return ar * br - ai * bi, ar * bi + ai * br


def _s5_discretise(ar, ai, ldt):
    dt = jnp.exp(ldt)
    mag = jnp.exp(ar * dt)
    p_re = mag * jnp.cos(ai * dt)
    p_im = mag * jnp.sin(ai * dt)
    den = ar * ar + ai * ai
    nr, ni = p_re - 1.0, p_im
    f_re = (nr * ar + ni * ai) / den
    f_im = (ni * ar - nr * ai) / den
    return p_re, p_im, f_re, f_im


def _powers(p_re, p_im, n):
    pw = [(jnp.ones_like(p_re), jnp.zeros_like(p_im))]
    for _ in range(n):
        pw.append(_cmul(pw[-1][0], pw[-1][1], p_re, p_im))
    return pw


def _block_diag(x):
    tiled = jnp.concatenate([x] * TILE_GROUPS, axis=1)
    r = lax.broadcasted_iota(jnp.int32, tiled.shape, 0)
    c = lax.broadcasted_iota(jnp.int32, tiled.shape, 1)
    return jnp.where(r // S5_GROUP == c // S5_STATE, tiled, 0.0)


def _s5_prep_kernel(a_ref, b_ref, c_ref, wb_ref, wck_ref, consts_ref):
    s = S5_TAPS
    half = TILE_STATES
    p_re, p_im, f_re, f_im = _s5_discretise(a_ref[0:1, :], a_ref[1:2, :], a_ref[2:3, :])
    pw = _powers(p_re, p_im, s)
    bbr, bbi = _cmul(f_re, f_im, _block_diag(b_ref[0]), _block_diag(b_ref[1]))
    for k in range(s):
        wr, wi = _cmul(bbr, bbi, *pw[s - 1 - k])
        wb_ref[k * LANES:(k + 1) * LANES, 0:half] = wr.astype(BF16)
        wb_ref[k * LANES:(k + 1) * LANES, half:2 * half] = wi.astype(BF16)

    shape = (SUBLANES, half)
    qr = _powers(pw[s][0], pw[s][1], S5_RUN)
    consts_ref[0, 0] = jnp.broadcast_to(qr[1][0], shape)
    consts_ref[0, 1] = jnp.broadcast_to(qr[1][1], shape)
    rp = _powers(qr[S5_RUN][0], qr[S5_RUN][1], SUBLANES)
    row = lax.broadcasted_iota(jnp.int32, shape, 0)
    for idx, d in enumerate((1, 2, 4)):
        consts_ref[1 + idx, 0] = jnp.where(row >= d, rp[d][0], 0.0)
        consts_ref[1 + idx, 1] = jnp.where(row >= d, rp[d][1], 0.0)
    cr = jnp.zeros(shape, F32)
    ci = jnp.zeros(shape, F32)
    for r in range(SUBLANES):
        cr = jnp.where(row == r, rp[r + 1][0], cr)
        ci = jnp.where(row == r, rp[r + 1][1], ci)
    consts_ref[4, 0] = cr
    consts_ref[4, 1] = ci

    ct_re = _block_diag(c_ref[0])
    ct_im = _block_diag(c_ref[1])
    bb = jnp.concatenate([bbr, bbi], axis=1)
    taps = []
    for d in range(s + 1):
        er, ei = _cmul(ct_re, ct_im, *pw[d])
        cwt = jnp.concatenate([er, -ei], axis=1)
        if d >= 1:
            wck_ref[0:2 * half, (d - 1) * LANES:d * LANES] = cwt.T.astype(BF16)
        if d < s:
            taps.append(lax.dot_general(bb, cwt, (((1,), (1,)), ((), ())), preferred_element_type=F32,
                                        precision=lax.Precision.HIGHEST))
    zero = jnp.zeros((LANES, LANES), BF16)
    for m in range(s):
        for i in range(s):
            blk = taps[i - m].astype(BF16) if m <= i else zero
            wck_ref[2 * half + m * LANES:2 * half + (m + 1) * LANES, i * LANES:(i + 1) * LANES] = blk


def _s5_prep(a_re, a_im, log_dt, b_re, b_im, c_re, c_im):
    s = S5_TAPS
    ldt = jnp.broadcast_to(log_dt[:, None], (S5_GROUPS, S5_STATE))
    a3 = jnp.stack([a_re, a_im, ldt]).reshape(3, S5_NSTATE)
    bt = jnp.stack([b_re, b_im]).transpose(0, 1, 3, 2).reshape(2, S5_WIDTH, S5_STATE)
    ct = jnp.stack([c_re, c_im]).reshape(2, S5_WIDTH, S5_STATE)
    tile = lambda *shape: pl.BlockSpec((None,) + shape, lambda j: (j,) + (0,) * len(shape))
    return pl.pallas_call(
        _s5_prep_kernel,
        grid=(LANE_TILES,),
        in_specs=[pl.BlockSpec((3, TILE_STATES), lambda j: (0, j)),
                  pl.BlockSpec((2, LANES, S5_STATE), lambda j: (0, j, 0)),
                  pl.BlockSpec((2, LANES, S5_STATE), lambda j: (0, j, 0))],
        out_specs=(tile(s * LANES, 2 * TILE_STATES),
                   tile(2 * TILE_STATES + s * LANES, s * LANES),
                   tile(5, 2, SUBLANES, TILE_STATES)),
        out_shape=(jax.ShapeDtypeStruct((LANE_TILES, s * LANES, 2 * TILE_STATES), BF16),
                   jax.ShapeDtypeStruct((LANE_TILES, 2 * TILE_STATES + s * LANES, s * LANES), BF16),
                   jax.ShapeDtypeStruct((LANE_TILES, 5, 2, SUBLANES, TILE_STATES), F32)),
        compiler_params=pltpu.CompilerParams(
            dimension_semantics=("arbitrary",), vmem_limit_bytes=VMEM_LIMIT),
        name="s5_prep",
    )(a3, bt, ct)


def _mem_kv_kernel(mem_ref, g_ref, wk_ref, wv_ref, k_ref, v_ref):
    m = _rms(mem_ref[...], g_ref[...]).astype(BF16)
    k_ref[...] = _dot(m, wk_ref[...]).astype(BF16)
    v_ref[...] = _dot(m, wv_ref[...]).astype(BF16)


def _mem_kv(mem, g, wk, wv):
    B, M, D = mem.shape
    return pl.pallas_call(
        _mem_kv_kernel,
        grid=(B,),
        in_specs=[pl.BlockSpec((None, M, D), lambda b: (b, 0, 0)),
                  pl.BlockSpec((1, D), lambda b: (0, 0)),
                  pl.BlockSpec((D, D), lambda b: (0, 0)),
                  pl.BlockSpec((D, D), lambda b: (0, 0))],
        out_specs=(pl.BlockSpec((None, M, D), lambda b: (b, 0, 0)),
                   pl.BlockSpec((None, M, D), lambda b: (b, 0, 0))),
        out_shape=(jax.ShapeDtypeStruct((B, M, D), BF16),
                   jax.ShapeDtypeStruct((B, M, D), BF16)),
        compiler_params=pltpu.CompilerParams(
            dimension_semantics=("arbitrary",), vmem_limit_bytes=VMEM_LIMIT),
        name="mem_kv",
    )(mem, g, wk, wv)


def _proj_ret_kernel(x_ref, pos_ref, inv_ref, g_ref, w_ref, gn_ref, u_ref, gs_ref, ret_ref,
                     cur_ref, prev_ref, state_ref, decay_ref, qw_ref, kw_ref, *, tiles_per_seq):
    t = pl.program_id(0)

    @pl.when(t == 0)
    def _init():
        _retention_constants(decay_ref, qw_ref, kw_ref)
        cur_ref[...] = jnp.zeros_like(cur_ref)

    @pl.when((t == 0) | (t % tiles_per_seq == 1))
    def _reset():
        state_ref[...] = jnp.zeros_like(state_ref)

    prev_ref[...] = cur_ref[...]

    h = _rms(x_ref[...], g_ref[...]).astype(BF16)
    ang = pos_ref[...].astype(F32) * inv_ref[...]
    cos = jnp.cos(ang)
    sin = jnp.sin(ang)
    lane = lax.broadcasted_iota(jnp.int32, (1, LANES), 1)
    first_half = (lane % RET_DK) < (RET_DK // 2)
    sin_signed = jnp.where(first_half, -sin, sin)

    def rope(p):
        partner = jnp.where(first_half,
                            pltpu.roll(p, LANES - RET_DK // 2, 1),
                            pltpu.roll(p, RET_DK // 2, 1))
        return p * cos + partner * sin_signed

    def proj_q():
        pq = _dot(h, w_ref[:, COL_Q:COL_K])
        for i in range(RET_QK_WIDTH // LANES):
            sl = slice(i * LANES, (i + 1) * LANES)
            cur_ref[:, COL_Q + i * LANES:COL_Q + (i + 1) * LANES] = rope(pq[:, sl]).astype(BF16)

    def proj_k():
        pk = _dot(h, w_ref[:, COL_K:COL_V])
        for i in range(RET_QK_WIDTH // LANES):
            sl = slice(i * LANES, (i + 1) * LANES)
            cur_ref[:, COL_K + i * LANES:COL_K + (i + 1) * LANES] = (
                rope(pk[:, sl]) * (RET_DK ** -0.5)).astype(BF16)

    def proj_v():
        cur_ref[:, COL_V:COL_GR] = _dot(h, w_ref[:, COL_V:COL_GR]).astype(BF16)

    def proj_gate():
        cur_ref[:, COL_GR:COL_U] = _silu(_dot(h, w_ref[:, COL_GR:COL_U])).astype(BF16)

    def proj_u():
        pu = _dot(h, w_ref[:, COL_U:COL_GS]).astype(BF16)
        for j in range(LANE_TILES):
            u_ref[j] = pu[:, j * LANES:(j + 1) * LANES]

    def proj_gs():
        gs_ref[...] = _silu(_dot(h, w_ref[:, COL_GS:IN_COLS])).astype(BF16)

    ret_items = _retention_items(prev_ref, gn_ref, ret_ref, state_ref, decay_ref, qw_ref, kw_ref)
    proj_items = [proj_q, proj_k, proj_v, proj_gate, proj_u, proj_gs]
    per = -(-len(ret_items) // len(proj_items))
    for i, item in enumerate(proj_items):
        for r in ret_items[i * per:(i + 1) * per]:
            r()
        item()


def _proj_ret(x, pos3, inv, g, w, gn):
    B, L, D = x.shape
    tm = TM_PROJ
    nt = L // tm
    n = B * nt
    cur = lambda t: jnp.minimum(t, n - 1)
    lag = lambda t: jnp.maximum(t - 1, 0)
    row = lambda width, tile: pl.BlockSpec((None, tm, width), lambda t: (tile(t) // nt, tile(t) % nt, 0))
    const = lambda shape: pl.BlockSpec(shape, lambda t: (0,) * len(shape))
    return pl.pallas_call(
        functools.partial(_proj_ret_kernel, tiles_per_seq=nt),
        grid=(n + 1,),
        in_specs=[row(D, cur), row(1, cur), const((1, LANES)), const((1, D)), const((D, IN_COLS)),
                  const((1, RET_V_WIDTH))],
        out_specs=(pl.BlockSpec((None, LANE_TILES, tm, LANES), lambda t: (cur(t) // nt, 0, cur(t) % nt, 0)),
                   row(S5_WIDTH, cur), row(RET_V_WIDTH, lag)),
        out_shape=(jax.ShapeDtypeStruct((B, LANE_TILES, L, LANES), BF16),
                   jax.ShapeDtypeStruct((B, L, S5_WIDTH), BF16),
                   jax.ShapeDtypeStruct((B, L, RET_V_WIDTH), BF16)),
        scratch_shapes=[pltpu.VMEM((tm, COL_U), BF16),
                        pltpu.VMEM((tm, COL_U), BF16),
                        pltpu.VMEM((RET_HEADS // 2, LANES, 2 * RET_DV), F32),
                        pltpu.VMEM((RET_HEADS // 2, CHUNK, 2 * CHUNK), F32),
                        pltpu.VMEM((RET_HEADS // 2, CHUNK, 2 * RET_DV), F32),
                        pltpu.VMEM((RET_HEADS // 2, CHUNK, LANES), F32)],
        compiler_params=pltpu.CompilerParams(
            dimension_semantics=("arbitrary",), vmem_limit_bytes=VMEM_LIMIT),
        name="proj_ret",
    )(x, pos3, inv, g, w, gn)


def _retention_constants(decay_ref, qw_ref, kw_ref):
    i = lax.broadcasted_iota(jnp.int32, (CHUNK, 2 * CHUNK), 0).astype(F32)
    c = lax.broadcasted_iota(jnp.int32, (CHUNK, 2 * CHUNK), 1)
    j = (c % CHUNK).astype(F32)
    diff = i - j
    lane = lax.broadcasted_iota(jnp.int32, (CHUNK, LANES), 1)
    ik = lax.broadcasted_iota(jnp.int32, (CHUNK, LANES), 0).astype(F32)
    for p in range(RET_HEADS // 2):
        lg = jnp.where(c < CHUNK, LOG_G[2 * p], LOG_G[2 * p + 1])
        decay_ref[p] = jnp.where(diff >= 0.0, jnp.exp(lg * jnp.maximum(diff, 0.0)), 0.0)
        qw_ref[p] = jnp.exp(lg * (i + 1.0))
        lgk = jnp.where(lane < RET_DK, LOG_G[2 * p], LOG_G[2 * p + 1])
        kw_ref[p] = jnp.exp(lgk * (CHUNK - 1.0 - ik))


def _retention_items(src_ref, gn_ref, o_ref, state_ref, decay_ref, qw_ref, kw_ref):
    pairs = RET_HEADS // 2
    low = lax.broadcasted_iota(jnp.int32, (1, LANES), 1) < RET_DK
    srow = lax.broadcasted_iota(jnp.int32, (LANES, 2 * RET_DV), 0)
    scol = lax.broadcasted_iota(jnp.int32, (LANES, 2 * RET_DV), 1)
    own_block = (srow < RET_DK) == (scol < RET_DV)
    scol1 = lax.broadcasted_iota(jnp.int32, (1, 2 * RET_DV), 1)
    zero_v = jnp.zeros((CHUNK, RET_DV), BF16)

    def pair_step(c, p):
        rows = slice(c * CHUNK, (c + 1) * CHUNK)
        qt = src_ref[rows, COL_Q + p * LANES:COL_Q + (p + 1) * LANES]
        kt = src_ref[rows, COL_K + p * LANES:COL_K + (p + 1) * LANES]
        vp = src_ref[rows, COL_V + 2 * p * RET_DV:COL_V + 2 * (p + 1) * RET_DV]
        state = state_ref[p]
        cross = _dot(qt, state.astype(BF16))
        kw = (kt.astype(F32) * kw_ref[p]).astype(BF16)
        upd = lax.dot_general(kw, vp, (((0,), (0,)), ((), ())), preferred_element_type=F32)
        chunk_decay = jnp.where(scol1 < RET_DV,
                                math.exp(LOG_G[2 * p] * CHUNK), math.exp(LOG_G[2 * p + 1] * CHUNK))
        state_ref[p] = state * chunk_decay + jnp.where(own_block, upd, 0.0)

        zero_k = jnp.zeros_like(kt)
        k_rows = jnp.concatenate([jnp.where(low, kt, zero_k), jnp.where(low, zero_k, kt)], axis=0)
        s = lax.dot_general(qt, k_rows, (((1,), (1,)), ((), ())), preferred_element_type=F32)
        pm = (s * decay_ref[p]).astype(BF16)
        v_diag = jnp.concatenate(
            [jnp.concatenate([vp[:, 0:RET_DV], zero_v], axis=1),
             jnp.concatenate([zero_v, vp[:, RET_DV:2 * RET_DV]], axis=1)], axis=0)
        o2 = _dot(pm, v_diag) + qw_ref[p] * cross
        for e in range(2):
            h = 2 * p + e
            cols = slice(h * RET_DV, (h + 1) * RET_DV)
            o = o2[:, e * RET_DV:(e + 1) * RET_DV]
            mu = jnp.mean(o, axis=-1, keepdims=True)
            oc = o - mu
            var = jnp.mean(oc * oc, axis=-1, keepdims=True)
            on = oc * lax.rsqrt(var + EPS) * gn_ref[:, cols]
            gate = src_ref[rows, COL_GR + h * RET_DV:COL_GR + (h + 1) * RET_DV].astype(F32)
            o_ref[rows, cols] = (on * gate).astype(BF16)

    return [functools.partial(pair_step, c, p)
            for c in range(src_ref.shape[0] // CHUNK) for p in range(pairs)]


def _s5_scan_kernel(u_ref, wb_ref, consts_ref, wck_ref, d_ref, y_ref, *scratch, tiles_per_seq):
    uf_refs, lhs_refs, st_refs = scratch[0:3], scratch[3:6], scratch[6:9]
    sb_ref, ys_ref, carry_ref = scratch[9:]
    n = pl.program_id(0)
    s = S5_TAPS
    tm = u_ref.shape[0]
    nblk = tm // s
    half = TILE_STATES

    @pl.when(n == 0)
    def _init():
        for ref in scratch[0:9]:
            ref[...] = jnp.zeros_like(ref)

    @pl.when((n == 0) | ((n - 1) % tiles_per_seq == 0))
    def _reset():
        carry_ref[...] = jnp.zeros_like(carry_ref)

    ncol = 2 * LANES

    def increments(slot):
        uf_ref, lhs_ref, st_ref = uf_refs[slot], lhs_refs[slot], st_refs[slot]

        def piece(c):
            if c == 0:
                uf_ref[...] = u_ref[...].astype(F32)
                lhs_ref[...] = jnp.concatenate(
                    [uf_ref[pl.ds(k, nblk, stride=s), :] for k in range(s)], axis=1).astype(BF16)
            z = _dot(lhs_ref[...], wb_ref[:, c * ncol:(c + 1) * ncol])
            for i in range(ncol // LANES):
                st_ref[c * (ncol // LANES) + i] = z[:, i * LANES:(i + 1) * LANES]

        return [functools.partial(piece, c) for c in range(2 * half // ncol)]

    ntile = half // LANES

    def scan(slot, row_pieces):
        st_ref = st_refs[slot]
        run = S5_RUN
        span = run * SUBLANES
        row = lax.broadcasted_iota(jnp.int32, (SUBLANES, LANES), 0)
        last = SUBLANES - 1
        groups = nblk // span
        per = -(-groups // row_pieces)
        tiles_per_piece = S5_SCAN_LANES // LANES

        def tile_group(t, g, carry):
            lanes = slice(t * LANES, (t + 1) * LANES)
            const = lambda idx: (consts_ref[idx, 0, :, lanes], consts_ref[idx, 1, :, lanes])
            cr, ci = carry
            rows = [pl.ds(g * span + j, SUBLANES, stride=run) for j in range(run)]
            loc = [(st_ref[t, rows[0], :], st_ref[ntile + t, rows[0], :])]
            for j in range(1, run):
                pr, pi = _cmul(*const(0), *loc[-1])
                loc.append((pr + st_ref[t, rows[j], :], pi + st_ref[ntile + t, rows[j], :]))
            fr, fi = loc[-1]
            for idx, d in enumerate((1, 2, 4)):
                pr, pi = _cmul(*const(1 + idx), pltpu.roll(fr, d, 0), pltpu.roll(fi, d, 0))
                fr, fi = fr + pr, fi + pi
            pr, pi = _cmul(*const(4), cr, ci)
            fr, fi = fr + pr, fi + pi
            er = jnp.where(row == 0, cr, pltpu.roll(fr, 1, 0))
            ei = jnp.where(row == 0, ci, pltpu.roll(fi, 1, 0))
            st_ref[t, rows[0], :] = er
            st_ref[ntile + t, rows[0], :] = ei
            for j in range(1, run):
                er, ei = _cmul(*const(0), er, ei)
                st_ref[t, rows[j], :] = loc[j - 1][0] + er
                st_ref[ntile + t, rows[j], :] = loc[j - 1][1] + ei
            return (jnp.broadcast_to(fr[last:, :], (SUBLANES, LANES)),
                    jnp.broadcast_to(fi[last:, :], (SUBLANES, LANES)))

        def piece(q, i, carries):
            tiles = range(q * tiles_per_piece, (q + 1) * tiles_per_piece)
            if i == 0:
                for t in tiles:
                    carries[t] = (carry_ref[:, t * LANES:(t + 1) * LANES],
                                  carry_ref[:, half + t * LANES:half + (t + 1) * LANES])
            for g in range(i * per, min((i + 1) * per, groups)):
                for t in tiles:
                    carries[t] = tile_group(t, g, carries[t])
            if i == row_pieces - 1:
                for t in tiles:
                    carry_ref[:, t * LANES:(t + 1) * LANES] = carries[t][0]
                    carry_ref[:, half + t * LANES:half + (t + 1) * LANES] = carries[t][1]

        carries = {}
        return [functools.partial(piece, q, i, carries)
                for q in range(ntile // tiles_per_piece) for i in range(row_pieces)]

    def outputs(slot):
        taps_per_piece = ncol // LANES

        def cast():
            for t in range(2 * ntile):
                sb_ref[:, t * LANES:(t + 1) * LANES] = st_refs[slot][t].astype(BF16)

        def piece(c):
            lhs = jnp.concatenate([sb_ref[...], lhs_refs[slot][...]], axis=1)
            yall = _dot(lhs, wck_ref[:, c * ncol:(c + 1) * ncol])
            for i in range(taps_per_piece):
                tap = c * taps_per_piece + i
                ys_ref[pl.ds(tap, nblk, stride=s), :] = yall[:, i * LANES:(i + 1) * LANES]

        def finish():
            y = ys_ref[...] + d_ref[...] * uf_refs[slot][...]
            y_ref[...] = jax.nn.gelu(y).astype(BF16)

        return [cast] + [functools.partial(piece, c) for c in range(s * LANES // ncol)] + [finish]

    for phase in range(3):
        @pl.when(n % 3 == phase)
        def _steps(phase=phase):
            mxu = increments(phase) + outputs((phase + 1) % 3)
            vpu = scan((phase + 2) % 3, 3)
            for i, piece in enumerate(mxu):
                piece()
                if i < len(vpu):
                    vpu[i]()
            for piece in vpu[len(mxu):]:
                piece()


def _s5_scan(u, wb, wck, consts, d):
    B, _, L, _ = u.shape
    s = S5_TAPS
    tm = TM_S5
    nt = L // tm
    ntiles = LANE_TILES * B * nt
    tile_of = lambda n, lag: jnp.clip(n - lag, 0, ntiles - 1)
    lane_tile = lambda n, lag: tile_of(n, lag) // (B * nt)

    def tok(lag):
        def index(n):
            i = tile_of(n, lag)
            return ((i // nt) % B, i // (B * nt), i % nt, 0)
        return pl.BlockSpec((None, None, tm, LANES), index)

    def par(lag, *shape):
        return pl.BlockSpec((None,) + shape, lambda n: (lane_tile(n, lag),) + (0,) * len(shape))

    nblk = tm // s
    return pl.pallas_call(
        functools.partial(_s5_scan_kernel, tiles_per_seq=nt),
        grid=(ntiles + 2,),
        in_specs=[tok(0), par(0, s * LANES, 2 * TILE_STATES), par(1, 5, 2, SUBLANES, TILE_STATES),
                  par(2, 2 * TILE_STATES + s * LANES, s * LANES), par(2, 1, LANES)],
        out_specs=tok(2),
        out_shape=jax.ShapeDtypeStruct((B, LANE_TILES, L, LANES), BF16),
        scratch_shapes=([pltpu.VMEM((tm, LANES), F32)] * 3
                        + [pltpu.VMEM((nblk, s * LANES), BF16)] * 3
                        + [pltpu.VMEM((2 * TILE_STATES // LANES, nblk, LANES), F32)] * 3
                        + [pltpu.VMEM((nblk, 2 * TILE_STATES), BF16), pltpu.VMEM((tm, LANES), F32),
                           pltpu.VMEM((SUBLANES, 2 * TILE_STATES), F32)]),
        compiler_params=pltpu.CompilerParams(
            dimension_semantics=("arbitrary",), vmem_limit_bytes=VMEM_LIMIT),
        name="s5_scan",
    )(u, wb, consts, wck, d)


def _out_xattn_kernel(x_ref, ret_ref, y_ref, gs_ref, gw_ref, gb_ref, wout_ref, g2_ref, wq_ref,
                      ka_ref, va_ref, wo_ref, gf_ref, o_ref):
    y = jnp.concatenate([y_ref[j] for j in range(LANE_TILES)], axis=1)
    z = _dot(y, gw_ref[...]) + gb_ref[...]
    ssm = (y.astype(F32) * _sigmoid(z) * gs_ref[...].astype(F32)).astype(BF16)
    x1 = (x_ref[...] + _dot(ret_ref[...], wout_ref[0:RET_V_WIDTH, :])
          + _dot(ssm, wout_ref[RET_V_WIDTH:RET_V_WIDTH + S5_WIDTH, :]))
    h2 = _rms(x1, g2_ref[...]).astype(BF16)
    qa = (_dot(h2, wq_ref[...]) * (XA_DH ** -0.5)).astype(BF16)
    heads = []
    for h in range(XA_HEADS):
        cols = slice(h * XA_DH, (h + 1) * XA_DH)
        s = lax.dot_general(qa[:, cols], ka_ref[:, cols], (((1,), (1,)), ((), ())),
                            preferred_element_type=F32)
        e = jnp.exp(s - jnp.max(s, axis=-1, keepdims=True))
        l = jnp.sum(e, axis=-1, keepdims=True)
        heads.append((_dot(e.astype(BF16), va_ref[:, cols]) * (1.0 / l)).astype(BF16))
    o = jnp.concatenate(heads, axis=-1)
    x2 = x1 + _dot(o, wo_ref[...])
    o_ref[...] = _rms(x2, gf_ref[...])


def _out_xattn(x, ret, y, gs, gw, gb, wout, g2, wq, ka, va, wo, gf):
    B, L, D = x.shape
    tm = TM_OUT
    row = pl.BlockSpec((None, tm, D), lambda b, t: (b, t, 0))
    const = lambda shape: pl.BlockSpec(shape, lambda b, t: (0,) * len(shape))
    mem = pl.BlockSpec((None, MEM_LEN, D), lambda b, t: (b, 0, 0))
    ytile = pl.BlockSpec((None, LANE_TILES, tm, LANES), lambda b, t: (b, 0, t, 0))
    return pl.pallas_call(
        _out_xattn_kernel,
        grid=(B, L // tm),
        in_specs=[row, row, ytile, row, const((D, D)), const((1, D)), const(wout.shape), const((1, D)),
                  const((D, D)), mem, mem, const((D, D)), const((1, D))],
        out_specs=row,
        out_shape=jax.ShapeDtypeStruct((B, L, D), F32),
        compiler_params=pltpu.CompilerParams(
            dimension_semantics=("arbitrary", "arbitrary"), vmem_limit_bytes=VMEM_LIMIT),
        name="out_xattn",
    )(x, ret, y, gs, gw, gb, wout, g2, wq, ka, va, wo, gf)


def _rope_inv_lanes():
    half = RET_DK // 2
    inv = ROPE_BASE ** (-np.arange(half, dtype=np.float64) / half)
    return jnp.asarray(np.tile(inv, LANES // half).reshape(1, LANES), F32)


def kernel(x, mem, positions, norm1_g, w_in, ret_gn_g, s5_a_re, s5_a_im, s5_log_dt, s5_b_re, s5_b_im, s5_c_re, s5_c_im, s5_d, s5_glu_w, s5_glu_b, w_out, norm2_g, norm_mem_g, xa_wq, xa_wk, xa_wv, xa_wo, norm_f_g):
    B, L, D = x.shape
    l = 0
    bf = lambda w: w.astype(BF16)
    rowvec = lambda v: v.reshape(1, -1)

    wb, wck, consts = _s5_prep(s5_a_re[l], s5_a_im[l], s5_log_dt[l], s5_b_re[l], s5_b_im[l],
                               s5_c_re[l], s5_c_im[l])
    ka, va = _mem_kv(mem, rowvec(norm_mem_g[l]), bf(xa_wk[l]), bf(xa_wv[l]))
    u, g_s5, ret = _proj_ret(x, positions.reshape(B, L, 1), _rope_inv_lanes(),
                             rowvec(norm1_g[l]), bf(w_in[l]), rowvec(ret_gn_g[l]))
    y = _s5_scan(u, wb, wck, consts, s5_d[l].reshape(LANE_TILES, 1, LANES))
    return _out_xattn(x, ret, y, g_s5, bf(s5_glu_w[l]), rowvec(s5_glu_b[l]), bf(w_out[l]),
                      rowvec(norm2_g[l]), bf(xa_wq[l]), ka, va, bf(xa_wo[l]), rowvec(norm_f_g))
```

```python
import functools
import math

import numpy as np
import jax
import jax.numpy as jnp
from jax import lax
from jax.experimental import pallas as pl
from jax.experimental.pallas import tpu as pltpu

F32 = jnp.float32
BF16 = jnp.bfloat16

D_MODEL = 1024
MEM_LEN = 256
EPS = 1e-6
ROPE_BASE = 10000.0

RET_HEADS = 8
RET_QK_WIDTH = 512
RET_V_WIDTH = 1024
RET_DK = 64
RET_DV = 128
CHUNK = 128

S5_WIDTH = 1024
S5_GROUP = 16
S5_GROUPS = 64
S5_STATE = 64
S5_NSTATE = S5_GROUPS * S5_STATE

IN_COLS = 5120
COL_Q, COL_K, COL_V, COL_GR, COL_U, COL_GS = 0, 512, 1024, 2048, 3072, 4096

XA_HEADS = 4
XA_DH = 256

LANES = 128
SUBLANES = 8
VMEM_LIMIT = 48 * 1024 * 1024

LANE_TILES = S5_WIDTH // LANES
TILE_GROUPS = LANES // S5_GROUP
TILE_STATES = S5_NSTATE // LANE_TILES
S5_TAPS = 4
S5_SCAN_LANES = 2 * LANES
S5_RUN = 4

LOG_G = tuple(math.log1p(-(2.0 ** (-5.0 - h))) for h in range(RET_HEADS))

TM_PROJ = 512
TM_S5 = 2048
TM_OUT = 512


def _rms(x, g):
    ms = jnp.mean(x * x, axis=-1, keepdims=True)
    return x * lax.rsqrt(ms + EPS) * g


def _sigmoid(z):
    return 1.0 / (1.0 + jnp.exp(-z))


def _silu(g):
    return g * _sigmoid(g)


def _dot(a, b):
    return jnp.dot(a, b, preferred_element_type=F32)


def _cmul(ar, ai, br, bi):
    return ar * br - ai * bi, ar * bi + ai * br


def _s5_discretise(ar, ai, ldt):
    dt = jnp.exp(ldt)
    mag = jnp.exp(ar * dt)
    p_re = mag * jnp.cos(ai * dt)
    p_im = mag * jnp.sin(ai * dt)
    den = ar * ar + ai * ai
    nr, ni = p_re - 1.0, p_im
    f_re = (nr * ar + ni * ai) / den
    f_im = (ni * ar - nr * ai) / den
    return p_re, p_im, f_re, f_im


def _powers(p_re, p_im, n):
    pw = [(jnp.ones_like(p_re), jnp.zeros_like(p_im))]
    for _ in range(n):
        pw.append(_cmul(pw[-1][0], pw[-1][1], p_re, p_im))
    return pw


def _block_diag(x):
    tiled = jnp.concatenate([x] * TILE_GROUPS, axis=1)
    r = lax.broadcasted_iota(jnp.int32, tiled.shape, 0)
    c = lax.broadcasted_iota(jnp.int32, tiled.shape, 1)
    return jnp.where(r // S5_GROUP == c // S5_STATE, tiled, 0.0)


def _s5_prep_kernel(a_ref, b_ref, c_ref, wb_ref, wck_ref, consts_ref):
    s = S5_TAPS
    half = TILE_STATES
    p_re, p_im, f_re, f_im = _s5_discretise(a_ref[0:1, :], a_ref[1:2, :], a_ref[2:3, :])
    pw = _powers(p_re, p_im, s)
    bbr, bbi = _cmul(f_re, f_im, _block_diag(b_ref[0]), _block_diag(b_ref[1]))
    for k in range(s):
        wr, wi = _cmul(bbr, bbi, *pw[s - 1 - k])
        wb_ref[k * LANES:(k + 1) * LANES, 0:half] = wr.astype(BF16)
        wb_ref[k * LANES:(k + 1) * LANES, half:2 * half] = wi.astype(BF16)

    shape = (SUBLANES, half)
    qr = _powers(pw[s][0], pw[s][1], S5_RUN)
    consts_ref[0, 0] = jnp.broadcast_to(qr[1][0], shape)
    consts_ref[0, 1] = jnp.broadcast_to(qr[1][1], shape)
    rp = _powers(qr[S5_RUN][0], qr[S5_RUN][1], SUBLANES)
    row = lax.broadcasted_iota(jnp.int32, shape, 0)
    for idx, d in enumerate((1, 2, 4)):
        consts_ref[1 + idx, 0] = jnp.where(row >= d, rp[d][0], 0.0)
        consts_ref[1 + idx, 1] = jnp.where(row >= d, rp[d][1], 0.0)
    cr = jnp.zeros(shape, F32)
    ci = jnp.zeros(shape, F32)
    for r in range(SUBLANES):
        cr = jnp.where(row == r, rp[r + 1][0], cr)
        ci = jnp.where(row == r, rp[r + 1][1], ci)
    consts_ref[4, 0] = cr
    consts_ref[4, 1] = ci

    ct_re = _block_diag(c_ref[0])
    ct_im = _block_diag(c_ref[1])
    bb = jnp.concatenate([bbr, bbi], axis=1)
    cwts = []
    for d in range(s + 1):
        er, ei = _cmul(ct_re, ct_im, *pw[d])
        cwt = jnp.concatenate([er, -ei], axis=1)
        if d >= 1:
            wck_ref[0:2 * half, (d - 1) * LANES:d * LANES] = cwt.T.astype(BF16)
        if d < s:
            cwts.append(cwt)
    taps_all = lax.dot_general(bb, jnp.concatenate(cwts, axis=0), (((1,), (1,)), ((), ())),
                               preferred_element_type=F32, precision=lax.Precision.HIGHEST)
    taps = [taps_all[:, d * LANES:(d + 1) * LANES] for d in range(s)]
    zero = jnp.zeros((LANES, LANES), BF16)
    for m in range(s):
        for i in range(s):
            blk = taps[i - m].astype(BF16) if m <= i else zero
            wck_ref[2 * half + m * LANES:2 * half + (m + 1) * LANES, i * LANES:(i + 1) * LANES] = blk


def _s5_prep(a_re, a_im, log_dt, b_re, b_im, c_re, c_im):
    s = S5_TAPS
    ldt = jnp.broadcast_to(log_dt[:, None], (S5_GROUPS, S5_STATE))
    a3 = jnp.stack([a_re, a_im, ldt]).reshape(3, S5_NSTATE)
    bt = jnp.stack([b_re, b_im]).transpose(0, 1, 3, 2).reshape(2, S5_WIDTH, S5_STATE)
    ct = jnp.stack([c_re, c_im]).reshape(2, S5_WIDTH, S5_STATE)
    tile = lambda *shape: pl.BlockSpec((None,) + shape, lambda j: (j,) + (0,) * len(shape))
    return pl.pallas_call(
        _s5_prep_kernel,
        grid=(LANE_TILES,),
        in_specs=[pl.BlockSpec((3, TILE_STATES), lambda j: (0, j)),
                  pl.BlockSpec((2, LANES, S5_STATE), lambda j: (0, j, 0)),
                  pl.BlockSpec((2, LANES, S5_STATE), lambda j: (0, j, 0))],
        out_specs=(tile(s * LANES, 2 * TILE_STATES),
                   tile(2 * TILE_STATES + s * LANES, s * LANES),
                   tile(5, 2, SUBLANES, TILE_STATES)),
        out_shape=(jax.ShapeDtypeStruct((LANE_TILES, s * LANES, 2 * TILE_STATES), BF16),
                   jax.ShapeDtypeStruct((LANE_TILES, 2 * TILE_STATES + s * LANES, s * LANES), BF16),
                   jax.ShapeDtypeStruct((LANE_TILES, 5, 2, SUBLANES, TILE_STATES), F32)),
        compiler_params=pltpu.CompilerParams(
            dimension_semantics=("arbitrary",), vmem_limit_bytes=VMEM_LIMIT),
        name="s5_prep",
    )(a3, bt, ct)


def _mem_kv_kernel(mem_ref, g_ref, wk_ref, wv_ref, k_ref, v_ref):
    m = _rms(mem_ref[...], g_ref[...]).astype(BF16)
    k_ref[...] = _dot(m, wk_ref[...]).astype(BF16)
    v_ref[...] = _dot(m, wv_ref[...]).astype(BF16)


def _mem_kv(mem, g, wk, wv):
    B, M, D = mem.shape
    return pl.pallas_call(
        _mem_kv_kernel,
        grid=(B,),
        in_specs=[pl.BlockSpec((None, M, D), lambda b: (b, 0, 0)),
                  pl.BlockSpec((1, D), lambda b: (0, 0)),
                  pl.BlockSpec((D, D), lambda b: (0, 0)),
                  pl.BlockSpec((D, D), lambda b: (0, 0))],
        out_specs=(pl.BlockSpec((None, M, D), lambda b: (b, 0, 0)),
                   pl.BlockSpec((None, M, D), lambda b: (b, 0, 0))),
        out_shape=(jax.ShapeDtypeStruct((B, M, D), BF16),
                   jax.ShapeDtypeStruct((B, M, D), BF16)),
        compiler_params=pltpu.CompilerParams(
            dimension_semantics=("arbitrary",), vmem_limit_bytes=VMEM_LIMIT),
        name="mem_kv",
    )(mem, g, wk, wv)


def _token_prologue(x_ref, pos_ref, inv_ref, g_ref, h_ref, cos_ref, sin_ref):
    h_ref[...] = _rms(x_ref[...], g_ref[...]).astype(BF16)
    lane = lax.broadcasted_iota(jnp.int32, (1, LANES), 1)
    nfreq = RET_DK // 2
    group = lane // nfreq
    nblock = LANES // nfreq
    rows = x_ref.shape[0] // nblock
    pos = pos_ref[...].astype(F32)
    pos4 = jnp.zeros((rows, LANES), F32)
    for k in range(nblock):
        pos4 = jnp.where(group == k, pos[k * rows:(k + 1) * rows, :], pos4)
    ang4 = pos4 * inv_ref[...]

    def spread(t4):
        blocks = []
        for k in range(nblock):
            m = jnp.where(group == k, t4, 0.0)
            m = m + pltpu.roll(m, nfreq, 1)
            blocks.append(m + pltpu.roll(m, 2 * nfreq, 1))
        return jnp.concatenate(blocks, axis=0)

    first_half = (lane % RET_DK) < nfreq
    sin = spread(jnp.sin(ang4))
    cos_ref[...] = spread(jnp.cos(ang4))
    sin_ref[...] = jnp.where(first_half, -sin, sin)


def _proj_ret_kernel(x0_ref, pos0_ref, x_ref, pos_ref, inv_ref, g_ref, w_ref, gn_ref, u_ref, gs_ref, ret_ref,
                     cur_ref, prev_ref, hn_ref, cn_ref, sn_ref, h_ref, cos_ref, sin_ref,
                     state_ref, decay_ref, qw_ref, kw_ref, *, tiles_per_seq):
    t = pl.program_id(0)

    @pl.when(t == 0)
    def _init():
        _retention_constants(decay_ref, qw_ref, kw_ref)
        cur_ref[...] = jnp.zeros_like(cur_ref)
        _token_prologue(x0_ref, pos0_ref, inv_ref, g_ref, hn_ref, cn_ref, sn_ref)

    @pl.when((t == 0) | (t % tiles_per_seq == 1))
    def _reset():
        state_ref[...] = jnp.zeros_like(state_ref)

    prev_ref[...] = cur_ref[...]
    h_ref[...] = hn_ref[...]
    cos_ref[...] = cn_ref[...]
    sin_ref[...] = sn_ref[...]
    h = h_ref[...]
    cos = cos_ref[...]
    sin_signed = sin_ref[...]
    lane = lax.broadcasted_iota(jnp.int32, (1, LANES), 1)
    first_half = (lane % RET_DK) < (RET_DK // 2)

    def rope(p):
        partner = jnp.where(first_half,
                            pltpu.roll(p, LANES - RET_DK // 2, 1),
                            pltpu.roll(p, RET_DK // 2, 1))
        return p * cos + partner * sin_signed

    def proj_q():
        pq = _dot(h, w_ref[:, COL_Q:COL_K])
        for i in range(RET_QK_WIDTH // LANES):
            sl = slice(i * LANES, (i + 1) * LANES)
            cur_ref[:, COL_Q + i * LANES:COL_Q + (i + 1) * LANES] = rope(pq[:, sl]).astype(BF16)

    def proj_k():
        pk = _dot(h, w_ref[:, COL_K:COL_V])
        for i in range(RET_QK_WIDTH // LANES):
            sl = slice(i * LANES, (i + 1) * LANES)
            cur_ref[:, COL_K + i * LANES:COL_K + (i + 1) * LANES] = (
                rope(pk[:, sl]) * (RET_DK ** -0.5)).astype(BF16)

    def proj_v():
        cur_ref[:, COL_V:COL_GR] = _dot(h, w_ref[:, COL_V:COL_GR]).astype(BF16)

    def proj_gate():
        cur_ref[:, COL_GR:COL_U] = _silu(_dot(h, w_ref[:, COL_GR:COL_U])).astype(BF16)

    def proj_u():
        pu = _dot(h, w_ref[:, COL_U:COL_GS]).astype(BF16)
        for j in range(LANE_TILES):
            u_ref[j] = pu[:, j * LANES:(j + 1) * LANES]

    def proj_gs():
        gs_ref[...] = _silu(_dot(h, w_ref[:, COL_GS:IN_COLS])).astype(BF16)

    ret_items = _retention_items(prev_ref, gn_ref, ret_ref, state_ref, decay_ref, qw_ref, kw_ref)
    proj_items = [proj_q, proj_k, proj_v, proj_gate, proj_u, proj_gs]
    per = -(-len(ret_items) // len(proj_items))
    for i, item in enumerate(proj_items):
        for r in ret_items[i * per:(i + 1) * per]:
            r()
        item()
    _token_prologue(x_ref, pos_ref, inv_ref, g_ref, hn_ref, cn_ref, sn_ref)


def _proj_ret(x, pos3, inv, g, w, gn):
    B, L, D = x.shape
    tm = TM_PROJ
    nt = L // tm
    n = B * nt
    cur = lambda t: jnp.minimum(t, n - 1)
    nxt = lambda t: jnp.minimum(t + 1, n - 1)
    lag = lambda t: jnp.maximum(t - 1, 0)
    first = lambda t: 0 * t
    row = lambda width, tile: pl.BlockSpec((None, tm, width), lambda t: (tile(t) // nt, tile(t) % nt, 0))
    const = lambda shape: pl.BlockSpec(shape, lambda t: (0,) * len(shape))
    return pl.pallas_call(
        functools.partial(_proj_ret_kernel, tiles_per_seq=nt),
        grid=(n + 1,),
        in_specs=[row(D, first), row(1, first), row(D, nxt), row(1, nxt), const((1, LANES)), const((1, D)),
                  const((D, IN_COLS)), const((1, RET_V_WIDTH))],
        out_specs=(pl.BlockSpec((None, LANE_TILES, tm, LANES), lambda t: (cur(t) // nt, 0, cur(t) % nt, 0)),
                   row(S5_WIDTH, cur), row(RET_V_WIDTH, lag)),
        out_shape=(jax.ShapeDtypeStruct((B, LANE_TILES, L, LANES), BF16),
                   jax.ShapeDtypeStruct((B, L, S5_WIDTH), BF16),
                   jax.ShapeDtypeStruct((B, L, RET_V_WIDTH), BF16)),
        scratch_shapes=[pltpu.VMEM((tm, COL_U), BF16),
                        pltpu.VMEM((tm, COL_U), BF16),
                        pltpu.VMEM((tm, D), BF16), pltpu.VMEM((tm, LANES), F32), pltpu.VMEM((tm, LANES), F32),
                        pltpu.VMEM((tm, D), BF16), pltpu.VMEM((tm, LANES), F32), pltpu.VMEM((tm, LANES), F32),
                        pltpu.VMEM((RET_HEADS // 2, LANES, 2 * RET_DV), F32),
                        pltpu.VMEM((RET_HEADS // 2, CHUNK, 2 * CHUNK), F32),
                        pltpu.VMEM((RET_HEADS // 2, CHUNK, 2 * RET_DV), F32),
                        pltpu.VMEM((RET_HEADS // 2, CHUNK, LANES), F32)],
        compiler_params=pltpu.CompilerParams(
            dimension_semantics=("arbitrary",), vmem_limit_bytes=VMEM_LIMIT),
        name="proj_ret",
    )(x, pos3, x, pos3, inv, g, w, gn)


def _retention_constants(decay_ref, qw_ref, kw_ref):
    i = lax.broadcasted_iota(jnp.int32, (CHUNK, 2 * CHUNK), 0).astype(F32)
    c = lax.broadcasted_iota(jnp.int32, (CHUNK, 2 * CHUNK), 1)
    j = (c % CHUNK).astype(F32)
    diff = i - j
    lane = lax.broadcasted_iota(jnp.int32, (CHUNK, LANES), 1)
    ik = lax.broadcasted_iota(jnp.int32, (CHUNK, LANES), 0).astype(F32)
    for p in range(RET_HEADS // 2):
        lg = jnp.where(c < CHUNK, LOG_G[2 * p], LOG_G[2 * p + 1])
        decay_ref[p] = jnp.where(diff >= 0.0, jnp.exp(lg * jnp.maximum(diff, 0.0)), 0.0)
        qw_ref[p] = jnp.exp(lg * (i + 1.0))
        lgk = jnp.where(lane < RET_DK, LOG_G[2 * p], LOG_G[2 * p + 1])
        kw_ref[p] = jnp.exp(lgk * (CHUNK - 1.0 - ik))


def _retention_items(src_ref, gn_ref, o_ref, state_ref, decay_ref, qw_ref, kw_ref):
    pairs = RET_HEADS // 2
    low = lax.broadcasted_iota(jnp.int32, (1, LANES), 1) < RET_DK
    srow = lax.broadcasted_iota(jnp.int32, (LANES, 2 * RET_DV), 0)
    scol = lax.broadcasted_iota(jnp.int32, (LANES, 2 * RET_DV), 1)
    own_block = (srow < RET_DK) == (scol < RET_DV)
    scol1 = lax.broadcasted_iota(jnp.int32, (1, 2 * RET_DV), 1)
    zero_v = jnp.zeros((CHUNK, RET_DV), BF16)

    def pair_step(c, p):
        rows = slice(c * CHUNK, (c + 1) * CHUNK)
        qt = src_ref[rows, COL_Q + p * LANES:COL_Q + (p + 1) * LANES]
        kt = src_ref[rows, COL_K + p * LANES:COL_K + (p + 1) * LANES]
        vp = src_ref[rows, COL_V + 2 * p * RET_DV:COL_V + 2 * (p + 1) * RET_DV]
        state = state_ref[p]
        cross = _dot(qt, state.astype(BF16))
        kw = (kt.astype(F32) * kw_ref[p]).astype(BF16)
        upd = lax.dot_general(kw, vp, (((0,), (0,)), ((), ())), preferred_element_type=F32)
        chunk_decay = jnp.where(scol1 < RET_DV,
                                math.exp(LOG_G[2 * p] * CHUNK), math.exp(LOG_G[2 * p + 1] * CHUNK))
        state_ref[p] = state * chunk_decay + jnp.where(own_block, upd, 0.0)

        zero_k = jnp.zeros_like(kt)
        k_rows = jnp.concatenate([jnp.where(low, kt, zero_k), jnp.where(low, zero_k, kt)], axis=0)
        s = lax.dot_general(qt, k_rows, (((1,), (1,)), ((), ())), preferred_element_type=F32)
        pm = (s * decay_ref[p]).astype(BF16)
        v_diag = jnp.concatenate(
            [jnp.concatenate([vp[:, 0:RET_DV], zero_v], axis=1),
             jnp.concatenate([zero_v, vp[:, RET_DV:2 * RET_DV]], axis=1)], axis=0)
        o2 = _dot(pm, v_diag) + qw_ref[p] * cross
        for e in range(2):
            h = 2 * p + e
            cols = slice(h * RET_DV, (h + 1) * RET_DV)
            o = o2[:, e * RET_DV:(e + 1) * RET_DV]
            mu = jnp.mean(o, axis=-1, keepdims=True)
            oc = o - mu
            var = jnp.mean(oc * oc, axis=-1, keepdims=True)
            on = oc * lax.rsqrt(var + EPS) * gn_ref[:, cols]
            gate = src_ref[rows, COL_GR + h * RET_DV:COL_GR + (h + 1) * RET_DV].astype(F32)
            o_ref[rows, cols] = (on * gate).astype(BF16)

    return [functools.partial(pair_step, c, p)
            for c in range(src_ref.shape[0] // CHUNK) for p in range(pairs)]


def _s5_scan_kernel(u_ref, wb_ref, consts_ref, wck_ref, d_ref, y_ref, *scratch, tiles_per_seq):
    uf_refs, lhs_refs, st_refs = scratch[0:3], scratch[3:6], scratch[6:9]
    sb_ref, ys_ref, carry_ref = scratch[9:]
    n = pl.program_id(0)
    s = S5_TAPS
    tm = u_ref.shape[0]
    nblk = tm // s
    half = TILE_STATES

    @pl.when(n == 0)
    def _init():
        for ref in scratch[0:9]:
            ref[...] = jnp.zeros_like(ref)

    @pl.when((n == 0) | ((n - 1) % tiles_per_seq == 0))
    def _reset():
        carry_ref[...] = jnp.zeros_like(carry_ref)

    ncol = 2 * LANES

    def increments(slot):
        uf_ref, lhs_ref, st_ref = uf_refs[slot], lhs_refs[slot], st_refs[slot]

        def piece(c):
            if c == 0:
                uf_ref[...] = u_ref[...].astype(F32)
                lhs_ref[...] = jnp.concatenate(
                    [uf_ref[pl.ds(k, nblk, stride=s), :] for k in range(s)], axis=1).astype(BF16)
            z = _dot(lhs_ref[...], wb_ref[:, c * ncol:(c + 1) * ncol])
            for i in range(ncol // LANES):
                st_ref[c * (ncol // LANES) + i] = z[:, i * LANES:(i + 1) * LANES]

        return [functools.partial(piece, c) for c in range(2 * half // ncol)]

    ntile = half // LANES

    def scan(slot, row_pieces):
        st_ref = st_refs[slot]
        run = S5_RUN
        span = run * SUBLANES
        row = lax.broadcasted_iota(jnp.int32, (SUBLANES, LANES), 0)
        last = SUBLANES - 1
        groups = nblk // span
        per = -(-groups // row_pieces)
        tiles_per_piece = S5_SCAN_LANES // LANES

        def tile_group(t, g, carry):
            lanes = slice(t * LANES, (t + 1) * LANES)
            const = lambda idx: (consts_ref[idx, 0, :, lanes], consts_ref[idx, 1, :, lanes])
            cr, ci = carry
            rows = [pl.ds(g * span + j, SUBLANES, stride=run) for j in range(run)]
            loc = [(st_ref[t, rows[0], :], st_ref[ntile + t, rows[0], :])]
            for j in range(1, run):
                pr, pi = _cmul(*const(0), *loc[-1])
                loc.append((pr + st_ref[t, rows[j], :], pi + st_ref[ntile + t, rows[j], :]))
            fr, fi = loc[-1]
            for idx, d in enumerate((1, 2, 4)):
                pr, pi = _cmul(*const(1 + idx), pltpu.roll(fr, d, 0), pltpu.roll(fi, d, 0))
                fr, fi = fr + pr, fi + pi
            pr, pi = _cmul(*const(4), cr, ci)
            fr, fi = fr + pr, fi + pi
            er = jnp.where(row == 0, cr, pltpu.roll(fr, 1, 0))
            ei = jnp.where(row == 0, ci, pltpu.roll(fi, 1, 0))
            st_ref[t, rows[0], :] = er
            st_ref[ntile + t, rows[0], :] = ei
            for j in range(1, run):
                er, ei = _cmul(*const(0), er, ei)
                st_ref[t, rows[j], :] = loc[j - 1][0] + er
                st_ref[ntile + t, rows[j], :] = loc[j - 1][1] + ei
            return (jnp.broadcast_to(fr[last:, :], (SUBLANES, LANES)),
                    jnp.broadcast_to(fi[last:, :], (SUBLANES, LANES)))

        def piece(q, i, carries):
            tiles = range(q * tiles_per_piece, (q + 1) * tiles_per_piece)
            if i == 0:
                for t in tiles:
                    carries[t] = (carry_ref[:, t * LANES:(t + 1) * LANES],
                                  carry_ref[:, half + t * LANES:half + (t + 1) * LANES])
            for g in range(i * per, min((i + 1) * per, groups)):
                for t in tiles:
                    carries[t] = tile_group(t, g, carries[t])
            if i == row_pieces - 1:
                for t in tiles:
                    carry_ref[:, t * LANES:(t + 1) * LANES] = carries[t][0]
                    carry_ref[:, half + t * LANES:half + (t + 1) * LANES] = carries[t][1]

        carries = {}
        return [functools.partial(piece, q, i, carries)
                for q in range(ntile // tiles_per_piece) for i in range(row_pieces)]

    def outputs(slot):
        taps_per_piece = ncol // LANES

        def cast():
            for t in range(2 * ntile):
                sb_ref[:, t * LANES:(t + 1) * LANES] = st_refs[slot][t].astype(BF16)

        def piece(c):
            lhs = jnp.concatenate([sb_ref[...], lhs_refs[slot][...]], axis=1)
            yall = _dot(lhs, wck_ref[:, c * ncol:(c + 1) * ncol])
            for i in range(taps_per_piece):
                tap = c * taps_per_piece + i
                ys_ref[pl.ds(tap, nblk, stride=s), :] = yall[:, i * LANES:(i + 1) * LANES]

        def finish():
            y = ys_ref[...] + d_ref[...] * uf_refs[slot][...]
            y_ref[...] = jax.nn.gelu(y).astype(BF16)

        return [cast] + [functools.partial(piece, c) for c in range(s * LANES // ncol)] + [finish]

    for phase in range(3):
        @pl.when(n % 3 == phase)
        def _steps(phase=phase):
            mxu = increments(phase) + outputs((phase + 1) % 3)
            vpu = scan((phase + 2) % 3, 3)
            for i, piece in enumerate(mxu):
                piece()
                if i < len(vpu):
                    vpu[i]()
            for piece in vpu[len(mxu):]:
                piece()


def _s5_scan(u, wb, wck, consts, d):
    B, _, L, _ = u.shape
    s = S5_TAPS
    tm = TM_S5
    nt = L // tm
    ntiles = LANE_TILES * B * nt
    tile_of = lambda n, lag: jnp.clip(n - lag, 0, ntiles - 1)
    lane_tile = lambda n, lag: tile_of(n, lag) // (B * nt)

    def tok(lag):
        def index(n):
            i = tile_of(n, lag)
            return ((i // nt) % B, i // (B * nt), i % nt, 0)
        return pl.BlockSpec((None, None, tm, LANES), index)

    def par(lag, *shape):
        return pl.BlockSpec((None,) + shape, lambda n: (lane_tile(n, lag),) + (0,) * len(shape))

    nblk = tm // s
    return pl.pallas_call(
        functools.partial(_s5_scan_kernel, tiles_per_seq=nt),
        grid=(ntiles + 2,),
        in_specs=[tok(0), par(0, s * LANES, 2 * TILE_STATES), par(1, 5, 2, SUBLANES, TILE_STATES),
                  par(2, 2 * TILE_STATES + s * LANES, s * LANES), par(2, 1, LANES)],
        out_specs=tok(2),
        out_shape=jax.ShapeDtypeStruct((B, LANE_TILES, L, LANES), BF16),
        scratch_shapes=([pltpu.VMEM((tm, LANES), F32)] * 3
                        + [pltpu.VMEM((nblk, s * LANES), BF16)] * 3
                        + [pltpu.VMEM((2 * TILE_STATES // LANES, nblk, LANES), F32)] * 3
                        + [pltpu.VMEM((nblk, 2 * TILE_STATES), BF16), pltpu.VMEM((tm, LANES), F32),
                           pltpu.VMEM((SUBLANES, 2 * TILE_STATES), F32)]),
        compiler_params=pltpu.CompilerParams(
            dimension_semantics=("arbitrary",), vmem_limit_bytes=VMEM_LIMIT),
        name="s5_scan",
    )(u, wb, consts, wck, d)


def _out_xattn_kernel(x_ref, ret_ref, y_ref, gs_ref, gw_ref, gb_ref, wout_ref, g2_ref, wq_ref,
                      ka_ref, va_ref, wo_ref, gf_ref, o_ref):
    y = jnp.concatenate([y_ref[j] for j in range(LANE_TILES)], axis=1)
    z = _dot(y, gw_ref[...]) + gb_ref[...]
    ssm = (y.astype(F32) * _sigmoid(z) * gs_ref[...].astype(F32)).astype(BF16)
    x1 = (x_ref[...] + _dot(ret_ref[...], wout_ref[0:RET_V_WIDTH, :])
          + _dot(ssm, wout_ref[RET_V_WIDTH:RET_V_WIDTH + S5_WIDTH, :]))
    h2 = _rms(x1, g2_ref[...]).astype(BF16)
    qa = (_dot(h2, wq_ref[...]) * (XA_DH ** -0.5)).astype(BF16)
    heads = []
    for h in range(XA_HEADS):
        cols = slice(h * XA_DH, (h + 1) * XA_DH)
        s = lax.dot_general(qa[:, cols], ka_ref[:, cols], (((1,), (1,)), ((), ())),
                            preferred_element_type=F32)
        e = jnp.exp(s - jnp.max(s, axis=-1, keepdims=True))
        l = jnp.sum(e, axis=-1, keepdims=True)
        heads.append((_dot(e.astype(BF16), va_ref[:, cols]) * (1.0 / l)).astype(BF16))
    o = jnp.concatenate(heads, axis=-1)
    x2 = x1 + _dot(o, wo_ref[...])
    o_ref[...] = _rms(x2, gf_ref[...])


def _out_xattn(x, ret, y, gs, gw, gb, wout, g2, wq, ka, va, wo, gf):
    B, L, D = x.shape
    tm = TM_OUT
    row = pl.BlockSpec((None, tm, D), lambda b, t: (b, t, 0))
    const = lambda shape: pl.BlockSpec(shape, lambda b, t: (0,) * len(shape))
    mem = pl.BlockSpec((None, MEM_LEN, D), lambda b, t: (b, 0, 0))
    ytile = pl.BlockSpec((None, LANE_TILES, tm, LANES), lambda b, t: (b, 0, t, 0))
    return pl.pallas_call(
        _out_xattn_kernel,
        grid=(B, L // tm),
        in_specs=[row, row, ytile, row, const((D, D)), const((1, D)), const(wout.shape), const((1, D)),
                  const((D, D)), mem, mem, const((D, D)), const((1, D))],
        out_specs=row,
        out_shape=jax.ShapeDtypeStruct((B, L, D), F32),
        compiler_params=pltpu.CompilerParams(
            dimension_semantics=("arbitrary", "arbitrary"), vmem_limit_bytes=VMEM_LIMIT),
        name="out_xattn",
    )(x, ret, y, gs, gw, gb, wout, g2, wq, ka, va, wo, gf)


def _rope_inv_lanes():
    half = RET_DK // 2
    inv = ROPE_BASE ** (-np.arange(half, dtype=np.float64) / half)
    return jnp.asarray(np.tile(inv, LANES // half).reshape(1, LANES), F32)


def kernel(x, mem, positions, norm1_g, w_in, ret_gn_g, s5_a_re, s5_a_im, s5_log_dt, s5_b_re, s5_b_im, s5_c_re, s5_c_im, s5_d, s5_glu_w, s5_glu_b, w_out, norm2_g, norm_mem_g, xa_wq, xa_wk, xa_wv, xa_wo, norm_f_g):
    B, L, D = x.shape
    l = 0
    bf = lambda w: w.astype(BF16)
    rowvec = lambda v: v.reshape(1, -1)

    wb, wck, consts = _s5_prep(s5_a_re[l], s5_a_im[l], s5_log_dt[l], s5_b_re[l], s5_b_im[l],
                               s5_c_re[l], s5_c_im[l])
    ka, va = _mem_kv(mem, rowvec(norm_mem_g[l]), bf(xa_wk[l]), bf(xa_wv[l]))
    u, g_s5, ret = _proj_ret(x, positions.reshape(B, L, 1), _rope_inv_lanes(),
                             rowvec(norm1_g[l]), bf(w_in[l]), rowvec(ret_gn_g[l]))
    y = _s5_scan(u, wb, wck, consts, s5_d[l].reshape(LANE_TILES, 1, LANES))
    return _out_xattn(x, ret, y, g_s5, bf(s5_glu_w[l]), rowvec(s5_glu_b[l]), bf(w_out[l]),
                      rowvec(norm2_g[l]), bf(xa_wq[l]), ka, va, bf(xa_wo[l]), rowvec(norm_f_g))
```

```python
import functools
import math

import numpy as np
import jax
import jax.numpy as jnp
from jax import lax
from jax.experimental import pallas as pl
from jax.experimental.pallas import tpu as pltpu

F32 = jnp.float32
BF16 = jnp.bfloat16

D_MODEL = 1024
MEM_LEN = 256
EPS = 1e-6
ROPE_BASE = 10000.0

RET_HEADS = 8
RET_QK_WIDTH = 512
RET_V_WIDTH = 1024
RET_DK = 64
RET_DV = 128
CHUNK = 128

S5_WIDTH = 1024
S5_GROUP = 16
S5_GROUPS = 64
S5_STATE = 64
S5_NSTATE = S5_GROUPS * S5_STATE

IN_COLS = 5120
COL_Q, COL_K, COL_V, COL_GR, COL_U, COL_GS = 0, 512, 1024, 2048, 3072, 4096

XA_HEADS = 4
XA_DH = 256

LANES = 128
SUBLANES = 8
VMEM_LIMIT = 48 * 1024 * 1024

LANE_TILES = S5_WIDTH // LANES
TILE_GROUPS = LANES // S5_GROUP
TILE_STATES = S5_NSTATE // LANE_TILES
S5_TAPS = 4
S5_SCAN_LANES = 2 * LANES
S5_RUN = 4

LOG_G = tuple(math.log1p(-(2.0 ** (-5.0 - h))) for h in range(RET_HEADS))

TM_PROJ = 512
TM_S5 = 2048
TM_OUT = 512


def _rms(x, g):
    ms = jnp.mean(x * x, axis=-1, keepdims=True)
    return x * lax.rsqrt(ms + EPS) * g


def _sigmoid(z):
    return 1.0 / (1.0 + jnp.exp(-z))


def _silu(g):
    return g * _sigmoid(g)


def _dot(a, b):
    return jnp.dot(a, b, preferred_element_type=F32)


def _cmul(ar, ai, br, bi):
    return ar * br - ai * bi, ar * bi + ai * br


def _s5_discretise(ar, ai, ldt):
    dt = jnp.exp(ldt)
    mag = jnp.exp(ar * dt)
    p_re = mag * jnp.cos(ai * dt)
    p_im = mag * jnp.sin(ai * dt)
    den = ar * ar + ai * ai
    nr, ni = p_re - 1.0, p_im
    f_re = (nr * ar + ni * ai) / den
    f_im = (ni * ar - nr * ai) / den
    return p_re, p_im, f_re, f_im


def _powers(p_re, p_im, n):
    pw = [(jnp.ones_like(p_re), jnp.zeros_like(p_im))]
    for _ in range(n):
        pw.append(_cmul(pw[-1][0], pw[-1][1], p_re, p_im))
    return pw


def _block_diag(x):
    tiled = jnp.concatenate([x] * TILE_GROUPS, axis=1)
    r = lax.broadcasted_iota(jnp.int32, tiled.shape, 0)
    c = lax.broadcasted_iota(jnp.int32, tiled.shape, 1)
    return jnp.where(r // S5_GROUP == c // S5_STATE, tiled, 0.0)


def _s5_prep_kernel(a_ref, ag_ref, b_ref, c_ref, wb_ref, wck_ref, consts_ref):
    s = S5_TAPS
    half = TILE_STATES
    def per_row(v):
        return jnp.broadcast_to(v[:, None, :], (TILE_GROUPS, S5_GROUP, S5_STATE)).reshape(LANES, S5_STATE)

    g_re, g_im, gf_re, gf_im = _s5_discretise(ag_ref[0], ag_ref[1], ag_ref[2])
    gw = [(per_row(r), per_row(i)) for r, i in _powers(g_re, g_im, s)]
    bbr, bbi = _cmul(per_row(gf_re), per_row(gf_im), b_ref[0], b_ref[1])
    for k in range(s):
        wr, wi = _cmul(bbr, bbi, *gw[s - 1 - k])
        wb_ref[k * LANES:(k + 1) * LANES, 0:half] = _block_diag(wr).astype(BF16)
        wb_ref[k * LANES:(k + 1) * LANES, half:2 * half] = _block_diag(wi).astype(BF16)

    bb = jnp.concatenate([bbr, bbi], axis=1)
    r = lax.broadcasted_iota(jnp.int32, (LANES, LANES), 0)
    c = lax.broadcasted_iota(jnp.int32, (LANES, LANES), 1)
    same_group = r // S5_GROUP == c // S5_GROUP
    taps = []
    for d in range(s + 1):
        er, ei = _cmul(c_ref[0], c_ref[1], *gw[d])
        if d >= 1:
            cwt = jnp.concatenate([_block_diag(er), -_block_diag(ei)], axis=1)
            wck_ref[0:2 * half, (d - 1) * LANES:d * LANES] = cwt.T.astype(BF16)
        if d < s:
            t = lax.dot_general(bb, jnp.concatenate([er, -ei], axis=1), (((1,), (1,)), ((), ())),
                                preferred_element_type=F32, precision=lax.Precision.HIGHEST)
            taps.append(jnp.where(same_group, t, 0.0))
    zero = jnp.zeros((LANES, LANES), BF16)
    for m in range(s):
        for i in range(s):
            blk = taps[i - m].astype(BF16) if m <= i else zero
            wck_ref[2 * half + m * LANES:2 * half + (m + 1) * LANES, i * LANES:(i + 1) * LANES] = blk

    p_re, p_im, _, _ = _s5_discretise(a_ref[0:1, :], a_ref[1:2, :], a_ref[2:3, :])
    pw = _powers(p_re, p_im, s)
    shape = (SUBLANES, half)
    qr = _powers(pw[s][0], pw[s][1], S5_RUN)
    consts_ref[0, 0] = jnp.broadcast_to(qr[1][0], shape)
    consts_ref[0, 1] = jnp.broadcast_to(qr[1][1], shape)
    rp = _powers(qr[S5_RUN][0], qr[S5_RUN][1], SUBLANES)
    row = lax.broadcasted_iota(jnp.int32, shape, 0)
    for idx, d in enumerate((1, 2, 4)):
        consts_ref[1 + idx, 0] = jnp.where(row >= d, rp[d][0], 0.0)
        consts_ref[1 + idx, 1] = jnp.where(row >= d, rp[d][1], 0.0)
    cr = jnp.zeros(shape, F32)
    ci = jnp.zeros(shape, F32)
    for k in range(SUBLANES):
        cr = jnp.where(row == k, rp[k + 1][0], cr)
        ci = jnp.where(row == k, rp[k + 1][1], ci)
    consts_ref[4, 0] = cr
    consts_ref[4, 1] = ci


def _s5_prep(a_re, a_im, log_dt, b_re, b_im, c_re, c_im):
    s = S5_TAPS
    ldt = jnp.broadcast_to(log_dt[:, None], (S5_GROUPS, S5_STATE))
    ag = jnp.stack([a_re, a_im, ldt])
    a3 = ag.reshape(3, S5_NSTATE)
    bt = jnp.stack([b_re, b_im]).transpose(0, 1, 3, 2).reshape(2, S5_WIDTH, S5_STATE)
    ct = jnp.stack([c_re, c_im]).reshape(2, S5_WIDTH, S5_STATE)
    tile = lambda *shape: pl.BlockSpec((None,) + shape, lambda j: (j,) + (0,) * len(shape))
    return pl.pallas_call(
        _s5_prep_kernel,
        grid=(LANE_TILES,),
        in_specs=[pl.BlockSpec((3, TILE_STATES), lambda j: (0, j)),
                  pl.BlockSpec((3, TILE_GROUPS, S5_STATE), lambda j: (0, j, 0)),
                  pl.BlockSpec((2, LANES, S5_STATE), lambda j: (0, j, 0)),
                  pl.BlockSpec((2, LANES, S5_STATE), lambda j: (0, j, 0))],
        out_specs=(tile(s * LANES, 2 * TILE_STATES),
                   tile(2 * TILE_STATES + s * LANES, s * LANES),
                   tile(5, 2, SUBLANES, TILE_STATES)),
        out_shape=(jax.ShapeDtypeStruct((LANE_TILES, s * LANES, 2 * TILE_STATES), BF16),
                   jax.ShapeDtypeStruct((LANE_TILES, 2 * TILE_STATES + s * LANES, s * LANES), BF16),
                   jax.ShapeDtypeStruct((LANE_TILES, 5, 2, SUBLANES, TILE_STATES), F32)),
        compiler_params=pltpu.CompilerParams(
            dimension_semantics=("arbitrary",), vmem_limit_bytes=VMEM_LIMIT),
        name="s5_prep",
    )(a3, ag, bt, ct)


def _mem_kv_kernel(mem_ref, g_ref, wk_ref, wv_ref, k_ref, v_ref):
    m = _rms(mem_ref[...], g_ref[...]).astype(BF16)
    k_ref[...] = _dot(m, wk_ref[...]).astype(BF16)
    v_ref[...] = _dot(m, wv_ref[...]).astype(BF16)


def _mem_kv(mem, g, wk, wv):
    B, M, D = mem.shape
    return pl.pallas_call(
        _mem_kv_kernel,
        grid=(B,),
        in_specs=[pl.BlockSpec((None, M, D), lambda b: (b, 0, 0)),
                  pl.BlockSpec((1, D), lambda b: (0, 0)),
                  pl.BlockSpec((D, D), lambda b: (0, 0)),
                  pl.BlockSpec((D, D), lambda b: (0, 0))],
        out_specs=(pl.BlockSpec((None, M, D), lambda b: (b, 0, 0)),
                   pl.BlockSpec((None, M, D), lambda b: (b, 0, 0))),
        out_shape=(jax.ShapeDtypeStruct((B, M, D), BF16),
                   jax.ShapeDtypeStruct((B, M, D), BF16)),
        compiler_params=pltpu.CompilerParams(
            dimension_semantics=("arbitrary",), vmem_limit_bytes=VMEM_LIMIT),
        name="mem_kv",
    )(mem, g, wk, wv)


def _token_prologue(x_ref, pos_ref, inv_ref, g_ref, h_ref, cos_ref, sin_ref):
    h_ref[...] = _rms(x_ref[...], g_ref[...]).astype(BF16)
    lane = lax.broadcasted_iota(jnp.int32, (1, LANES), 1)
    nfreq = RET_DK // 2
    group = lane // nfreq
    nblock = LANES // nfreq
    rows = x_ref.shape[0] // nblock
    pos = pos_ref[...].astype(F32)
    pos4 = jnp.zeros((rows, LANES), F32)
    for k in range(nblock):
        pos4 = jnp.where(group == k, pos[k * rows:(k + 1) * rows, :], pos4)
    ang4 = pos4 * inv_ref[...]

    def spread(t4):
        blocks = []
        for k in range(nblock):
            m = jnp.where(group == k, t4, 0.0)
            m = m + pltpu.roll(m, nfreq, 1)
            blocks.append(m + pltpu.roll(m, 2 * nfreq, 1))
        return jnp.concatenate(blocks, axis=0)

    first_half = (lane % RET_DK) < nfreq
    sin = spread(jnp.sin(ang4))
    cos_ref[...] = spread(jnp.cos(ang4))
    sin_ref[...] = jnp.where(first_half, -sin, sin)


def _proj_ret_kernel(x0_ref, pos0_ref, x_ref, pos_ref, inv_ref, g_ref, w_ref, gn_ref, u_ref, gs_ref, ret_ref,
                     cur_ref, prev_ref, hn_ref, cn_ref, sn_ref, h_ref, cos_ref, sin_ref,
                     state_ref, decay_ref, qw_ref, kw_ref, *, tiles_per_seq):
    t = pl.program_id(0)

    @pl.when(t == 0)
    def _init():
        _retention_constants(decay_ref, qw_ref, kw_ref)
        cur_ref[...] = jnp.zeros_like(cur_ref)
        _token_prologue(x0_ref, pos0_ref, inv_ref, g_ref, hn_ref, cn_ref, sn_ref)

    @pl.when((t == 0) | (t % tiles_per_seq == 1))
    def _reset():
        state_ref[...] = jnp.zeros_like(state_ref)

    prev_ref[...] = cur_ref[...]
    h_ref[...] = hn_ref[...]
    cos_ref[...] = cn_ref[...]
    sin_ref[...] = sn_ref[...]
    h = h_ref[...]
    cos = cos_ref[...]
    sin_signed = sin_ref[...]
    lane = lax.broadcasted_iota(jnp.int32, (1, LANES), 1)
    first_half = (lane % RET_DK) < (RET_DK // 2)

    def rope(p):
        partner = jnp.where(first_half,
                            pltpu.roll(p, LANES - RET_DK // 2, 1),
                            pltpu.roll(p, RET_DK // 2, 1))
        return p * cos + partner * sin_signed

    def proj_q():
        pq = _dot(h, w_ref[:, COL_Q:COL_K])
        for i in range(RET_QK_WIDTH // LANES):
            sl = slice(i * LANES, (i + 1) * LANES)
            cur_ref[:, COL_Q + i * LANES:COL_Q + (i + 1) * LANES] = rope(pq[:, sl]).astype(BF16)

    def proj_k():
        pk = _dot(h, w_ref[:, COL_K:COL_V])
        for i in range(RET_QK_WIDTH // LANES):
            sl = slice(i * LANES, (i + 1) * LANES)
            cur_ref[:, COL_K + i * LANES:COL_K + (i + 1) * LANES] = (
                rope(pk[:, sl]) * (RET_DK ** -0.5)).astype(BF16)

    def proj_v():
        cur_ref[:, COL_V:COL_GR] = _dot(h, w_ref[:, COL_V:COL_GR]).astype(BF16)

    def proj_gate():
        cur_ref[:, COL_GR:COL_U] = _silu(_dot(h, w_ref[:, COL_GR:COL_U])).astype(BF16)

    def proj_u():
        pu = _dot(h, w_ref[:, COL_U:COL_GS]).astype(BF16)
        for j in range(LANE_TILES):
            u_ref[j] = pu[:, j * LANES:(j + 1) * LANES]

    def proj_gs():
        gs_ref[...] = _silu(_dot(h, w_ref[:, COL_GS:IN_COLS])).astype(BF16)

    ret_items = _retention_items(prev_ref, gn_ref, ret_ref, state_ref, decay_ref, qw_ref, kw_ref)
    proj_items = [proj_q, proj_k, proj_v, proj_gate, proj_u, proj_gs]
    per = -(-len(ret_items) // len(proj_items))
    for i, item in enumerate(proj_items):
        for r in ret_items[i * per:(i + 1) * per]:
            r()
        item()
    _token_prologue(x_ref, pos_ref, inv_ref, g_ref, hn_ref, cn_ref, sn_ref)


def _proj_ret(x, pos3, inv, g, w, gn):
    B, L, D = x.shape
    tm = TM_PROJ
    nt = L // tm
    n = B * nt
    cur = lambda t: jnp.minimum(t, n - 1)
    nxt = lambda t: jnp.minimum(t + 1, n - 1)
    lag = lambda t: jnp.maximum(t - 1, 0)
    first = lambda t: 0 * t
    row = lambda width, tile: pl.BlockSpec((None, tm, width), lambda t: (tile(t) // nt, tile(t) % nt, 0))
    const = lambda shape: pl.BlockSpec(shape, lambda t: (0,) * len(shape))
    return pl.pallas_call(
        functools.partial(_proj_ret_kernel, tiles_per_seq=nt),
        grid=(n + 1,),
        in_specs=[row(D, first), row(1, first), row(D, nxt), row(1, nxt), const((1, LANES)), const((1, D)),
                  const((D, IN_COLS)), const((1, RET_V_WIDTH))],
        out_specs=(pl.BlockSpec((None, LANE_TILES, tm, LANES), lambda t: (cur(t) // nt, 0, cur(t) % nt, 0)),
                   row(S5_WIDTH, cur), row(RET_V_WIDTH, lag)),
        out_shape=(jax.ShapeDtypeStruct((B, LANE_TILES, L, LANES), BF16),
                   jax.ShapeDtypeStruct((B, L, S5_WIDTH), BF16),
                   jax.ShapeDtypeStruct((B, L, RET_V_WIDTH), BF16)),
        scratch_shapes=[pltpu.VMEM((tm, COL_U), BF16),
                        pltpu.VMEM((tm, COL_U), BF16),
                        pltpu.VMEM((tm, D), BF16), pltpu.VMEM((tm, LANES), F32), pltpu.VMEM((tm, LANES), F32),
                        pltpu.VMEM((tm, D), BF16), pltpu.VMEM((tm, LANES), F32), pltpu.VMEM((tm, LANES), F32),
                        pltpu.VMEM((RET_HEADS // 2, LANES, 2 * RET_DV), F32),
                        pltpu.VMEM((RET_HEADS // 2, CHUNK, 2 * CHUNK), F32),
                        pltpu.VMEM((RET_HEADS // 2, CHUNK, 2 * RET_DV), F32),
                        pltpu.VMEM((RET_HEADS // 2, CHUNK, LANES), F32)],
        compiler_params=pltpu.CompilerParams(
            dimension_semantics=("arbitrary",), vmem_limit_bytes=VMEM_LIMIT),
        name="proj_ret",
    )(x, pos3, x, pos3, inv, g, w, gn)


def _retention_constants(decay_ref, qw_ref, kw_ref):
    i = lax.broadcasted_iota(jnp.int32, (CHUNK, 2 * CHUNK), 0).astype(F32)
    c = lax.broadcasted_iota(jnp.int32, (CHUNK, 2 * CHUNK), 1)
    j = (c % CHUNK).astype(F32)
    diff = i - j
    lane = lax.broadcasted_iota(jnp.int32, (CHUNK, LANES), 1)
    ik = lax.broadcasted_iota(jnp.int32, (CHUNK, LANES), 0).astype(F32)
    for p in range(RET_HEADS // 2):
        lg = jnp.where(c < CHUNK, LOG_G[2 * p], LOG_G[2 * p + 1])
        decay_ref[p] = jnp.where(diff >= 0.0, jnp.exp(lg * jnp.maximum(diff, 0.0)), 0.0)
        qw_ref[p] = jnp.exp(lg * (i + 1.0))
        lgk = jnp.where(lane < RET_DK, LOG_G[2 * p], LOG_G[2 * p + 1])
        kw_ref[p] = jnp.exp(lgk * (CHUNK - 1.0 - ik))


def _retention_items(src_ref, gn_ref, o_ref, state_ref, decay_ref, qw_ref, kw_ref):
    pairs = RET_HEADS // 2
    low = lax.broadcasted_iota(jnp.int32, (1, LANES), 1) < RET_DK
    srow = lax.broadcasted_iota(jnp.int32, (LANES, 2 * RET_DV), 0)
    scol = lax.broadcasted_iota(jnp.int32, (LANES, 2 * RET_DV), 1)
    own_block = (srow < RET_DK) == (scol < RET_DV)
    scol1 = lax.broadcasted_iota(jnp.int32, (1, 2 * RET_DV), 1)
    zero_v = jnp.zeros((CHUNK, RET_DV), BF16)

    def pair_step(c, p):
        rows = slice(c * CHUNK, (c + 1) * CHUNK)
        qt = src_ref[rows, COL_Q + p * LANES:COL_Q + (p + 1) * LANES]
        kt = src_ref[rows, COL_K + p * LANES:COL_K + (p + 1) * LANES]
        vp = src_ref[rows, COL_V + 2 * p * RET_DV:COL_V + 2 * (p + 1) * RET_DV]
        state = state_ref[p]
        cross = _dot(qt, state.astype(BF16))
        kw = (kt.astype(F32) * kw_ref[p]).astype(BF16)
        upd = lax.dot_general(kw, vp, (((0,), (0,)), ((), ())), preferred_element_type=F32)
        chunk_decay = jnp.where(scol1 < RET_DV,
                                math.exp(LOG_G[2 * p] * CHUNK), math.exp(LOG_G[2 * p + 1] * CHUNK))
        state_ref[p] = state * chunk_decay + jnp.where(own_block, upd, 0.0)

        zero_k = jnp.zeros_like(kt)
        k_rows = jnp.concatenate([jnp.where(low, kt, zero_k), jnp.where(low, zero_k, kt)], axis=0)
        s = lax.dot_general(qt, k_rows, (((1,), (1,)), ((), ())), preferred_element_type=F32)
        pm = (s * decay_ref[p]).astype(BF16)
        v_diag = jnp.concatenate(
            [jnp.concatenate([vp[:, 0:RET_DV], zero_v], axis=1),
             jnp.concatenate([zero_v, vp[:, RET_DV:2 * RET_DV]], axis=1)], axis=0)
        o2 = _dot(pm, v_diag) + qw_ref[p] * cross
        for e in range(2):
            h = 2 * p + e
            cols = slice(h * RET_DV, (h + 1) * RET_DV)
            o = o2[:, e * RET_DV:(e + 1) * RET_DV]
            mu = jnp.mean(o, axis=-1, keepdims=True)
            oc = o - mu
            var = jnp.mean(oc * oc, axis=-1, keepdims=True)
            on = oc * lax.rsqrt(var + EPS) * gn_ref[:, cols]
            gate = src_ref[rows, COL_GR + h * RET_DV:COL_GR + (h + 1) * RET_DV].astype(F32)
            o_ref[rows, cols] = (on * gate).astype(BF16)

    return [functools.partial(pair_step, c, p)
            for c in range(src_ref.shape[0] // CHUNK) for p in range(pairs)]


def _s5_scan_kernel(u_ref, wb_ref, consts_ref, wck_ref, d_ref, y_ref, *scratch, tiles_per_seq):
    uf_refs, lhs_refs, st_refs = scratch[0:3], scratch[3:6], scratch[6:9]
    sb_ref, ys_ref, carry_ref = scratch[9:]
    n = pl.program_id(0)
    s = S5_TAPS
    tm = u_ref.shape[0]
    nblk = tm // s
    half = TILE_STATES

    @pl.when(n == 0)
    def _init():
        for ref in scratch[0:9]:
            ref[...] = jnp.zeros_like(ref)

    @pl.when((n == 0) | ((n - 1) % tiles_per_seq == 0))
    def _reset():
        carry_ref[...] = jnp.zeros_like(carry_ref)

    ncol = 2 * LANES

    def increments(slot):
        uf_ref, lhs_ref, st_ref = uf_refs[slot], lhs_refs[slot], st_refs[slot]

        def piece(c):
            if c == 0:
                uf_ref[...] = u_ref[...].astype(F32)
                lhs_ref[...] = jnp.concatenate(
                    [uf_ref[pl.ds(k, nblk, stride=s), :] for k in range(s)], axis=1).astype(BF16)
            z = _dot(lhs_ref[...], wb_ref[:, c * ncol:(c + 1) * ncol])
            for i in range(ncol // LANES):
                st_ref[c * (ncol // LANES) + i] = z[:, i * LANES:(i + 1) * LANES]

        return [functools.partial(piece, c) for c in range(2 * half // ncol)]

    ntile = half // LANES

    def scan(slot, row_pieces):
        st_ref = st_refs[slot]
        run = S5_RUN
        span = run * SUBLANES
        row = lax.broadcasted_iota(jnp.int32, (SUBLANES, LANES), 0)
        last = SUBLANES - 1
        groups = nblk // span
        per = -(-groups // row_pieces)
        tiles_per_piece = S5_SCAN_LANES // LANES

        def tile_group(t, g, carry):
            lanes = slice(t * LANES, (t + 1) * LANES)
            const = lambda idx: (consts_ref[idx, 0, :, lanes], consts_ref[idx, 1, :, lanes])
            cr, ci = carry
            rows = [pl.ds(g * span + j, SUBLANES, stride=run) for j in range(run)]
            loc = [(st_ref[t, rows[0], :], st_ref[ntile + t, rows[0], :])]
            for j in range(1, run):
                pr, pi = _cmul(*const(0), *loc[-1])
                loc.append((pr + st_ref[t, rows[j], :], pi + st_ref[ntile + t, rows[j], :]))
            fr, fi = loc[-1]
            for idx, d in enumerate((1, 2, 4)):
                pr, pi = _cmul(*const(1 + idx), pltpu.roll(fr, d, 0), pltpu.roll(fi, d, 0))
                fr, fi = fr + pr, fi + pi
            pr, pi = _cmul(*const(4), cr, ci)
            fr, fi = fr + pr, fi + pi
            er = jnp.where(row == 0, cr, pltpu.roll(fr, 1, 0))
            ei = jnp.where(row == 0, ci, pltpu.roll(fi, 1, 0))
            st_ref[t, rows[0], :] = er
            st_ref[ntile + t, rows[0], :] = ei
            for j in range(1, run):
                er, ei = _cmul(*const(0), er, ei)
                st_ref[t, rows[j], :] = loc[j - 1][0] + er
                st_ref[ntile + t, rows[j], :] = loc[j - 1][1] + ei
            return (jnp.broadcast_to(fr[last:, :], (SUBLANES, LANES)),
                    jnp.broadcast_to(fi[last:, :], (SUBLANES, LANES)))

        def piece(q, i, carries):
            tiles = range(q * tiles_per_piece, (q + 1) * tiles_per_piece)
            if i == 0:
                for t in tiles:
                    carries[t] = (carry_ref[:, t * LANES:(t + 1) * LANES],
                                  carry_ref[:, half + t * LANES:half + (t + 1) * LANES])
            for g in range(i * per, min((i + 1) * per, groups)):
                for t in tiles:
                    carries[t] = tile_group(t, g, carries[t])
            if i == row_pieces - 1:
                for t in tiles:
                    carry_ref[:, t * LANES:(t + 1) * LANES] = carries[t][0]
                    carry_ref[:, half + t * LANES:half + (t + 1) * LANES] = carries[t][1]

        carries = {}
        return [functools.partial(piece, q, i, carries)
                for q in range(ntile // tiles_per_piece) for i in range(row_pieces)]

    def outputs(slot):
        taps_per_piece = ncol // LANES

        def cast():
            for t in range(2 * ntile):
                sb_ref[:, t * LANES:(t + 1) * LANES] = st_refs[slot][t].astype(BF16)

        def piece(c):
            lhs = jnp.concatenate([sb_ref[...], lhs_refs[slot][...]], axis=1)
            yall = _dot(lhs, wck_ref[:, c * ncol:(c + 1) * ncol])
            for i in range(taps_per_piece):
                tap = c * taps_per_piece + i
                ys_ref[pl.ds(tap, nblk, stride=s), :] = yall[:, i * LANES:(i + 1) * LANES]

        def finish():
            y = ys_ref[...] + d_ref[...] * uf_refs[slot][...]
            y_ref[...] = jax.nn.gelu(y).astype(BF16)

        return [cast] + [functools.partial(piece, c) for c in range(s * LANES // ncol)] + [finish]

    for phase in range(3):
        @pl.when(n % 3 == phase)
        def _steps(phase=phase):
            mxu = increments(phase) + outputs((phase + 1) % 3)
            vpu = scan((phase + 2) % 3, 3)
            for i, piece in enumerate(mxu):
                piece()
                if i < len(vpu):
                    vpu[i]()
            for piece in vpu[len(mxu):]:
                piece()


def _s5_scan(u, wb, wck, consts, d):
    B, _, L, _ = u.shape
    s = S5_TAPS
    tm = TM_S5
    nt = L // tm
    ntiles = LANE_TILES * B * nt
    tile_of = lambda n, lag: jnp.clip(n - lag, 0, ntiles - 1)
    lane_tile = lambda n, lag: tile_of(n, lag) // (B * nt)

    def tok(lag):
        def index(n):
            i = tile_of(n, lag)
            return ((i // nt) % B, i // (B * nt), i % nt, 0)
        return pl.BlockSpec((None, None, tm, LANES), index)

    def par(lag, *shape):
        return pl.BlockSpec((None,) + shape, lambda n: (lane_tile(n, lag),) + (0,) * len(shape))

    nblk = tm // s
    return pl.pallas_call(
        functools.partial(_s5_scan_kernel, tiles_per_seq=nt),
        grid=(ntiles + 2,),
        in_specs=[tok(0), par(0, s * LANES, 2 * TILE_STATES), par(1, 5, 2, SUBLANES, TILE_STATES),
                  par(2, 2 * TILE_STATES + s * LANES, s * LANES), par(2, 1, LANES)],
        out_specs=tok(2),
        out_shape=jax.ShapeDtypeStruct((B, LANE_TILES, L, LANES), BF16),
        scratch_shapes=([pltpu.VMEM((tm, LANES), F32)] * 3
                        + [pltpu.VMEM((nblk, s * LANES), BF16)] * 3
                        + [pltpu.VMEM((2 * TILE_STATES // LANES, nblk, LANES), F32)] * 3
                        + [pltpu.VMEM((nblk, 2 * TILE_STATES), BF16), pltpu.VMEM((tm, LANES), F32),
                           pltpu.VMEM((SUBLANES, 2 * TILE_STATES), F32)]),
        compiler_params=pltpu.CompilerParams(
            dimension_semantics=("arbitrary",), vmem_limit_bytes=VMEM_LIMIT),
        name="s5_scan",
    )(u, wb, consts, wck, d)


def _out_xattn_kernel(x_ref, ret_ref, y_ref, gs_ref, gw_ref, gb_ref, wout_ref, g2_ref, wq_ref,
                      ka_ref, va_ref, wo_ref, gf_ref, o_ref):
    y = jnp.concatenate([y_ref[j] for j in range(LANE_TILES)], axis=1)
    z = _dot(y, gw_ref[...]) + gb_ref[...]
    ssm = (y.astype(F32) * _sigmoid(z) * gs_ref[...].astype(F32)).astype(BF16)
    x1 = (x_ref[...] + _dot(ret_ref[...], wout_ref[0:RET_V_WIDTH, :])
          + _dot(ssm, wout_ref[RET_V_WIDTH:RET_V_WIDTH + S5_WIDTH, :]))
    h2 = _rms(x1, g2_ref[...]).astype(BF16)
    qa = (_dot(h2, wq_ref[...]) * (XA_DH ** -0.5)).astype(BF16)
    heads = []
    for h in range(XA_HEADS):
        cols = slice(h * XA_DH, (h + 1) * XA_DH)
        s = lax.dot_general(qa[:, cols], ka_ref[:, cols], (((1,), (1,)), ((), ())),
                            preferred_element_type=F32)
        e = jnp.exp(s - jnp.max(s, axis=-1, keepdims=True))
        l = jnp.sum(e, axis=-1, keepdims=True)
        heads.append((_dot(e.astype(BF16), va_ref[:, cols]) * (1.0 / l)).astype(BF16))
    o = jnp.concatenate(heads, axis=-1)
    x2 = x1 + _dot(o, wo_ref[...])
    o_ref[...] = _rms(x2, gf_ref[...])


def _out_xattn(x, ret, y, gs, gw, gb, wout, g2, wq, ka, va, wo, gf):
    B, L, D = x.shape
    tm = TM_OUT
    row = pl.BlockSpec((None, tm, D), lambda b, t: (b, t, 0))
    const = lambda shape: pl.BlockSpec(shape, lambda b, t: (0,) * len(shape))
    mem = pl.BlockSpec((None, MEM_LEN, D), lambda b, t: (b, 0, 0))
    ytile = pl.BlockSpec((None, LANE_TILES, tm, LANES), lambda b, t: (b, 0, t, 0))
    return pl.pallas_call(
        _out_xattn_kernel,
        grid=(B, L // tm),
        in_specs=[row, row, ytile, row, const((D, D)), const((1, D)), const(wout.shape), const((1, D)),
                  const((D, D)), mem, mem, const((D, D)), const((1, D))],
        out_specs=row,
        out_shape=jax.ShapeDtypeStruct((B, L, D), F32),
        compiler_params=pltpu.CompilerParams(
            dimension_semantics=("arbitrary", "arbitrary"), vmem_limit_bytes=VMEM_LIMIT),
        name="out_xattn",
    )(x, ret, y, gs, gw, gb, wout, g2, wq, ka, va, wo, gf)


def _rope_inv_lanes():
    half = RET_DK // 2
    inv = ROPE_BASE ** (-np.arange(half, dtype=np.float64) / half)
    return jnp.asarray(np.tile(inv, LANES // half).reshape(1, LANES), F32)


def kernel(x, mem, positions, norm1_g, w_in, ret_gn_g, s5_a_re, s5_a_im, s5_log_dt, s5_b_re, s5_b_im, s5_c_re, s5_c_im, s5_d, s5_glu_w, s5_glu_b, w_out, norm2_g, norm_mem_g, xa_wq, xa_wk, xa_wv, xa_wo, norm_f_g):
    B, L, D = x.shape
    l = 0
    bf = lambda w: w.astype(BF16)
    rowvec = lambda v: v.reshape(1, -1)

    wb, wck, consts = _s5_prep(s5_a_re[l], s5_a_im[l], s5_log_dt[l], s5_b_re[l], s5_b_im[l],
                               s5_c_re[l], s5_c_im[l])
    ka, va = _mem_kv(mem, rowvec(norm_mem_g[l]), bf(xa_wk[l]), bf(xa_wv[l]))
    u, g_s5, ret = _proj_ret(x, positions.reshape(B, L, 1), _rope_inv_lanes(),
                             rowvec(norm1_g[l]), bf(w_in[l]), rowvec(ret_gn_g[l]))
    y = _s5_scan(u, wb, wck, consts, s5_d[l].reshape(LANE_TILES, 1, LANES))
    return _out_xattn(x, ret, y, g_s5, bf(s5_glu_w[l]), rowvec(s5_glu_b[l]), bf(w_out[l]),
                      rowvec(norm2_g[l]), bf(xa_wq[l]), ka, va, bf(xa_wo[l]), rowvec(norm_f_g))
```

```python
import functools
import math

import numpy as np
import jax
import jax.numpy as jnp
from jax import lax
from jax.experimental import pallas as pl
from jax.experimental.pallas import tpu as pltpu

F32 = jnp.float32
BF16 = jnp.bfloat16

D_MODEL = 1024
MEM_LEN = 256
EPS = 1e-6
ROPE_BASE = 10000.0

RET_HEADS = 8
RET_QK_WIDTH = 512
RET_V_WIDTH = 1024
RET_DK = 64
RET_DV = 128
CHUNK = 128

S5_WIDTH = 1024
S5_GROUP = 16
S5_GROUPS = 64
S5_STATE = 64
S5_NSTATE = S5_GROUPS * S5_STATE

IN_COLS = 5120
COL_Q, COL_K, COL_V, COL_GR, COL_U, COL_GS = 0, 512, 1024, 2048, 3072, 4096

XA_HEADS = 4
XA_DH = 256

LANES = 128
SUBLANES = 8
VMEM_LIMIT = 48 * 1024 * 1024

LANE_TILES = S5_WIDTH // LANES
TILE_GROUPS = LANES // S5_GROUP
TILE_STATES = S5_NSTATE // LANE_TILES
S5_TAPS = 4
S5_SCAN_LANES = 2 * LANES
S5_RUN = 4

LOG_G = tuple(math.log1p(-(2.0 ** (-5.0 - h))) for h in range(RET_HEADS))

TM_PROJ = 512
TM_S5 = 4096
TM_OUT = 512


def _rms(x, g):
    ms = jnp.mean(x * x, axis=-1, keepdims=True)
    return x * lax.rsqrt(ms + EPS) * g


def _sigmoid(z):
    return 1.0 / (1.0 + jnp.exp(-z))


def _silu(g):
    return g * _sigmoid(g)


def _dot(a, b):
    return jnp.dot(a, b, preferred_element_type=F32)


def _cmul(ar, ai, br, bi):
    return ar * br - ai * bi, ar * bi + ai * br


def _s5_discretise(ar, ai, ldt):
    dt = jnp.exp(ldt)
    mag = jnp.exp(ar * dt)
    p_re = mag * jnp.cos(ai * dt)
    p_im = mag * jnp.sin(ai * dt)
    den = ar * ar + ai * ai
    nr, ni = p_re - 1.0, p_im
    f_re = (nr * ar + ni * ai) / den
    f_im = (ni * ar - nr * ai) / den
    return p_re, p_im, f_re, f_im


def _powers(p_re, p_im, n):
    pw = [(jnp.ones_like(p_re), jnp.zeros_like(p_im))]
    for _ in range(n):
        pw.append(_cmul(pw[-1][0], pw[-1][1], p_re, p_im))
    return pw


def _block_diag(x):
    tiled = jnp.concatenate([x] * TILE_GROUPS, axis=1)
    r = lax.broadcasted_iota(jnp.int32, tiled.shape, 0)
    c = lax.broadcasted_iota(jnp.int32, tiled.shape, 1)
    return jnp.where(r // S5_GROUP == c // S5_STATE, tiled, 0.0)


def _s5_prep_kernel(a_ref, ag_ref, b_ref, c_ref, wb_ref, wck_ref, consts_ref):
    s = S5_TAPS
    half = TILE_STATES
    def per_row(v):
        return jnp.broadcast_to(v[:, None, :], (TILE_GROUPS, S5_GROUP, S5_STATE)).reshape(LANES, S5_STATE)

    g_re, g_im, gf_re, gf_im = _s5_discretise(ag_ref[0], ag_ref[1], ag_ref[2])
    gw = [(per_row(r), per_row(i)) for r, i in _powers(g_re, g_im, s)]
    bbr, bbi = _cmul(per_row(gf_re), per_row(gf_im), b_ref[0], b_ref[1])
    for k in range(s):
        wr, wi = _cmul(bbr, bbi, *gw[s - 1 - k])
        wb_ref[k * LANES:(k + 1) * LANES, 0:half] = _block_diag(wr).astype(BF16)
        wb_ref[k * LANES:(k + 1) * LANES, half:2 * half] = _block_diag(wi).astype(BF16)

    bb = jnp.concatenate([bbr, bbi], axis=1)
    r = lax.broadcasted_iota(jnp.int32, (LANES, LANES), 0)
    c = lax.broadcasted_iota(jnp.int32, (LANES, LANES), 1)
    same_group = r // S5_GROUP == c // S5_GROUP
    taps = []
    for d in range(s + 1):
        er, ei = _cmul(c_ref[0], c_ref[1], *gw[d])
        if d >= 1:
            cwt = jnp.concatenate([_block_diag(er), -_block_diag(ei)], axis=1)
            wck_ref[0:2 * half, (d - 1) * LANES:d * LANES] = cwt.T.astype(BF16)
        if d < s:
            t = lax.dot_general(bb, jnp.concatenate([er, -ei], axis=1), (((1,), (1,)), ((), ())),
                                preferred_element_type=F32, precision=lax.Precision.HIGHEST)
            taps.append(jnp.where(same_group, t, 0.0))
    zero = jnp.zeros((LANES, LANES), BF16)
    for m in range(s):
        for i in range(s):
            blk = taps[i - m].astype(BF16) if m <= i else zero
            wck_ref[2 * half + m * LANES:2 * half + (m + 1) * LANES, i * LANES:(i + 1) * LANES] = blk

    p_re, p_im, _, _ = _s5_discretise(a_ref[0:1, :], a_ref[1:2, :], a_ref[2:3, :])
    pw = _powers(p_re, p_im, s)
    shape = (SUBLANES, half)
    qr = _powers(pw[s][0], pw[s][1], S5_RUN)
    consts_ref[0, 0] = jnp.broadcast_to(qr[1][0], shape)
    consts_ref[0, 1] = jnp.broadcast_to(qr[1][1], shape)
    rp = _powers(qr[S5_RUN][0], qr[S5_RUN][1], SUBLANES)
    row = lax.broadcasted_iota(jnp.int32, shape, 0)
    for idx, d in enumerate((1, 2, 4)):
        consts_ref[1 + idx, 0] = jnp.where(row >= d, rp[d][0], 0.0)
        consts_ref[1 + idx, 1] = jnp.where(row >= d, rp[d][1], 0.0)
    cr = jnp.zeros(shape, F32)
    ci = jnp.zeros(shape, F32)
    for k in range(SUBLANES):
        cr = jnp.where(row == k, rp[k + 1][0], cr)
        ci = jnp.where(row == k, rp[k + 1][1], ci)
    consts_ref[4, 0] = cr
    consts_ref[4, 1] = ci


def _s5_prep(a_re, a_im, log_dt, b_re, b_im, c_re, c_im):
    s = S5_TAPS
    ldt = jnp.broadcast_to(log_dt[:, None], (S5_GROUPS, S5_STATE))
    ag = jnp.stack([a_re, a_im, ldt])
    a3 = ag.reshape(3, S5_NSTATE)
    bt = jnp.stack([b_re, b_im]).transpose(0, 1, 3, 2).reshape(2, S5_WIDTH, S5_STATE)
    ct = jnp.stack([c_re, c_im]).reshape(2, S5_WIDTH, S5_STATE)
    tile = lambda *shape: pl.BlockSpec((None,) + shape, lambda j: (j,) + (0,) * len(shape))
    return pl.pallas_call(
        _s5_prep_kernel,
        grid=(LANE_TILES,),
        in_specs=[pl.BlockSpec((3, TILE_STATES), lambda j: (0, j)),
                  pl.BlockSpec((3, TILE_GROUPS, S5_STATE), lambda j: (0, j, 0)),
                  pl.BlockSpec((2, LANES, S5_STATE), lambda j: (0, j, 0)),
                  pl.BlockSpec((2, LANES, S5_STATE), lambda j: (0, j, 0))],
        out_specs=(tile(s * LANES, 2 * TILE_STATES),
                   tile(2 * TILE_STATES + s * LANES, s * LANES),
                   tile(5, 2, SUBLANES, TILE_STATES)),
        out_shape=(jax.ShapeDtypeStruct((LANE_TILES, s * LANES, 2 * TILE_STATES), BF16),
                   jax.ShapeDtypeStruct((LANE_TILES, 2 * TILE_STATES + s * LANES, s * LANES), BF16),
                   jax.ShapeDtypeStruct((LANE_TILES, 5, 2, SUBLANES, TILE_STATES), F32)),
        compiler_params=pltpu.CompilerParams(
            dimension_semantics=("arbitrary",), vmem_limit_bytes=VMEM_LIMIT),
        name="s5_prep",
    )(a3, ag, bt, ct)


def _mem_kv_kernel(mem_ref, g_ref, wk_ref, wv_ref, k_ref, v_ref):
    m = _rms(mem_ref[...], g_ref[...]).astype(BF16)
    k_ref[...] = _dot(m, wk_ref[...]).astype(BF16)
    v_ref[...] = _dot(m, wv_ref[...]).astype(BF16)


def _mem_kv(mem, g, wk, wv):
    B, M, D = mem.shape
    return pl.pallas_call(
        _mem_kv_kernel,
        grid=(B,),
        in_specs=[pl.BlockSpec((None, M, D), lambda b: (b, 0, 0)),
                  pl.BlockSpec((1, D), lambda b: (0, 0)),
                  pl.BlockSpec((D, D), lambda b: (0, 0)),
                  pl.BlockSpec((D, D), lambda b: (0, 0))],
        out_specs=(pl.BlockSpec((None, M, D), lambda b: (b, 0, 0)),
                   pl.BlockSpec((None, M, D), lambda b: (b, 0, 0))),
        out_shape=(jax.ShapeDtypeStruct((B, M, D), BF16),
                   jax.ShapeDtypeStruct((B, M, D), BF16)),
        compiler_params=pltpu.CompilerParams(
            dimension_semantics=("arbitrary",), vmem_limit_bytes=VMEM_LIMIT),
        name="mem_kv",
    )(mem, g, wk, wv)


def _token_prologue(x_ref, pos_ref, inv_ref, g_ref, h_ref, cos_ref, sin_ref):
    h_ref[...] = _rms(x_ref[...], g_ref[...]).astype(BF16)
    lane = lax.broadcasted_iota(jnp.int32, (1, LANES), 1)
    nfreq = RET_DK // 2
    group = lane // nfreq
    nblock = LANES // nfreq
    rows = x_ref.shape[0] // nblock
    pos = pos_ref[...].astype(F32)
    pos4 = jnp.zeros((rows, LANES), F32)
    for k in range(nblock):
        pos4 = jnp.where(group == k, pos[k * rows:(k + 1) * rows, :], pos4)
    ang4 = pos4 * inv_ref[...]

    def spread(t4):
        blocks = []
        for k in range(nblock):
            m = jnp.where(group == k, t4, 0.0)
            m = m + pltpu.roll(m, nfreq, 1)
            blocks.append(m + pltpu.roll(m, 2 * nfreq, 1))
        return jnp.concatenate(blocks, axis=0)

    first_half = (lane % RET_DK) < nfreq
    sin = spread(jnp.sin(ang4))
    cos_ref[...] = spread(jnp.cos(ang4))
    sin_ref[...] = jnp.where(first_half, -sin, sin)


def _proj_ret_kernel(x0_ref, pos0_ref, x_ref, pos_ref, inv_ref, g_ref, w_ref, gn_ref, u_ref, gs_ref, ret_ref,
                     cur_ref, prev_ref, hn_ref, cn_ref, sn_ref, h_ref, cos_ref, sin_ref,
                     state_ref, decay_ref, qw_ref, kw_ref, *, tiles_per_seq, num_tiles):
    t = pl.program_id(0)

    @pl.when(t == 0)
    def _init():
        _retention_constants(decay_ref, qw_ref, kw_ref)
        cur_ref[...] = jnp.zeros_like(cur_ref)
        _token_prologue(x0_ref, pos0_ref, inv_ref, g_ref, hn_ref, cn_ref, sn_ref)

    @pl.when((t == 0) | (t % tiles_per_seq == 1))
    def _reset():
        state_ref[...] = jnp.zeros_like(state_ref)

    ret_items = _retention_items(prev_ref, gn_ref, ret_ref, state_ref, decay_ref, qw_ref, kw_ref)

    @pl.when(t == num_tiles)
    def _drain():
        prev_ref[...] = cur_ref[...]
        for r in ret_items:
            r()

    @pl.when(t < num_tiles)
    def _step():
        prev_ref[...] = cur_ref[...]
        h_ref[...] = hn_ref[...]
        cos_ref[...] = cn_ref[...]
        sin_ref[...] = sn_ref[...]
        h = h_ref[...]
        cos = cos_ref[...]
        sin_signed = sin_ref[...]
        lane = lax.broadcasted_iota(jnp.int32, (1, LANES), 1)
        first_half = (lane % RET_DK) < (RET_DK // 2)

        def rope(p):
            partner = jnp.where(first_half,
                                pltpu.roll(p, LANES - RET_DK // 2, 1),
                                pltpu.roll(p, RET_DK // 2, 1))
            return p * cos + partner * sin_signed

        def proj_q():
            pq = _dot(h, w_ref[:, COL_Q:COL_K])
            for i in range(RET_QK_WIDTH // LANES):
                sl = slice(i * LANES, (i + 1) * LANES)
                cur_ref[:, COL_Q + i * LANES:COL_Q + (i + 1) * LANES] = rope(pq[:, sl]).astype(BF16)

        def proj_k():
            pk = _dot(h, w_ref[:, COL_K:COL_V])
            for i in range(RET_QK_WIDTH // LANES):
                sl = slice(i * LANES, (i + 1) * LANES)
                cur_ref[:, COL_K + i * LANES:COL_K + (i + 1) * LANES] = (
                    rope(pk[:, sl]) * (RET_DK ** -0.5)).astype(BF16)

        def proj_v():
            cur_ref[:, COL_V:COL_GR] = _dot(h, w_ref[:, COL_V:COL_GR]).astype(BF16)

        def proj_gate():
            cur_ref[:, COL_GR:COL_U] = _silu(_dot(h, w_ref[:, COL_GR:COL_U])).astype(BF16)

        def proj_u():
            pu = _dot(h, w_ref[:, COL_U:COL_GS]).astype(BF16)
            for j in range(LANE_TILES):
                u_ref[j] = pu[:, j * LANES:(j + 1) * LANES]

        def proj_gs():
            gs_ref[...] = _silu(_dot(h, w_ref[:, COL_GS:IN_COLS])).astype(BF16)

        proj_items = [proj_q, proj_k, proj_v, proj_gate, proj_u, proj_gs]
        per = -(-len(ret_items) // len(proj_items))
        for i, item in enumerate(proj_items):
            for r in ret_items[i * per:(i + 1) * per]:
                r()
            item()
        _token_prologue(x_ref, pos_ref, inv_ref, g_ref, hn_ref, cn_ref, sn_ref)


def _proj_ret(x, pos3, inv, g, w, gn):
    B, L, D = x.shape
    tm = TM_PROJ
    nt = L // tm
    n = B * nt
    cur = lambda t: jnp.minimum(t, n - 1)
    nxt = lambda t: jnp.minimum(t + 1, n - 1)
    lag = lambda t: jnp.maximum(t - 1, 0)
    first = lambda t: 0 * t
    row = lambda width, tile: pl.BlockSpec((None, tm, width), lambda t: (tile(t) // nt, tile(t) % nt, 0))
    const = lambda shape: pl.BlockSpec(shape, lambda t: (0,) * len(shape))
    return pl.pallas_call(
        functools.partial(_proj_ret_kernel, tiles_per_seq=nt, num_tiles=n),
        grid=(n + 1,),
        in_specs=[row(D, first), row(1, first), row(D, nxt), row(1, nxt), const((1, LANES)), const((1, D)),
                  const((D, IN_COLS)), const((1, RET_V_WIDTH))],
        out_specs=(pl.BlockSpec((None, LANE_TILES, tm, LANES), lambda t: (cur(t) // nt, 0, cur(t) % nt, 0)),
                   row(S5_WIDTH, cur), row(RET_V_WIDTH, lag)),
        out_shape=(jax.ShapeDtypeStruct((B, LANE_TILES, L, LANES), BF16),
                   jax.ShapeDtypeStruct((B, L, S5_WIDTH), BF16),
                   jax.ShapeDtypeStruct((B, L, RET_V_WIDTH), BF16)),
        scratch_shapes=[pltpu.VMEM((tm, COL_U), BF16),
                        pltpu.VMEM((tm, COL_U), BF16),
                        pltpu.VMEM((tm, D), BF16), pltpu.VMEM((tm, LANES), F32), pltpu.VMEM((tm, LANES), F32),
                        pltpu.VMEM((tm, D), BF16), pltpu.VMEM((tm, LANES), F32), pltpu.VMEM((tm, LANES), F32),
                        pltpu.VMEM((RET_HEADS // 2, LANES, 2 * RET_DV), F32),
                        pltpu.VMEM((RET_HEADS // 2, CHUNK, 2 * CHUNK), F32),
                        pltpu.VMEM((RET_HEADS // 2, CHUNK, 2 * RET_DV), F32),
                        pltpu.VMEM((RET_HEADS // 2, CHUNK, LANES), F32)],
        compiler_params=pltpu.CompilerParams(
            dimension_semantics=("arbitrary",), vmem_limit_bytes=VMEM_LIMIT),
        name="proj_ret",
    )(x, pos3, x, pos3, inv, g, w, gn)


def _retention_constants(decay_ref, qw_ref, kw_ref):
    i = lax.broadcasted_iota(jnp.int32, (CHUNK, 2 * CHUNK), 0).astype(F32)
    c = lax.broadcasted_iota(jnp.int32, (CHUNK, 2 * CHUNK), 1)
    j = (c % CHUNK).astype(F32)
    diff = i - j
    lane = lax.broadcasted_iota(jnp.int32, (CHUNK, LANES), 1)
    ik = lax.broadcasted_iota(jnp.int32, (CHUNK, LANES), 0).astype(F32)
    for p in range(RET_HEADS // 2):
        lg = jnp.where(c < CHUNK, LOG_G[2 * p], LOG_G[2 * p + 1])
        decay_ref[p] = jnp.where(diff >= 0.0, jnp.exp(lg * jnp.maximum(diff, 0.0)), 0.0)
        qw_ref[p] = jnp.exp(lg * (i + 1.0))
        lgk = jnp.where(lane < RET_DK, LOG_G[2 * p], LOG_G[2 * p + 1])
        kw_ref[p] = jnp.exp(lgk * (CHUNK - 1.0 - ik))


def _retention_items(src_ref, gn_ref, o_ref, state_ref, decay_ref, qw_ref, kw_ref):
    pairs = RET_HEADS // 2
    low = lax.broadcasted_iota(jnp.int32, (1, LANES), 1) < RET_DK
    srow = lax.broadcasted_iota(jnp.int32, (LANES, 2 * RET_DV), 0)
    scol = lax.broadcasted_iota(jnp.int32, (LANES, 2 * RET_DV), 1)
    own_block = (srow < RET_DK) == (scol < RET_DV)
    scol1 = lax.broadcasted_iota(jnp.int32, (1, 2 * RET_DV), 1)
    zero_v = jnp.zeros((CHUNK, RET_DV), BF16)

    def pair_step(c, p):
        rows = slice(c * CHUNK, (c + 1) * CHUNK)
        qt = src_ref[rows, COL_Q + p * LANES:COL_Q + (p + 1) * LANES]
        kt = src_ref[rows, COL_K + p * LANES:COL_K + (p + 1) * LANES]
        vp = src_ref[rows, COL_V + 2 * p * RET_DV:COL_V + 2 * (p + 1) * RET_DV]
        state = state_ref[p]
        cross = _dot(qt, state.astype(BF16))
        kw = (kt.astype(F32) * kw_ref[p]).astype(BF16)
        upd = lax.dot_general(kw, vp, (((0,), (0,)), ((), ())), preferred_element_type=F32)
        chunk_decay = jnp.where(scol1 < RET_DV,
                                math.exp(LOG_G[2 * p] * CHUNK), math.exp(LOG_G[2 * p + 1] * CHUNK))
        state_ref[p] = state * chunk_decay + jnp.where(own_block, upd, 0.0)

        zero_k = jnp.zeros_like(kt)
        k_rows = jnp.concatenate([jnp.where(low, kt, zero_k), jnp.where(low, zero_k, kt)], axis=0)
        s = lax.dot_general(qt, k_rows, (((1,), (1,)), ((), ())), preferred_element_type=F32)
        pm = (s * decay_ref[p]).astype(BF16)
        v_diag = jnp.concatenate(
            [jnp.concatenate([vp[:, 0:RET_DV], zero_v], axis=1),
             jnp.concatenate([zero_v, vp[:, RET_DV:2 * RET_DV]], axis=1)], axis=0)
        o2 = _dot(pm, v_diag) + qw_ref[p] * cross
        for e in range(2):
            h = 2 * p + e
            cols = slice(h * RET_DV, (h + 1) * RET_DV)
            o = o2[:, e * RET_DV:(e + 1) * RET_DV]
            mu = jnp.mean(o, axis=-1, keepdims=True)
            oc = o - mu
            var = jnp.mean(oc * oc, axis=-1, keepdims=True)
            on = oc * lax.rsqrt(var + EPS) * gn_ref[:, cols]
            gate = src_ref[rows, COL_GR + h * RET_DV:COL_GR + (h + 1) * RET_DV].astype(F32)
            o_ref[rows, cols] = (on * gate).astype(BF16)

    return [functools.partial(pair_step, c, p)
            for c in range(src_ref.shape[0] // CHUNK) for p in range(pairs)]


def _s5_scan_kernel(u_ref, wb_ref, consts_ref, wck_ref, d_ref, y_ref, *scratch, tiles_per_seq):
    uf_refs, lhs_refs, st_refs = scratch[0:3], scratch[3:6], scratch[6:9]
    sb_ref, ys_ref, carry_ref = scratch[9:]
    n = pl.program_id(0)
    s = S5_TAPS
    tm = u_ref.shape[0]
    nblk = tm // s
    half = TILE_STATES

    @pl.when(n == 0)
    def _init():
        for ref in scratch[0:9]:
            ref[...] = jnp.zeros_like(ref)

    @pl.when((n == 0) | ((n - 1) % tiles_per_seq == 0))
    def _reset():
        carry_ref[...] = jnp.zeros_like(carry_ref)

    ncol = 2 * LANES

    def increments(slot):
        uf_ref, lhs_ref, st_ref = uf_refs[slot], lhs_refs[slot], st_refs[slot]

        def piece(c):
            if c == 0:
                uf_ref[...] = u_ref[...].astype(F32)
                lhs_ref[...] = jnp.concatenate(
                    [uf_ref[pl.ds(k, nblk, stride=s), :] for k in range(s)], axis=1).astype(BF16)
            z = _dot(lhs_ref[...], wb_ref[:, c * ncol:(c + 1) * ncol])
            for i in range(ncol // LANES):
                st_ref[c * (ncol // LANES) + i] = z[:, i * LANES:(i + 1) * LANES]

        return [functools.partial(piece, c) for c in range(2 * half // ncol)]

    ntile = half // LANES

    def scan(slot, row_pieces):
        st_ref = st_refs[slot]
        run = S5_RUN
        span = run * SUBLANES
        row = lax.broadcasted_iota(jnp.int32, (SUBLANES, LANES), 0)
        last = SUBLANES - 1
        groups = nblk // span
        per = -(-groups // row_pieces)
        tiles_per_piece = S5_SCAN_LANES // LANES

        def tile_group(t, g, carry):
            lanes = slice(t * LANES, (t + 1) * LANES)
            const = lambda idx: (consts_ref[idx, 0, :, lanes], consts_ref[idx, 1, :, lanes])
            cr, ci = carry
            rows = [pl.ds(g * span + j, SUBLANES, stride=run) for j in range(run)]
            loc = [(st_ref[t, rows[0], :], st_ref[ntile + t, rows[0], :])]
            for j in range(1, run):
                pr, pi = _cmul(*const(0), *loc[-1])
                loc.append((pr + st_ref[t, rows[j], :], pi + st_ref[ntile + t, rows[j], :]))
            fr, fi = loc[-1]
            for idx, d in enumerate((1, 2, 4)):
                pr, pi = _cmul(*const(1 + idx), pltpu.roll(fr, d, 0), pltpu.roll(fi, d, 0))
                fr, fi = fr + pr, fi + pi
            pr, pi = _cmul(*const(4), cr, ci)
            fr, fi = fr + pr, fi + pi
            er = jnp.where(row == 0, cr, pltpu.roll(fr, 1, 0))
            ei = jnp.where(row == 0, ci, pltpu.roll(fi, 1, 0))
            st_ref[t, rows[0], :] = er
            st_ref[ntile + t, rows[0], :] = ei
            for j in range(1, run):
                er, ei = _cmul(*const(0), er, ei)
                st_ref[t, rows[j], :] = loc[j - 1][0] + er
                st_ref[ntile + t, rows[j], :] = loc[j - 1][1] + ei
            return (jnp.broadcast_to(fr[last:, :], (SUBLANES, LANES)),
                    jnp.broadcast_to(fi[last:, :], (SUBLANES, LANES)))

        def piece(q, i, carries):
            tiles = range(q * tiles_per_piece, (q + 1) * tiles_per_piece)
            if i == 0:
                for t in tiles:
                    carries[t] = (carry_ref[:, t * LANES:(t + 1) * LANES],
                                  carry_ref[:, half + t * LANES:half + (t + 1) * LANES])
            for g in range(i * per, min((i + 1) * per, groups)):
                for t in tiles:
                    carries[t] = tile_group(t, g, carries[t])
            if i == row_pieces - 1:
                for t in tiles:
                    carry_ref[:, t * LANES:(t + 1) * LANES] = carries[t][0]
                    carry_ref[:, half + t * LANES:half + (t + 1) * LANES] = carries[t][1]

        carries = {}
        return [functools.partial(piece, q, i, carries)
                for q in range(ntile // tiles_per_piece) for i in range(row_pieces)]

    def outputs(slot):
        taps_per_piece = ncol // LANES

        def cast():
            for t in range(2 * ntile):
                sb_ref[:, t * LANES:(t + 1) * LANES] = st_refs[slot][t].astype(BF16)

        def piece(c):
            lhs = jnp.concatenate([sb_ref[...], lhs_refs[slot][...]], axis=1)
            yall = _dot(lhs, wck_ref[:, c * ncol:(c + 1) * ncol])
            for i in range(taps_per_piece):
                tap = c * taps_per_piece + i
                ys_ref[pl.ds(tap, nblk, stride=s), :] = yall[:, i * LANES:(i + 1) * LANES]

        def finish():
            y = ys_ref[...] + d_ref[...] * uf_refs[slot][...]
            y_ref[...] = jax.nn.gelu(y).astype(BF16)

        return [cast] + [functools.partial(piece, c) for c in range(s * LANES // ncol)] + [finish]

    for phase in range(3):
        @pl.when(n % 3 == phase)
        def _steps(phase=phase):
            mxu = increments(phase) + outputs((phase + 1) % 3)
            vpu = scan((phase + 2) % 3, 3)
            for i, piece in enumerate(mxu):
                piece()
                if i < len(vpu):
                    vpu[i]()
            for piece in vpu[len(mxu):]:
                piece()


def _s5_scan(u, wb, wck, consts, d):
    B, _, L, _ = u.shape
    s = S5_TAPS
    tm = TM_S5
    nt = L // tm
    ntiles = LANE_TILES * B * nt
    tile_of = lambda n, lag: jnp.clip(n - lag, 0, ntiles - 1)
    lane_tile = lambda n, lag: tile_of(n, lag) // (B * nt)

    def tok(lag):
        def index(n):
            i = tile_of(n, lag)
            return ((i // nt) % B, i // (B * nt), i % nt, 0)
        return pl.BlockSpec((None, None, tm, LANES), index)

    def par(lag, *shape):
        return pl.BlockSpec((None,) + shape, lambda n: (lane_tile(n, lag),) + (0,) * len(shape))

    nblk = tm // s
    return pl.pallas_call(
        functools.partial(_s5_scan_kernel, tiles_per_seq=nt),
        grid=(ntiles + 2,),
        in_specs=[tok(0), par(0, s * LANES, 2 * TILE_STATES), par(1, 5, 2, SUBLANES, TILE_STATES),
                  par(2, 2 * TILE_STATES + s * LANES, s * LANES), par(2, 1, LANES)],
        out_specs=tok(2),
        out_shape=jax.ShapeDtypeStruct((B, LANE_TILES, L, LANES), BF16),
        scratch_shapes=([pltpu.VMEM((tm, LANES), F32)] * 3
                        + [pltpu.VMEM((nblk, s * LANES), BF16)] * 3
                        + [pltpu.VMEM((2 * TILE_STATES // LANES, nblk, LANES), F32)] * 3
                        + [pltpu.VMEM((nblk, 2 * TILE_STATES), BF16), pltpu.VMEM((tm, LANES), F32),
                           pltpu.VMEM((SUBLANES, 2 * TILE_STATES), F32)]),
        compiler_params=pltpu.CompilerParams(
            dimension_semantics=("arbitrary",), vmem_limit_bytes=VMEM_LIMIT),
        name="s5_scan",
    )(u, wb, consts, wck, d)


def _out_xattn_kernel(x_ref, ret_ref, y_ref, gs_ref, gw_ref, gb_ref, wout_ref, g2_ref, wq_ref,
                      ka_ref, va_ref, wo_ref, gf_ref, o_ref):
    y = jnp.concatenate([y_ref[j] for j in range(LANE_TILES)], axis=1)
    z = _dot(y, gw_ref[...]) + gb_ref[...]
    ssm = (y.astype(F32) * _sigmoid(z) * gs_ref[...].astype(F32)).astype(BF16)
    x1 = (x_ref[...] + _dot(ret_ref[...], wout_ref[0:RET_V_WIDTH, :])
          + _dot(ssm, wout_ref[RET_V_WIDTH:RET_V_WIDTH + S5_WIDTH, :]))
    h2 = _rms(x1, g2_ref[...]).astype(BF16)
    qa = (_dot(h2, wq_ref[...]) * (XA_DH ** -0.5)).astype(BF16)
    heads = []
    for h in range(XA_HEADS):
        cols = slice(h * XA_DH, (h + 1) * XA_DH)
        s = lax.dot_general(qa[:, cols], ka_ref[:, cols], (((1,), (1,)), ((), ())),
                            preferred_element_type=F32)
        e = jnp.exp(s - jnp.max(s, axis=-1, keepdims=True))
        l = jnp.sum(e, axis=-1, keepdims=True)
        heads.append((_dot(e.astype(BF16), va_ref[:, cols]) * (1.0 / l)).astype(BF16))
    o = jnp.concatenate(heads, axis=-1)
    x2 = x1 + _dot(o, wo_ref[...])
    o_ref[...] = _rms(x2, gf_ref[...])


def _out_xattn(x, ret, y, gs, gw, gb, wout, g2, wq, ka, va, wo, gf):
    B, L, D = x.shape
    tm = TM_OUT
    row = pl.BlockSpec((None, tm, D), lambda b, t: (b, t, 0))
    const = lambda shape: pl.BlockSpec(shape, lambda b, t: (0,) * len(shape))
    mem = pl.BlockSpec((None, MEM_LEN, D), lambda b, t: (b, 0, 0))
    ytile = pl.BlockSpec((None, LANE_TILES, tm, LANES), lambda b, t: (b, 0, t, 0))
    return pl.pallas_call(
        _out_xattn_kernel,
        grid=(B, L // tm),
        in_specs=[row, row, ytile, row, const((D, D)), const((1, D)), const(wout.shape), const((1, D)),
                  const((D, D)), mem, mem, const((D, D)), const((1, D))],
        out_specs=row,
        out_shape=jax.ShapeDtypeStruct((B, L, D), F32),
        compiler_params=pltpu.CompilerParams(
            dimension_semantics=("arbitrary", "arbitrary"), vmem_limit_bytes=VMEM_LIMIT),
        name="out_xattn",
    )(x, ret, y, gs, gw, gb, wout, g2, wq, ka, va, wo, gf)


def _rope_inv_lanes():
    half = RET_DK // 2
    inv = ROPE_BASE ** (-np.arange(half, dtype=np.float64) / half)
    return jnp.asarray(np.tile(inv, LANES // half).reshape(1, LANES), F32)


def kernel(x, mem, positions, norm1_g, w_in, ret_gn_g, s5_a_re, s5_a_im, s5_log_dt, s5_b_re, s5_b_im, s5_c_re, s5_c_im, s5_d, s5_glu_w, s5_glu_b, w_out, norm2_g, norm_mem_g, xa_wq, xa_wk, xa_wv, xa_wo, norm_f_g):
    B, L, D = x.shape
    l = 0
    bf = lambda w: w.astype(BF16)
    rowvec = lambda v: v.reshape(1, -1)

    wb, wck, consts = _s5_prep(s5_a_re[l], s5_a_im[l], s5_log_dt[l], s5_b_re[l], s5_b_im[l],
                               s5_c_re[l], s5_c_im[l])
    ka, va = _mem_kv(mem, rowvec(norm_mem_g[l]), bf(xa_wk[l]), bf(xa_wv[l]))
    u, g_s5, ret = _proj_ret(x, positions.reshape(B, L, 1), _rope_inv_lanes(),
                             rowvec(norm1_g[l]), bf(w_in[l]), rowvec(ret_gn_g[l]))
    y = _s5_scan(u, wb, wck, consts, s5_d[l].reshape(LANE_TILES, 1, LANES))
    return _out_xattn(x, ret, y, g_s5, bf(s5_glu_w[l]), rowvec(s5_glu_b[l]), bf(w_out[l]),
                      rowvec(norm2_g[l]), bf(xa_wq[l]), ka, va, bf(xa_wo[l]), rowvec(norm_f_g))
```

```python
import functools
import math

import numpy as np
import jax
import jax.numpy as jnp
from jax import lax
from jax.experimental import pallas as pl
from jax.experimental.pallas import tpu as pltpu

F32 = jnp.float32
BF16 = jnp.bfloat16

D_MODEL = 1024
MEM_LEN = 256
EPS = 1e-6
ROPE_BASE = 10000.0

RET_HEADS = 8
RET_QK_WIDTH = 512
RET_V_WIDTH = 1024
RET_DK = 64
RET_DV = 128
CHUNK = 128

S5_WIDTH = 1024
S5_GROUP = 16
S5_GROUPS = 64
S5_STATE = 64
S5_NSTATE = S5_GROUPS * S5_STATE

IN_COLS = 5120
COL_Q, COL_K, COL_V, COL_GR, COL_U, COL_GS = 0, 512, 1024, 2048, 3072, 4096

XA_HEADS = 4
XA_DH = 256

LANES = 128
SUBLANES = 8
VMEM_LIMIT = 48 * 1024 * 1024

LANE_TILES = S5_WIDTH // LANES
TILE_GROUPS = LANES // S5_GROUP
TILE_STATES = S5_NSTATE // LANE_TILES
S5_TAPS = 4
S5_SCAN_LANES = 2 * LANES
S5_RUN = 4

LOG_G = tuple(math.log1p(-(2.0 ** (-5.0 - h))) for h in range(RET_HEADS))

TM_PROJ = 512
TM_S5 = 4096
TM_OUT = 512


def _rms(x, g):
    ms = jnp.mean(x * x, axis=-1, keepdims=True)
    return x * lax.rsqrt(ms + EPS) * g


def _sigmoid(z):
    return 1.0 / (1.0 + jnp.exp(-z))


def _silu(g):
    return g * _sigmoid(g)


def _dot(a, b):
    return jnp.dot(a, b, preferred_element_type=F32)


def _cmul(ar, ai, br, bi):
    return ar * br - ai * bi, ar * bi + ai * br


def _s5_discretise(ar, ai, ldt):
    dt = jnp.exp(ldt)
    mag = jnp.exp(ar * dt)
    p_re = mag * jnp.cos(ai * dt)
    p_im = mag * jnp.sin(ai * dt)
    den = ar * ar + ai * ai
    nr, ni = p_re - 1.0, p_im
    f_re = (nr * ar + ni * ai) / den
    f_im = (ni * ar - nr * ai) / den
    return p_re, p_im, f_re, f_im


def _powers(p_re, p_im, n):
    pw = [(jnp.ones_like(p_re), jnp.zeros_like(p_im))]
    for _ in range(n):
        pw.append(_cmul(pw[-1][0], pw[-1][1], p_re, p_im))
    return pw


def _block_diag(x):
    tiled = jnp.concatenate([x] * TILE_GROUPS, axis=1)
    r = lax.broadcasted_iota(jnp.int32, tiled.shape, 0)
    c = lax.broadcasted_iota(jnp.int32, tiled.shape, 1)
    return jnp.where(r // S5_GROUP == c // S5_STATE, tiled, 0.0)


def _s5_prep_kernel(a_ref, ag_ref, b_ref, c_ref, wb_ref, wck_ref, consts_ref):
    s = S5_TAPS
    half = TILE_STATES
    def per_row(v):
        return jnp.broadcast_to(v[:, None, :], (TILE_GROUPS, S5_GROUP, S5_STATE)).reshape(LANES, S5_STATE)

    g_re, g_im, gf_re, gf_im = _s5_discretise(ag_ref[0], ag_ref[1], ag_ref[2])
    gw = [(per_row(r), per_row(i)) for r, i in _powers(g_re, g_im, s)]
    b_rows = [jnp.swapaxes(b_ref[i], 1, 2).reshape(LANES, S5_STATE) for i in range(2)]
    bbr, bbi = _cmul(per_row(gf_re), per_row(gf_im), b_rows[0], b_rows[1])
    for k in range(s):
        wr, wi = _cmul(bbr, bbi, *gw[s - 1 - k])
        wb_ref[k * LANES:(k + 1) * LANES, 0:half] = _block_diag(wr).astype(BF16)
        wb_ref[k * LANES:(k + 1) * LANES, half:2 * half] = _block_diag(wi).astype(BF16)

    bb = jnp.concatenate([bbr, bbi], axis=1)
    r = lax.broadcasted_iota(jnp.int32, (LANES, LANES), 0)
    c = lax.broadcasted_iota(jnp.int32, (LANES, LANES), 1)
    same_group = r // S5_GROUP == c // S5_GROUP
    taps = []
    for d in range(s + 1):
        er, ei = _cmul(c_ref[0], c_ref[1], *gw[d])
        if d >= 1:
            cwt = jnp.concatenate([_block_diag(er), -_block_diag(ei)], axis=1)
            wck_ref[0:2 * half, (d - 1) * LANES:d * LANES] = cwt.T.astype(BF16)
        if d < s:
            t = lax.dot_general(bb, jnp.concatenate([er, -ei], axis=1), (((1,), (1,)), ((), ())),
                                preferred_element_type=F32, precision=lax.Precision.HIGHEST)
            taps.append(jnp.where(same_group, t, 0.0))
    zero = jnp.zeros((LANES, LANES), BF16)
    for m in range(s):
        for i in range(s):
            blk = taps[i - m].astype(BF16) if m <= i else zero
            wck_ref[2 * half + m * LANES:2 * half + (m + 1) * LANES, i * LANES:(i + 1) * LANES] = blk

    p_re, p_im, _, _ = _s5_discretise(a_ref[0:1, :], a_ref[1:2, :], a_ref[2:3, :])
    pw = _powers(p_re, p_im, s)
    shape = (SUBLANES, half)
    qr = _powers(pw[s][0], pw[s][1], S5_RUN)
    consts_ref[0, 0] = jnp.broadcast_to(qr[1][0], shape)
    consts_ref[0, 1] = jnp.broadcast_to(qr[1][1], shape)
    rp = _powers(qr[S5_RUN][0], qr[S5_RUN][1], SUBLANES)
    row = lax.broadcasted_iota(jnp.int32, shape, 0)
    for idx, d in enumerate((1, 2, 4)):
        consts_ref[1 + idx, 0] = jnp.where(row >= d, rp[d][0], 0.0)
        consts_ref[1 + idx, 1] = jnp.where(row >= d, rp[d][1], 0.0)
    cr = jnp.zeros(shape, F32)
    ci = jnp.zeros(shape, F32)
    for k in range(SUBLANES):
        cr = jnp.where(row == k, rp[k + 1][0], cr)
        ci = jnp.where(row == k, rp[k + 1][1], ci)
    consts_ref[4, 0] = cr
    consts_ref[4, 1] = ci


def _s5_prep(a_re, a_im, log_dt, b_re, b_im, c_re, c_im):
    s = S5_TAPS
    ldt = jnp.broadcast_to(log_dt[:, None], (S5_GROUPS, S5_STATE))
    ag = jnp.stack([a_re, a_im, ldt])
    a3 = ag.reshape(3, S5_NSTATE)
    bn = jnp.stack([b_re, b_im])
    ct = jnp.stack([c_re, c_im]).reshape(2, S5_WIDTH, S5_STATE)
    tile = lambda *shape: pl.BlockSpec((None,) + shape, lambda j: (j,) + (0,) * len(shape))
    return pl.pallas_call(
        _s5_prep_kernel,
        grid=(LANE_TILES,),
        in_specs=[pl.BlockSpec((3, TILE_STATES), lambda j: (0, j)),
                  pl.BlockSpec((3, TILE_GROUPS, S5_STATE), lambda j: (0, j, 0)),
                  pl.BlockSpec((2, TILE_GROUPS, S5_STATE, S5_GROUP), lambda j: (0, j, 0, 0)),
                  pl.BlockSpec((2, LANES, S5_STATE), lambda j: (0, j, 0))],
        out_specs=(tile(s * LANES, 2 * TILE_STATES),
                   tile(2 * TILE_STATES + s * LANES, s * LANES),
                   tile(5, 2, SUBLANES, TILE_STATES)),
        out_shape=(jax.ShapeDtypeStruct((LANE_TILES, s * LANES, 2 * TILE_STATES), BF16),
                   jax.ShapeDtypeStruct((LANE_TILES, 2 * TILE_STATES + s * LANES, s * LANES), BF16),
                   jax.ShapeDtypeStruct((LANE_TILES, 5, 2, SUBLANES, TILE_STATES), F32)),
        compiler_params=pltpu.CompilerParams(
            dimension_semantics=("arbitrary",), vmem_limit_bytes=VMEM_LIMIT),
        name="s5_prep",
    )(a3, ag, bn, ct)


def _mem_kv_kernel(mem_ref, g_ref, wk_ref, wv_ref, k_ref, v_ref):
    m = _rms(mem_ref[...], g_ref[...]).astype(BF16)
    k_ref[...] = _dot(m, wk_ref[...]).astype(BF16)
    v_ref[...] = _dot(m, wv_ref[...]).astype(BF16)


def _mem_kv(mem, g, wk, wv):
    B, M, D = mem.shape
    return pl.pallas_call(
        _mem_kv_kernel,
        grid=(B,),
        in_specs=[pl.BlockSpec((None, M, D), lambda b: (b, 0, 0)),
                  pl.BlockSpec((1, D), lambda b: (0, 0)),
                  pl.BlockSpec((D, D), lambda b: (0, 0)),
                  pl.BlockSpec((D, D), lambda b: (0, 0))],
        out_specs=(pl.BlockSpec((None, M, D), lambda b: (b, 0, 0)),
                   pl.BlockSpec((None, M, D), lambda b: (b, 0, 0))),
        out_shape=(jax.ShapeDtypeStruct((B, M, D), BF16),
                   jax.ShapeDtypeStruct((B, M, D), BF16)),
        compiler_params=pltpu.CompilerParams(
            dimension_semantics=("arbitrary",), vmem_limit_bytes=VMEM_LIMIT),
        name="mem_kv",
    )(mem, g, wk, wv)


def _token_prologue(x_ref, pos_ref, inv_ref, g_ref, h_ref, cos_ref, sin_ref):
    h_ref[...] = _rms(x_ref[...], g_ref[...]).astype(BF16)
    lane = lax.broadcasted_iota(jnp.int32, (1, LANES), 1)
    nfreq = RET_DK // 2
    group = lane // nfreq
    nblock = LANES // nfreq
    rows = x_ref.shape[0] // nblock
    pos = pos_ref[...].astype(F32)
    pos4 = jnp.zeros((rows, LANES), F32)
    for k in range(nblock):
        pos4 = jnp.where(group == k, pos[k * rows:(k + 1) * rows, :], pos4)
    ang4 = pos4 * inv_ref[...]

    def spread(t4):
        blocks = []
        for k in range(nblock):
            m = jnp.where(group == k, t4, 0.0)
            m = m + pltpu.roll(m, nfreq, 1)
            blocks.append(m + pltpu.roll(m, 2 * nfreq, 1))
        return jnp.concatenate(blocks, axis=0)

    first_half = (lane % RET_DK) < nfreq
    sin = spread(jnp.sin(ang4))
    cos_ref[...] = spread(jnp.cos(ang4))
    sin_ref[...] = jnp.where(first_half, -sin, sin)


def _proj_ret_kernel(x0_ref, pos0_ref, x_ref, pos_ref, inv_ref, g_ref, w_ref, gn_ref, u_ref, gs_ref, ret_ref,
                     cur_ref, prev_ref, hn_ref, cn_ref, sn_ref, h_ref, cos_ref, sin_ref,
                     state_ref, decay_ref, qw_ref, kw_ref, *, tiles_per_seq, num_tiles):
    t = pl.program_id(0)

    @pl.when(t == 0)
    def _init():
        _retention_constants(decay_ref, qw_ref, kw_ref)
        cur_ref[...] = jnp.zeros_like(cur_ref)
        _token_prologue(x0_ref, pos0_ref, inv_ref, g_ref, hn_ref, cn_ref, sn_ref)

    @pl.when((t == 0) | (t % tiles_per_seq == 1))
    def _reset():
        state_ref[...] = jnp.zeros_like(state_ref)

    ret_items = _retention_items(prev_ref, gn_ref, ret_ref, state_ref, decay_ref, qw_ref, kw_ref)

    @pl.when(t == num_tiles)
    def _drain():
        prev_ref[...] = cur_ref[...]
        for r in ret_items:
            r()

    @pl.when(t < num_tiles)
    def _step():
        prev_ref[...] = cur_ref[...]
        h_ref[...] = hn_ref[...]
        cos_ref[...] = cn_ref[...]
        sin_ref[...] = sn_ref[...]
        h = h_ref[...]
        cos = cos_ref[...]
        sin_signed = sin_ref[...]
        lane = lax.broadcasted_iota(jnp.int32, (1, LANES), 1)
        first_half = (lane % RET_DK) < (RET_DK // 2)

        def rope(p):
            partner = jnp.where(first_half,
                                pltpu.roll(p, LANES - RET_DK // 2, 1),
                                pltpu.roll(p, RET_DK // 2, 1))
            return p * cos + partner * sin_signed

        def proj_q():
            pq = _dot(h, w_ref[:, COL_Q:COL_K])
            for i in range(RET_QK_WIDTH // LANES):
                sl = slice(i * LANES, (i + 1) * LANES)
                cur_ref[:, COL_Q + i * LANES:COL_Q + (i + 1) * LANES] = rope(pq[:, sl]).astype(BF16)

        def proj_k():
            pk = _dot(h, w_ref[:, COL_K:COL_V])
            for i in range(RET_QK_WIDTH // LANES):
                sl = slice(i * LANES, (i + 1) * LANES)
                cur_ref[:, COL_K + i * LANES:COL_K + (i + 1) * LANES] = (
                    rope(pk[:, sl]) * (RET_DK ** -0.5)).astype(BF16)

        def proj_v():
            cur_ref[:, COL_V:COL_GR] = _dot(h, w_ref[:, COL_V:COL_GR]).astype(BF16)

        def proj_gate():
            cur_ref[:, COL_GR:COL_U] = _silu(_dot(h, w_ref[:, COL_GR:COL_U])).astype(BF16)

        def proj_u():
            pu = _dot(h, w_ref[:, COL_U:COL_GS]).astype(BF16)
            for j in range(LANE_TILES):
                u_ref[j] = pu[:, j * LANES:(j + 1) * LANES]

        def proj_gs():
            gs_ref[...] = _silu(_dot(h, w_ref[:, COL_GS:IN_COLS])).astype(BF16)

        proj_items = [proj_q, proj_k, proj_v, proj_gate, proj_u, proj_gs]
        per = -(-len(ret_items) // len(proj_items))
        for i, item in enumerate(proj_items):
            for r in ret_items[i * per:(i + 1) * per]:
                r()
            item()
        _token_prologue(x_ref, pos_ref, inv_ref, g_ref, hn_ref, cn_ref, sn_ref)


def _proj_ret(x, pos3, inv, g, w, gn):
    B, L, D = x.shape
    tm = TM_PROJ
    nt = L // tm
    n = B * nt
    cur = lambda t: jnp.minimum(t, n - 1)
    nxt = lambda t: jnp.minimum(t + 1, n - 1)
    lag = lambda t: jnp.maximum(t - 1, 0)
    first = lambda t: 0 * t
    row = lambda width, tile: pl.BlockSpec((None, tm, width), lambda t: (tile(t) // nt, tile(t) % nt, 0))
    const = lambda shape: pl.BlockSpec(shape, lambda t: (0,) * len(shape))
    return pl.pallas_call(
        functools.partial(_proj_ret_kernel, tiles_per_seq=nt, num_tiles=n),
        grid=(n + 1,),
        in_specs=[row(D, first), row(1, first), row(D, nxt), row(1, nxt), const((1, LANES)), const((1, D)),
                  const((D, IN_COLS)), const((1, RET_V_WIDTH))],
        out_specs=(pl.BlockSpec((None, LANE_TILES, tm, LANES), lambda t: (cur(t) // nt, 0, cur(t) % nt, 0)),
                   row(S5_WIDTH, cur), row(RET_V_WIDTH, lag)),
        out_shape=(jax.ShapeDtypeStruct((B, LANE_TILES, L, LANES), BF16),
                   jax.ShapeDtypeStruct((B, L, S5_WIDTH), BF16),
                   jax.ShapeDtypeStruct((B, L, RET_V_WIDTH), BF16)),
        scratch_shapes=[pltpu.VMEM((tm, COL_U), BF16),
                        pltpu.VMEM((tm, COL_U), BF16),
                        pltpu.VMEM((tm, D), BF16), pltpu.VMEM((tm, LANES), F32), pltpu.VMEM((tm, LANES), F32),
                        pltpu.VMEM((tm, D), BF16), pltpu.VMEM((tm, LANES), F32), pltpu.VMEM((tm, LANES), F32),
                        pltpu.VMEM((RET_HEADS // 2, LANES, 2 * RET_DV), F32),
                        pltpu.VMEM((RET_HEADS // 2, CHUNK, 2 * CHUNK), F32),
                        pltpu.VMEM((RET_HEADS // 2, CHUNK, 2 * RET_DV), F32),
                        pltpu.VMEM((RET_HEADS // 2, CHUNK, LANES), F32)],
        compiler_params=pltpu.CompilerParams(
            dimension_semantics=("arbitrary",), vmem_limit_bytes=VMEM_LIMIT),
        name="proj_ret",
    )(x, pos3, x, pos3, inv, g, w, gn)


def _retention_constants(decay_ref, qw_ref, kw_ref):
    i = lax.broadcasted_iota(jnp.int32, (CHUNK, 2 * CHUNK), 0).astype(F32)
    c = lax.broadcasted_iota(jnp.int32, (CHUNK, 2 * CHUNK), 1)
    j = (c % CHUNK).astype(F32)
    diff = i - j
    lane = lax.broadcasted_iota(jnp.int32, (CHUNK, LANES), 1)
    ik = lax.broadcasted_iota(jnp.int32, (CHUNK, LANES), 0).astype(F32)
    for p in range(RET_HEADS // 2):
        lg = jnp.where(c < CHUNK, LOG_G[2 * p], LOG_G[2 * p + 1])
        decay_ref[p] = jnp.where(diff >= 0.0, jnp.exp(lg * jnp.maximum(diff, 0.0)), 0.0)
        qw_ref[p] = jnp.exp(lg * (i + 1.0))
        lgk = jnp.where(lane < RET_DK, LOG_G[2 * p], LOG_G[2 * p + 1])
        kw_ref[p] = jnp.exp(lgk * (CHUNK - 1.0 - ik))


def _retention_items(src_ref, gn_ref, o_ref, state_ref, decay_ref, qw_ref, kw_ref):
    pairs = RET_HEADS // 2
    low = lax.broadcasted_iota(jnp.int32, (1, LANES), 1) < RET_DK
    srow = lax.broadcasted_iota(jnp.int32, (LANES, 2 * RET_DV), 0)
    scol = lax.broadcasted_iota(jnp.int32, (LANES, 2 * RET_DV), 1)
    own_block = (srow < RET_DK) == (scol < RET_DV)
    scol1 = lax.broadcasted_iota(jnp.int32, (1, 2 * RET_DV), 1)
    zero_v = jnp.zeros((CHUNK, RET_DV), BF16)

    def pair_step(c, p):
        rows = slice(c * CHUNK, (c + 1) * CHUNK)
        qt = src_ref[rows, COL_Q + p * LANES:COL_Q + (p + 1) * LANES]
        kt = src_ref[rows, COL_K + p * LANES:COL_K + (p + 1) * LANES]
        vp = src_ref[rows, COL_V + 2 * p * RET_DV:COL_V + 2 * (p + 1) * RET_DV]
        state = state_ref[p]
        cross = _dot(qt, state.astype(BF16))
        kw = (kt.astype(F32) * kw_ref[p]).astype(BF16)
        upd = lax.dot_general(kw, vp, (((0,), (0,)), ((), ())), preferred_element_type=F32)
        chunk_decay = jnp.where(scol1 < RET_DV,
                                math.exp(LOG_G[2 * p] * CHUNK), math.exp(LOG_G[2 * p + 1] * CHUNK))
        state_ref[p] = state * chunk_decay + jnp.where(own_block, upd, 0.0)

        zero_k = jnp.zeros_like(kt)
        k_rows = jnp.concatenate([jnp.where(low, kt, zero_k), jnp.where(low, zero_k, kt)], axis=0)
        s = lax.dot_general(qt, k_rows, (((1,), (1,)), ((), ())), preferred_element_type=F32)
        pm = (s * decay_ref[p]).astype(BF16)
        v_diag = jnp.concatenate(
            [jnp.concatenate([vp[:, 0:RET_DV], zero_v], axis=1),
             jnp.concatenate([zero_v, vp[:, RET_DV:2 * RET_DV]], axis=1)], axis=0)
        o2 = _dot(pm, v_diag) + qw_ref[p] * cross
        for e in range(2):
            h = 2 * p + e
            cols = slice(h * RET_DV, (h + 1) * RET_DV)
            o = o2[:, e * RET_DV:(e + 1) * RET_DV]
            mu = jnp.mean(o, axis=-1, keepdims=True)
            oc = o - mu
            var = jnp.mean(oc * oc, axis=-1, keepdims=True)
            on = oc * lax.rsqrt(var + EPS) * gn_ref[:, cols]
            gate = src_ref[rows, COL_GR + h * RET_DV:COL_GR + (h + 1) * RET_DV].astype(F32)
            o_ref[rows, cols] = (on * gate).astype(BF16)

    return [functools.partial(pair_step, c, p)
            for c in range(src_ref.shape[0] // CHUNK) for p in range(pairs)]


def _s5_scan_kernel(u_ref, wb_ref, consts_ref, wck_ref, d_ref, y_ref, *scratch, tiles_per_seq):
    uf_refs, lhs_refs, st_refs = scratch[0:3], scratch[3:6], scratch[6:9]
    sb_ref, ys_ref, carry_ref = scratch[9:]
    n = pl.program_id(0)
    s = S5_TAPS
    tm = u_ref.shape[0]
    nblk = tm // s
    half = TILE_STATES

    @pl.when(n == 0)
    def _init():
        for ref in scratch[0:9]:
            ref[...] = jnp.zeros_like(ref)

    @pl.when((n == 0) | ((n - 1) % tiles_per_seq == 0))
    def _reset():
        carry_ref[...] = jnp.zeros_like(carry_ref)

    ncol = 2 * LANES

    def increments(slot):
        uf_ref, lhs_ref, st_ref = uf_refs[slot], lhs_refs[slot], st_refs[slot]

        def piece(c):
            if c == 0:
                uf_ref[...] = u_ref[...].astype(F32)
                lhs_ref[...] = jnp.concatenate(
                    [uf_ref[pl.ds(k, nblk, stride=s), :] for k in range(s)], axis=1).astype(BF16)
            z = _dot(lhs_ref[...], wb_ref[:, c * ncol:(c + 1) * ncol])
            for i in range(ncol // LANES):
                st_ref[c * (ncol // LANES) + i] = z[:, i * LANES:(i + 1) * LANES]

        return [functools.partial(piece, c) for c in range(2 * half // ncol)]

    ntile = half // LANES

    def scan(slot, row_pieces):
        st_ref = st_refs[slot]
        run = S5_RUN
        span = run * SUBLANES
        row = lax.broadcasted_iota(jnp.int32, (SUBLANES, LANES), 0)
        last = SUBLANES - 1
        groups = nblk // span
        per = -(-groups // row_pieces)
        tiles_per_piece = S5_SCAN_LANES // LANES

        def tile_group(t, g, carry):
            lanes = slice(t * LANES, (t + 1) * LANES)
            const = lambda idx: (consts_ref[idx, 0, :, lanes], consts_ref[idx, 1, :, lanes])
            cr, ci = carry
            rows = [pl.ds(g * span + j, SUBLANES, stride=run) for j in range(run)]
            loc = [(st_ref[t, rows[0], :], st_ref[ntile + t, rows[0], :])]
            for j in range(1, run):
                pr, pi = _cmul(*const(0), *loc[-1])
                loc.append((pr + st_ref[t, rows[j], :], pi + st_ref[ntile + t, rows[j], :]))
            fr, fi = loc[-1]
            for idx, d in enumerate((1, 2, 4)):
                pr, pi = _cmul(*const(1 + idx), pltpu.roll(fr, d, 0), pltpu.roll(fi, d, 0))
                fr, fi = fr + pr, fi + pi
            pr, pi = _cmul(*const(4), cr, ci)
            fr, fi = fr + pr, fi + pi
            er = jnp.where(row == 0, cr, pltpu.roll(fr, 1, 0))
            ei = jnp.where(row == 0, ci, pltpu.roll(fi, 1, 0))
            st_ref[t, rows[0], :] = er
            st_ref[ntile + t, rows[0], :] = ei
            for j in range(1, run):
                er, ei = _cmul(*const(0), er, ei)
                st_ref[t, rows[j], :] = loc[j - 1][0] + er
                st_ref[ntile + t, rows[j], :] = loc[j - 1][1] + ei
            return (jnp.broadcast_to(fr[last:, :], (SUBLANES, LANES)),
                    jnp.broadcast_to(fi[last:, :], (SUBLANES, LANES)))

        def piece(q, i, carries):
            tiles = range(q * tiles_per_piece, (q + 1) * tiles_per_piece)
            if i == 0:
                for t in tiles:
                    carries[t] = (carry_ref[:, t * LANES:(t + 1) * LANES],
                                  carry_ref[:, half + t * LANES:half + (t + 1) * LANES])
            for g in range(i * per, min((i + 1) * per, groups)):
                for t in tiles:
                    carries[t] = tile_group(t, g, carries[t])
            if i == row_pieces - 1:
                for t in tiles:
                    carry_ref[:, t * LANES:(t + 1) * LANES] = carries[t][0]
                    carry_ref[:, half + t * LANES:half + (t + 1) * LANES] = carries[t][1]

        carries = {}
        return [functools.partial(piece, q, i, carries)
                for q in range(ntile // tiles_per_piece) for i in range(row_pieces)]

    def outputs(slot):
        taps_per_piece = ncol // LANES

        def cast():
            for t in range(2 * ntile):
                sb_ref[:, t * LANES:(t + 1) * LANES] = st_refs[slot][t].astype(BF16)

        def piece(c):
            lhs = jnp.concatenate([sb_ref[...], lhs_refs[slot][...]], axis=1)
            yall = _dot(lhs, wck_ref[:, c * ncol:(c + 1) * ncol])
            for i in range(taps_per_piece):
                tap = c * taps_per_piece + i
                ys_ref[pl.ds(tap, nblk, stride=s), :] = yall[:, i * LANES:(i + 1) * LANES]

        def finish():
            y = ys_ref[...] + d_ref[...] * uf_refs[slot][...]
            y_ref[...] = jax.nn.gelu(y).astype(BF16)

        return [cast] + [functools.partial(piece, c) for c in range(s * LANES // ncol)] + [finish]

    for phase in range(3):
        @pl.when(n % 3 == phase)
        def _steps(phase=phase):
            mxu = increments(phase) + outputs((phase + 1) % 3)
            vpu = scan((phase + 2) % 3, 3)
            for i, piece in enumerate(mxu):
                piece()
                if i < len(vpu):
                    vpu[i]()
            for piece in vpu[len(mxu):]:
                piece()


def _s5_scan(u, wb, wck, consts, d):
    B, _, L, _ = u.shape
    s = S5_TAPS
    tm = TM_S5
    nt = L // tm
    ntiles = LANE_TILES * B * nt
    tile_of = lambda n, lag: jnp.clip(n - lag, 0, ntiles - 1)
    lane_tile = lambda n, lag: tile_of(n, lag) // (B * nt)

    def tok(lag):
        def index(n):
            i = tile_of(n, lag)
            return ((i // nt) % B, i // (B * nt), i % nt, 0)
        return pl.BlockSpec((None, None, tm, LANES), index)

    def par(lag, *shape):
        return pl.BlockSpec((None,) + shape, lambda n: (lane_tile(n, lag),) + (0,) * len(shape))

    nblk = tm // s
    return pl.pallas_call(
        functools.partial(_s5_scan_kernel, tiles_per_seq=nt),
        grid=(ntiles + 2,),
        in_specs=[tok(0), par(0, s * LANES, 2 * TILE_STATES), par(1, 5, 2, SUBLANES, TILE_STATES),
                  par(2, 2 * TILE_STATES + s * LANES, s * LANES), par(2, 1, LANES)],
        out_specs=tok(2),
        out_shape=jax.ShapeDtypeStruct((B, LANE_TILES, L, LANES), BF16),
        scratch_shapes=([pltpu.VMEM((tm, LANES), F32)] * 3
                        + [pltpu.VMEM((nblk, s * LANES), BF16)] * 3
                        + [pltpu.VMEM((2 * TILE_STATES // LANES, nblk, LANES), F32)] * 3
                        + [pltpu.VMEM((nblk, 2 * TILE_STATES), BF16), pltpu.VMEM((tm, LANES), F32),
                           pltpu.VMEM((SUBLANES, 2 * TILE_STATES), F32)]),
        compiler_params=pltpu.CompilerParams(
            dimension_semantics=("arbitrary",), vmem_limit_bytes=VMEM_LIMIT),
        name="s5_scan",
    )(u, wb, consts, wck, d)


def _out_xattn_kernel(x_ref, ret_ref, y_ref, gs_ref, gw_ref, gb_ref, wout_ref, g2_ref, wq_ref,
                      ka_ref, va_ref, wo_ref, gf_ref, o_ref):
    y = jnp.concatenate([y_ref[j] for j in range(LANE_TILES)], axis=1)
    z = _dot(y, gw_ref[...]) + gb_ref[...]
    ssm = (y.astype(F32) * _sigmoid(z) * gs_ref[...].astype(F32)).astype(BF16)
    x1 = (x_ref[...] + _dot(ret_ref[...], wout_ref[0:RET_V_WIDTH, :])
          + _dot(ssm, wout_ref[RET_V_WIDTH:RET_V_WIDTH + S5_WIDTH, :]))
    h2 = _rms(x1, g2_ref[...]).astype(BF16)
    qa = (_dot(h2, wq_ref[...]) * (XA_DH ** -0.5)).astype(BF16)
    heads = []
    for h in range(XA_HEADS):
        cols = slice(h * XA_DH, (h + 1) * XA_DH)
        s = lax.dot_general(qa[:, cols], ka_ref[:, cols], (((1,), (1,)), ((), ())),
                            preferred_element_type=F32)
        e = jnp.exp(s - jnp.max(s, axis=-1, keepdims=True))
        l = jnp.sum(e, axis=-1, keepdims=True)
        heads.append((_dot(e.astype(BF16), va_ref[:, cols]) * (1.0 / l)).astype(BF16))
    o = jnp.concatenate(heads, axis=-1)
    x2 = x1 + _dot(o, wo_ref[...])
    o_ref[...] = _rms(x2, gf_ref[...])


def _out_xattn(x, ret, y, gs, gw, gb, wout, g2, wq, ka, va, wo, gf):
    B, L, D = x.shape
    tm = TM_OUT
    row = pl.BlockSpec((None, tm, D), lambda b, t: (b, t, 0))
    const = lambda shape: pl.BlockSpec(shape, lambda b, t: (0,) * len(shape))
    mem = pl.BlockSpec((None, MEM_LEN, D), lambda b, t: (b, 0, 0))
    ytile = pl.BlockSpec((None, LANE_TILES, tm, LANES), lambda b, t: (b, 0, t, 0))
    return pl.pallas_call(
        _out_xattn_kernel,
        grid=(B, L // tm),
        in_specs=[row, row, ytile, row, const((D, D)), const((1, D)), const(wout.shape), const((1, D)),
                  const((D, D)), mem, mem, const((D, D)), const((1, D))],
        out_specs=row,
        out_shape=jax.ShapeDtypeStruct((B, L, D), F32),
        compiler_params=pltpu.CompilerParams(
            dimension_semantics=("arbitrary", "arbitrary"), vmem_limit_bytes=VMEM_LIMIT),
        name="out_xattn",
    )(x, ret, y, gs, gw, gb, wout, g2, wq, ka, va, wo, gf)


def _rope_inv_lanes():
    half = RET_DK // 2
    inv = ROPE_BASE ** (-np.arange(half, dtype=np.float64) / half)
    return jnp.asarray(np.tile(inv, LANES // half).reshape(1, LANES), F32)


def kernel(x, mem, positions, norm1_g, w_in, ret_gn_g, s5_a_re, s5_a_im, s5_log_dt, s5_b_re, s5_b_im, s5_c_re, s5_c_im, s5_d, s5_glu_w, s5_glu_b, w_out, norm2_g, norm_mem_g, xa_wq, xa_wk, xa_wv, xa_wo, norm_f_g):
    B, L, D = x.shape
    l = 0
    bf = lambda w: w.astype(BF16)
    rowvec = lambda v: v.reshape(1, -1)

    wb, wck, consts = _s5_prep(s5_a_re[l], s5_a_im[l], s5_log_dt[l], s5_b_re[l], s5_b_im[l],
                               s5_c_re[l], s5_c_im[l])
    ka, va = _mem_kv(mem, rowvec(norm_mem_g[l]), bf(xa_wk[l]), bf(xa_wv[l]))
    u, g_s5, ret = _proj_ret(x, positions.reshape(B, L, 1), _rope_inv_lanes(),
                             rowvec(norm1_g[l]), bf(w_in[l]), rowvec(ret_gn_g[l]))
    y = _s5_scan(u, wb, wck, consts, s5_d[l].reshape(LANE_TILES, 1, LANES))
    return _out_xattn(x, ret, y, g_s5, bf(s5_glu_w[l]), rowvec(s5_glu_b[l]), bf(w_out[l]),
                      rowvec(norm2_g[l]), bf(xa_wq[l]), ka, va, bf(xa_wo[l]), rowvec(norm_f_g))
```

```python
import functools
import math

import numpy as np
import jax
import jax.numpy as jnp
from jax import lax
from jax.experimental import pallas as pl
from jax.experimental.pallas import tpu as pltpu

F32 = jnp.float32
BF16 = jnp.bfloat16

D_MODEL = 1024
MEM_LEN = 256
EPS = 1e-6
ROPE_BASE = 10000.0

RET_HEADS = 8
RET_QK_WIDTH = 512
RET_V_WIDTH = 1024
RET_DK = 64
RET_DV = 128
CHUNK = 128

S5_WIDTH = 1024
S5_GROUP = 16
S5_GROUPS = 64
S5_STATE = 64
S5_NSTATE = S5_GROUPS * S5_STATE

IN_COLS = 5120
COL_Q, COL_K, COL_V, COL_GR, COL_U, COL_GS = 0, 512, 1024, 2048, 3072, 4096

XA_HEADS = 4
XA_DH = 256

LANES = 128
SUBLANES = 8
VMEM_LIMIT = 48 * 1024 * 1024

LANE_TILES = S5_WIDTH // LANES
TILE_GROUPS = LANES // S5_GROUP
TILE_STATES = S5_NSTATE // LANE_TILES
S5_TAPS = 4
S5_SCAN_LANES = 2 * LANES
S5_RUN = 4

LOG_G = tuple(math.log1p(-(2.0 ** (-5.0 - h))) for h in range(RET_HEADS))

TM_PROJ = 512
TM_S5 = 4096
TM_OUT = 512


def _rms(x, g):
    ms = jnp.mean(x * x, axis=-1, keepdims=True)
    return x * lax.rsqrt(ms + EPS) * g


def _sigmoid(z):
    return 1.0 / (1.0 + jnp.exp(-z))


def _silu(g):
    return g * _sigmoid(g)


def _dot(a, b):
    return jnp.dot(a, b, preferred_element_type=F32)


def _cmul(ar, ai, br, bi):
    return ar * br - ai * bi, ar * bi + ai * br


def _s5_discretise(ar, ai, ldt):
    dt = jnp.exp(ldt)
    mag = jnp.exp(ar * dt)
    p_re = mag * jnp.cos(ai * dt)
    p_im = mag * jnp.sin(ai * dt)
    den = ar * ar + ai * ai
    nr, ni = p_re - 1.0, p_im
    f_re = (nr * ar + ni * ai) / den
    f_im = (ni * ar - nr * ai) / den
    return p_re, p_im, f_re, f_im


def _powers(p_re, p_im, n):
    pw = [(jnp.ones_like(p_re), jnp.zeros_like(p_im))]
    for _ in range(n):
        pw.append(_cmul(pw[-1][0], pw[-1][1], p_re, p_im))
    return pw


def _block_diag(x):
    tiled = jnp.concatenate([x] * TILE_GROUPS, axis=1)
    r = lax.broadcasted_iota(jnp.int32, tiled.shape, 0)
    c = lax.broadcasted_iota(jnp.int32, tiled.shape, 1)
    return jnp.where(r // S5_GROUP == c // S5_STATE, tiled, 0.0)


def _s5_prep_kernel(a_ref, ag_ref, b_ref, c_ref, wb_ref, wck_ref, consts_ref):
    s = S5_TAPS
    half = TILE_STATES
    def per_row(v):
        return jnp.broadcast_to(v[:, None, :], (TILE_GROUPS, S5_GROUP, S5_STATE)).reshape(LANES, S5_STATE)

    g_re, g_im, gf_re, gf_im = _s5_discretise(ag_ref[0], ag_ref[1], ag_ref[2])
    gw = [(per_row(r), per_row(i)) for r, i in _powers(g_re, g_im, s)]
    bbr, bbi = _cmul(per_row(gf_re), per_row(gf_im), b_ref[0], b_ref[1])
    for k in range(s):
        wr, wi = _cmul(bbr, bbi, *gw[s - 1 - k])
        wb_ref[k * LANES:(k + 1) * LANES, 0:half] = _block_diag(wr).astype(BF16)
        wb_ref[k * LANES:(k + 1) * LANES, half:2 * half] = _block_diag(wi).astype(BF16)

    bb = jnp.concatenate([bbr, bbi], axis=1)
    r = lax.broadcasted_iota(jnp.int32, (LANES, LANES), 0)
    c = lax.broadcasted_iota(jnp.int32, (LANES, LANES), 1)
    same_group = r // S5_GROUP == c // S5_GROUP
    taps = []
    for d in range(s + 1):
        er, ei = _cmul(c_ref[0], c_ref[1], *gw[d])
        if d >= 1:
            cwt = jnp.concatenate([_block_diag(er), -_block_diag(ei)], axis=1)
            wck_ref[0:2 * half, (d - 1) * LANES:d * LANES] = cwt.T.astype(BF16)
        if d < s:
            t = lax.dot_general(bb, jnp.concatenate([er, -ei], axis=1), (((1,), (1,)), ((), ())),
                                preferred_element_type=F32, precision=lax.Precision.HIGHEST)
            taps.append(jnp.where(same_group, t, 0.0))
    zero = jnp.zeros((LANES, LANES), BF16)
    for m in range(s):
        for i in range(s):
            blk = taps[i - m].astype(BF16) if m <= i else zero
            wck_ref[2 * half + m * LANES:2 * half + (m + 1) * LANES, i * LANES:(i + 1) * LANES] = blk

    p_re, p_im, _, _ = _s5_discretise(a_ref[0:1, :], a_ref[1:2, :], a_ref[2:3, :])
    pw = _powers(p_re, p_im, s)
    shape = (SUBLANES, half)
    qr = _powers(pw[s][0], pw[s][1], S5_RUN)
    consts_ref[0, 0] = jnp.broadcast_to(qr[1][0], shape)
    consts_ref[0, 1] = jnp.broadcast_to(qr[1][1], shape)
    rp = _powers(qr[S5_RUN][0], qr[S5_RUN][1], SUBLANES)
    row = lax.broadcasted_iota(jnp.int32, shape, 0)
    for idx, d in enumerate((1, 2, 4)):
        consts_ref[1 + idx, 0] = jnp.where(row >= d, rp[d][0], 0.0)
        consts_ref[1 + idx, 1] = jnp.where(row >= d, rp[d][1], 0.0)
    cr = jnp.zeros(shape, F32)
    ci = jnp.zeros(shape, F32)
    for k in range(SUBLANES):
        cr = jnp.where(row == k, rp[k + 1][0], cr)
        ci = jnp.where(row == k, rp[k + 1][1], ci)
    consts_ref[4, 0] = cr
    consts_ref[4, 1] = ci


def _s5_prep(a_re, a_im, log_dt, b_re, b_im, c_re, c_im):
    s = S5_TAPS
    ldt = jnp.broadcast_to(log_dt[:, None], (S5_GROUPS, S5_STATE))
    ag = jnp.stack([a_re, a_im, ldt])
    a3 = ag.reshape(3, S5_NSTATE)
    bt = jnp.stack([b_re, b_im]).transpose(0, 1, 3, 2).reshape(2, S5_WIDTH, S5_STATE)
    ct = jnp.stack([c_re, c_im]).reshape(2, S5_WIDTH, S5_STATE)
    tile = lambda *shape: pl.BlockSpec((None,) + shape, lambda j: (j,) + (0,) * len(shape))
    return pl.pallas_call(
        _s5_prep_kernel,
        grid=(LANE_TILES,),
        in_specs=[pl.BlockSpec((3, TILE_STATES), lambda j: (0, j)),
                  pl.BlockSpec((3, TILE_GROUPS, S5_STATE), lambda j: (0, j, 0)),
                  pl.BlockSpec((2, LANES, S5_STATE), lambda j: (0, j, 0)),
                  pl.BlockSpec((2, LANES, S5_STATE), lambda j: (0, j, 0))],
        out_specs=(tile(s * LANES, 2 * TILE_STATES),
                   tile(2 * TILE_STATES + s * LANES, s * LANES),
                   tile(5, 2, SUBLANES, TILE_STATES)),
        out_shape=(jax.ShapeDtypeStruct((LANE_TILES, s * LANES, 2 * TILE_STATES), BF16),
                   jax.ShapeDtypeStruct((LANE_TILES, 2 * TILE_STATES + s * LANES, s * LANES), BF16),
                   jax.ShapeDtypeStruct((LANE_TILES, 5, 2, SUBLANES, TILE_STATES), F32)),
        compiler_params=pltpu.CompilerParams(
            dimension_semantics=("arbitrary",), vmem_limit_bytes=VMEM_LIMIT),
        name="s5_prep",
    )(a3, ag, bt, ct)


def _mem_kv_kernel(mem_ref, g_ref, wk_ref, wv_ref, k_ref, v_ref):
    m = _rms(mem_ref[...], g_ref[...]).astype(BF16)
    k_ref[...] = _dot(m, wk_ref[...]).astype(BF16)
    v_ref[...] = _dot(m, wv_ref[...]).astype(BF16)


def _mem_kv(mem, g, wk, wv):
    B, M, D = mem.shape
    return pl.pallas_call(
        _mem_kv_kernel,
        grid=(B,),
        in_specs=[pl.BlockSpec((None, M, D), lambda b: (b, 0, 0)),
                  pl.BlockSpec((1, D), lambda b: (0, 0)),
                  pl.BlockSpec((D, D), lambda b: (0, 0)),
                  pl.BlockSpec((D, D), lambda b: (0, 0))],
        out_specs=(pl.BlockSpec((None, M, D), lambda b: (b, 0, 0)),
                   pl.BlockSpec((None, M, D), lambda b: (b, 0, 0))),
        out_shape=(jax.ShapeDtypeStruct((B, M, D), BF16),
                   jax.ShapeDtypeStruct((B, M, D), BF16)),
        compiler_params=pltpu.CompilerParams(
            dimension_semantics=("arbitrary",), vmem_limit_bytes=VMEM_LIMIT),
        name="mem_kv",
    )(mem, g, wk, wv)


def _token_prologue(x_ref, pos_ref, inv_ref, g_ref, h_ref, cos_ref, sin_ref):
    h_ref[...] = _rms(x_ref[...], g_ref[...]).astype(BF16)
    lane = lax.broadcasted_iota(jnp.int32, (1, LANES), 1)
    nfreq = RET_DK // 2
    group = lane // nfreq
    nblock = LANES // nfreq
    rows = x_ref.shape[0] // nblock
    pos = pos_ref[...].astype(F32)
    pos4 = jnp.zeros((rows, LANES), F32)
    for k in range(nblock):
        pos4 = jnp.where(group == k, pos[k * rows:(k + 1) * rows, :], pos4)
    ang4 = pos4 * inv_ref[...]

    def spread(t4):
        blocks = []
        for k in range(nblock):
            m = jnp.where(group == k, t4, 0.0)
            m = m + pltpu.roll(m, nfreq, 1)
            blocks.append(m + pltpu.roll(m, 2 * nfreq, 1))
        return jnp.concatenate(blocks, axis=0)

    first_half = (lane % RET_DK) < nfreq
    sin = spread(jnp.sin(ang4))
    cos_ref[...] = spread(jnp.cos(ang4))
    sin_ref[...] = jnp.where(first_half, -sin, sin)


def _proj_ret_kernel(x0_ref, pos0_ref, x_ref, pos_ref, inv_ref, g_ref, w_ref, gn_ref, *rest,
                     tiles_per_seq, num_tiles, num_casts):
    f32_refs, rest = rest[:num_casts], rest[num_casts:]
    (u_ref, gs_ref, ret_ref), rest = rest[:3], rest[3:]
    bf16_refs, rest = rest[:num_casts], rest[num_casts:]
    (cur_ref, prev_ref, hn_ref, cn_ref, sn_ref, h_ref, cos_ref, sin_ref,
     state_ref, decay_ref, qw_ref, kw_ref) = rest
    t = pl.program_id(0)

    @pl.when(t == 0)
    def _init():
        _retention_constants(decay_ref, qw_ref, kw_ref)
        cur_ref[...] = jnp.zeros_like(cur_ref)
        _token_prologue(x0_ref, pos0_ref, inv_ref, g_ref, hn_ref, cn_ref, sn_ref)

    @pl.when((t == 0) | (t % tiles_per_seq == 1))
    def _reset():
        state_ref[...] = jnp.zeros_like(state_ref)

    ret_items = _retention_items(prev_ref, gn_ref, ret_ref, state_ref, decay_ref, qw_ref, kw_ref)

    @pl.when(t == num_tiles)
    def _drain():
        prev_ref[...] = cur_ref[...]
        for r in ret_items:
            r()

    @pl.when(t < num_tiles)
    def _step():
        prev_ref[...] = cur_ref[...]
        h_ref[...] = hn_ref[...]
        cos_ref[...] = cn_ref[...]
        sin_ref[...] = sn_ref[...]
        h = h_ref[...]
        cos = cos_ref[...]
        sin_signed = sin_ref[...]
        lane = lax.broadcasted_iota(jnp.int32, (1, LANES), 1)
        first_half = (lane % RET_DK) < (RET_DK // 2)

        def rope(p):
            partner = jnp.where(first_half,
                                pltpu.roll(p, LANES - RET_DK // 2, 1),
                                pltpu.roll(p, RET_DK // 2, 1))
            return p * cos + partner * sin_signed

        def proj_q():
            pq = _dot(h, w_ref[:, COL_Q:COL_K])
            for i in range(RET_QK_WIDTH // LANES):
                sl = slice(i * LANES, (i + 1) * LANES)
                cur_ref[:, COL_Q + i * LANES:COL_Q + (i + 1) * LANES] = rope(pq[:, sl]).astype(BF16)

        def proj_k():
            pk = _dot(h, w_ref[:, COL_K:COL_V])
            for i in range(RET_QK_WIDTH // LANES):
                sl = slice(i * LANES, (i + 1) * LANES)
                cur_ref[:, COL_K + i * LANES:COL_K + (i + 1) * LANES] = (
                    rope(pk[:, sl]) * (RET_DK ** -0.5)).astype(BF16)

        def proj_v():
            cur_ref[:, COL_V:COL_GR] = _dot(h, w_ref[:, COL_V:COL_GR]).astype(BF16)

        def proj_gate():
            cur_ref[:, COL_GR:COL_U] = _silu(_dot(h, w_ref[:, COL_GR:COL_U])).astype(BF16)

        def proj_u():
            pu = _dot(h, w_ref[:, COL_U:COL_GS]).astype(BF16)
            for j in range(LANE_TILES):
                u_ref[j] = pu[:, j * LANES:(j + 1) * LANES]

        def proj_gs():
            gs_ref[...] = _silu(_dot(h, w_ref[:, COL_GS:IN_COLS])).astype(BF16)

        proj_items = [proj_q, proj_k, proj_v, proj_gate, proj_u, proj_gs]
        per = -(-len(ret_items) // len(proj_items))
        for i, item in enumerate(proj_items):
            for r in ret_items[i * per:(i + 1) * per]:
                r()
            item()
        _token_prologue(x_ref, pos_ref, inv_ref, g_ref, hn_ref, cn_ref, sn_ref)
        for src_ref, dst_ref in zip(f32_refs, bf16_refs):
            dst_ref[...] = src_ref[...].astype(BF16)


def _proj_ret(x, pos3, inv, g, w, gn, later):
    B, L, D = x.shape
    tm = TM_PROJ
    nt = L // tm
    n = B * nt
    cur = lambda t: jnp.minimum(t, n - 1)
    nxt = lambda t: jnp.minimum(t + 1, n - 1)
    lag = lambda t: jnp.maximum(t - 1, 0)
    first = lambda t: 0 * t
    row = lambda width, tile: pl.BlockSpec((None, tm, width), lambda t: (tile(t) // nt, tile(t) % nt, 0))
    const = lambda shape: pl.BlockSpec(shape, lambda t: (0,) * len(shape))
    slab = lambda m: pl.BlockSpec((m.shape[0] // n, m.shape[1]), lambda t: (cur(t), 0))
    outs = pl.pallas_call(
        functools.partial(_proj_ret_kernel, tiles_per_seq=nt, num_tiles=n, num_casts=len(later)),
        grid=(n + 1,),
        in_specs=[row(D, first), row(1, first), row(D, nxt), row(1, nxt), const((1, LANES)), const((1, D)),
                  const((D, IN_COLS)), const((1, RET_V_WIDTH))] + [slab(m) for m in later],
        out_specs=[pl.BlockSpec((None, LANE_TILES, tm, LANES), lambda t: (cur(t) // nt, 0, cur(t) % nt, 0)),
                   row(S5_WIDTH, cur), row(RET_V_WIDTH, lag)] + [slab(m) for m in later],
        out_shape=[jax.ShapeDtypeStruct((B, LANE_TILES, L, LANES), BF16),
                   jax.ShapeDtypeStruct((B, L, S5_WIDTH), BF16),
                   jax.ShapeDtypeStruct((B, L, RET_V_WIDTH), BF16)]
        + [jax.ShapeDtypeStruct(m.shape, BF16) for m in later],
        scratch_shapes=[pltpu.VMEM((tm, COL_U), BF16),
                        pltpu.VMEM((tm, COL_U), BF16),
                        pltpu.VMEM((tm, D), BF16), pltpu.VMEM((tm, LANES), F32), pltpu.VMEM((tm, LANES), F32),
                        pltpu.VMEM((tm, D), BF16), pltpu.VMEM((tm, LANES), F32), pltpu.VMEM((tm, LANES), F32),
                        pltpu.VMEM((RET_HEADS // 2, LANES, 2 * RET_DV), F32),
                        pltpu.VMEM((RET_HEADS // 2, CHUNK, 2 * CHUNK), F32),
                        pltpu.VMEM((RET_HEADS // 2, CHUNK, 2 * RET_DV), F32),
                        pltpu.VMEM((RET_HEADS // 2, CHUNK, LANES), F32)],
        compiler_params=pltpu.CompilerParams(
            dimension_semantics=("arbitrary",), vmem_limit_bytes=VMEM_LIMIT),
        name="proj_ret",
    )(x, pos3, x, pos3, inv, g, w, gn, *later)
    return outs[:3], outs[3:]


def _retention_constants(decay_ref, qw_ref, kw_ref):
    i = lax.broadcasted_iota(jnp.int32, (CHUNK, 2 * CHUNK), 0).astype(F32)
    c = lax.broadcasted_iota(jnp.int32, (CHUNK, 2 * CHUNK), 1)
    j = (c % CHUNK).astype(F32)
    diff = i - j
    lane = lax.broadcasted_iota(jnp.int32, (CHUNK, LANES), 1)
    ik = lax.broadcasted_iota(jnp.int32, (CHUNK, LANES), 0).astype(F32)
    for p in range(RET_HEADS // 2):
        lg = jnp.where(c < CHUNK, LOG_G[2 * p], LOG_G[2 * p + 1])
        decay_ref[p] = jnp.where(diff >= 0.0, jnp.exp(lg * jnp.maximum(diff, 0.0)), 0.0)
        qw_ref[p] = jnp.exp(lg * (i + 1.0))
        lgk = jnp.where(lane < RET_DK, LOG_G[2 * p], LOG_G[2 * p + 1])
        kw_ref[p] = jnp.exp(lgk * (CHUNK - 1.0 - ik))


def _retention_items(src_ref, gn_ref, o_ref, state_ref, decay_ref, qw_ref, kw_ref):
    pairs = RET_HEADS // 2
    low = lax.broadcasted_iota(jnp.int32, (1, LANES), 1) < RET_DK
    srow = lax.broadcasted_iota(jnp.int32, (LANES, 2 * RET_DV), 0)
    scol = lax.broadcasted_iota(jnp.int32, (LANES, 2 * RET_DV), 1)
    own_block = (srow < RET_DK) == (scol < RET_DV)
    scol1 = lax.broadcasted_iota(jnp.int32, (1, 2 * RET_DV), 1)
    zero_v = jnp.zeros((CHUNK, RET_DV), BF16)

    def pair_step(c, p):
        rows = slice(c * CHUNK, (c + 1) * CHUNK)
        qt = src_ref[rows, COL_Q + p * LANES:COL_Q + (p + 1) * LANES]
        kt = src_ref[rows, COL_K + p * LANES:COL_K + (p + 1) * LANES]
        vp = src_ref[rows, COL_V + 2 * p * RET_DV:COL_V + 2 * (p + 1) * RET_DV]
        state = state_ref[p]
        cross = _dot(qt, state.astype(BF16))
        kw = (kt.astype(F32) * kw_ref[p]).astype(BF16)
        upd = lax.dot_general(kw, vp, (((0,), (0,)), ((), ())), preferred_element_type=F32)
        chunk_decay = jnp.where(scol1 < RET_DV,
                                math.exp(LOG_G[2 * p] * CHUNK), math.exp(LOG_G[2 * p + 1] * CHUNK))
        state_ref[p] = state * chunk_decay + jnp.where(own_block, upd, 0.0)

        zero_k = jnp.zeros_like(kt)
        k_rows = jnp.concatenate([jnp.where(low, kt, zero_k), jnp.where(low, zero_k, kt)], axis=0)
        s = lax.dot_general(qt, k_rows, (((1,), (1,)), ((), ())), preferred_element_type=F32)
        pm = (s * decay_ref[p]).astype(BF16)
        v_diag = jnp.concatenate(
            [jnp.concatenate([vp[:, 0:RET_DV], zero_v], axis=1),
             jnp.concatenate([zero_v, vp[:, RET_DV:2 * RET_DV]], axis=1)], axis=0)
        o2 = _dot(pm, v_diag) + qw_ref[p] * cross
        for e in range(2):
            h = 2 * p + e
            cols = slice(h * RET_DV, (h + 1) * RET_DV)
            o = o2[:, e * RET_DV:(e + 1) * RET_DV]
            mu = jnp.mean(o, axis=-1, keepdims=True)
            oc = o - mu
            var = jnp.mean(oc * oc, axis=-1, keepdims=True)
            on = oc * lax.rsqrt(var + EPS) * gn_ref[:, cols]
            gate = src_ref[rows, COL_GR + h * RET_DV:COL_GR + (h + 1) * RET_DV].astype(F32)
            o_ref[rows, cols] = (on * gate).astype(BF16)

    return [functools.partial(pair_step, c, p)
            for c in range(src_ref.shape[0] // CHUNK) for p in range(pairs)]


def _s5_scan_kernel(u_ref, wb_ref, consts_ref, wck_ref, d_ref, y_ref, *scratch, tiles_per_seq):
    uf_refs, lhs_refs, st_refs = scratch[0:3], scratch[3:6], scratch[6:9]
    sb_ref, ys_ref, carry_ref = scratch[9:]
    n = pl.program_id(0)
    s = S5_TAPS
    tm = u_ref.shape[0]
    nblk = tm // s
    half = TILE_STATES

    @pl.when(n == 0)
    def _init():
        for ref in scratch[0:9]:
            ref[...] = jnp.zeros_like(ref)

    @pl.when((n == 0) | ((n - 1) % tiles_per_seq == 0))
    def _reset():
        carry_ref[...] = jnp.zeros_like(carry_ref)

    ncol = 2 * LANES

    def increments(slot):
        uf_ref, lhs_ref, st_ref = uf_refs[slot], lhs_refs[slot], st_refs[slot]

        def piece(c):
            if c == 0:
                uf_ref[...] = u_ref[...].astype(F32)
                lhs_ref[...] = jnp.concatenate(
                    [uf_ref[pl.ds(k, nblk, stride=s), :] for k in range(s)], axis=1).astype(BF16)
            z = _dot(lhs_ref[...], wb_ref[:, c * ncol:(c + 1) * ncol])
            for i in range(ncol // LANES):
                st_ref[c * (ncol // LANES) + i] = z[:, i * LANES:(i + 1) * LANES]

        return [functools.partial(piece, c) for c in range(2 * half // ncol)]

    ntile = half // LANES

    def scan(slot, row_pieces):
        st_ref = st_refs[slot]
        run = S5_RUN
        span = run * SUBLANES
        row = lax.broadcasted_iota(jnp.int32, (SUBLANES, LANES), 0)
        last = SUBLANES - 1
        groups = nblk // span
        per = -(-groups // row_pieces)
        tiles_per_piece = S5_SCAN_LANES // LANES

        def tile_group(t, g, carry):
            lanes = slice(t * LANES, (t + 1) * LANES)
            const = lambda idx: (consts_ref[idx, 0, :, lanes], consts_ref[idx, 1, :, lanes])
            cr, ci = carry
            rows = [pl.ds(g * span + j, SUBLANES, stride=run) for j in range(run)]
            loc = [(st_ref[t, rows[0], :], st_ref[ntile + t, rows[0], :])]
            for j in range(1, run):
                pr, pi = _cmul(*const(0), *loc[-1])
                loc.append((pr + st_ref[t, rows[j], :], pi + st_ref[ntile + t, rows[j], :]))
            fr, fi = loc[-1]
            for idx, d in enumerate((1, 2, 4)):
                pr, pi = _cmul(*const(1 + idx), pltpu.roll(fr, d, 0), pltpu.roll(fi, d, 0))
                fr, fi = fr + pr, fi + pi
            pr, pi = _cmul(*const(4), cr, ci)
            fr, fi = fr + pr, fi + pi
            er = jnp.where(row == 0, cr, pltpu.roll(fr, 1, 0))
            ei = jnp.where(row == 0, ci, pltpu.roll(fi, 1, 0))
            st_ref[t, rows[0], :] = er
            st_ref[ntile + t, rows[0], :] = ei
            for j in range(1, run):
                er, ei = _cmul(*const(0), er, ei)
                st_ref[t, rows[j], :] = loc[j - 1][0] + er
                st_ref[ntile + t, rows[j], :] = loc[j - 1][1] + ei
            return (jnp.broadcast_to(fr[last:, :], (SUBLANES, LANES)),
                    jnp.broadcast_to(fi[last:, :], (SUBLANES, LANES)))

        def piece(q, i, carries):
            tiles = range(q * tiles_per_piece, (q + 1) * tiles_per_piece)
            if i == 0:
                for t in tiles:
                    carries[t] = (carry_ref[:, t * LANES:(t + 1) * LANES],
                                  carry_ref[:, half + t * LANES:half + (t + 1) * LANES])
            for g in range(i * per, min((i + 1) * per, groups)):
                for t in tiles:
                    carries[t] = tile_group(t, g, carries[t])
            if i == row_pieces - 1:
                for t in tiles:
                    carry_ref[:, t * LANES:(t + 1) * LANES] = carries[t][0]
                    carry_ref[:, half + t * LANES:half + (t + 1) * LANES] = carries[t][1]

        carries = {}
        return [functools.partial(piece, q, i, carries)
                for q in range(ntile // tiles_per_piece) for i in range(row_pieces)]

    def outputs(slot):
        taps_per_piece = ncol // LANES

        def cast():
            for t in range(2 * ntile):
                sb_ref[:, t * LANES:(t + 1) * LANES] = st_refs[slot][t].astype(BF16)

        def piece(c):
            lhs = jnp.concatenate([sb_ref[...], lhs_refs[slot][...]], axis=1)
            yall = _dot(lhs, wck_ref[:, c * ncol:(c + 1) * ncol])
            for i in range(taps_per_piece):
                tap = c * taps_per_piece + i
                ys_ref[pl.ds(tap, nblk, stride=s), :] = yall[:, i * LANES:(i + 1) * LANES]

        def finish():
            y = ys_ref[...] + d_ref[...] * uf_refs[slot][...]
            y_ref[...] = jax.nn.gelu(y).astype(BF16)

        return [cast] + [functools.partial(piece, c) for c in range(s * LANES // ncol)] + [finish]

    for phase in range(3):
        @pl.when(n % 3 == phase)
        def _steps(phase=phase):
            mxu = increments(phase) + outputs((phase + 1) % 3)
            vpu = scan((phase + 2) % 3, 3)
            for i, piece in enumerate(mxu):
                piece()
                if i < len(vpu):
                    vpu[i]()
            for piece in vpu[len(mxu):]:
                piece()


def _s5_scan(u, wb, wck, consts, d):
    B, _, L, _ = u.shape
    s = S5_TAPS
    tm = TM_S5
    nt = L // tm
    ntiles = LANE_TILES * B * nt
    tile_of = lambda n, lag: jnp.clip(n - lag, 0, ntiles - 1)
    lane_tile = lambda n, lag: tile_of(n, lag) // (B * nt)

    def tok(lag):
        def index(n):
            i = tile_of(n, lag)
            return ((i // nt) % B, i // (B * nt), i % nt, 0)
        return pl.BlockSpec((None, None, tm, LANES), index)

    def par(lag, *shape):
        return pl.BlockSpec((None,) + shape, lambda n: (lane_tile(n, lag),) + (0,) * len(shape))

    nblk = tm // s
    return pl.pallas_call(
        functools.partial(_s5_scan_kernel, tiles_per_seq=nt),
        grid=(ntiles + 2,),
        in_specs=[tok(0), par(0, s * LANES, 2 * TILE_STATES), par(1, 5, 2, SUBLANES, TILE_STATES),
                  par(2, 2 * TILE_STATES + s * LANES, s * LANES), par(2, 1, LANES)],
        out_specs=tok(2),
        out_shape=jax.ShapeDtypeStruct((B, LANE_TILES, L, LANES), BF16),
        scratch_shapes=([pltpu.VMEM((tm, LANES), F32)] * 3
                        + [pltpu.VMEM((nblk, s * LANES), BF16)] * 3
                        + [pltpu.VMEM((2 * TILE_STATES // LANES, nblk, LANES), F32)] * 3
                        + [pltpu.VMEM((nblk, 2 * TILE_STATES), BF16), pltpu.VMEM((tm, LANES), F32),
                           pltpu.VMEM((SUBLANES, 2 * TILE_STATES), F32)]),
        compiler_params=pltpu.CompilerParams(
            dimension_semantics=("arbitrary",), vmem_limit_bytes=VMEM_LIMIT),
        name="s5_scan",
    )(u, wb, consts, wck, d)


def _out_xattn_kernel(x_ref, ret_ref, y_ref, gs_ref, gw_ref, gb_ref, wout_ref, g2_ref, wq_ref,
                      ka_ref, va_ref, wo_ref, gf_ref, o_ref):
    y = jnp.concatenate([y_ref[j] for j in range(LANE_TILES)], axis=1)
    z = _dot(y, gw_ref[...]) + gb_ref[...]
    ssm = (y.astype(F32) * _sigmoid(z) * gs_ref[...].astype(F32)).astype(BF16)
    x1 = (x_ref[...] + _dot(ret_ref[...], wout_ref[0:RET_V_WIDTH, :])
          + _dot(ssm, wout_ref[RET_V_WIDTH:RET_V_WIDTH + S5_WIDTH, :]))
    h2 = _rms(x1, g2_ref[...]).astype(BF16)
    qa = (_dot(h2, wq_ref[...]) * (XA_DH ** -0.5)).astype(BF16)
    heads = []
    for h in range(XA_HEADS):
        cols = slice(h * XA_DH, (h + 1) * XA_DH)
        s = lax.dot_general(qa[:, cols], ka_ref[:, cols], (((1,), (1,)), ((), ())),
                            preferred_element_type=F32)
        e = jnp.exp(s - jnp.max(s, axis=-1, keepdims=True))
        l = jnp.sum(e, axis=-1, keepdims=True)
        heads.append((_dot(e.astype(BF16), va_ref[:, cols]) * (1.0 / l)).astype(BF16))
    o = jnp.concatenate(heads, axis=-1)
    x2 = x1 + _dot(o, wo_ref[...])
    o_ref[...] = _rms(x2, gf_ref[...])


def _out_xattn(x, ret, y, gs, gw, gb, wout, g2, wq, ka, va, wo, gf):
    B, L, D = x.shape
    tm = TM_OUT
    row = pl.BlockSpec((None, tm, D), lambda b, t: (b, t, 0))
    const = lambda shape: pl.BlockSpec(shape, lambda b, t: (0,) * len(shape))
    mem = pl.BlockSpec((None, MEM_LEN, D), lambda b, t: (b, 0, 0))
    ytile = pl.BlockSpec((None, LANE_TILES, tm, LANES), lambda b, t: (b, 0, t, 0))
    return pl.pallas_call(
        _out_xattn_kernel,
        grid=(B, L // tm),
        in_specs=[row, row, ytile, row, const((D, D)), const((1, D)), const(wout.shape), const((1, D)),
                  const((D, D)), mem, mem, const((D, D)), const((1, D))],
        out_specs=row,
        out_shape=jax.ShapeDtypeStruct((B, L, D), F32),
        compiler_params=pltpu.CompilerParams(
            dimension_semantics=("arbitrary", "arbitrary"), vmem_limit_bytes=VMEM_LIMIT),
        name="out_xattn",
    )(x, ret, y, gs, gw, gb, wout, g2, wq, ka, va, wo, gf)


def _rope_inv_lanes():
    half = RET_DK // 2
    inv = ROPE_BASE ** (-np.arange(half, dtype=np.float64) / half)
    return jnp.asarray(np.tile(inv, LANES // half).reshape(1, LANES), F32)


def kernel(x, mem, positions, norm1_g, w_in, ret_gn_g, s5_a_re, s5_a_im, s5_log_dt, s5_b_re, s5_b_im, s5_c_re, s5_c_im, s5_d, s5_glu_w, s5_glu_b, w_out, norm2_g, norm_mem_g, xa_wq, xa_wk, xa_wv, xa_wo, norm_f_g):
    B, L, D = x.shape
    l = 0
    bf = lambda w: w.astype(BF16)
    rowvec = lambda v: v.reshape(1, -1)

    wb, wck, consts = _s5_prep(s5_a_re[l], s5_a_im[l], s5_log_dt[l], s5_b_re[l], s5_b_im[l],
                               s5_c_re[l], s5_c_im[l])
    (u, g_s5, ret), (glu_w, wout, wq, wk, wv, wo) = _proj_ret(
        x, positions.reshape(B, L, 1), _rope_inv_lanes(), rowvec(norm1_g[l]), bf(w_in[l]), rowvec(ret_gn_g[l]),
        [s5_glu_w[l], w_out[l], xa_wq[l], xa_wk[l], xa_wv[l], xa_wo[l]])
    ka, va = _mem_kv(mem, rowvec(norm_mem_g[l]), wk, wv)
    y = _s5_scan(u, wb, wck, consts, s5_d[l].reshape(LANE_TILES, 1, LANES))
    return _out_xattn(x, ret, y, g_s5, glu_w, rowvec(s5_glu_b[l]), wout,
                      rowvec(norm2_g[l]), wq, ka, va, wo, rowvec(norm_f_g))
```

```python
import functools
import math

import numpy as np
import jax
import jax.numpy as jnp
from jax import lax
from jax.experimental import pallas as pl
from jax.experimental.pallas import tpu as pltpu

F32 = jnp.float32
BF16 = jnp.bfloat16

D_MODEL = 1024
MEM_LEN = 256
EPS = 1e-6
ROPE_BASE = 10000.0

RET_HEADS = 8
RET_QK_WIDTH = 512
RET_V_WIDTH = 1024
RET_DK = 64
RET_DV = 128
CHUNK = 128

S5_WIDTH = 1024
S5_GROUP = 16
S5_GROUPS = 64
S5_STATE = 64
S5_NSTATE = S5_GROUPS * S5_STATE

IN_COLS = 5120
COL_Q, COL_K, COL_V, COL_GR, COL_U, COL_GS = 0, 512, 1024, 2048, 3072, 4096

XA_HEADS = 4
XA_DH = 256

LANES = 128
SUBLANES = 8
VMEM_LIMIT = 48 * 1024 * 1024

LANE_TILES = S5_WIDTH // LANES
TILE_GROUPS = LANES // S5_GROUP
TILE_STATES = S5_NSTATE // LANE_TILES
S5_TAPS = 4
S5_SCAN_LANES = 2 * LANES
S5_RUN = 4

LOG_G = tuple(math.log1p(-(2.0 ** (-5.0 - h))) for h in range(RET_HEADS))

TM_PROJ = 512
TM_S5 = 4096
TM_OUT = 512


def _rms(x, g):
    ms = jnp.mean(x * x, axis=-1, keepdims=True)
    return x * lax.rsqrt(ms + EPS) * g


def _sigmoid(z):
    return 1.0 / (1.0 + jnp.exp(-z))


def _silu(g):
    return g * _sigmoid(g)


def _dot(a, b):
    return jnp.dot(a, b, preferred_element_type=F32)


def _cmul(ar, ai, br, bi):
    return ar * br - ai * bi, ar * bi + ai * br


def _s5_discretise(ar, ai, ldt):
    dt = jnp.exp(ldt)
    mag = jnp.exp(ar * dt)
    p_re = mag * jnp.cos(ai * dt)
    p_im = mag * jnp.sin(ai * dt)
    den = ar * ar + ai * ai
    nr, ni = p_re - 1.0, p_im
    f_re = (nr * ar + ni * ai) / den
    f_im = (ni * ar - nr * ai) / den
    return p_re, p_im, f_re, f_im


def _powers(p_re, p_im, n):
    pw = [(jnp.ones_like(p_re), jnp.zeros_like(p_im))]
    for _ in range(n):
        pw.append(_cmul(pw[-1][0], pw[-1][1], p_re, p_im))
    return pw


def _block_diag(x):
    tiled = jnp.concatenate([x] * TILE_GROUPS, axis=1)
    r = lax.broadcasted_iota(jnp.int32, tiled.shape, 0)
    c = lax.broadcasted_iota(jnp.int32, tiled.shape, 1)
    return jnp.where(r // S5_GROUP == c // S5_STATE, tiled, 0.0)


def _s5_prep_kernel(a_ref, ag_ref, b_ref, c_ref, win_ref, wb_ref, wck_ref, consts_ref, win16_ref):
    s = S5_TAPS
    half = TILE_STATES
    def per_row(v):
        return jnp.broadcast_to(v[:, None, :], (TILE_GROUPS, S5_GROUP, S5_STATE)).reshape(LANES, S5_STATE)

    g_re, g_im, gf_re, gf_im = _s5_discretise(ag_ref[0], ag_ref[1], ag_ref[2])
    gw = [(per_row(r), per_row(i)) for r, i in _powers(g_re, g_im, s)]
    bbr, bbi = _cmul(per_row(gf_re), per_row(gf_im), b_ref[0], b_ref[1])
    for k in range(s):
        wr, wi = _cmul(bbr, bbi, *gw[s - 1 - k])
        wb_ref[k * LANES:(k + 1) * LANES, 0:half] = _block_diag(wr).astype(BF16)
        wb_ref[k * LANES:(k + 1) * LANES, half:2 * half] = _block_diag(wi).astype(BF16)

    bb = jnp.concatenate([bbr, bbi], axis=1)
    r = lax.broadcasted_iota(jnp.int32, (LANES, LANES), 0)
    c = lax.broadcasted_iota(jnp.int32, (LANES, LANES), 1)
    same_group = r // S5_GROUP == c // S5_GROUP
    taps = []
    for d in range(s + 1):
        er, ei = _cmul(c_ref[0], c_ref[1], *gw[d])
        if d >= 1:
            cwt = jnp.concatenate([_block_diag(er), -_block_diag(ei)], axis=1)
            wck_ref[0:2 * half, (d - 1) * LANES:d * LANES] = cwt.T.astype(BF16)
        if d < s:
            t = lax.dot_general(bb, jnp.concatenate([er, -ei], axis=1), (((1,), (1,)), ((), ())),
                                preferred_element_type=F32, precision=lax.Precision.HIGHEST)
            taps.append(jnp.where(same_group, t, 0.0))
    zero = jnp.zeros((LANES, LANES), BF16)
    for m in range(s):
        for i in range(s):
            blk = taps[i - m].astype(BF16) if m <= i else zero
            wck_ref[2 * half + m * LANES:2 * half + (m + 1) * LANES, i * LANES:(i + 1) * LANES] = blk

    p_re, p_im, _, _ = _s5_discretise(a_ref[0:1, :], a_ref[1:2, :], a_ref[2:3, :])
    pw = _powers(p_re, p_im, s)
    shape = (SUBLANES, half)
    qr = _powers(pw[s][0], pw[s][1], S5_RUN)
    consts_ref[0, 0] = jnp.broadcast_to(qr[1][0], shape)
    consts_ref[0, 1] = jnp.broadcast_to(qr[1][1], shape)
    rp = _powers(qr[S5_RUN][0], qr[S5_RUN][1], SUBLANES)
    row = lax.broadcasted_iota(jnp.int32, shape, 0)
    for idx, d in enumerate((1, 2, 4)):
        consts_ref[1 + idx, 0] = jnp.where(row >= d, rp[d][0], 0.0)
        consts_ref[1 + idx, 1] = jnp.where(row >= d, rp[d][1], 0.0)
    cr = jnp.zeros(shape, F32)
    ci = jnp.zeros(shape, F32)
    for k in range(SUBLANES):
        cr = jnp.where(row == k, rp[k + 1][0], cr)
        ci = jnp.where(row == k, rp[k + 1][1], ci)
    consts_ref[4, 0] = cr
    consts_ref[4, 1] = ci

    win16_ref[...] = win_ref[...].astype(BF16)


def _s5_prep(a_re, a_im, log_dt, b_re, b_im, c_re, c_im, w_in):
    s = S5_TAPS
    ldt = jnp.broadcast_to(log_dt[:, None], (S5_GROUPS, S5_STATE))
    ag = jnp.stack([a_re, a_im, ldt])
    a3 = ag.reshape(3, S5_NSTATE)
    bt = jnp.stack([b_re, b_im]).transpose(0, 1, 3, 2).reshape(2, S5_WIDTH, S5_STATE)
    ct = jnp.stack([c_re, c_im]).reshape(2, S5_WIDTH, S5_STATE)
    tile = lambda *shape: pl.BlockSpec((None,) + shape, lambda j: (j,) + (0,) * len(shape))
    return pl.pallas_call(
        _s5_prep_kernel,
        grid=(LANE_TILES,),
        in_specs=[pl.BlockSpec((3, TILE_STATES), lambda j: (0, j)),
                  pl.BlockSpec((3, TILE_GROUPS, S5_STATE), lambda j: (0, j, 0)),
                  pl.BlockSpec((2, LANES, S5_STATE), lambda j: (0, j, 0)),
                  pl.BlockSpec((2, LANES, S5_STATE), lambda j: (0, j, 0)),
                  pl.BlockSpec((D_MODEL // LANE_TILES, IN_COLS), lambda j: (j, 0))],
        out_specs=(tile(s * LANES, 2 * TILE_STATES),
                   tile(2 * TILE_STATES + s * LANES, s * LANES),
                   tile(5, 2, SUBLANES, TILE_STATES),
                   pl.BlockSpec((D_MODEL // LANE_TILES, IN_COLS), lambda j: (j, 0))),
        out_shape=(jax.ShapeDtypeStruct((LANE_TILES, s * LANES, 2 * TILE_STATES), BF16),
                   jax.ShapeDtypeStruct((LANE_TILES, 2 * TILE_STATES + s * LANES, s * LANES), BF16),
                   jax.ShapeDtypeStruct((LANE_TILES, 5, 2, SUBLANES, TILE_STATES), F32),
                   jax.ShapeDtypeStruct((D_MODEL, IN_COLS), BF16)),
        compiler_params=pltpu.CompilerParams(
            dimension_semantics=("arbitrary",), vmem_limit_bytes=VMEM_LIMIT),
        name="s5_prep",
    )(a3, ag, bt, ct, w_in)


def _mem_kv_kernel(mem_ref, g_ref, wk_ref, wv_ref, k_ref, v_ref):
    m = _rms(mem_ref[...], g_ref[...]).astype(BF16)
    k_ref[...] = _dot(m, wk_ref[...]).astype(BF16)
    v_ref[...] = _dot(m, wv_ref[...]).astype(BF16)


def _mem_kv(mem, g, wk, wv):
    B, M, D = mem.shape
    return pl.pallas_call(
        _mem_kv_kernel,
        grid=(B,),
        in_specs=[pl.BlockSpec((None, M, D), lambda b: (b, 0, 0)),
                  pl.BlockSpec((1, D), lambda b: (0, 0)),
                  pl.BlockSpec((D, D), lambda b: (0, 0)),
                  pl.BlockSpec((D, D), lambda b: (0, 0))],
        out_specs=(pl.BlockSpec((None, M, D), lambda b: (b, 0, 0)),
                   pl.BlockSpec((None, M, D), lambda b: (b, 0, 0))),
        out_shape=(jax.ShapeDtypeStruct((B, M, D), BF16),
                   jax.ShapeDtypeStruct((B, M, D), BF16)),
        compiler_params=pltpu.CompilerParams(
            dimension_semantics=("arbitrary",), vmem_limit_bytes=VMEM_LIMIT),
        name="mem_kv",
    )(mem, g, wk, wv)


def _token_prologue(x_ref, pos_ref, inv_ref, g_ref, h_ref, cos_ref, sin_ref):
    h_ref[...] = _rms(x_ref[...], g_ref[...]).astype(BF16)
    lane = lax.broadcasted_iota(jnp.int32, (1, LANES), 1)
    nfreq = RET_DK // 2
    group = lane // nfreq
    nblock = LANES // nfreq
    rows = x_ref.shape[0] // nblock
    pos = pos_ref[...].astype(F32)
    pos4 = jnp.zeros((rows, LANES), F32)
    for k in range(nblock):
        pos4 = jnp.where(group == k, pos[k * rows:(k + 1) * rows, :], pos4)
    ang4 = pos4 * inv_ref[...]

    def spread(t4):
        blocks = []
        for k in range(nblock):
            m = jnp.where(group == k, t4, 0.0)
            m = m + pltpu.roll(m, nfreq, 1)
            blocks.append(m + pltpu.roll(m, 2 * nfreq, 1))
        return jnp.concatenate(blocks, axis=0)

    first_half = (lane % RET_DK) < nfreq
    sin = spread(jnp.sin(ang4))
    cos_ref[...] = spread(jnp.cos(ang4))
    sin_ref[...] = jnp.where(first_half, -sin, sin)


def _proj_ret_kernel(x0_ref, pos0_ref, x_ref, pos_ref, inv_ref, g_ref, w_ref, gn_ref, *rest,
                     tiles_per_seq, num_tiles, num_casts):
    f32_refs, rest = rest[:num_casts], rest[num_casts:]
    (u_ref, gs_ref, ret_ref), rest = rest[:3], rest[3:]
    bf16_refs, rest = rest[:num_casts], rest[num_casts:]
    (cur_ref, prev_ref, hn_ref, cn_ref, sn_ref, h_ref, cos_ref, sin_ref,
     state_ref, decay_ref, qw_ref, kw_ref) = rest
    t = pl.program_id(0)

    @pl.when(t == 0)
    def _init():
        _retention_constants(decay_ref, qw_ref, kw_ref)
        cur_ref[...] = jnp.zeros_like(cur_ref)
        _token_prologue(x0_ref, pos0_ref, inv_ref, g_ref, hn_ref, cn_ref, sn_ref)

    @pl.when((t == 0) | (t % tiles_per_seq == 1))
    def _reset():
        state_ref[...] = jnp.zeros_like(state_ref)

    ret_items = _retention_items(prev_ref, gn_ref, ret_ref, state_ref, decay_ref, qw_ref, kw_ref)

    @pl.when(t == num_tiles)
    def _drain():
        prev_ref[...] = cur_ref[...]
        for r in ret_items:
            r()

    @pl.when(t < num_tiles)
    def _step():
        prev_ref[...] = cur_ref[...]
        h_ref[...] = hn_ref[...]
        cos_ref[...] = cn_ref[...]
        sin_ref[...] = sn_ref[...]
        h = h_ref[...]
        cos = cos_ref[...]
        sin_signed = sin_ref[...]
        lane = lax.broadcasted_iota(jnp.int32, (1, LANES), 1)
        first_half = (lane % RET_DK) < (RET_DK // 2)

        def rope(p):
            partner = jnp.where(first_half,
                                pltpu.roll(p, LANES - RET_DK // 2, 1),
                                pltpu.roll(p, RET_DK // 2, 1))
            return p * cos + partner * sin_signed

        def proj_q():
            pq = _dot(h, w_ref[:, COL_Q:COL_K])
            for i in range(RET_QK_WIDTH // LANES):
                sl = slice(i * LANES, (i + 1) * LANES)
                cur_ref[:, COL_Q + i * LANES:COL_Q + (i + 1) * LANES] = rope(pq[:, sl]).astype(BF16)

        def proj_k():
            pk = _dot(h, w_ref[:, COL_K:COL_V])
            for i in range(RET_QK_WIDTH // LANES):
                sl = slice(i * LANES, (i + 1) * LANES)
                cur_ref[:, COL_K + i * LANES:COL_K + (i + 1) * LANES] = (
                    rope(pk[:, sl]) * (RET_DK ** -0.5)).astype(BF16)

        def proj_v():
            cur_ref[:, COL_V:COL_GR] = _dot(h, w_ref[:, COL_V:COL_GR]).astype(BF16)

        def proj_gate():
            cur_ref[:, COL_GR:COL_U] = _silu(_dot(h, w_ref[:, COL_GR:COL_U])).astype(BF16)

        def proj_u():
            pu = _dot(h, w_ref[:, COL_U:COL_GS]).astype(BF16)
            for j in range(LANE_TILES):
                u_ref[j] = pu[:, j * LANES:(j + 1) * LANES]

        def proj_gs():
            gs_ref[...] = _silu(_dot(h, w_ref[:, COL_GS:IN_COLS])).astype(BF16)

        proj_items = [proj_q, proj_k, proj_v, proj_gate, proj_u, proj_gs]
        per = -(-len(ret_items) // len(proj_items))
        for i, item in enumerate(proj_items):
            for r in ret_items[i * per:(i + 1) * per]:
                r()
            item()
        _token_prologue(x_ref, pos_ref, inv_ref, g_ref, hn_ref, cn_ref, sn_ref)
        for src_ref, dst_ref in zip(f32_refs, bf16_refs):
            dst_ref[...] = src_ref[...].astype(BF16)


def _proj_ret(x, pos3, inv, g, w, gn, later):
    B, L, D = x.shape
    tm = TM_PROJ
    nt = L // tm
    n = B * nt
    cur = lambda t: jnp.minimum(t, n - 1)
    nxt = lambda t: jnp.minimum(t + 1, n - 1)
    lag = lambda t: jnp.maximum(t - 1, 0)
    first = lambda t: 0 * t
    row = lambda width, tile: pl.BlockSpec((None, tm, width), lambda t: (tile(t) // nt, tile(t) % nt, 0))
    const = lambda shape: pl.BlockSpec(shape, lambda t: (0,) * len(shape))
    slab = lambda m: pl.BlockSpec((m.shape[0] // n, m.shape[1]), lambda t: (cur(t), 0))
    outs = pl.pallas_call(
        functools.partial(_proj_ret_kernel, tiles_per_seq=nt, num_tiles=n, num_casts=len(later)),
        grid=(n + 1,),
        in_specs=[row(D, first), row(1, first), row(D, nxt), row(1, nxt), const((1, LANES)), const((1, D)),
                  const((D, IN_COLS)), const((1, RET_V_WIDTH))] + [slab(m) for m in later],
        out_specs=[pl.BlockSpec((None, LANE_TILES, tm, LANES), lambda t: (cur(t) // nt, 0, cur(t) % nt, 0)),
                   row(S5_WIDTH, cur), row(RET_V_WIDTH, lag)] + [slab(m) for m in later],
        out_shape=[jax.ShapeDtypeStruct((B, LANE_TILES, L, LANES), BF16),
                   jax.ShapeDtypeStruct((B, L, S5_WIDTH), BF16),
                   jax.ShapeDtypeStruct((B, L, RET_V_WIDTH), BF16)]
        + [jax.ShapeDtypeStruct(m.shape, BF16) for m in later],
        scratch_shapes=[pltpu.VMEM((tm, COL_U), BF16),
                        pltpu.VMEM((tm, COL_U), BF16),
                        pltpu.VMEM((tm, D), BF16), pltpu.VMEM((tm, LANES), F32), pltpu.VMEM((tm, LANES), F32),
                        pltpu.VMEM((tm, D), BF16), pltpu.VMEM((tm, LANES), F32), pltpu.VMEM((tm, LANES), F32),
                        pltpu.VMEM((RET_HEADS // 2, LANES, 2 * RET_DV), F32),
                        pltpu.VMEM((RET_HEADS // 2, CHUNK, 2 * CHUNK), F32),
                        pltpu.VMEM((RET_HEADS // 2, CHUNK, 2 * RET_DV), F32),
                        pltpu.VMEM((RET_HEADS // 2, CHUNK, LANES), F32)],
        compiler_params=pltpu.CompilerParams(
            dimension_semantics=("arbitrary",), vmem_limit_bytes=VMEM_LIMIT),
        name="proj_ret",
    )(x, pos3, x, pos3, inv, g, w, gn, *later)
    return outs[:3], outs[3:]


def _retention_constants(decay_ref, qw_ref, kw_ref):
    i = lax.broadcasted_iota(jnp.int32, (CHUNK, 2 * CHUNK), 0).astype(F32)
    c = lax.broadcasted_iota(jnp.int32, (CHUNK, 2 * CHUNK), 1)
    j = (c % CHUNK).astype(F32)
    diff = i - j
    lane = lax.broadcasted_iota(jnp.int32, (CHUNK, LANES), 1)
    ik = lax.broadcasted_iota(jnp.int32, (CHUNK, LANES), 0).astype(F32)
    for p in range(RET_HEADS // 2):
        lg = jnp.where(c < CHUNK, LOG_G[2 * p], LOG_G[2 * p + 1])
        decay_ref[p] = jnp.where(diff >= 0.0, jnp.exp(lg * jnp.maximum(diff, 0.0)), 0.0)
        qw_ref[p] = jnp.exp(lg * (i + 1.0))
        lgk = jnp.where(lane < RET_DK, LOG_G[2 * p], LOG_G[2 * p + 1])
        kw_ref[p] = jnp.exp(lgk * (CHUNK - 1.0 - ik))


def _retention_items(src_ref, gn_ref, o_ref, state_ref, decay_ref, qw_ref, kw_ref):
    pairs = RET_HEADS // 2
    low = lax.broadcasted_iota(jnp.int32, (1, LANES), 1) < RET_DK
    srow = lax.broadcasted_iota(jnp.int32, (LANES, 2 * RET_DV), 0)
    scol = lax.broadcasted_iota(jnp.int32, (LANES, 2 * RET_DV), 1)
    own_block = (srow < RET_DK) == (scol < RET_DV)
    scol1 = lax.broadcasted_iota(jnp.int32, (1, 2 * RET_DV), 1)
    zero_v = jnp.zeros((CHUNK, RET_DV), BF16)

    def pair_step(c, p):
        rows = slice(c * CHUNK, (c + 1) * CHUNK)
        qt = src_ref[rows, COL_Q + p * LANES:COL_Q + (p + 1) * LANES]
        kt = src_ref[rows, COL_K + p * LANES:COL_K + (p + 1) * LANES]
        vp = src_ref[rows, COL_V + 2 * p * RET_DV:COL_V + 2 * (p + 1) * RET_DV]
        state = state_ref[p]
        cross = _dot(qt, state.astype(BF16))
        kw = (kt.astype(F32) * kw_ref[p]).astype(BF16)
        upd = lax.dot_general(kw, vp, (((0,), (0,)), ((), ())), preferred_element_type=F32)
        chunk_decay = jnp.where(scol1 < RET_DV,
                                math.exp(LOG_G[2 * p] * CHUNK), math.exp(LOG_G[2 * p + 1] * CHUNK))
        state_ref[p] = state * chunk_decay + jnp.where(own_block, upd, 0.0)

        zero_k = jnp.zeros_like(kt)
        k_rows = jnp.concatenate([jnp.where(low, kt, zero_k), jnp.where(low, zero_k, kt)], axis=0)
        s = lax.dot_general(qt, k_rows, (((1,), (1,)), ((), ())), preferred_element_type=F32)
        pm = (s * decay_ref[p]).astype(BF16)
        v_diag = jnp.concatenate(
            [jnp.concatenate([vp[:, 0:RET_DV], zero_v], axis=1),
             jnp.concatenate([zero_v, vp[:, RET_DV:2 * RET_DV]], axis=1)], axis=0)
        o2 = _dot(pm, v_diag) + qw_ref[p] * cross
        for e in range(2):
            h = 2 * p + e
            cols = slice(h * RET_DV, (h + 1) * RET_DV)
            o = o2[:, e * RET_DV:(e + 1) * RET_DV]
            mu = jnp.mean(o, axis=-1, keepdims=True)
            oc = o - mu
            var = jnp.mean(oc * oc, axis=-1, keepdims=True)
            on = oc * lax.rsqrt(var + EPS) * gn_ref[:, cols]
            gate = src_ref[rows, COL_GR + h * RET_DV:COL_GR + (h + 1) * RET_DV].astype(F32)
            o_ref[rows, cols] = (on * gate).astype(BF16)

    return [functools.partial(pair_step, c, p)
            for c in range(src_ref.shape[0] // CHUNK) for p in range(pairs)]


def _s5_scan_kernel(u_ref, wb_ref, consts_ref, wck_ref, d_ref, y_ref, *scratch, tiles_per_seq):
    uf_refs, lhs_refs, st_refs = scratch[0:3], scratch[3:6], scratch[6:9]
    sb_ref, ys_ref, carry_ref = scratch[9:]
    n = pl.program_id(0)
    s = S5_TAPS
    tm = u_ref.shape[0]
    nblk = tm // s
    half = TILE_STATES

    @pl.when(n == 0)
    def _init():
        for ref in scratch[0:9]:
            ref[...] = jnp.zeros_like(ref)

    @pl.when((n == 0) | ((n - 1) % tiles_per_seq == 0))
    def _reset():
        carry_ref[...] = jnp.zeros_like(carry_ref)

    ncol = 2 * LANES

    def increments(slot):
        uf_ref, lhs_ref, st_ref = uf_refs[slot], lhs_refs[slot], st_refs[slot]

        def piece(c):
            if c == 0:
                uf_ref[...] = u_ref[...].astype(F32)
                lhs_ref[...] = jnp.concatenate(
                    [uf_ref[pl.ds(k, nblk, stride=s), :] for k in range(s)], axis=1).astype(BF16)
            z = _dot(lhs_ref[...], wb_ref[:, c * ncol:(c + 1) * ncol])
            for i in range(ncol // LANES):
                st_ref[c * (ncol // LANES) + i] = z[:, i * LANES:(i + 1) * LANES]

        return [functools.partial(piece, c) for c in range(2 * half // ncol)]

    ntile = half // LANES

    def scan(slot, row_pieces):
        st_ref = st_refs[slot]
        run = S5_RUN
        span = run * SUBLANES
        row = lax.broadcasted_iota(jnp.int32, (SUBLANES, LANES), 0)
        last = SUBLANES - 1
        groups = nblk // span
        per = -(-groups // row_pieces)
        tiles_per_piece = S5_SCAN_LANES // LANES

        def tile_group(t, g, carry):
            lanes = slice(t * LANES, (t + 1) * LANES)
            const = lambda idx: (consts_ref[idx, 0, :, lanes], consts_ref[idx, 1, :, lanes])
            cr, ci = carry
            rows = [pl.ds(g * span + j, SUBLANES, stride=run) for j in range(run)]
            loc = [(st_ref[t, rows[0], :], st_ref[ntile + t, rows[0], :])]
            for j in range(1, run):
                pr, pi = _cmul(*const(0), *loc[-1])
                loc.append((pr + st_ref[t, rows[j], :], pi + st_ref[ntile + t, rows[j], :]))
            fr, fi = loc[-1]
            for idx, d in enumerate((1, 2, 4)):
                pr, pi = _cmul(*const(1 + idx), pltpu.roll(fr, d, 0), pltpu.roll(fi, d, 0))
                fr, fi = fr + pr, fi + pi
            pr, pi = _cmul(*const(4), cr, ci)
            fr, fi = fr + pr, fi + pi
            er = jnp.where(row == 0, cr, pltpu.roll(fr, 1, 0))
            ei = jnp.where(row == 0, ci, pltpu.roll(fi, 1, 0))
            st_ref[t, rows[0], :] = er
            st_ref[ntile + t, rows[0], :] = ei
            for j in range(1, run):
                er, ei = _cmul(*const(0), er, ei)
                st_ref[t, rows[j], :] = loc[j - 1][0] + er
                st_ref[ntile + t, rows[j], :] = loc[j - 1][1] + ei
            return (jnp.broadcast_to(fr[last:, :], (SUBLANES, LANES)),
                    jnp.broadcast_to(fi[last:, :], (SUBLANES, LANES)))

        def piece(q, i, carries):
            tiles = range(q * tiles_per_piece, (q + 1) * tiles_per_piece)
            if i == 0:
                for t in tiles:
                    carries[t] = (carry_ref[:, t * LANES:(t + 1) * LANES],
                                  carry_ref[:, half + t * LANES:half + (t + 1) * LANES])
            for g in range(i * per, min((i + 1) * per, groups)):
                for t in tiles:
                    carries[t] = tile_group(t, g, carries[t])
            if i == row_pieces - 1:
                for t in tiles:
                    carry_ref[:, t * LANES:(t + 1) * LANES] = carries[t][0]
                    carry_ref[:, half + t * LANES:half + (t + 1) * LANES] = carries[t][1]

        carries = {}
        return [functools.partial(piece, q, i, carries)
                for q in range(ntile // tiles_per_piece) for i in range(row_pieces)]

    def outputs(slot):
        taps_per_piece = ncol // LANES

        def cast():
            for t in range(2 * ntile):
                sb_ref[:, t * LANES:(t + 1) * LANES] = st_refs[slot][t].astype(BF16)

        def piece(c):
            lhs = jnp.concatenate([sb_ref[...], lhs_refs[slot][...]], axis=1)
            yall = _dot(lhs, wck_ref[:, c * ncol:(c + 1) * ncol])
            for i in range(taps_per_piece):
                tap = c * taps_per_piece + i
                ys_ref[pl.ds(tap, nblk, stride=s), :] = yall[:, i * LANES:(i + 1) * LANES]

        def finish():
            y = ys_ref[...] + d_ref[...] * uf_refs[slot][...]
            y_ref[...] = jax.nn.gelu(y).astype(BF16)

        return [cast] + [functools.partial(piece, c) for c in range(s * LANES // ncol)] + [finish]

    for phase in range(3):
        @pl.when(n % 3 == phase)
        def _steps(phase=phase):
            mxu = increments(phase) + outputs((phase + 1) % 3)
            vpu = scan((phase + 2) % 3, 3)
            for i, piece in enumerate(mxu):
                piece()
                if i < len(vpu):
                    vpu[i]()
            for piece in vpu[len(mxu):]:
                piece()


def _s5_scan(u, wb, wck, consts, d):
    B, _, L, _ = u.shape
    s = S5_TAPS
    tm = TM_S5
    nt = L // tm
    ntiles = LANE_TILES * B * nt
    tile_of = lambda n, lag: jnp.clip(n - lag, 0, ntiles - 1)
    lane_tile = lambda n, lag: tile_of(n, lag) // (B * nt)

    def tok(lag):
        def index(n):
            i = tile_of(n, lag)
            return ((i // nt) % B, i // (B * nt), i % nt, 0)
        return pl.BlockSpec((None, None, tm, LANES), index)

    def par(lag, *shape):
        return pl.BlockSpec((None,) + shape, lambda n: (lane_tile(n, lag),) + (0,) * len(shape))

    nblk = tm // s
    return pl.pallas_call(
        functools.partial(_s5_scan_kernel, tiles_per_seq=nt),
        grid=(ntiles + 2,),
        in_specs=[tok(0), par(0, s * LANES, 2 * TILE_STATES), par(1, 5, 2, SUBLANES, TILE_STATES),
                  par(2, 2 * TILE_STATES + s * LANES, s * LANES), par(2, 1, LANES)],
        out_specs=tok(2),
        out_shape=jax.ShapeDtypeStruct((B, LANE_TILES, L, LANES), BF16),
        scratch_shapes=([pltpu.VMEM((tm, LANES), F32)] * 3
                        + [pltpu.VMEM((nblk, s * LANES), BF16)] * 3
                        + [pltpu.VMEM((2 * TILE_STATES // LANES, nblk, LANES), F32)] * 3
                        + [pltpu.VMEM((nblk, 2 * TILE_STATES), BF16), pltpu.VMEM((tm, LANES), F32),
                           pltpu.VMEM((SUBLANES, 2 * TILE_STATES), F32)]),
        compiler_params=pltpu.CompilerParams(
            dimension_semantics=("arbitrary",), vmem_limit_bytes=VMEM_LIMIT),
        name="s5_scan",
    )(u, wb, consts, wck, d)


def _out_xattn_kernel(x_ref, ret_ref, y_ref, gs_ref, gw_ref, gb_ref, wout_ref, g2_ref, wq_ref,
                      ka_ref, va_ref, wo_ref, gf_ref, o_ref):
    y = jnp.concatenate([y_ref[j] for j in range(LANE_TILES)], axis=1)
    z = _dot(y, gw_ref[...]) + gb_ref[...]
    ssm = (y.astype(F32) * _sigmoid(z) * gs_ref[...].astype(F32)).astype(BF16)
    x1 = (x_ref[...] + _dot(ret_ref[...], wout_ref[0:RET_V_WIDTH, :])
          + _dot(ssm, wout_ref[RET_V_WIDTH:RET_V_WIDTH + S5_WIDTH, :]))
    h2 = _rms(x1, g2_ref[...]).astype(BF16)
    qa = (_dot(h2, wq_ref[...]) * (XA_DH ** -0.5)).astype(BF16)
    heads = []
    for h in range(XA_HEADS):
        cols = slice(h * XA_DH, (h + 1) * XA_DH)
        s = lax.dot_general(qa[:, cols], ka_ref[:, cols], (((1,), (1,)), ((), ())),
                            preferred_element_type=F32)
        e = jnp.exp(s - jnp.max(s, axis=-1, keepdims=True))
        l = jnp.sum(e, axis=-1, keepdims=True)
        heads.append((_dot(e.astype(BF16), va_ref[:, cols]) * (1.0 / l)).astype(BF16))
    o = jnp.concatenate(heads, axis=-1)
    x2 = x1 + _dot(o, wo_ref[...])
    o_ref[...] = _rms(x2, gf_ref[...])


def _out_xattn(x, ret, y, gs, gw, gb, wout, g2, wq, ka, va, wo, gf):
    B, L, D = x.shape
    tm = TM_OUT
    row = pl.BlockSpec((None, tm, D), lambda b, t: (b, t, 0))
    const = lambda shape: pl.BlockSpec(shape, lambda b, t: (0,) * len(shape))
    mem = pl.BlockSpec((None, MEM_LEN, D), lambda b, t: (b, 0, 0))
    ytile = pl.BlockSpec((None, LANE_TILES, tm, LANES), lambda b, t: (b, 0, t, 0))
    return pl.pallas_call(
        _out_xattn_kernel,
        grid=(B, L // tm),
        in_specs=[row, row, ytile, row, const((D, D)), const((1, D)), const(wout.shape), const((1, D)),
                  const((D, D)), mem, mem, const((D, D)), const((1, D))],
        out_specs=row,
        out_shape=jax.ShapeDtypeStruct((B, L, D), F32),
        compiler_params=pltpu.CompilerParams(
            dimension_semantics=("arbitrary", "arbitrary"), vmem_limit_bytes=VMEM_LIMIT),
        name="out_xattn",
    )(x, ret, y, gs, gw, gb, wout, g2, wq, ka, va, wo, gf)


def _rope_inv_lanes():
    half = RET_DK // 2
    inv = ROPE_BASE ** (-np.arange(half, dtype=np.float64) / half)
    return jnp.asarray(np.tile(inv, LANES // half).reshape(1, LANES), F32)


def kernel(x, mem, positions, norm1_g, w_in, ret_gn_g, s5_a_re, s5_a_im, s5_log_dt, s5_b_re, s5_b_im, s5_c_re, s5_c_im, s5_d, s5_glu_w, s5_glu_b, w_out, norm2_g, norm_mem_g, xa_wq, xa_wk, xa_wv, xa_wo, norm_f_g):
    B, L, D = x.shape
    l = 0
    rowvec = lambda v: v.reshape(1, -1)

    wb, wck, consts, win = _s5_prep(s5_a_re[l], s5_a_im[l], s5_log_dt[l], s5_b_re[l], s5_b_im[l],
                                    s5_c_re[l], s5_c_im[l], w_in[l])
    (u, g_s5, ret), (glu_w, wout, wq, wk, wv, wo) = _proj_ret(
        x, positions.reshape(B, L, 1), _rope_inv_lanes(), rowvec(norm1_g[l]), win, rowvec(ret_gn_g[l]),
        [s5_glu_w[l], w_out[l], xa_wq[l], xa_wk[l], xa_wv[l], xa_wo[l]])
    ka, va = _mem_kv(mem, rowvec(norm_mem_g[l]), wk, wv)
    y = _s5_scan(u, wb, wck, consts, s5_d[l].reshape(LANE_TILES, 1, LANES))
    return _out_xattn(x, ret, y, g_s5, glu_w, rowvec(s5_glu_b[l]), wout,
                      rowvec(norm2_g[l]), wq, ka, va, wo, rowvec(norm_f_g))
```

```python
import functools
import math

import numpy as np
import jax
import jax.numpy as jnp
from jax import lax
from jax.experimental import pallas as pl
from jax.experimental.pallas import tpu as pltpu

F32 = jnp.float32
BF16 = jnp.bfloat16

D_MODEL = 1024
MEM_LEN = 256
EPS = 1e-6
ROPE_BASE = 10000.0

RET_HEADS = 8
RET_QK_WIDTH = 512
RET_V_WIDTH = 1024
RET_DK = 64
RET_DV = 128
CHUNK = 128

S5_WIDTH = 1024
S5_GROUP = 16
S5_GROUPS = 64
S5_STATE = 64
S5_NSTATE = S5_GROUPS * S5_STATE

IN_COLS = 5120
COL_Q, COL_K, COL_V, COL_GR, COL_U, COL_GS = 0, 512, 1024, 2048, 3072, 4096

XA_HEADS = 4
XA_DH = 256

LANES = 128
SUBLANES = 8
VMEM_LIMIT = 48 * 1024 * 1024

LANE_TILES = S5_WIDTH // LANES
TILE_GROUPS = LANES // S5_GROUP
TILE_STATES = S5_NSTATE // LANE_TILES
S5_TAPS = 4
S5_SCAN_LANES = 2 * LANES
S5_RUN = 4

LOG_G = tuple(math.log1p(-(2.0 ** (-5.0 - h))) for h in range(RET_HEADS))

TM_PROJ = 512
TM_S5 = 4096
TM_OUT = 512


def _rms(x, g):
    ms = jnp.mean(x * x, axis=-1, keepdims=True)
    return x * lax.rsqrt(ms + EPS) * g


def _sigmoid(z):
    return 1.0 / (1.0 + jnp.exp(-z))


def _silu(g):
    return g * _sigmoid(g)


def _dot(a, b):
    return jnp.dot(a, b, preferred_element_type=F32)


def _cmul(ar, ai, br, bi):
    return ar * br - ai * bi, ar * bi + ai * br


def _s5_discretise(ar, ai, ldt):
    dt = jnp.exp(ldt)
    mag = jnp.exp(ar * dt)
    p_re = mag * jnp.cos(ai * dt)
    p_im = mag * jnp.sin(ai * dt)
    den = ar * ar + ai * ai
    nr, ni = p_re - 1.0, p_im
    f_re = (nr * ar + ni * ai) / den
    f_im = (ni * ar - nr * ai) / den
    return p_re, p_im, f_re, f_im


def _powers(p_re, p_im, n):
    pw = [(jnp.ones_like(p_re), jnp.zeros_like(p_im))]
    for _ in range(n):
        pw.append(_cmul(pw[-1][0], pw[-1][1], p_re, p_im))
    return pw


def _block_diag(x):
    tiled = jnp.concatenate([x] * TILE_GROUPS, axis=1)
    r = lax.broadcasted_iota(jnp.int32, tiled.shape, 0)
    c = lax.broadcasted_iota(jnp.int32, tiled.shape, 1)
    return jnp.where(r // S5_GROUP == c // S5_STATE, tiled, 0.0)


def _s5_prep_kernel(a_ref, ag_ref, b_ref, c_ref, win_ref, wb_ref, wck_ref, consts_ref, win16_ref):
    s = S5_TAPS
    half = TILE_STATES
    def per_row(v):
        return jnp.broadcast_to(v[:, None, :], (TILE_GROUPS, S5_GROUP, S5_STATE)).reshape(LANES, S5_STATE)

    g_re, g_im, gf_re, gf_im = _s5_discretise(ag_ref[0], ag_ref[1], ag_ref[2])
    gw = [(per_row(r), per_row(i)) for r, i in _powers(g_re, g_im, s)]
    bbr, bbi = _cmul(per_row(gf_re), per_row(gf_im), b_ref[0], b_ref[1])
    for k in range(s):
        wr, wi = _cmul(bbr, bbi, *gw[s - 1 - k])
        wb_ref[k * LANES:(k + 1) * LANES, 0:half] = _block_diag(wr).astype(BF16)
        wb_ref[k * LANES:(k + 1) * LANES, half:2 * half] = _block_diag(wi).astype(BF16)

    bb = jnp.concatenate([bbr, bbi], axis=1)
    r = lax.broadcasted_iota(jnp.int32, (LANES, LANES), 0)
    c = lax.broadcasted_iota(jnp.int32, (LANES, LANES), 1)
    same_group = r // S5_GROUP == c // S5_GROUP
    taps = []
    for d in range(s + 1):
        er, ei = _cmul(c_ref[0], c_ref[1], *gw[d])
        if d >= 1:
            cwt = jnp.concatenate([_block_diag(er), -_block_diag(ei)], axis=1)
            wck_ref[0:2 * half, (d - 1) * LANES:d * LANES] = cwt.T.astype(BF16)
        if d < s:
            t = lax.dot_general(bb, jnp.concatenate([er, -ei], axis=1), (((1,), (1,)), ((), ())),
                                preferred_element_type=F32, precision=lax.Precision.HIGHEST)
            taps.append(jnp.where(same_group, t, 0.0))
    zero = jnp.zeros((LANES, LANES), BF16)
    for m in range(s):
        for i in range(s):
            blk = taps[i - m].astype(BF16) if m <= i else zero
            wck_ref[2 * half + m * LANES:2 * half + (m + 1) * LANES, i * LANES:(i + 1) * LANES] = blk

    p_re, p_im, _, _ = _s5_discretise(a_ref[0:1, :], a_ref[1:2, :], a_ref[2:3, :])
    pw = _powers(p_re, p_im, s)
    shape = (SUBLANES, half)
    qr = _powers(pw[s][0], pw[s][1], S5_RUN)
    consts_ref[0, 0] = jnp.broadcast_to(qr[1][0], shape)
    consts_ref[0, 1] = jnp.broadcast_to(qr[1][1], shape)
    rp = _powers(qr[S5_RUN][0], qr[S5_RUN][1], SUBLANES)
    row = lax.broadcasted_iota(jnp.int32, shape, 0)
    for idx, d in enumerate((1, 2, 4)):
        consts_ref[1 + idx, 0] = jnp.where(row >= d, rp[d][0], 0.0)
        consts_ref[1 + idx, 1] = jnp.where(row >= d, rp[d][1], 0.0)
    cr = jnp.zeros(shape, F32)
    ci = jnp.zeros(shape, F32)
    for k in range(SUBLANES):
        cr = jnp.where(row == k, rp[k + 1][0], cr)
        ci = jnp.where(row == k, rp[k + 1][1], ci)
    consts_ref[4, 0] = cr
    consts_ref[4, 1] = ci

    win16_ref[...] = win_ref[...].astype(BF16)


def _s5_prep(a_re, a_im, log_dt, b_re, b_im, c_re, c_im, w_in):
    s = S5_TAPS
    ldt = jnp.broadcast_to(log_dt[:, None], (S5_GROUPS, S5_STATE))
    ag = jnp.stack([a_re, a_im, ldt])
    a3 = ag.reshape(3, S5_NSTATE)
    bt = jnp.stack([b_re, b_im]).transpose(0, 1, 3, 2).reshape(2, S5_WIDTH, S5_STATE)
    ct = jnp.stack([c_re, c_im]).reshape(2, S5_WIDTH, S5_STATE)
    tile = lambda *shape: pl.BlockSpec((None,) + shape, lambda j: (j,) + (0,) * len(shape))
    return pl.pallas_call(
        _s5_prep_kernel,
        grid=(LANE_TILES,),
        in_specs=[pl.BlockSpec((3, TILE_STATES), lambda j: (0, j)),
                  pl.BlockSpec((3, TILE_GROUPS, S5_STATE), lambda j: (0, j, 0)),
                  pl.BlockSpec((2, LANES, S5_STATE), lambda j: (0, j, 0)),
                  pl.BlockSpec((2, LANES, S5_STATE), lambda j: (0, j, 0)),
                  pl.BlockSpec((D_MODEL // LANE_TILES, IN_COLS), lambda j: (j, 0))],
        out_specs=(tile(s * LANES, 2 * TILE_STATES),
                   tile(2 * TILE_STATES + s * LANES, s * LANES),
                   tile(5, 2, SUBLANES, TILE_STATES),
                   pl.BlockSpec((D_MODEL // LANE_TILES, IN_COLS), lambda j: (j, 0))),
        out_shape=(jax.ShapeDtypeStruct((LANE_TILES, s * LANES, 2 * TILE_STATES), BF16),
                   jax.ShapeDtypeStruct((LANE_TILES, 2 * TILE_STATES + s * LANES, s * LANES), BF16),
                   jax.ShapeDtypeStruct((LANE_TILES, 5, 2, SUBLANES, TILE_STATES), F32),
                   jax.ShapeDtypeStruct((D_MODEL, IN_COLS), BF16)),
        compiler_params=pltpu.CompilerParams(
            dimension_semantics=("arbitrary",), vmem_limit_bytes=VMEM_LIMIT),
        name="s5_prep",
    )(a3, ag, bt, ct, w_in)


def _mem_kv_kernel(mem_ref, g_ref, wk_ref, wv_ref, k_ref, v_ref):
    m = _rms(mem_ref[...], g_ref[...]).astype(BF16)
    k_ref[...] = _dot(m, wk_ref[...]).astype(BF16)
    v_ref[...] = _dot(m, wv_ref[...]).astype(BF16)


def _mem_kv(mem, g, wk, wv):
    B, M, D = mem.shape
    return pl.pallas_call(
        _mem_kv_kernel,
        grid=(B,),
        in_specs=[pl.BlockSpec((None, M, D), lambda b: (b, 0, 0)),
                  pl.BlockSpec((1, D), lambda b: (0, 0)),
                  pl.BlockSpec((D, D), lambda b: (0, 0)),
                  pl.BlockSpec((D, D), lambda b: (0, 0))],
        out_specs=(pl.BlockSpec((None, M, D), lambda b: (b, 0, 0)),
                   pl.BlockSpec((None, M, D), lambda b: (b, 0, 0))),
        out_shape=(jax.ShapeDtypeStruct((B, M, D), BF16),
                   jax.ShapeDtypeStruct((B, M, D), BF16)),
        compiler_params=pltpu.CompilerParams(
            dimension_semantics=("arbitrary",), vmem_limit_bytes=VMEM_LIMIT),
        name="mem_kv",
    )(mem, g, wk, wv)


def _token_prologue(x_ref, pos_ref, inv_ref, g_ref, h_ref, cos_ref, sin_ref):
    h_ref[...] = _rms(x_ref[...], g_ref[...]).astype(BF16)
    lane = lax.broadcasted_iota(jnp.int32, (1, LANES), 1)
    nfreq = RET_DK // 2
    group = lane // nfreq
    nblock = LANES // nfreq
    rows = x_ref.shape[0] // nblock
    pos = pos_ref[...].astype(F32)
    pos4 = jnp.zeros((rows, LANES), F32)
    for k in range(nblock):
        pos4 = jnp.where(group == k, pos[k * rows:(k + 1) * rows, :], pos4)
    ang4 = pos4 * inv_ref[...]

    def spread(t4):
        blocks = []
        for k in range(nblock):
            m = jnp.where(group == k, t4, 0.0)
            m = m + pltpu.roll(m, nfreq, 1)
            blocks.append(m + pltpu.roll(m, 2 * nfreq, 1))
        return jnp.concatenate(blocks, axis=0)

    first_half = (lane % RET_DK) < nfreq
    sin = spread(jnp.sin(ang4))
    cos_ref[...] = spread(jnp.cos(ang4))
    sin_ref[...] = jnp.where(first_half, -sin, sin)


def _proj_ret_kernel(x0_ref, pos0_ref, x_ref, pos_ref, inv_ref, g_ref, w_ref, gn_ref, *rest,
                     tiles_per_seq, num_tiles, num_casts):
    f32_refs, rest = rest[:num_casts], rest[num_casts:]
    (u_ref, gs_ref, ret_ref), rest = rest[:3], rest[3:]
    bf16_refs, rest = rest[:num_casts], rest[num_casts:]
    (cur_ref, prev_ref, hn_ref, cn_ref, sn_ref, h_ref, cos_ref, sin_ref,
     state_ref, decay_ref, qw_ref, kw_ref) = rest
    t = pl.program_id(0)

    @pl.when(t == 0)
    def _init():
        _retention_constants(decay_ref, qw_ref, kw_ref)
        cur_ref[...] = jnp.zeros_like(cur_ref)
        _token_prologue(x0_ref, pos0_ref, inv_ref, g_ref, hn_ref, cn_ref, sn_ref)

    @pl.when((t == 0) | (t % tiles_per_seq == 1))
    def _reset():
        state_ref[...] = jnp.zeros_like(state_ref)

    ret_items = _retention_items(prev_ref, gn_ref, ret_ref, state_ref, decay_ref, qw_ref, kw_ref)

    @pl.when(t == num_tiles)
    def _drain():
        prev_ref[...] = cur_ref[...]
        for r in ret_items:
            r()

    @pl.when(t < num_tiles)
    def _step():
        prev_ref[...] = cur_ref[...]
        h_ref[...] = hn_ref[...]
        cos_ref[...] = cn_ref[...]
        sin_ref[...] = sn_ref[...]
        h = h_ref[...]
        cos = cos_ref[...]
        sin_signed = sin_ref[...]
        lane = lax.broadcasted_iota(jnp.int32, (1, LANES), 1)
        first_half = (lane % RET_DK) < (RET_DK // 2)

        def rope(p):
            partner = jnp.where(first_half,
                                pltpu.roll(p, LANES - RET_DK // 2, 1),
                                pltpu.roll(p, RET_DK // 2, 1))
            return p * cos + partner * sin_signed

        def proj_q():
            pq = _dot(h, w_ref[:, COL_Q:COL_K])
            for i in range(RET_QK_WIDTH // LANES):
                sl = slice(i * LANES, (i + 1) * LANES)
                cur_ref[:, COL_Q + i * LANES:COL_Q + (i + 1) * LANES] = rope(pq[:, sl]).astype(BF16)

        def proj_k():
            pk = _dot(h, w_ref[:, COL_K:COL_V])
            for i in range(RET_QK_WIDTH // LANES):
                sl = slice(i * LANES, (i + 1) * LANES)
                cur_ref[:, COL_K + i * LANES:COL_K + (i + 1) * LANES] = (
                    rope(pk[:, sl]) * (RET_DK ** -0.5)).astype(BF16)

        def proj_v():
            cur_ref[:, COL_V:COL_GR] = _dot(h, w_ref[:, COL_V:COL_GR]).astype(BF16)

        def proj_gate():
            cur_ref[:, COL_GR:COL_U] = _silu(_dot(h, w_ref[:, COL_GR:COL_U])).astype(BF16)

        def proj_u():
            pu = _dot(h, w_ref[:, COL_U:COL_GS]).astype(BF16)
            for j in range(LANE_TILES):
                u_ref[j] = pu[:, j * LANES:(j + 1) * LANES]

        def proj_gs():
            gs_ref[...] = _silu(_dot(h, w_ref[:, COL_GS:IN_COLS])).astype(BF16)

        proj_items = [proj_q, proj_k, proj_v, proj_gate, proj_u, proj_gs]
        per = -(-len(ret_items) // len(proj_items))
        for i, item in enumerate(proj_items):
            for r in ret_items[i * per:(i + 1) * per]:
                r()
            item()
        _token_prologue(x_ref, pos_ref, inv_ref, g_ref, hn_ref, cn_ref, sn_ref)
        for src_ref, dst_ref in zip(f32_refs, bf16_refs):
            dst_ref[...] = src_ref[...].astype(BF16)


def _proj_ret(x, pos3, inv, g, w, gn, later):
    B, L, D = x.shape
    tm = TM_PROJ
    nt = L // tm
    n = B * nt
    cur = lambda t: jnp.minimum(t, n - 1)
    nxt = lambda t: jnp.minimum(t + 1, n - 1)
    lag = lambda t: jnp.maximum(t - 1, 0)
    first = lambda t: 0 * t
    row = lambda width, tile: pl.BlockSpec((None, tm, width), lambda t: (tile(t) // nt, tile(t) % nt, 0))
    const = lambda shape: pl.BlockSpec(shape, lambda t: (0,) * len(shape))
    slab = lambda m: pl.BlockSpec((m.shape[0] // n, m.shape[1]), lambda t: (cur(t), 0))
    outs = pl.pallas_call(
        functools.partial(_proj_ret_kernel, tiles_per_seq=nt, num_tiles=n, num_casts=len(later)),
        grid=(n + 1,),
        in_specs=[row(D, first), row(1, first), row(D, nxt), row(1, nxt), const((1, LANES)), const((1, D)),
                  const((D, IN_COLS)), const((1, RET_V_WIDTH))] + [slab(m) for m in later],
        out_specs=[pl.BlockSpec((None, LANE_TILES, tm, LANES), lambda t: (cur(t) // nt, 0, cur(t) % nt, 0)),
                   row(S5_WIDTH, cur), row(RET_V_WIDTH, lag)] + [slab(m) for m in later],
        out_shape=[jax.ShapeDtypeStruct((B, LANE_TILES, L, LANES), BF16),
                   jax.ShapeDtypeStruct((B, L, S5_WIDTH), BF16),
                   jax.ShapeDtypeStruct((B, L, RET_V_WIDTH), BF16)]
        + [jax.ShapeDtypeStruct(m.shape, BF16) for m in later],
        scratch_shapes=[pltpu.VMEM((tm, COL_U), BF16),
                        pltpu.VMEM((tm, COL_U), BF16),
                        pltpu.VMEM((tm, D), BF16), pltpu.VMEM((tm, LANES), F32), pltpu.VMEM((tm, LANES), F32),
                        pltpu.VMEM((tm, D), BF16), pltpu.VMEM((tm, LANES), F32), pltpu.VMEM((tm, LANES), F32),
                        pltpu.VMEM((RET_HEADS // 2, LANES, 2 * RET_DV), F32),
                        pltpu.VMEM((RET_HEADS // 2, CHUNK, 2 * CHUNK), F32),
                        pltpu.VMEM((RET_HEADS // 2, CHUNK, 2 * RET_DV), F32),
                        pltpu.VMEM((RET_HEADS // 2, CHUNK, LANES), F32)],
        compiler_params=pltpu.CompilerParams(
            dimension_semantics=("arbitrary",), vmem_limit_bytes=VMEM_LIMIT),
        name="proj_ret",
    )(x, pos3, x, pos3, inv, g, w, gn, *later)
    return outs[:3], outs[3:]


def _retention_constants(decay_ref, qw_ref, kw_ref):
    i = lax.broadcasted_iota(jnp.int32, (CHUNK, 2 * CHUNK), 0).astype(F32)
    c = lax.broadcasted_iota(jnp.int32, (CHUNK, 2 * CHUNK), 1)
    j = (c % CHUNK).astype(F32)
    diff = i - j
    lane = lax.broadcasted_iota(jnp.int32, (CHUNK, LANES), 1)
    ik = lax.broadcasted_iota(jnp.int32, (CHUNK, LANES), 0).astype(F32)
    for p in range(RET_HEADS // 2):
        lg = jnp.where(c < CHUNK, LOG_G[2 * p], LOG_G[2 * p + 1])
        decay_ref[p] = jnp.where(diff >= 0.0, jnp.exp(lg * jnp.maximum(diff, 0.0)), 0.0)
        qw_ref[p] = jnp.exp(lg * (i + 1.0))
        lgk = jnp.where(lane < RET_DK, LOG_G[2 * p], LOG_G[2 * p + 1])
        kw_ref[p] = jnp.exp(lgk * (CHUNK - 1.0 - ik))


def _retention_items(src_ref, gn_ref, o_ref, state_ref, decay_ref, qw_ref, kw_ref):
    pairs = RET_HEADS // 2
    low = lax.broadcasted_iota(jnp.int32, (1, LANES), 1) < RET_DK
    srow = lax.broadcasted_iota(jnp.int32, (LANES, 2 * RET_DV), 0)
    scol = lax.broadcasted_iota(jnp.int32, (LANES, 2 * RET_DV), 1)
    own_block = (srow < RET_DK) == (scol < RET_DV)
    scol1 = lax.broadcasted_iota(jnp.int32, (1, 2 * RET_DV), 1)
    zero_v = jnp.zeros((CHUNK, RET_DV), BF16)

    def pair_step(c, p):
        rows = slice(c * CHUNK, (c + 1) * CHUNK)
        qt = src_ref[rows, COL_Q + p * LANES:COL_Q + (p + 1) * LANES]
        kt = src_ref[rows, COL_K + p * LANES:COL_K + (p + 1) * LANES]
        vp = src_ref[rows, COL_V + 2 * p * RET_DV:COL_V + 2 * (p + 1) * RET_DV]
        state = state_ref[p]
        cross = _dot(qt, state.astype(BF16))
        kw = (kt.astype(F32) * kw_ref[p]).astype(BF16)
        upd = lax.dot_general(kw, vp, (((0,), (0,)), ((), ())), preferred_element_type=F32)
        chunk_decay = jnp.where(scol1 < RET_DV,
                                math.exp(LOG_G[2 * p] * CHUNK), math.exp(LOG_G[2 * p + 1] * CHUNK))
        state_ref[p] = state * chunk_decay + jnp.where(own_block, upd, 0.0)

        zero_k = jnp.zeros_like(kt)
        k_rows = jnp.concatenate([jnp.where(low, kt, zero_k), jnp.where(low, zero_k, kt)], axis=0)
        s = lax.dot_general(qt, k_rows, (((1,), (1,)), ((), ())), preferred_element_type=F32)
        pm = (s * decay_ref[p]).astype(BF16)
        v_diag = jnp.concatenate(
            [jnp.concatenate([vp[:, 0:RET_DV], zero_v], axis=1),
             jnp.concatenate([zero_v, vp[:, RET_DV:2 * RET_DV]], axis=1)], axis=0)
        o2 = _dot(pm, v_diag) + qw_ref[p] * cross
        for e in range(2):
            h = 2 * p + e
            cols = slice(h * RET_DV, (h + 1) * RET_DV)
            o = o2[:, e * RET_DV:(e + 1) * RET_DV]
            mu = jnp.mean(o, axis=-1, keepdims=True)
            oc = o - mu
            var = jnp.mean(oc * oc, axis=-1, keepdims=True)
            on = oc * lax.rsqrt(var + EPS) * gn_ref[:, cols]
            gate = src_ref[rows, COL_GR + h * RET_DV:COL_GR + (h + 1) * RET_DV].astype(F32)
            o_ref[rows, cols] = (on * gate).astype(BF16)

    return [functools.partial(pair_step, c, p)
            for c in range(src_ref.shape[0] // CHUNK) for p in range(pairs)]


def _s5_scan_kernel(u_ref, wb_ref, consts_ref, wck_ref, d_ref, y_ref, *scratch, tiles_per_seq, num_tiles):
    uf_refs, lhs_refs, st_refs = scratch[0:3], scratch[3:6], scratch[6:9]
    sb_ref, ys_ref, carry_ref = scratch[9:]
    n = pl.program_id(0)
    s = S5_TAPS
    tm = u_ref.shape[0]
    nblk = tm // s
    half = TILE_STATES

    @pl.when(n == 0)
    def _init():
        for ref in scratch[0:9]:
            ref[...] = jnp.zeros_like(ref)

    @pl.when((n == 0) | ((n - 1) % tiles_per_seq == 0))
    def _reset():
        carry_ref[...] = jnp.zeros_like(carry_ref)

    ncol = 2 * LANES

    def increments(slot):
        uf_ref, lhs_ref, st_ref = uf_refs[slot], lhs_refs[slot], st_refs[slot]

        def piece(c):
            if c == 0:
                uf_ref[...] = u_ref[...].astype(F32)
                lhs_ref[...] = jnp.concatenate(
                    [uf_ref[pl.ds(k, nblk, stride=s), :] for k in range(s)], axis=1).astype(BF16)
            z = _dot(lhs_ref[...], wb_ref[:, c * ncol:(c + 1) * ncol])
            for i in range(ncol // LANES):
                st_ref[c * (ncol // LANES) + i] = z[:, i * LANES:(i + 1) * LANES]

        return [functools.partial(piece, c) for c in range(2 * half // ncol)]

    ntile = half // LANES

    def scan(slot, row_pieces):
        st_ref = st_refs[slot]
        run = S5_RUN
        span = run * SUBLANES
        row = lax.broadcasted_iota(jnp.int32, (SUBLANES, LANES), 0)
        last = SUBLANES - 1
        groups = nblk // span
        per = -(-groups // row_pieces)
        tiles_per_piece = S5_SCAN_LANES // LANES

        def tile_group(t, g, carry):
            lanes = slice(t * LANES, (t + 1) * LANES)
            const = lambda idx: (consts_ref[idx, 0, :, lanes], consts_ref[idx, 1, :, lanes])
            cr, ci = carry
            rows = [pl.ds(g * span + j, SUBLANES, stride=run) for j in range(run)]
            loc = [(st_ref[t, rows[0], :], st_ref[ntile + t, rows[0], :])]
            for j in range(1, run):
                pr, pi = _cmul(*const(0), *loc[-1])
                loc.append((pr + st_ref[t, rows[j], :], pi + st_ref[ntile + t, rows[j], :]))
            fr, fi = loc[-1]
            for idx, d in enumerate((1, 2, 4)):
                pr, pi = _cmul(*const(1 + idx), pltpu.roll(fr, d, 0), pltpu.roll(fi, d, 0))
                fr, fi = fr + pr, fi + pi
            pr, pi = _cmul(*const(4), cr, ci)
            fr, fi = fr + pr, fi + pi
            er = jnp.where(row == 0, cr, pltpu.roll(fr, 1, 0))
            ei = jnp.where(row == 0, ci, pltpu.roll(fi, 1, 0))
            st_ref[t, rows[0], :] = er
            st_ref[ntile + t, rows[0], :] = ei
            for j in range(1, run):
                er, ei = _cmul(*const(0), er, ei)
                st_ref[t, rows[j], :] = loc[j - 1][0] + er
                st_ref[ntile + t, rows[j], :] = loc[j - 1][1] + ei
            return (jnp.broadcast_to(fr[last:, :], (SUBLANES, LANES)),
                    jnp.broadcast_to(fi[last:, :], (SUBLANES, LANES)))

        def piece(q, i, carries):
            tiles = range(q * tiles_per_piece, (q + 1) * tiles_per_piece)
            if i == 0:
                for t in tiles:
                    carries[t] = (carry_ref[:, t * LANES:(t + 1) * LANES],
                                  carry_ref[:, half + t * LANES:half + (t + 1) * LANES])
            for g in range(i * per, min((i + 1) * per, groups)):
                for t in tiles:
                    carries[t] = tile_group(t, g, carries[t])
            if i == row_pieces - 1:
                for t in tiles:
                    carry_ref[:, t * LANES:(t + 1) * LANES] = carries[t][0]
                    carry_ref[:, half + t * LANES:half + (t + 1) * LANES] = carries[t][1]

        carries = {}
        return [functools.partial(piece, q, i, carries)
                for q in range(ntile // tiles_per_piece) for i in range(row_pieces)]

    def outputs(slot):
        taps_per_piece = ncol // LANES

        def cast():
            for t in range(2 * ntile):
                sb_ref[:, t * LANES:(t + 1) * LANES] = st_refs[slot][t].astype(BF16)

        def piece(c):
            lhs = jnp.concatenate([sb_ref[...], lhs_refs[slot][...]], axis=1)
            yall = _dot(lhs, wck_ref[:, c * ncol:(c + 1) * ncol])
            for i in range(taps_per_piece):
                tap = c * taps_per_piece + i
                ys_ref[pl.ds(tap, nblk, stride=s), :] = yall[:, i * LANES:(i + 1) * LANES]

        def finish():
            y = ys_ref[...] + d_ref[...] * uf_refs[slot][...]
            y_ref[...] = jax.nn.gelu(y).astype(BF16)

        return [cast] + [functools.partial(piece, c) for c in range(s * LANES // ncol)] + [finish]

    def steps(phase, stages):
        mxu = (increments(phase) if stages > 2 else []) + outputs((phase + 1) % 3)
        vpu = scan((phase + 2) % 3, 3) if stages > 1 else []
        for i in range(max(len(mxu), len(vpu))):
            for piece in mxu[i:i + 1] + vpu[i:i + 1]:
                piece()

    for phase in range(3):
        pl.when((n % 3 == phase) & (n < num_tiles))(functools.partial(steps, phase, 3))
    pl.when(n == num_tiles)(functools.partial(steps, num_tiles % 3, 2))
    pl.when(n == num_tiles + 1)(functools.partial(steps, (num_tiles + 1) % 3, 1))


def _s5_scan(u, wb, wck, consts, d):
    B, _, L, _ = u.shape
    s = S5_TAPS
    tm = TM_S5
    nt = L // tm
    ntiles = LANE_TILES * B * nt
    tile_of = lambda n, lag: jnp.clip(n - lag, 0, ntiles - 1)
    lane_tile = lambda n, lag: tile_of(n, lag) // (B * nt)

    def tok(lag):
        def index(n):
            i = tile_of(n, lag)
            return ((i // nt) % B, i // (B * nt), i % nt, 0)
        return pl.BlockSpec((None, None, tm, LANES), index)

    def par(lag, *shape):
        return pl.BlockSpec((None,) + shape, lambda n: (lane_tile(n, lag),) + (0,) * len(shape))

    nblk = tm // s
    return pl.pallas_call(
        functools.partial(_s5_scan_kernel, tiles_per_seq=nt, num_tiles=ntiles),
        grid=(ntiles + 2,),
        in_specs=[tok(0), par(0, s * LANES, 2 * TILE_STATES), par(1, 5, 2, SUBLANES, TILE_STATES),
                  par(2, 2 * TILE_STATES + s * LANES, s * LANES), par(2, 1, LANES)],
        out_specs=tok(2),
        out_shape=jax.ShapeDtypeStruct((B, LANE_TILES, L, LANES), BF16),
        scratch_shapes=([pltpu.VMEM((tm, LANES), F32)] * 3
                        + [pltpu.VMEM((nblk, s * LANES), BF16)] * 3
                        + [pltpu.VMEM((2 * TILE_STATES // LANES, nblk, LANES), F32)] * 3
                        + [pltpu.VMEM((nblk, 2 * TILE_STATES), BF16), pltpu.VMEM((tm, LANES), F32),
                           pltpu.VMEM((SUBLANES, 2 * TILE_STATES), F32)]),
        compiler_params=pltpu.CompilerParams(
            dimension_semantics=("arbitrary",), vmem_limit_bytes=VMEM_LIMIT),
        name="s5_scan",
    )(u, wb, consts, wck, d)


def _out_xattn_kernel(x_ref, ret_ref, y_ref, gs_ref, gw_ref, gb_ref, wout_ref, g2_ref, wq_ref,
                      ka_ref, va_ref, wo_ref, gf_ref, o_ref):
    y = jnp.concatenate([y_ref[j] for j in range(LANE_TILES)], axis=1)
    z = _dot(y, gw_ref[...]) + gb_ref[...]
    ssm = (y.astype(F32) * _sigmoid(z) * gs_ref[...].astype(F32)).astype(BF16)
    x1 = (x_ref[...] + _dot(ret_ref[...], wout_ref[0:RET_V_WIDTH, :])
          + _dot(ssm, wout_ref[RET_V_WIDTH:RET_V_WIDTH + S5_WIDTH, :]))
    h2 = _rms(x1, g2_ref[...]).astype(BF16)
    qa = (_dot(h2, wq_ref[...]) * (XA_DH ** -0.5)).astype(BF16)
    heads = []
    for h in range(XA_HEADS):
        cols = slice(h * XA_DH, (h + 1) * XA_DH)
        s = lax.dot_general(qa[:, cols], ka_ref[:, cols], (((1,), (1,)), ((), ())),
                            preferred_element_type=F32)
        e = jnp.exp(s - jnp.max(s, axis=-1, keepdims=True))
        l = jnp.sum(e, axis=-1, keepdims=True)
        heads.append((_dot(e.astype(BF16), va_ref[:, cols]) * (1.0 / l)).astype(BF16))
    o = jnp.concatenate(heads, axis=-1)
    x2 = x1 + _dot(o, wo_ref[...])
    o_ref[...] = _rms(x2, gf_ref[...])


def _out_xattn(x, ret, y, gs, gw, gb, wout, g2, wq, ka, va, wo, gf):
    B, L, D = x.shape
    tm = TM_OUT
    row = pl.BlockSpec((None, tm, D), lambda b, t: (b, t, 0))
    const = lambda shape: pl.BlockSpec(shape, lambda b, t: (0,) * len(shape))
    mem = pl.BlockSpec((None, MEM_LEN, D), lambda b, t: (b, 0, 0))
    ytile = pl.BlockSpec((None, LANE_TILES, tm, LANES), lambda b, t: (b, 0, t, 0))
    return pl.pallas_call(
        _out_xattn_kernel,
        grid=(B, L // tm),
        in_specs=[row, row, ytile, row, const((D, D)), const((1, D)), const(wout.shape), const((1, D)),
                  const((D, D)), mem, mem, const((D, D)), const((1, D))],
        out_specs=row,
        out_shape=jax.ShapeDtypeStruct((B, L, D), F32),
        compiler_params=pltpu.CompilerParams(
            dimension_semantics=("arbitrary", "arbitrary"), vmem_limit_bytes=VMEM_LIMIT),
        name="out_xattn",
    )(x, ret, y, gs, gw, gb, wout, g2, wq, ka, va, wo, gf)


def _rope_inv_lanes():
    half = RET_DK // 2
    inv = ROPE_BASE ** (-np.arange(half, dtype=np.float64) / half)
    return jnp.asarray(np.tile(inv, LANES // half).reshape(1, LANES), F32)


def kernel(x, mem, positions, norm1_g, w_in, ret_gn_g, s5_a_re, s5_a_im, s5_log_dt, s5_b_re, s5_b_im, s5_c_re, s5_c_im, s5_d, s5_glu_w, s5_glu_b, w_out, norm2_g, norm_mem_g, xa_wq, xa_wk, xa_wv, xa_wo, norm_f_g):
    B, L, D = x.shape
    l = 0
    rowvec = lambda v: v.reshape(1, -1)

    wb, wck, consts, win = _s5_prep(s5_a_re[l], s5_a_im[l], s5_log_dt[l], s5_b_re[l], s5_b_im[l],
                                    s5_c_re[l], s5_c_im[l], w_in[l])
    (u, g_s5, ret), (glu_w, wout, wq, wk, wv, wo) = _proj_ret(
        x, positions.reshape(B, L, 1), _rope_inv_lanes(), rowvec(norm1_g[l]), win, rowvec(ret_gn_g[l]),
        [s5_glu_w[l], w_out[l], xa_wq[l], xa_wk[l], xa_wv[l], xa_wo[l]])
    ka, va = _mem_kv(mem, rowvec(norm_mem_g[l]), wk, wv)
    y = _s5_scan(u, wb, wck, consts, s5_d[l].reshape(LANE_TILES, 1, LANES))
    return _out_xattn(x, ret, y, g_s5, glu_w, rowvec(s5_glu_b[l]), wout,
                      rowvec(norm2_g[l]), wq, ka, va, wo, rowvec(norm_f_g))
```

```python
import functools
import math

import numpy as np
import jax
import jax.numpy as jnp
from jax import lax
from jax.experimental import pallas as pl
from jax.experimental.pallas import tpu as pltpu

F32 = jnp.float32
BF16 = jnp.bfloat16

D_MODEL = 1024
MEM_LEN = 256
EPS = 1e-6
ROPE_BASE = 10000.0

RET_HEADS = 8
RET_QK_WIDTH = 512
RET_V_WIDTH = 1024
RET_DK = 64
RET_DV = 128
CHUNK = 128

S5_WIDTH = 1024
S5_GROUP = 16
S5_GROUPS = 64
S5_STATE = 64
S5_NSTATE = S5_GROUPS * S5_STATE

IN_COLS = 5120
COL_Q, COL_K, COL_V, COL_GR, COL_U, COL_GS = 0, 512, 1024, 2048, 3072, 4096

XA_HEADS = 4
XA_DH = 256

LANES = 128
SUBLANES = 8
VMEM_LIMIT = 48 * 1024 * 1024

LANE_TILES = S5_WIDTH // LANES
TILE_GROUPS = LANES // S5_GROUP
TILE_STATES = S5_NSTATE // LANE_TILES
S5_TAPS = 4
S5_SCAN_LANES = 2 * LANES
S5_RUN = 4

LOG_G = tuple(math.log1p(-(2.0 ** (-5.0 - h))) for h in range(RET_HEADS))

TM_PROJ = 512
TM_S5 = 4096
TM_OUT = 512


def _rms(x, g):
    ms = jnp.mean(x * x, axis=-1, keepdims=True)
    return x * lax.rsqrt(ms + EPS) * g


def _sigmoid(z):
    return 1.0 / (1.0 + jnp.exp(-z))


def _silu(g):
    return g * _sigmoid(g)


def _dot(a, b):
    return jnp.dot(a, b, preferred_element_type=F32)


def _cmul(ar, ai, br, bi):
    return ar * br - ai * bi, ar * bi + ai * br


def _s5_discretise(ar, ai, ldt):
    dt = jnp.exp(ldt)
    mag = jnp.exp(ar * dt)
    p_re = mag * jnp.cos(ai * dt)
    p_im = mag * jnp.sin(ai * dt)
    den = ar * ar + ai * ai
    nr, ni = p_re - 1.0, p_im
    f_re = (nr * ar + ni * ai) / den
    f_im = (ni * ar - nr * ai) / den
    return p_re, p_im, f_re, f_im


def _powers(p_re, p_im, n):
    pw = [(jnp.ones_like(p_re), jnp.zeros_like(p_im))]
    for _ in range(n):
        pw.append(_cmul(pw[-1][0], pw[-1][1], p_re, p_im))
    return pw


def _block_diag(x):
    tiled = jnp.concatenate([x] * TILE_GROUPS, axis=1)
    r = lax.broadcasted_iota(jnp.int32, tiled.shape, 0)
    c = lax.broadcasted_iota(jnp.int32, tiled.shape, 1)
    return jnp.where(r // S5_GROUP == c // S5_STATE, tiled, 0.0)


def _s5_prep_kernel(a_ref, ag_ref, b_ref, c_ref, d_ref, win_ref, wb_ref, wck_ref, consts_ref, win16_ref):
    s = S5_TAPS
    half = TILE_STATES
    def per_row(v):
        return jnp.broadcast_to(v[:, None, :], (TILE_GROUPS, S5_GROUP, S5_STATE)).reshape(LANES, S5_STATE)

    g_re, g_im, gf_re, gf_im = _s5_discretise(ag_ref[0], ag_ref[1], ag_ref[2])
    gw = [(per_row(r), per_row(i)) for r, i in _powers(g_re, g_im, s)]
    bbr, bbi = _cmul(per_row(gf_re), per_row(gf_im), b_ref[0], b_ref[1])
    for k in range(s):
        wr, wi = _cmul(bbr, bbi, *gw[s - 1 - k])
        wb_ref[k * LANES:(k + 1) * LANES, 0:half] = _block_diag(wr).astype(BF16)
        wb_ref[k * LANES:(k + 1) * LANES, half:2 * half] = _block_diag(wi).astype(BF16)

    bb = jnp.concatenate([bbr, bbi], axis=1)
    r = lax.broadcasted_iota(jnp.int32, (LANES, LANES), 0)
    c = lax.broadcasted_iota(jnp.int32, (LANES, LANES), 1)
    same_group = r // S5_GROUP == c // S5_GROUP
    taps = []
    for d in range(s + 1):
        er, ei = _cmul(c_ref[0], c_ref[1], *gw[d])
        if d >= 1:
            cwt = jnp.concatenate([_block_diag(er), -_block_diag(ei)], axis=1)
            wck_ref[0:2 * half, (d - 1) * LANES:d * LANES] = cwt.T.astype(BF16)
        if d < s:
            t = lax.dot_general(bb, jnp.concatenate([er, -ei], axis=1), (((1,), (1,)), ((), ())),
                                preferred_element_type=F32, precision=lax.Precision.HIGHEST)
            if d == 0:
                t = t + jnp.where(r == c, d_ref[...], 0.0)
            taps.append(jnp.where(same_group, t, 0.0))
    zero = jnp.zeros((LANES, LANES), BF16)
    for m in range(s):
        for i in range(s):
            blk = taps[i - m].astype(BF16) if m <= i else zero
            wck_ref[2 * half + m * LANES:2 * half + (m + 1) * LANES, i * LANES:(i + 1) * LANES] = blk

    p_re, p_im, _, _ = _s5_discretise(a_ref[0:1, :], a_ref[1:2, :], a_ref[2:3, :])
    pw = _powers(p_re, p_im, s)
    shape = (SUBLANES, half)
    qr = _powers(pw[s][0], pw[s][1], S5_RUN)
    consts_ref[0, 0] = jnp.broadcast_to(qr[1][0], shape)
    consts_ref[0, 1] = jnp.broadcast_to(qr[1][1], shape)
    rp = _powers(qr[S5_RUN][0], qr[S5_RUN][1], SUBLANES)
    row = lax.broadcasted_iota(jnp.int32, shape, 0)
    for idx, d in enumerate((1, 2, 4)):
        consts_ref[1 + idx, 0] = jnp.where(row >= d, rp[d][0], 0.0)
        consts_ref[1 + idx, 1] = jnp.where(row >= d, rp[d][1], 0.0)
    cr = jnp.zeros(shape, F32)
    ci = jnp.zeros(shape, F32)
    for k in range(SUBLANES):
        cr = jnp.where(row == k, rp[k + 1][0], cr)
        ci = jnp.where(row == k, rp[k + 1][1], ci)
    consts_ref[4, 0] = cr
    consts_ref[4, 1] = ci

    win16_ref[...] = win_ref[...].astype(BF16)


def _s5_prep(a_re, a_im, log_dt, b_re, b_im, c_re, c_im, d, w_in):
    s = S5_TAPS
    ldt = jnp.broadcast_to(log_dt[:, None], (S5_GROUPS, S5_STATE))
    ag = jnp.stack([a_re, a_im, ldt])
    a3 = ag.reshape(3, S5_NSTATE)
    bt = jnp.stack([b_re, b_im]).transpose(0, 1, 3, 2).reshape(2, S5_WIDTH, S5_STATE)
    ct = jnp.stack([c_re, c_im]).reshape(2, S5_WIDTH, S5_STATE)
    tile = lambda *shape: pl.BlockSpec((None,) + shape, lambda j: (j,) + (0,) * len(shape))
    return pl.pallas_call(
        _s5_prep_kernel,
        grid=(LANE_TILES,),
        in_specs=[pl.BlockSpec((3, TILE_STATES), lambda j: (0, j)),
                  pl.BlockSpec((3, TILE_GROUPS, S5_STATE), lambda j: (0, j, 0)),
                  pl.BlockSpec((2, LANES, S5_STATE), lambda j: (0, j, 0)),
                  pl.BlockSpec((2, LANES, S5_STATE), lambda j: (0, j, 0)),
                  pl.BlockSpec((1, LANES), lambda j: (0, j)),
                  pl.BlockSpec((D_MODEL // LANE_TILES, IN_COLS), lambda j: (j, 0))],
        out_specs=(tile(s * LANES, 2 * TILE_STATES),
                   tile(2 * TILE_STATES + s * LANES, s * LANES),
                   tile(5, 2, SUBLANES, TILE_STATES),
                   pl.BlockSpec((D_MODEL // LANE_TILES, IN_COLS), lambda j: (j, 0))),
        out_shape=(jax.ShapeDtypeStruct((LANE_TILES, s * LANES, 2 * TILE_STATES), BF16),
                   jax.ShapeDtypeStruct((LANE_TILES, 2 * TILE_STATES + s * LANES, s * LANES), BF16),
                   jax.ShapeDtypeStruct((LANE_TILES, 5, 2, SUBLANES, TILE_STATES), F32),
                   jax.ShapeDtypeStruct((D_MODEL, IN_COLS), BF16)),
        compiler_params=pltpu.CompilerParams(
            dimension_semantics=("arbitrary",), vmem_limit_bytes=VMEM_LIMIT),
        name="s5_prep",
    )(a3, ag, bt, ct, d.reshape(1, S5_WIDTH), w_in)


def _mem_kv_kernel(mem_ref, g_ref, wk_ref, wv_ref, k_ref, v_ref):
    m = _rms(mem_ref[...], g_ref[...]).astype(BF16)
    k_ref[...] = _dot(m, wk_ref[...]).astype(BF16)
    v_ref[...] = _dot(m, wv_ref[...]).astype(BF16)


def _mem_kv(mem, g, wk, wv):
    B, M, D = mem.shape
    return pl.pallas_call(
        _mem_kv_kernel,
        grid=(B,),
        in_specs=[pl.BlockSpec((None, M, D), lambda b: (b, 0, 0)),
                  pl.BlockSpec((1, D), lambda b: (0, 0)),
                  pl.BlockSpec((D, D), lambda b: (0, 0)),
                  pl.BlockSpec((D, D), lambda b: (0, 0))],
        out_specs=(pl.BlockSpec((None, M, D), lambda b: (b, 0, 0)),
                   pl.BlockSpec((None, M, D), lambda b: (b, 0, 0))),
        out_shape=(jax.ShapeDtypeStruct((B, M, D), BF16),
                   jax.ShapeDtypeStruct((B, M, D), BF16)),
        compiler_params=pltpu.CompilerParams(
            dimension_semantics=("arbitrary",), vmem_limit_bytes=VMEM_LIMIT),
        name="mem_kv",
    )(mem, g, wk, wv)


def _token_prologue(x_ref, pos_ref, inv_ref, g_ref, h_ref, cos_ref, sin_ref):
    h_ref[...] = _rms(x_ref[...], g_ref[...]).astype(BF16)
    lane = lax.broadcasted_iota(jnp.int32, (1, LANES), 1)
    nfreq = RET_DK // 2
    group = lane // nfreq
    nblock = LANES // nfreq
    rows = x_ref.shape[0] // nblock
    pos = pos_ref[...].astype(F32)
    pos4 = jnp.zeros((rows, LANES), F32)
    for k in range(nblock):
        pos4 = jnp.where(group == k, pos[k * rows:(k + 1) * rows, :], pos4)
    ang4 = pos4 * inv_ref[...]

    def spread(t4):
        blocks = []
        for k in range(nblock):
            m = jnp.where(group == k, t4, 0.0)
            m = m + pltpu.roll(m, nfreq, 1)
            blocks.append(m + pltpu.roll(m, 2 * nfreq, 1))
        return jnp.concatenate(blocks, axis=0)

    first_half = (lane % RET_DK) < nfreq
    sin = spread(jnp.sin(ang4))
    cos_ref[...] = spread(jnp.cos(ang4))
    sin_ref[...] = jnp.where(first_half, -sin, sin)


def _proj_ret_kernel(x0_ref, pos0_ref, x_ref, pos_ref, inv_ref, g_ref, w_ref, gn_ref, *rest,
                     tiles_per_seq, num_tiles, num_casts):
    f32_refs, rest = rest[:num_casts], rest[num_casts:]
    (u_ref, gs_ref, ret_ref), rest = rest[:3], rest[3:]
    bf16_refs, rest = rest[:num_casts], rest[num_casts:]
    (cur_ref, prev_ref, hn_ref, cn_ref, sn_ref, h_ref, cos_ref, sin_ref, us_ref,
     state_ref, decay_ref, qw_ref, kw_ref) = rest
    t = pl.program_id(0)

    @pl.when(t == 0)
    def _init():
        _retention_constants(decay_ref, qw_ref, kw_ref)
        cur_ref[...] = jnp.zeros_like(cur_ref)
        _token_prologue(x0_ref, pos0_ref, inv_ref, g_ref, hn_ref, cn_ref, sn_ref)

    @pl.when((t == 0) | (t % tiles_per_seq == 1))
    def _reset():
        state_ref[...] = jnp.zeros_like(state_ref)

    ret_items = _retention_items(prev_ref, gn_ref, ret_ref, state_ref, decay_ref, qw_ref, kw_ref)

    @pl.when(t == num_tiles)
    def _drain():
        prev_ref[...] = cur_ref[...]
        for r in ret_items:
            r()

    @pl.when(t < num_tiles)
    def _step():
        prev_ref[...] = cur_ref[...]
        h_ref[...] = hn_ref[...]
        cos_ref[...] = cn_ref[...]
        sin_ref[...] = sn_ref[...]
        h = h_ref[...]
        cos = cos_ref[...]
        sin_signed = sin_ref[...]
        lane = lax.broadcasted_iota(jnp.int32, (1, LANES), 1)
        first_half = (lane % RET_DK) < (RET_DK // 2)

        def rope(p):
            partner = jnp.where(first_half,
                                pltpu.roll(p, LANES - RET_DK // 2, 1),
                                pltpu.roll(p, RET_DK // 2, 1))
            return p * cos + partner * sin_signed

        def proj_q():
            pq = _dot(h, w_ref[:, COL_Q:COL_K])
            for i in range(RET_QK_WIDTH // LANES):
                sl = slice(i * LANES, (i + 1) * LANES)
                cur_ref[:, COL_Q + i * LANES:COL_Q + (i + 1) * LANES] = rope(pq[:, sl]).astype(BF16)

        def proj_k():
            pk = _dot(h, w_ref[:, COL_K:COL_V])
            for i in range(RET_QK_WIDTH // LANES):
                sl = slice(i * LANES, (i + 1) * LANES)
                cur_ref[:, COL_K + i * LANES:COL_K + (i + 1) * LANES] = (
                    rope(pk[:, sl]) * (RET_DK ** -0.5)).astype(BF16)

        def proj_v():
            cur_ref[:, COL_V:COL_GR] = _dot(h, w_ref[:, COL_V:COL_GR]).astype(BF16)

        def proj_gate():
            cur_ref[:, COL_GR:COL_U] = _silu(_dot(h, w_ref[:, COL_GR:COL_U])).astype(BF16)

        def proj_u():
            pu = _dot(h, w_ref[:, COL_U:COL_GS])
            for j in range(LANE_TILES):
                us_ref[j] = pu[:, j * LANES:(j + 1) * LANES]
            for j in range(LANE_TILES):
                for k in range(S5_TAPS):
                    u_ref[j, :, k * LANES:(k + 1) * LANES] = us_ref[
                        j, pl.ds(k, us_ref.shape[1] // S5_TAPS, stride=S5_TAPS), :].astype(BF16)

        def proj_gs():
            gs_ref[...] = _silu(_dot(h, w_ref[:, COL_GS:IN_COLS])).astype(BF16)

        proj_items = [proj_q, proj_k, proj_v, proj_gate, proj_u, proj_gs]
        per = -(-len(ret_items) // len(proj_items))
        for i, item in enumerate(proj_items):
            for r in ret_items[i * per:(i + 1) * per]:
                r()
            item()
        _token_prologue(x_ref, pos_ref, inv_ref, g_ref, hn_ref, cn_ref, sn_ref)
        for src_ref, dst_ref in zip(f32_refs, bf16_refs):
            dst_ref[...] = src_ref[...].astype(BF16)


def _proj_ret(x, pos3, inv, g, w, gn, later):
    B, L, D = x.shape
    tm = TM_PROJ
    nt = L // tm
    n = B * nt
    cur = lambda t: jnp.minimum(t, n - 1)
    nxt = lambda t: jnp.minimum(t + 1, n - 1)
    lag = lambda t: jnp.maximum(t - 1, 0)
    first = lambda t: 0 * t
    row = lambda width, tile: pl.BlockSpec((None, tm, width), lambda t: (tile(t) // nt, tile(t) % nt, 0))
    const = lambda shape: pl.BlockSpec(shape, lambda t: (0,) * len(shape))
    slab = lambda m: pl.BlockSpec((m.shape[0] // n, m.shape[1]), lambda t: (cur(t), 0))
    outs = pl.pallas_call(
        functools.partial(_proj_ret_kernel, tiles_per_seq=nt, num_tiles=n, num_casts=len(later)),
        grid=(n + 1,),
        in_specs=[row(D, first), row(1, first), row(D, nxt), row(1, nxt), const((1, LANES)), const((1, D)),
                  const((D, IN_COLS)), const((1, RET_V_WIDTH))] + [slab(m) for m in later],
        out_specs=[pl.BlockSpec((None, LANE_TILES, tm // S5_TAPS, S5_TAPS * LANES),
                                lambda t: (cur(t) // nt, 0, cur(t) % nt, 0)),
                   row(S5_WIDTH, cur), row(RET_V_WIDTH, lag)] + [slab(m) for m in later],
        out_shape=[jax.ShapeDtypeStruct((B, LANE_TILES, L // S5_TAPS, S5_TAPS * LANES), BF16),
                   jax.ShapeDtypeStruct((B, L, S5_WIDTH), BF16),
                   jax.ShapeDtypeStruct((B, L, RET_V_WIDTH), BF16)]
        + [jax.ShapeDtypeStruct(m.shape, BF16) for m in later],
        scratch_shapes=[pltpu.VMEM((tm, COL_U), BF16),
                        pltpu.VMEM((tm, COL_U), BF16),
                        pltpu.VMEM((tm, D), BF16), pltpu.VMEM((tm, LANES), F32), pltpu.VMEM((tm, LANES), F32),
                        pltpu.VMEM((tm, D), BF16), pltpu.VMEM((tm, LANES), F32), pltpu.VMEM((tm, LANES), F32),
                        pltpu.VMEM((LANE_TILES, tm, LANES), F32),
                        pltpu.VMEM((RET_HEADS // 2, LANES, 2 * RET_DV), F32),
                        pltpu.VMEM((RET_HEADS // 2, CHUNK, 2 * CHUNK), F32),
                        pltpu.VMEM((RET_HEADS // 2, CHUNK, 2 * RET_DV), F32),
                        pltpu.VMEM((RET_HEADS // 2, CHUNK, LANES), F32)],
        compiler_params=pltpu.CompilerParams(
            dimension_semantics=("arbitrary",), vmem_limit_bytes=VMEM_LIMIT),
        name="proj_ret",
    )(x, pos3, x, pos3, inv, g, w, gn, *later)
    return outs[:3], outs[3:]


def _retention_constants(decay_ref, qw_ref, kw_ref):
    i = lax.broadcasted_iota(jnp.int32, (CHUNK, 2 * CHUNK), 0).astype(F32)
    c = lax.broadcasted_iota(jnp.int32, (CHUNK, 2 * CHUNK), 1)
    j = (c % CHUNK).astype(F32)
    diff = i - j
    lane = lax.broadcasted_iota(jnp.int32, (CHUNK, LANES), 1)
    ik = lax.broadcasted_iota(jnp.int32, (CHUNK, LANES), 0).astype(F32)
    for p in range(RET_HEADS // 2):
        lg = jnp.where(c < CHUNK, LOG_G[2 * p], LOG_G[2 * p + 1])
        decay_ref[p] = jnp.where(diff >= 0.0, jnp.exp(lg * jnp.maximum(diff, 0.0)), 0.0)
        qw_ref[p] = jnp.exp(lg * (i + 1.0))
        lgk = jnp.where(lane < RET_DK, LOG_G[2 * p], LOG_G[2 * p + 1])
        kw_ref[p] = jnp.exp(lgk * (CHUNK - 1.0 - ik))


def _retention_items(src_ref, gn_ref, o_ref, state_ref, decay_ref, qw_ref, kw_ref):
    pairs = RET_HEADS // 2
    low = lax.broadcasted_iota(jnp.int32, (1, LANES), 1) < RET_DK
    srow = lax.broadcasted_iota(jnp.int32, (LANES, 2 * RET_DV), 0)
    scol = lax.broadcasted_iota(jnp.int32, (LANES, 2 * RET_DV), 1)
    own_block = (srow < RET_DK) == (scol < RET_DV)
    scol1 = lax.broadcasted_iota(jnp.int32, (1, 2 * RET_DV), 1)
    zero_v = jnp.zeros((CHUNK, RET_DV), BF16)

    def pair_step(c, p):
        rows = slice(c * CHUNK, (c + 1) * CHUNK)
        qt = src_ref[rows, COL_Q + p * LANES:COL_Q + (p + 1) * LANES]
        kt = src_ref[rows, COL_K + p * LANES:COL_K + (p + 1) * LANES]
        vp = src_ref[rows, COL_V + 2 * p * RET_DV:COL_V + 2 * (p + 1) * RET_DV]
        state = state_ref[p]
        cross = _dot(qt, state.astype(BF16))
        kw = (kt.astype(F32) * kw_ref[p]).astype(BF16)
        upd = lax.dot_general(kw, vp, (((0,), (0,)), ((), ())), preferred_element_type=F32)
        chunk_decay = jnp.where(scol1 < RET_DV,
                                math.exp(LOG_G[2 * p] * CHUNK), math.exp(LOG_G[2 * p + 1] * CHUNK))
        state_ref[p] = state * chunk_decay + jnp.where(own_block, upd, 0.0)

        zero_k = jnp.zeros_like(kt)
        k_rows = jnp.concatenate([jnp.where(low, kt, zero_k), jnp.where(low, zero_k, kt)], axis=0)
        s = lax.dot_general(qt, k_rows, (((1,), (1,)), ((), ())), preferred_element_type=F32)
        pm = (s * decay_ref[p]).astype(BF16)
        v_diag = jnp.concatenate(
            [jnp.concatenate([vp[:, 0:RET_DV], zero_v], axis=1),
             jnp.concatenate([zero_v, vp[:, RET_DV:2 * RET_DV]], axis=1)], axis=0)
        o2 = _dot(pm, v_diag) + qw_ref[p] * cross
        for e in range(2):
            h = 2 * p + e
            cols = slice(h * RET_DV, (h + 1) * RET_DV)
            o = o2[:, e * RET_DV:(e + 1) * RET_DV]
            mu = jnp.mean(o, axis=-1, keepdims=True)
            oc = o - mu
            var = jnp.mean(oc * oc, axis=-1, keepdims=True)
            on = oc * lax.rsqrt(var + EPS) * gn_ref[:, cols]
            gate = src_ref[rows, COL_GR + h * RET_DV:COL_GR + (h + 1) * RET_DV].astype(F32)
            o_ref[rows, cols] = (on * gate).astype(BF16)

    return [functools.partial(pair_step, c, p)
            for c in range(src_ref.shape[0] // CHUNK) for p in range(pairs)]


def _s5_scan_kernel(u_ref, wb_ref, consts_ref, wck_ref, y_ref, *scratch, tiles_per_seq):
    lhs_refs, st_refs = scratch[0:3], scratch[3:6]
    sb_ref, ys_ref, carry_ref = scratch[6:]
    n = pl.program_id(0)
    s = S5_TAPS
    nblk = u_ref.shape[0]
    half = TILE_STATES

    @pl.when(n == 0)
    def _init():
        for ref in scratch[0:6]:
            ref[...] = jnp.zeros_like(ref)

    @pl.when((n == 0) | ((n - 1) % tiles_per_seq == 0))
    def _reset():
        carry_ref[...] = jnp.zeros_like(carry_ref)

    ncol = 2 * LANES

    def increments(slot):
        lhs_ref, st_ref = lhs_refs[slot], st_refs[slot]

        def piece(c):
            if c == 0:
                lhs_ref[...] = u_ref[...]
            z = _dot(u_ref[...], wb_ref[:, c * ncol:(c + 1) * ncol])
            for i in range(ncol // LANES):
                st_ref[c * (ncol // LANES) + i] = z[:, i * LANES:(i + 1) * LANES]

        return [functools.partial(piece, c) for c in range(2 * half // ncol)]

    ntile = half // LANES

    def scan(slot, row_pieces):
        st_ref = st_refs[slot]
        run = S5_RUN
        span = run * SUBLANES
        row = lax.broadcasted_iota(jnp.int32, (SUBLANES, LANES), 0)
        last = SUBLANES - 1
        groups = nblk // span
        per = -(-groups // row_pieces)
        tiles_per_piece = S5_SCAN_LANES // LANES

        def tile_group(t, g, carry):
            lanes = slice(t * LANES, (t + 1) * LANES)
            const = lambda idx: (consts_ref[idx, 0, :, lanes], consts_ref[idx, 1, :, lanes])
            cr, ci = carry
            rows = [pl.ds(g * span + j, SUBLANES, stride=run) for j in range(run)]
            loc = [(st_ref[t, rows[0], :], st_ref[ntile + t, rows[0], :])]
            for j in range(1, run):
                pr, pi = _cmul(*const(0), *loc[-1])
                loc.append((pr + st_ref[t, rows[j], :], pi + st_ref[ntile + t, rows[j], :]))
            fr, fi = loc[-1]
            for idx, d in enumerate((1, 2, 4)):
                pr, pi = _cmul(*const(1 + idx), pltpu.roll(fr, d, 0), pltpu.roll(fi, d, 0))
                fr, fi = fr + pr, fi + pi
            pr, pi = _cmul(*const(4), cr, ci)
            fr, fi = fr + pr, fi + pi
            er = jnp.where(row == 0, cr, pltpu.roll(fr, 1, 0))
            ei = jnp.where(row == 0, ci, pltpu.roll(fi, 1, 0))
            st_ref[t, rows[0], :] = er
            st_ref[ntile + t, rows[0], :] = ei
            for j in range(1, run):
                er, ei = _cmul(*const(0), er, ei)
                st_ref[t, rows[j], :] = loc[j - 1][0] + er
                st_ref[ntile + t, rows[j], :] = loc[j - 1][1] + ei
            return (jnp.broadcast_to(fr[last:, :], (SUBLANES, LANES)),
                    jnp.broadcast_to(fi[last:, :], (SUBLANES, LANES)))

        def piece(q, i, carries):
            tiles = range(q * tiles_per_piece, (q + 1) * tiles_per_piece)
            if i == 0:
                for t in tiles:
                    carries[t] = (carry_ref[:, t * LANES:(t + 1) * LANES],
                                  carry_ref[:, half + t * LANES:half + (t + 1) * LANES])
            for g in range(i * per, min((i + 1) * per, groups)):
                for t in tiles:
                    carries[t] = tile_group(t, g, carries[t])
            if i == row_pieces - 1:
                for t in tiles:
                    carry_ref[:, t * LANES:(t + 1) * LANES] = carries[t][0]
                    carry_ref[:, half + t * LANES:half + (t + 1) * LANES] = carries[t][1]

        carries = {}
        return [functools.partial(piece, q, i, carries)
                for q in range(ntile // tiles_per_piece) for i in range(row_pieces)]

    def outputs(slot):
        taps_per_piece = ncol // LANES

        def cast():
            for t in range(2 * ntile):
                sb_ref[:, t * LANES:(t + 1) * LANES] = st_refs[slot][t].astype(BF16)

        def piece(c):
            lhs = jnp.concatenate([sb_ref[...], lhs_refs[slot][...]], axis=1)
            yall = _dot(lhs, wck_ref[:, c * ncol:(c + 1) * ncol])
            for i in range(taps_per_piece):
                tap = c * taps_per_piece + i
                ys_ref[pl.ds(tap, nblk, stride=s), :] = yall[:, i * LANES:(i + 1) * LANES]

        def finish():
            y_ref[...] = jax.nn.gelu(ys_ref[...]).astype(BF16)

        return [cast] + [functools.partial(piece, c) for c in range(s * LANES // ncol)] + [finish]

    for phase in range(3):
        @pl.when(n % 3 == phase)
        def _steps(phase=phase):
            mxu = increments(phase) + outputs((phase + 1) % 3)
            vpu = scan((phase + 2) % 3, 3)
            for i, piece in enumerate(mxu):
                piece()
                if i < len(vpu):
                    vpu[i]()
            for piece in vpu[len(mxu):]:
                piece()


def _s5_scan(u, wb, wck, consts):
    B = u.shape[0]
    s = S5_TAPS
    L = u.shape[2] * s
    tm = TM_S5
    nt = L // tm
    ntiles = LANE_TILES * B * nt
    tile_of = lambda n, lag: jnp.clip(n - lag, 0, ntiles - 1)
    lane_tile = lambda n, lag: tile_of(n, lag) // (B * nt)

    def tok(lag, rows, width):
        def index(n):
            i = tile_of(n, lag)
            return ((i // nt) % B, i // (B * nt), i % nt, 0)
        return pl.BlockSpec((None, None, rows, width), index)

    def par(lag, *shape):
        return pl.BlockSpec((None,) + shape, lambda n: (lane_tile(n, lag),) + (0,) * len(shape))

    nblk = tm // s
    return pl.pallas_call(
        functools.partial(_s5_scan_kernel, tiles_per_seq=nt),
        grid=(ntiles + 2,),
        in_specs=[tok(0, nblk, s * LANES), par(0, s * LANES, 2 * TILE_STATES),
                  par(1, 5, 2, SUBLANES, TILE_STATES), par(2, 2 * TILE_STATES + s * LANES, s * LANES)],
        out_specs=tok(2, tm, LANES),
        out_shape=jax.ShapeDtypeStruct((B, LANE_TILES, L, LANES), BF16),
        scratch_shapes=([pltpu.VMEM((nblk, s * LANES), BF16)] * 3
                        + [pltpu.VMEM((2 * TILE_STATES // LANES, nblk, LANES), F32)] * 3
                        + [pltpu.VMEM((nblk, 2 * TILE_STATES), BF16), pltpu.VMEM((tm, LANES), F32),
                           pltpu.VMEM((SUBLANES, 2 * TILE_STATES), F32)]),
        compiler_params=pltpu.CompilerParams(
            dimension_semantics=("arbitrary",), vmem_limit_bytes=VMEM_LIMIT),
        name="s5_scan",
    )(u, wb, consts, wck)


def _out_xattn_kernel(x_ref, ret_ref, y_ref, gs_ref, gw_ref, gb_ref, wout_ref, g2_ref, wq_ref,
                      ka_ref, va_ref, wo_ref, gf_ref, o_ref):
    y = jnp.concatenate([y_ref[j] for j in range(LANE_TILES)], axis=1)
    z = _dot(y, gw_ref[...]) + gb_ref[...]
    ssm = (y.astype(F32) * _sigmoid(z) * gs_ref[...].astype(F32)).astype(BF16)
    x1 = (x_ref[...] + _dot(ret_ref[...], wout_ref[0:RET_V_WIDTH, :])
          + _dot(ssm, wout_ref[RET_V_WIDTH:RET_V_WIDTH + S5_WIDTH, :]))
    h2 = _rms(x1, g2_ref[...]).astype(BF16)
    qa = (_dot(h2, wq_ref[...]) * (XA_DH ** -0.5)).astype(BF16)
    heads = []
    for h in range(XA_HEADS):
        cols = slice(h * XA_DH, (h + 1) * XA_DH)
        s = lax.dot_general(qa[:, cols], ka_ref[:, cols], (((1,), (1,)), ((), ())),
                            preferred_element_type=F32)
        e = jnp.exp(s - jnp.max(s, axis=-1, keepdims=True))
        l = jnp.sum(e, axis=-1, keepdims=True)
        heads.append((_dot(e.astype(BF16), va_ref[:, cols]) * (1.0 / l)).astype(BF16))
    o = jnp.concatenate(heads, axis=-1)
    x2 = x1 + _dot(o, wo_ref[...])
    o_ref[...] = _rms(x2, gf_ref[...])


def _out_xattn(x, ret, y, gs, gw, gb, wout, g2, wq, ka, va, wo, gf):
    B, L, D = x.shape
    tm = TM_OUT
    row = pl.BlockSpec((None, tm, D), lambda b, t: (b, t, 0))
    const = lambda shape: pl.BlockSpec(shape, lambda b, t: (0,) * len(shape))
    mem = pl.BlockSpec((None, MEM_LEN, D), lambda b, t: (b, 0, 0))
    ytile = pl.BlockSpec((None, LANE_TILES, tm, LANES), lambda b, t: (b, 0, t, 0))
    return pl.pallas_call(
        _out_xattn_kernel,
        grid=(B, L // tm),
        in_specs=[row, row, ytile, row, const((D, D)), const((1, D)), const(wout.shape), const((1, D)),
                  const((D, D)), mem, mem, const((D, D)), const((1, D))],
        out_specs=row,
        out_shape=jax.ShapeDtypeStruct((B, L, D), F32),
        compiler_params=pltpu.CompilerParams(
            dimension_semantics=("arbitrary", "arbitrary"), vmem_limit_bytes=VMEM_LIMIT),
        name="out_xattn",
    )(x, ret, y, gs, gw, gb, wout, g2, wq, ka, va, wo, gf)


def _rope_inv_lanes():
    half = RET_DK // 2
    inv = ROPE_BASE ** (-np.arange(half, dtype=np.float64) / half)
    return jnp.asarray(np.tile(inv, LANES // half).reshape(1, LANES), F32)


def kernel(x, mem, positions, norm1_g, w_in, ret_gn_g, s5_a_re, s5_a_im, s5_log_dt, s5_b_re, s5_b_im, s5_c_re, s5_c_im, s5_d, s5_glu_w, s5_glu_b, w_out, norm2_g, norm_mem_g, xa_wq, xa_wk, xa_wv, xa_wo, norm_f_g):
    B, L, D = x.shape
    l = 0
    rowvec = lambda v: v.reshape(1, -1)

    wb, wck, consts, win = _s5_prep(s5_a_re[l], s5_a_im[l], s5_log_dt[l], s5_b_re[l], s5_b_im[l],
                                    s5_c_re[l], s5_c_im[l], s5_d[l], w_in[l])
    (u, g_s5, ret), (glu_w, wout, wq, wk, wv, wo) = _proj_ret(
        x, positions.reshape(B, L, 1), _rope_inv_lanes(), rowvec(norm1_g[l]), win, rowvec(ret_gn_g[l]),
        [s5_glu_w[l], w_out[l], xa_wq[l], xa_wk[l], xa_wv[l], xa_wo[l]])
    ka, va = _mem_kv(mem, rowvec(norm_mem_g[l]), wk, wv)
    y = _s5_scan(u, wb, wck, consts)
    return _out_xattn(x, ret, y, g_s5, glu_w, rowvec(s5_glu_b[l]), wout,
                      rowvec(norm2_g[l]), wq, ka, va, wo, rowvec(norm_f_g))
```

```python
import functools
import math

import numpy as np
import jax
import jax.numpy as jnp
from jax import lax
from jax.experimental import pallas as pl
from jax.experimental.pallas import tpu as pltpu

F32 = jnp.float32
BF16 = jnp.bfloat16

D_MODEL = 1024
MEM_LEN = 256
EPS = 1e-6
ROPE_BASE = 10000.0

RET_HEADS = 8
RET_QK_WIDTH = 512
RET_V_WIDTH = 1024
RET_DK = 64
RET_DV = 128
CHUNK = 128

S5_WIDTH = 1024
S5_GROUP = 16
S5_GROUPS = 64
S5_STATE = 64
S5_NSTATE = S5_GROUPS * S5_STATE

IN_COLS = 5120
COL_Q, COL_K, COL_V, COL_GR, COL_U, COL_GS = 0, 512, 1024, 2048, 3072, 4096

XA_HEADS = 4
XA_DH = 256

LANES = 128
SUBLANES = 8
VMEM_LIMIT = 48 * 1024 * 1024

LANE_TILES = S5_WIDTH // LANES
TILE_GROUPS = LANES // S5_GROUP
TILE_STATES = S5_NSTATE // LANE_TILES
S5_TAPS = 4
S5_SCAN_LANES = 2 * LANES
S5_RUN = 4

LOG_G = tuple(math.log1p(-(2.0 ** (-5.0 - h))) for h in range(RET_HEADS))

TM_PROJ = 512
TM_S5 = 4096
TM_OUT = 512


def _rms(x, g):
    ms = jnp.mean(x * x, axis=-1, keepdims=True)
    return x * lax.rsqrt(ms + EPS) * g


def _sigmoid(z):
    return 1.0 / (1.0 + jnp.exp(-z))


def _silu(g):
    return g * _sigmoid(g)


def _dot(a, b):
    return jnp.dot(a, b, preferred_element_type=F32)


def _cmul(ar, ai, br, bi):
    return ar * br - ai * bi, ar * bi + ai * br


def _s5_discretise(ar, ai, ldt):
    dt = jnp.exp(ldt)
    mag = jnp.exp(ar * dt)
    p_re = mag * jnp.cos(ai * dt)
    p_im = mag * jnp.sin(ai * dt)
    den = ar * ar + ai * ai
    nr, ni = p_re - 1.0, p_im
    f_re = (nr * ar + ni * ai) / den
    f_im = (ni * ar - nr * ai) / den
    return p_re, p_im, f_re, f_im


def _powers(p_re, p_im, n):
    pw = [(jnp.ones_like(p_re), jnp.zeros_like(p_im))]
    for _ in range(n):
        pw.append(_cmul(pw[-1][0], pw[-1][1], p_re, p_im))
    return pw


def _block_diag(x):
    tiled = jnp.concatenate([x] * TILE_GROUPS, axis=1)
    r = lax.broadcasted_iota(jnp.int32, tiled.shape, 0)
    c = lax.broadcasted_iota(jnp.int32, tiled.shape, 1)
    return jnp.where(r // S5_GROUP == c // S5_STATE, tiled, 0.0)


def _s5_prep_kernel(a_ref, ag_ref, b_ref, c_ref, d_ref, win_ref, wb_ref, wck_ref, consts_ref, win16_ref):
    s = S5_TAPS
    half = TILE_STATES
    def per_row(v):
        return jnp.broadcast_to(v[:, None, :], (TILE_GROUPS, S5_GROUP, S5_STATE)).reshape(LANES, S5_STATE)

    g_re, g_im, gf_re, gf_im = _s5_discretise(ag_ref[0], ag_ref[1], ag_ref[2])
    gw = [(per_row(r), per_row(i)) for r, i in _powers(g_re, g_im, s)]
    bbr, bbi = _cmul(per_row(gf_re), per_row(gf_im), b_ref[0], b_ref[1])
    for k in range(s):
        wr, wi = _cmul(bbr, bbi, *gw[s - 1 - k])
        wb_ref[k * LANES:(k + 1) * LANES, 0:half] = _block_diag(wr).astype(BF16)
        wb_ref[k * LANES:(k + 1) * LANES, half:2 * half] = _block_diag(wi).astype(BF16)

    bb = jnp.concatenate([bbr, bbi], axis=1)
    r = lax.broadcasted_iota(jnp.int32, (LANES, LANES), 0)
    c = lax.broadcasted_iota(jnp.int32, (LANES, LANES), 1)
    same_group = r // S5_GROUP == c // S5_GROUP
    taps = []
    for d in range(s + 1):
        er, ei = _cmul(c_ref[0], c_ref[1], *gw[d])
        if d >= 1:
            cwt = jnp.concatenate([_block_diag(er), -_block_diag(ei)], axis=1)
            wck_ref[0:2 * half, (d - 1) * LANES:d * LANES] = cwt.T.astype(BF16)
        if d < s:
            t = lax.dot_general(bb, jnp.concatenate([er, -ei], axis=1), (((1,), (1,)), ((), ())),
                                preferred_element_type=F32, precision=lax.Precision.HIGHEST)
            if d == 0:
                t = t + jnp.where(r == c, d_ref[...], 0.0)
            taps.append(jnp.where(same_group, t, 0.0))
    zero = jnp.zeros((LANES, LANES), BF16)
    for m in range(s):
        for i in range(s):
            blk = taps[i - m].astype(BF16) if m <= i else zero
            wck_ref[2 * half + m * LANES:2 * half + (m + 1) * LANES, i * LANES:(i + 1) * LANES] = blk

    p_re, p_im, _, _ = _s5_discretise(a_ref[0:1, :], a_ref[1:2, :], a_ref[2:3, :])
    pw = _powers(p_re, p_im, s)
    shape = (SUBLANES, half)
    qr = _powers(pw[s][0], pw[s][1], S5_RUN)
    consts_ref[0, 0] = jnp.broadcast_to(qr[1][0], shape)
    consts_ref[0, 1] = jnp.broadcast_to(qr[1][1], shape)
    rp = _powers(qr[S5_RUN][0], qr[S5_RUN][1], SUBLANES)
    row = lax.broadcasted_iota(jnp.int32, shape, 0)
    for idx, d in enumerate((1, 2, 4)):
        consts_ref[1 + idx, 0] = jnp.where(row >= d, rp[d][0], 0.0)
        consts_ref[1 + idx, 1] = jnp.where(row >= d, rp[d][1], 0.0)
    cr = jnp.zeros(shape, F32)
    ci = jnp.zeros(shape, F32)
    for k in range(SUBLANES):
        cr = jnp.where(row == k, rp[k + 1][0], cr)
        ci = jnp.where(row == k, rp[k + 1][1], ci)
    consts_ref[4, 0] = cr
    consts_ref[4, 1] = ci

    win16_ref[...] = win_ref[...].astype(BF16)


def _s5_prep(a_re, a_im, log_dt, b_re, b_im, c_re, c_im, d, w_in):
    s = S5_TAPS
    ldt = jnp.broadcast_to(log_dt[:, None], (S5_GROUPS, S5_STATE))
    ag = jnp.stack([a_re, a_im, ldt])
    a3 = ag.reshape(3, S5_NSTATE)
    bt = jnp.stack([b_re, b_im]).transpose(0, 1, 3, 2).reshape(2, S5_WIDTH, S5_STATE)
    ct = jnp.stack([c_re, c_im]).reshape(2, S5_WIDTH, S5_STATE)
    tile = lambda *shape: pl.BlockSpec((None,) + shape, lambda j: (j,) + (0,) * len(shape))
    return pl.pallas_call(
        _s5_prep_kernel,
        grid=(LANE_TILES,),
        in_specs=[pl.BlockSpec((3, TILE_STATES), lambda j: (0, j)),
                  pl.BlockSpec((3, TILE_GROUPS, S5_STATE), lambda j: (0, j, 0)),
                  pl.BlockSpec((2, LANES, S5_STATE), lambda j: (0, j, 0)),
                  pl.BlockSpec((2, LANES, S5_STATE), lambda j: (0, j, 0)),
                  pl.BlockSpec((1, LANES), lambda j: (0, j)),
                  pl.BlockSpec((D_MODEL // LANE_TILES, IN_COLS), lambda j: (j, 0))],
        out_specs=(tile(s * LANES, 2 * TILE_STATES),
                   tile(2 * TILE_STATES + s * LANES, s * LANES),
                   tile(5, 2, SUBLANES, TILE_STATES),
                   pl.BlockSpec((D_MODEL // LANE_TILES, IN_COLS), lambda j: (j, 0))),
        out_shape=(jax.ShapeDtypeStruct((LANE_TILES, s * LANES, 2 * TILE_STATES), BF16),
                   jax.ShapeDtypeStruct((LANE_TILES, 2 * TILE_STATES + s * LANES, s * LANES), BF16),
                   jax.ShapeDtypeStruct((LANE_TILES, 5, 2, SUBLANES, TILE_STATES), F32),
                   jax.ShapeDtypeStruct((D_MODEL, IN_COLS), BF16)),
        compiler_params=pltpu.CompilerParams(
            dimension_semantics=("arbitrary",), vmem_limit_bytes=VMEM_LIMIT),
        name="s5_prep",
    )(a3, ag, bt, ct, d.reshape(1, S5_WIDTH), w_in)


def _mem_kv_kernel(mem_ref, g_ref, wk_ref, wv_ref, k_ref, v_ref):
    m = _rms(mem_ref[...], g_ref[...]).astype(BF16)
    k_ref[...] = _dot(m, wk_ref[...]).astype(BF16)
    v_ref[...] = _dot(m, wv_ref[...]).astype(BF16)


def _mem_kv(mem, g, wk, wv):
    B, M, D = mem.shape
    return pl.pallas_call(
        _mem_kv_kernel,
        grid=(B,),
        in_specs=[pl.BlockSpec((None, M, D), lambda b: (b, 0, 0)),
                  pl.BlockSpec((1, D), lambda b: (0, 0)),
                  pl.BlockSpec((D, D), lambda b: (0, 0)),
                  pl.BlockSpec((D, D), lambda b: (0, 0))],
        out_specs=(pl.BlockSpec((None, M, D), lambda b: (b, 0, 0)),
                   pl.BlockSpec((None, M, D), lambda b: (b, 0, 0))),
        out_shape=(jax.ShapeDtypeStruct((B, M, D), BF16),
                   jax.ShapeDtypeStruct((B, M, D), BF16)),
        compiler_params=pltpu.CompilerParams(
            dimension_semantics=("arbitrary",), vmem_limit_bytes=VMEM_LIMIT),
        name="mem_kv",
    )(mem, g, wk, wv)


def _token_prologue(x_ref, pos_ref, inv_ref, g_ref, h_ref, cos_ref, sin_ref):
    h_ref[...] = _rms(x_ref[...], g_ref[...]).astype(BF16)
    lane = lax.broadcasted_iota(jnp.int32, (1, LANES), 1)
    nfreq = RET_DK // 2
    group = lane // nfreq
    nblock = LANES // nfreq
    rows = x_ref.shape[0] // nblock
    pos = pos_ref[...].astype(F32)
    pos4 = jnp.zeros((rows, LANES), F32)
    for k in range(nblock):
        pos4 = jnp.where(group == k, pos[k * rows:(k + 1) * rows, :], pos4)
    ang4 = pos4 * inv_ref[...]

    def spread(t4):
        blocks = []
        for k in range(nblock):
            m = jnp.where(group == k, t4, 0.0)
            m = m + pltpu.roll(m, nfreq, 1)
            blocks.append(m + pltpu.roll(m, 2 * nfreq, 1))
        return jnp.concatenate(blocks, axis=0)

    first_half = (lane % RET_DK) < nfreq
    sin = spread(jnp.sin(ang4))
    cos_ref[...] = spread(jnp.cos(ang4))
    sin_ref[...] = jnp.where(first_half, -sin, sin)


def _proj_ret_kernel(x0_ref, pos0_ref, x_ref, pos_ref, inv_ref, g_ref, w_ref, gn_ref, *rest,
                     tiles_per_seq, num_tiles, num_casts):
    f32_refs, rest = rest[:num_casts], rest[num_casts:]
    (u_ref, gs_ref, ret_ref), rest = rest[:3], rest[3:]
    bf16_refs, rest = rest[:num_casts], rest[num_casts:]
    (cur_ref, prev_ref, hn_ref, cn_ref, sn_ref, h_ref, cos_ref, sin_ref, us_ref,
     state_ref, decay_ref, qw_ref, kw_ref) = rest
    t = pl.program_id(0)

    @pl.when(t == 0)
    def _init():
        _retention_constants(decay_ref, qw_ref, kw_ref)
        cur_ref[...] = jnp.zeros_like(cur_ref)
        _token_prologue(x0_ref, pos0_ref, inv_ref, g_ref, hn_ref, cn_ref, sn_ref)

    @pl.when((t == 0) | (t % tiles_per_seq == 1))
    def _reset():
        state_ref[...] = jnp.zeros_like(state_ref)

    ret_items = _retention_items(prev_ref, gn_ref, ret_ref, state_ref, decay_ref, qw_ref, kw_ref)

    @pl.when(t == num_tiles)
    def _drain():
        prev_ref[...] = cur_ref[...]
        for r in ret_items:
            r()

    @pl.when(t < num_tiles)
    def _step():
        prev_ref[...] = cur_ref[...]
        h_ref[...] = hn_ref[...]
        cos_ref[...] = cn_ref[...]
        sin_ref[...] = sn_ref[...]
        h = h_ref[...]
        cos = cos_ref[...]
        sin_signed = sin_ref[...]
        lane = lax.broadcasted_iota(jnp.int32, (1, LANES), 1)
        first_half = (lane % RET_DK) < (RET_DK // 2)

        def rope(p):
            partner = jnp.where(first_half,
                                pltpu.roll(p, LANES - RET_DK // 2, 1),
                                pltpu.roll(p, RET_DK // 2, 1))
            return p * cos + partner * sin_signed

        def proj_q():
            pq = _dot(h, w_ref[:, COL_Q:COL_K])
            for i in range(RET_QK_WIDTH // LANES):
                sl = slice(i * LANES, (i + 1) * LANES)
                cur_ref[:, COL_Q + i * LANES:COL_Q + (i + 1) * LANES] = rope(pq[:, sl]).astype(BF16)

        def proj_k():
            pk = _dot(h, w_ref[:, COL_K:COL_V])
            for i in range(RET_QK_WIDTH // LANES):
                sl = slice(i * LANES, (i + 1) * LANES)
                cur_ref[:, COL_K + i * LANES:COL_K + (i + 1) * LANES] = (
                    rope(pk[:, sl]) * (RET_DK ** -0.5)).astype(BF16)

        def proj_v():
            cur_ref[:, COL_V:COL_GR] = _dot(h, w_ref[:, COL_V:COL_GR]).astype(BF16)

        def proj_gate():
            cur_ref[:, COL_GR:COL_U] = _silu(_dot(h, w_ref[:, COL_GR:COL_U])).astype(BF16)

        def proj_u():
            pu = _dot(h, w_ref[:, COL_U:COL_GS])
            for j in range(LANE_TILES):
                us_ref[j] = pu[:, j * LANES:(j + 1) * LANES]
            for j in range(LANE_TILES):
                for k in range(S5_TAPS):
                    u_ref[j, :, k * LANES:(k + 1) * LANES] = us_ref[
                        j, pl.ds(k, us_ref.shape[1] // S5_TAPS, stride=S5_TAPS), :].astype(BF16)

        def proj_gs():
            gs_ref[...] = _silu(_dot(h, w_ref[:, COL_GS:IN_COLS])).astype(BF16)

        proj_items = [proj_q, proj_k, proj_v, proj_gate, proj_u, proj_gs]
        per = -(-len(ret_items) // len(proj_items))
        for i, item in enumerate(proj_items):
            for r in ret_items[i * per:(i + 1) * per]:
                r()
            item()
        _token_prologue(x_ref, pos_ref, inv_ref, g_ref, hn_ref, cn_ref, sn_ref)
        for src_ref, dst_ref in zip(f32_refs, bf16_refs):
            dst_ref[...] = src_ref[...].astype(BF16)


def _proj_ret(x, pos3, inv, g, w, gn, later):
    B, L, D = x.shape
    tm = TM_PROJ
    nt = L // tm
    n = B * nt
    cur = lambda t: jnp.minimum(t, n - 1)
    nxt = lambda t: jnp.minimum(t + 1, n - 1)
    lag = lambda t: jnp.maximum(t - 1, 0)
    first = lambda t: 0 * t
    row = lambda width, tile: pl.BlockSpec((None, tm, width), lambda t: (tile(t) // nt, tile(t) % nt, 0))
    const = lambda shape: pl.BlockSpec(shape, lambda t: (0,) * len(shape))
    slab = lambda m: pl.BlockSpec((m.shape[0] // n, m.shape[1]), lambda t: (cur(t), 0))
    outs = pl.pallas_call(
        functools.partial(_proj_ret_kernel, tiles_per_seq=nt, num_tiles=n, num_casts=len(later)),
        grid=(n + 1,),
        in_specs=[row(D, first), row(1, first), row(D, nxt), row(1, nxt), const((1, LANES)), const((1, D)),
                  const((D, IN_COLS)), const((1, RET_V_WIDTH))] + [slab(m) for m in later],
        out_specs=[pl.BlockSpec((None, LANE_TILES, tm // S5_TAPS, S5_TAPS * LANES),
                                lambda t: (cur(t) // nt, 0, cur(t) % nt, 0)),
                   row(S5_WIDTH, cur), row(RET_V_WIDTH, lag)] + [slab(m) for m in later],
        out_shape=[jax.ShapeDtypeStruct((B, LANE_TILES, L // S5_TAPS, S5_TAPS * LANES), BF16),
                   jax.ShapeDtypeStruct((B, L, S5_WIDTH), BF16),
                   jax.ShapeDtypeStruct((B, L, RET_V_WIDTH), BF16)]
        + [jax.ShapeDtypeStruct(m.shape, BF16) for m in later],
        scratch_shapes=[pltpu.VMEM((tm, COL_U), BF16),
                        pltpu.VMEM((tm, COL_U), BF16),
                        pltpu.VMEM((tm, D), BF16), pltpu.VMEM((tm, LANES), F32), pltpu.VMEM((tm, LANES), F32),
                        pltpu.VMEM((tm, D), BF16), pltpu.VMEM((tm, LANES), F32), pltpu.VMEM((tm, LANES), F32),
                        pltpu.VMEM((LANE_TILES, tm, LANES), F32),
                        pltpu.VMEM((RET_HEADS // 2, LANES, 2 * RET_DV), F32),
                        pltpu.VMEM((RET_HEADS // 2, CHUNK, 2 * CHUNK), F32),
                        pltpu.VMEM((RET_HEADS // 2, CHUNK, 2 * RET_DV), F32),
                        pltpu.VMEM((RET_HEADS // 2, CHUNK, LANES), F32)],
        compiler_params=pltpu.CompilerParams(
            dimension_semantics=("arbitrary",), vmem_limit_bytes=VMEM_LIMIT),
        name="proj_ret",
    )(x, pos3, x, pos3, inv, g, w, gn, *later)
    return outs[:3], outs[3:]


def _retention_constants(decay_ref, qw_ref, kw_ref):
    i = lax.broadcasted_iota(jnp.int32, (CHUNK, 2 * CHUNK), 0).astype(F32)
    c = lax.broadcasted_iota(jnp.int32, (CHUNK, 2 * CHUNK), 1)
    j = (c % CHUNK).astype(F32)
    diff = i - j
    lane = lax.broadcasted_iota(jnp.int32, (CHUNK, LANES), 1)
    ik = lax.broadcasted_iota(jnp.int32, (CHUNK, LANES), 0).astype(F32)
    for p in range(RET_HEADS // 2):
        lg = jnp.where(c < CHUNK, LOG_G[2 * p], LOG_G[2 * p + 1])
        decay_ref[p] = jnp.where(diff >= 0.0, jnp.exp(lg * jnp.maximum(diff, 0.0)), 0.0)
        qw_ref[p] = jnp.exp(lg * (i + 1.0))
        lgk = jnp.where(lane < RET_DK, LOG_G[2 * p], LOG_G[2 * p + 1])
        kw_ref[p] = jnp.exp(lgk * (CHUNK - 1.0 - ik))


def _retention_items(src_ref, gn_ref, o_ref, state_ref, decay_ref, qw_ref, kw_ref):
    pairs = RET_HEADS // 2
    low = lax.broadcasted_iota(jnp.int32, (1, LANES), 1) < RET_DK
    srow = lax.broadcasted_iota(jnp.int32, (LANES, 2 * RET_DV), 0)
    scol = lax.broadcasted_iota(jnp.int32, (LANES, 2 * RET_DV), 1)
    own_block = (srow < RET_DK) == (scol < RET_DV)
    scol1 = lax.broadcasted_iota(jnp.int32, (1, 2 * RET_DV), 1)
    zero_v = jnp.zeros((CHUNK, RET_DV), BF16)

    def pair_step(c, p):
        rows = slice(c * CHUNK, (c + 1) * CHUNK)
        qt = src_ref[rows, COL_Q + p * LANES:COL_Q + (p + 1) * LANES]
        kt = src_ref[rows, COL_K + p * LANES:COL_K + (p + 1) * LANES]
        vp = src_ref[rows, COL_V + 2 * p * RET_DV:COL_V + 2 * (p + 1) * RET_DV]
        state = state_ref[p]
        cross = _dot(qt, state.astype(BF16))
        kw = (kt.astype(F32) * kw_ref[p]).astype(BF16)
        upd = lax.dot_general(kw, vp, (((0,), (0,)), ((), ())), preferred_element_type=F32)
        chunk_decay = jnp.where(scol1 < RET_DV,
                                math.exp(LOG_G[2 * p] * CHUNK), math.exp(LOG_G[2 * p + 1] * CHUNK))
        state_ref[p] = state * chunk_decay + jnp.where(own_block, upd, 0.0)

        zero_k = jnp.zeros_like(kt)
        k_rows = jnp.concatenate([jnp.where(low, kt, zero_k), jnp.where(low, zero_k, kt)], axis=0)
        s = lax.dot_general(qt, k_rows, (((1,), (1,)), ((), ())), preferred_element_type=F32)
        pm = (s * decay_ref[p]).astype(BF16)
        v_diag = jnp.concatenate(
            [jnp.concatenate([vp[:, 0:RET_DV], zero_v], axis=1),
             jnp.concatenate([zero_v, vp[:, RET_DV:2 * RET_DV]], axis=1)], axis=0)
        o2 = _dot(pm, v_diag) + qw_ref[p] * cross
        for e in range(2):
            h = 2 * p + e
            cols = slice(h * RET_DV, (h + 1) * RET_DV)
            o = o2[:, e * RET_DV:(e + 1) * RET_DV]
            mu = jnp.mean(o, axis=-1, keepdims=True)
            oc = o - mu
            var = jnp.mean(oc * oc, axis=-1, keepdims=True)
            on = oc * lax.rsqrt(var + EPS) * gn_ref[:, cols]
            gate = src_ref[rows, COL_GR + h * RET_DV:COL_GR + (h + 1) * RET_DV].astype(F32)
            o_ref[rows, cols] = (on * gate).astype(BF16)

    return [functools.partial(pair_step, c, p)
            for c in range(src_ref.shape[0] // CHUNK) for p in range(pairs)]


def _s5_scan_kernel(u_ref, wb_ref, consts_ref, u2_ref, wck_ref, y_ref, *scratch, tiles_per_seq):
    st_refs = scratch[0:3]
    sb_ref, ys_ref, carry_ref = scratch[3:]
    n = pl.program_id(0)
    s = S5_TAPS
    nblk = u_ref.shape[0]
    half = TILE_STATES

    @pl.when(n == 0)
    def _init():
        for ref in scratch[0:3]:
            ref[...] = jnp.zeros_like(ref)

    @pl.when((n == 0) | ((n - 1) % tiles_per_seq == 0))
    def _reset():
        carry_ref[...] = jnp.zeros_like(carry_ref)

    ncol = 2 * LANES

    def increments(slot):
        st_ref = st_refs[slot]

        def piece(c):
            z = _dot(u_ref[...], wb_ref[:, c * ncol:(c + 1) * ncol])
            for i in range(ncol // LANES):
                st_ref[c * (ncol // LANES) + i] = z[:, i * LANES:(i + 1) * LANES]

        return [functools.partial(piece, c) for c in range(2 * half // ncol)]

    ntile = half // LANES

    def scan(slot, row_pieces):
        st_ref = st_refs[slot]
        run = S5_RUN
        span = run * SUBLANES
        row = lax.broadcasted_iota(jnp.int32, (SUBLANES, LANES), 0)
        last = SUBLANES - 1
        groups = nblk // span
        per = -(-groups // row_pieces)
        tiles_per_piece = S5_SCAN_LANES // LANES

        def tile_group(t, g, carry):
            lanes = slice(t * LANES, (t + 1) * LANES)
            const = lambda idx: (consts_ref[idx, 0, :, lanes], consts_ref[idx, 1, :, lanes])
            cr, ci = carry
            rows = [pl.ds(g * span + j, SUBLANES, stride=run) for j in range(run)]
            loc = [(st_ref[t, rows[0], :], st_ref[ntile + t, rows[0], :])]
            for j in range(1, run):
                pr, pi = _cmul(*const(0), *loc[-1])
                loc.append((pr + st_ref[t, rows[j], :], pi + st_ref[ntile + t, rows[j], :]))
            fr, fi = loc[-1]
            for idx, d in enumerate((1, 2, 4)):
                pr, pi = _cmul(*const(1 + idx), pltpu.roll(fr, d, 0), pltpu.roll(fi, d, 0))
                fr, fi = fr + pr, fi + pi
            pr, pi = _cmul(*const(4), cr, ci)
            fr, fi = fr + pr, fi + pi
            er = jnp.where(row == 0, cr, pltpu.roll(fr, 1, 0))
            ei = jnp.where(row == 0, ci, pltpu.roll(fi, 1, 0))
            st_ref[t, rows[0], :] = er
            st_ref[ntile + t, rows[0], :] = ei
            for j in range(1, run):
                er, ei = _cmul(*const(0), er, ei)
                st_ref[t, rows[j], :] = loc[j - 1][0] + er
                st_ref[ntile + t, rows[j], :] = loc[j - 1][1] + ei
            return (jnp.broadcast_to(fr[last:, :], (SUBLANES, LANES)),
                    jnp.broadcast_to(fi[last:, :], (SUBLANES, LANES)))

        def piece(q, i, carries):
            tiles = range(q * tiles_per_piece, (q + 1) * tiles_per_piece)
            if i == 0:
                for t in tiles:
                    carries[t] = (carry_ref[:, t * LANES:(t + 1) * LANES],
                                  carry_ref[:, half + t * LANES:half + (t + 1) * LANES])
            for g in range(i * per, min((i + 1) * per, groups)):
                for t in tiles:
                    carries[t] = tile_group(t, g, carries[t])
            if i == row_pieces - 1:
                for t in tiles:
                    carry_ref[:, t * LANES:(t + 1) * LANES] = carries[t][0]
                    carry_ref[:, half + t * LANES:half + (t + 1) * LANES] = carries[t][1]

        carries = {}
        return [functools.partial(piece, q, i, carries)
                for q in range(ntile // tiles_per_piece) for i in range(row_pieces)]

    def outputs(slot):
        taps_per_piece = ncol // LANES

        def cast():
            for t in range(2 * ntile):
                sb_ref[:, t * LANES:(t + 1) * LANES] = st_refs[slot][t].astype(BF16)

        def piece(c):
            lhs = jnp.concatenate([sb_ref[...], u2_ref[...]], axis=1)
            yall = _dot(lhs, wck_ref[:, c * ncol:(c + 1) * ncol])
            for i in range(taps_per_piece):
                tap = c * taps_per_piece + i
                ys_ref[pl.ds(tap, nblk, stride=s), :] = yall[:, i * LANES:(i + 1) * LANES]

        def finish():
            y_ref[...] = jax.nn.gelu(ys_ref[...]).astype(BF16)

        return [cast] + [functools.partial(piece, c) for c in range(s * LANES // ncol)] + [finish]

    for phase in range(3):
        @pl.when(n % 3 == phase)
        def _steps(phase=phase):
            mxu = increments(phase) + outputs((phase + 1) % 3)
            vpu = scan((phase + 2) % 3, 3)
            for i, piece in enumerate(mxu):
                piece()
                if i < len(vpu):
                    vpu[i]()
            for piece in vpu[len(mxu):]:
                piece()


def _s5_scan(u, wb, wck, consts):
    B = u.shape[0]
    s = S5_TAPS
    L = u.shape[2] * s
    tm = TM_S5
    nt = L // tm
    ntiles = LANE_TILES * B * nt
    tile_of = lambda n, lag: jnp.clip(n - lag, 0, ntiles - 1)
    lane_tile = lambda n, lag: tile_of(n, lag) // (B * nt)

    def tok(lag, rows, width):
        def index(n):
            i = tile_of(n, lag)
            return ((i // nt) % B, i // (B * nt), i % nt, 0)
        return pl.BlockSpec((None, None, rows, width), index)

    def par(lag, *shape):
        return pl.BlockSpec((None,) + shape, lambda n: (lane_tile(n, lag),) + (0,) * len(shape))

    nblk = tm // s
    return pl.pallas_call(
        functools.partial(_s5_scan_kernel, tiles_per_seq=nt),
        grid=(ntiles + 2,),
        in_specs=[tok(0, nblk, s * LANES), par(0, s * LANES, 2 * TILE_STATES),
                  par(1, 5, 2, SUBLANES, TILE_STATES), tok(2, nblk, s * LANES),
                  par(2, 2 * TILE_STATES + s * LANES, s * LANES)],
        out_specs=tok(2, tm, LANES),
        out_shape=jax.ShapeDtypeStruct((B, LANE_TILES, L, LANES), BF16),
        scratch_shapes=([pltpu.VMEM((2 * TILE_STATES // LANES, nblk, LANES), F32)] * 3
                        + [pltpu.VMEM((nblk, 2 * TILE_STATES), BF16), pltpu.VMEM((tm, LANES), F32),
                           pltpu.VMEM((SUBLANES, 2 * TILE_STATES), F32)]),
        compiler_params=pltpu.CompilerParams(
            dimension_semantics=("arbitrary",), vmem_limit_bytes=VMEM_LIMIT),
        name="s5_scan",
    )(u, wb, consts, u, wck)


def _out_xattn_kernel(x_ref, ret_ref, y_ref, gs_ref, gw_ref, gb_ref, wout_ref, g2_ref, wq_ref,
                      ka_ref, va_ref, wo_ref, gf_ref, o_ref):
    y = jnp.concatenate([y_ref[j] for j in range(LANE_TILES)], axis=1)
    z = _dot(y, gw_ref[...]) + gb_ref[...]
    ssm = (y.astype(F32) * _sigmoid(z) * gs_ref[...].astype(F32)).astype(BF16)
    x1 = (x_ref[...] + _dot(ret_ref[...], wout_ref[0:RET_V_WIDTH, :])
          + _dot(ssm, wout_ref[RET_V_WIDTH:RET_V_WIDTH + S5_WIDTH, :]))
    h2 = _rms(x1, g2_ref[...]).astype(BF16)
    qa = (_dot(h2, wq_ref[...]) * (XA_DH ** -0.5)).astype(BF16)
    heads = []
    for h in range(XA_HEADS):
        cols = slice(h * XA_DH, (h + 1) * XA_DH)
        s = lax.dot_general(qa[:, cols], ka_ref[:, cols], (((1,), (1,)), ((), ())),
                            preferred_element_type=F32)
        e = jnp.exp(s - jnp.max(s, axis=-1, keepdims=True))
        l = jnp.sum(e, axis=-1, keepdims=True)
        heads.append((_dot(e.astype(BF16), va_ref[:, cols]) * (1.0 / l)).astype(BF16))
    o = jnp.concatenate(heads, axis=-1)
    x2 = x1 + _dot(o, wo_ref[...])
    o_ref[...] = _rms(x2, gf_ref[...])


def _out_xattn(x, ret, y, gs, gw, gb, wout, g2, wq, ka, va, wo, gf):
    B, L, D = x.shape
    tm = TM_OUT
    row = pl.BlockSpec((None, tm, D), lambda b, t: (b, t, 0))
    const = lambda shape: pl.BlockSpec(shape, lambda b, t: (0,) * len(shape))
    mem = pl.BlockSpec((None, MEM_LEN, D), lambda b, t: (b, 0, 0))
    ytile = pl.BlockSpec((None, LANE_TILES, tm, LANES), lambda b, t: (b, 0, t, 0))
    return pl.pallas_call(
        _out_xattn_kernel,
        grid=(B, L // tm),
        in_specs=[row, row, ytile, row, const((D, D)), const((1, D)), const(wout.shape), const((1, D)),
                  const((D, D)), mem, mem, const((D, D)), const((1, D))],
        out_specs=row,
        out_shape=jax.ShapeDtypeStruct((B, L, D), F32),
        compiler_params=pltpu.CompilerParams(
            dimension_semantics=("arbitrary", "arbitrary"), vmem_limit_bytes=VMEM_LIMIT),
        name="out_xattn",
    )(x, ret, y, gs, gw, gb, wout, g2, wq, ka, va, wo, gf)


def _rope_inv_lanes():
    half = RET_DK // 2
    inv = ROPE_BASE ** (-np.arange(half, dtype=np.float64) / half)
    return jnp.asarray(np.tile(inv, LANES // half).reshape(1, LANES), F32)


def kernel(x, mem, positions, norm1_g, w_in, ret_gn_g, s5_a_re, s5_a_im, s5_log_dt, s5_b_re, s5_b_im, s5_c_re, s5_c_im, s5_d, s5_glu_w, s5_glu_b, w_out, norm2_g, norm_mem_g, xa_wq, xa_wk, xa_wv, xa_wo, norm_f_g):
    B, L, D = x.shape
    l = 0
    rowvec = lambda v: v.reshape(1, -1)

    wb, wck, consts, win = _s5_prep(s5_a_re[l], s5_a_im[l], s5_log_dt[l], s5_b_re[l], s5_b_im[l],
                                    s5_c_re[l], s5_c_im[l], s5_d[l], w_in[l])
    (u, g_s5, ret), (glu_w, wout, wq, wk, wv, wo) = _proj_ret(
        x, positions.reshape(B, L, 1), _rope_inv_lanes(), rowvec(norm1_g[l]), win, rowvec(ret_gn_g[l]),
        [s5_glu_w[l], w_out[l], xa_wq[l], xa_wk[l], xa_wv[l], xa_wo[l]])
    ka, va = _mem_kv(mem, rowvec(norm_mem_g[l]), wk, wv)
    y = _s5_scan(u, wb, wck, consts)
    return _out_xattn(x, ret, y, g_s5, glu_w, rowvec(s5_glu_b[l]), wout,
                      rowvec(norm2_g[l]), wq, ka, va, wo, rowvec(norm_f_g))
```

```python
import functools
import math

import numpy as np
import jax
import jax.numpy as jnp
from jax import lax
from jax.experimental import pallas as pl
from jax.experimental.pallas import tpu as pltpu

F32 = jnp.float32
BF16 = jnp.bfloat16

D_MODEL = 1024
MEM_LEN = 256
EPS = 1e-6
ROPE_BASE = 10000.0

RET_HEADS = 8
RET_QK_WIDTH = 512
RET_V_WIDTH = 1024
RET_DK = 64
RET_DV = 128
CHUNK = 128

S5_WIDTH = 1024
S5_GROUP = 16
S5_GROUPS = 64
S5_STATE = 64
S5_NSTATE = S5_GROUPS * S5_STATE

IN_COLS = 5120
COL_Q, COL_K, COL_V, COL_GR, COL_U, COL_GS = 0, 512, 1024, 2048, 3072, 4096

XA_HEADS = 4
XA_DH = 256

LANES = 128
SUBLANES = 8
VMEM_LIMIT = 48 * 1024 * 1024

LANE_TILES = S5_WIDTH // LANES
TILE_GROUPS = LANES // S5_GROUP
TILE_STATES = S5_NSTATE // LANE_TILES
S5_TAPS = 4
S5_SCAN_LANES = 2 * LANES
S5_RUN = 4

LOG_G = tuple(math.log1p(-(2.0 ** (-5.0 - h))) for h in range(RET_HEADS))

TM_PROJ = 512
TM_S5 = 4096
TM_OUT = 1024


def _rms(x, g):
    ms = jnp.mean(x * x, axis=-1, keepdims=True)
    return x * lax.rsqrt(ms + EPS) * g


def _sigmoid(z):
    return 1.0 / (1.0 + jnp.exp(-z))


def _silu(g):
    return g * _sigmoid(g)


def _dot(a, b):
    return jnp.dot(a, b, preferred_element_type=F32)


def _cmul(ar, ai, br, bi):
    return ar * br - ai * bi, ar * bi + ai * br


def _s5_discretise(ar, ai, ldt):
    dt = jnp.exp(ldt)
    mag = jnp.exp(ar * dt)
    p_re = mag * jnp.cos(ai * dt)
    p_im = mag * jnp.sin(ai * dt)
    den = ar * ar + ai * ai
    nr, ni = p_re - 1.0, p_im
    f_re = (nr * ar + ni * ai) / den
    f_im = (ni * ar - nr * ai) / den
    return p_re, p_im, f_re, f_im


def _powers(p_re, p_im, n):
    pw = [(jnp.ones_like(p_re), jnp.zeros_like(p_im))]
    for _ in range(n):
        pw.append(_cmul(pw[-1][0], pw[-1][1], p_re, p_im))
    return pw


def _block_diag(x):
    tiled = jnp.concatenate([x] * TILE_GROUPS, axis=1)
    r = lax.broadcasted_iota(jnp.int32, tiled.shape, 0)
    c = lax.broadcasted_iota(jnp.int32, tiled.shape, 1)
    return jnp.where(r // S5_GROUP == c // S5_STATE, tiled, 0.0)


def _s5_prep_kernel(a_ref, ag_ref, b_ref, c_ref, d_ref, win_ref, wb_ref, wck_ref, consts_ref, win16_ref):
    s = S5_TAPS
    half = TILE_STATES
    def per_row(v):
        return jnp.broadcast_to(v[:, None, :], (TILE_GROUPS, S5_GROUP, S5_STATE)).reshape(LANES, S5_STATE)

    g_re, g_im, gf_re, gf_im = _s5_discretise(ag_ref[0], ag_ref[1], ag_ref[2])
    gw = [(per_row(r), per_row(i)) for r, i in _powers(g_re, g_im, s)]
    bbr, bbi = _cmul(per_row(gf_re), per_row(gf_im), b_ref[0], b_ref[1])
    for k in range(s):
        wr, wi = _cmul(bbr, bbi, *gw[s - 1 - k])
        wb_ref[k * LANES:(k + 1) * LANES, 0:half] = _block_diag(wr).astype(BF16)
        wb_ref[k * LANES:(k + 1) * LANES, half:2 * half] = _block_diag(wi).astype(BF16)

    bb = jnp.concatenate([bbr, bbi], axis=1)
    r = lax.broadcasted_iota(jnp.int32, (LANES, LANES), 0)
    c = lax.broadcasted_iota(jnp.int32, (LANES, LANES), 1)
    same_group = r // S5_GROUP == c // S5_GROUP
    taps = []
    for d in range(s + 1):
        er, ei = _cmul(c_ref[0], c_ref[1], *gw[d])
        if d >= 1:
            cwt = jnp.concatenate([_block_diag(er), -_block_diag(ei)], axis=1)
            wck_ref[0:2 * half, (d - 1) * LANES:d * LANES] = cwt.T.astype(BF16)
        if d < s:
            t = lax.dot_general(bb, jnp.concatenate([er, -ei], axis=1), (((1,), (1,)), ((), ())),
                                preferred_element_type=F32, precision=lax.Precision.HIGHEST)
            if d == 0:
                t = t + jnp.where(r == c, d_ref[...], 0.0)
            taps.append(jnp.where(same_group, t, 0.0))
    zero = jnp.zeros((LANES, LANES), BF16)
    for m in range(s):
        for i in range(s):
            blk = taps[i - m].astype(BF16) if m <= i else zero
            wck_ref[2 * half + m * LANES:2 * half + (m + 1) * LANES, i * LANES:(i + 1) * LANES] = blk

    p_re, p_im, _, _ = _s5_discretise(a_ref[0:1, :], a_ref[1:2, :], a_ref[2:3, :])
    pw = _powers(p_re, p_im, s)
    shape = (SUBLANES, half)
    qr = _powers(pw[s][0], pw[s][1], S5_RUN)
    consts_ref[0, 0] = jnp.broadcast_to(qr[1][0], shape)
    consts_ref[0, 1] = jnp.broadcast_to(qr[1][1], shape)
    rp = _powers(qr[S5_RUN][0], qr[S5_RUN][1], SUBLANES)
    row = lax.broadcasted_iota(jnp.int32, shape, 0)
    for idx, d in enumerate((1, 2, 4)):
        consts_ref[1 + idx, 0] = jnp.where(row >= d, rp[d][0], 0.0)
        consts_ref[1 + idx, 1] = jnp.where(row >= d, rp[d][1], 0.0)
    cr = jnp.zeros(shape, F32)
    ci = jnp.zeros(shape, F32)
    for k in range(SUBLANES):
        cr = jnp.where(row == k, rp[k + 1][0], cr)
        ci = jnp.where(row == k, rp[k + 1][1], ci)
    consts_ref[4, 0] = cr
    consts_ref[4, 1] = ci

    win16_ref[...] = win_ref[...].astype(BF16)


def _s5_prep(a_re, a_im, log_dt, b_re, b_im, c_re, c_im, d, w_in):
    s = S5_TAPS
    ldt = jnp.broadcast_to(log_dt[:, None], (S5_GROUPS, S5_STATE))
    ag = jnp.stack([a_re, a_im, ldt])
    a3 = ag.reshape(3, S5_NSTATE)
    bt = jnp.stack([b_re, b_im]).transpose(0, 1, 3, 2).reshape(2, S5_WIDTH, S5_STATE)
    ct = jnp.stack([c_re, c_im]).reshape(2, S5_WIDTH, S5_STATE)
    tile = lambda *shape: pl.BlockSpec((None,) + shape, lambda j: (j,) + (0,) * len(shape))
    return pl.pallas_call(
        _s5_prep_kernel,
        grid=(LANE_TILES,),
        in_specs=[pl.BlockSpec((3, TILE_STATES), lambda j: (0, j)),
                  pl.BlockSpec((3, TILE_GROUPS, S5_STATE), lambda j: (0, j, 0)),
                  pl.BlockSpec((2, LANES, S5_STATE), lambda j: (0, j, 0)),
                  pl.BlockSpec((2, LANES, S5_STATE), lambda j: (0, j, 0)),
                  pl.BlockSpec((1, LANES), lambda j: (0, j)),
                  pl.BlockSpec((D_MODEL // LANE_TILES, IN_COLS), lambda j: (j, 0))],
        out_specs=(tile(s * LANES, 2 * TILE_STATES),
                   tile(2 * TILE_STATES + s * LANES, s * LANES),
                   tile(5, 2, SUBLANES, TILE_STATES),
                   pl.BlockSpec((D_MODEL // LANE_TILES, IN_COLS), lambda j: (j, 0))),
        out_shape=(jax.ShapeDtypeStruct((LANE_TILES, s * LANES, 2 * TILE_STATES), BF16),
                   jax.ShapeDtypeStruct((LANE_TILES, 2 * TILE_STATES + s * LANES, s * LANES), BF16),
                   jax.ShapeDtypeStruct((LANE_TILES, 5, 2, SUBLANES, TILE_STATES), F32),
                   jax.ShapeDtypeStruct((D_MODEL, IN_COLS), BF16)),
        compiler_params=pltpu.CompilerParams(
            dimension_semantics=("arbitrary",), vmem_limit_bytes=VMEM_LIMIT),
        name="s5_prep",
    )(a3, ag, bt, ct, d.reshape(1, S5_WIDTH), w_in)


def _mem_kv_kernel(mem_ref, g_ref, wk_ref, wv_ref, k_ref, v_ref):
    m = _rms(mem_ref[...], g_ref[...]).astype(BF16)
    k_ref[...] = _dot(m, wk_ref[...]).astype(BF16)
    v_ref[...] = _dot(m, wv_ref[...]).astype(BF16)


def _mem_kv(mem, g, wk, wv):
    B, M, D = mem.shape
    return pl.pallas_call(
        _mem_kv_kernel,
        grid=(B,),
        in_specs=[pl.BlockSpec((None, M, D), lambda b: (b, 0, 0)),
                  pl.BlockSpec((1, D), lambda b: (0, 0)),
                  pl.BlockSpec((D, D), lambda b: (0, 0)),
                  pl.BlockSpec((D, D), lambda b: (0, 0))],
        out_specs=(pl.BlockSpec((None, M, D), lambda b: (b, 0, 0)),
                   pl.BlockSpec((None, M, D), lambda b: (b, 0, 0))),
        out_shape=(jax.ShapeDtypeStruct((B, M, D), BF16),
                   jax.ShapeDtypeStruct((B, M, D), BF16)),
        compiler_params=pltpu.CompilerParams(
            dimension_semantics=("arbitrary",), vmem_limit_bytes=VMEM_LIMIT),
        name="mem_kv",
    )(mem, g, wk, wv)


def _token_prologue(x_ref, pos_ref, inv_ref, g_ref, h_ref, cos_ref, sin_ref):
    h_ref[...] = _rms(x_ref[...], g_ref[...]).astype(BF16)
    lane = lax.broadcasted_iota(jnp.int32, (1, LANES), 1)
    nfreq = RET_DK // 2
    group = lane // nfreq
    nblock = LANES // nfreq
    rows = x_ref.shape[0] // nblock
    pos = pos_ref[...].astype(F32)
    pos4 = jnp.zeros((rows, LANES), F32)
    for k in range(nblock):
        pos4 = jnp.where(group == k, pos[k * rows:(k + 1) * rows, :], pos4)
    ang4 = pos4 * inv_ref[...]

    def spread(t4):
        blocks = []
        for k in range(nblock):
            m = jnp.where(group == k, t4, 0.0)
            m = m + pltpu.roll(m, nfreq, 1)
            blocks.append(m + pltpu.roll(m, 2 * nfreq, 1))
        return jnp.concatenate(blocks, axis=0)

    first_half = (lane % RET_DK) < nfreq
    sin = spread(jnp.sin(ang4))
    cos_ref[...] = spread(jnp.cos(ang4))
    sin_ref[...] = jnp.where(first_half, -sin, sin)


def _proj_ret_kernel(x0_ref, pos0_ref, x_ref, pos_ref, inv_ref, g_ref, w_ref, gn_ref, *rest,
                     tiles_per_seq, num_tiles, num_casts):
    f32_refs, rest = rest[:num_casts], rest[num_casts:]
    (u_ref, gs_ref, ret_ref), rest = rest[:3], rest[3:]
    bf16_refs, rest = rest[:num_casts], rest[num_casts:]
    (cur_ref, prev_ref, hn_ref, cn_ref, sn_ref, h_ref, cos_ref, sin_ref, us_ref,
     state_ref, decay_ref, qw_ref, kw_ref) = rest
    t = pl.program_id(0)

    @pl.when(t == 0)
    def _init():
        _retention_constants(decay_ref, qw_ref, kw_ref)
        cur_ref[...] = jnp.zeros_like(cur_ref)
        _token_prologue(x0_ref, pos0_ref, inv_ref, g_ref, hn_ref, cn_ref, sn_ref)

    @pl.when((t == 0) | (t % tiles_per_seq == 1))
    def _reset():
        state_ref[...] = jnp.zeros_like(state_ref)

    ret_items = _retention_items(prev_ref, gn_ref, ret_ref, state_ref, decay_ref, qw_ref, kw_ref)

    @pl.when(t == num_tiles)
    def _drain():
        prev_ref[...] = cur_ref[...]
        for r in ret_items:
            r()

    @pl.when(t < num_tiles)
    def _step():
        prev_ref[...] = cur_ref[...]
        h_ref[...] = hn_ref[...]
        cos_ref[...] = cn_ref[...]
        sin_ref[...] = sn_ref[...]
        h = h_ref[...]
        cos = cos_ref[...]
        sin_signed = sin_ref[...]
        lane = lax.broadcasted_iota(jnp.int32, (1, LANES), 1)
        first_half = (lane % RET_DK) < (RET_DK // 2)

        def rope(p):
            partner = jnp.where(first_half,
                                pltpu.roll(p, LANES - RET_DK // 2, 1),
                                pltpu.roll(p, RET_DK // 2, 1))
            return p * cos + partner * sin_signed

        def proj_q():
            pq = _dot(h, w_ref[:, COL_Q:COL_K])
            for i in range(RET_QK_WIDTH // LANES):
                sl = slice(i * LANES, (i + 1) * LANES)
                cur_ref[:, COL_Q + i * LANES:COL_Q + (i + 1) * LANES] = rope(pq[:, sl]).astype(BF16)

        def proj_k():
            pk = _dot(h, w_ref[:, COL_K:COL_V])
            for i in range(RET_QK_WIDTH // LANES):
                sl = slice(i * LANES, (i + 1) * LANES)
                cur_ref[:, COL_K + i * LANES:COL_K + (i + 1) * LANES] = (
                    rope(pk[:, sl]) * (RET_DK ** -0.5)).astype(BF16)

        def proj_v():
            cur_ref[:, COL_V:COL_GR] = _dot(h, w_ref[:, COL_V:COL_GR]).astype(BF16)

        def proj_gate():
            cur_ref[:, COL_GR:COL_U] = _silu(_dot(h, w_ref[:, COL_GR:COL_U])).astype(BF16)

        def proj_u():
            pu = _dot(h, w_ref[:, COL_U:COL_GS])
            for j in range(LANE_TILES):
                us_ref[j] = pu[:, j * LANES:(j + 1) * LANES]
            for j in range(LANE_TILES):
                for k in range(S5_TAPS):
                    u_ref[j, :, k * LANES:(k + 1) * LANES] = us_ref[
                        j, pl.ds(k, us_ref.shape[1] // S5_TAPS, stride=S5_TAPS), :].astype(BF16)

        def proj_gs():
            gs_ref[...] = _silu(_dot(h, w_ref[:, COL_GS:IN_COLS])).astype(BF16)

        proj_items = [proj_q, proj_k, proj_v, proj_gate, proj_u, proj_gs]
        per = -(-len(ret_items) // len(proj_items))
        for i, item in enumerate(proj_items):
            for r in ret_items[i * per:(i + 1) * per]:
                r()
            item()
        _token_prologue(x_ref, pos_ref, inv_ref, g_ref, hn_ref, cn_ref, sn_ref)
        for src_ref, dst_ref in zip(f32_refs, bf16_refs):
            dst_ref[...] = src_ref[...].astype(BF16)


def _proj_ret(x, pos3, inv, g, w, gn, later):
    B, L, D = x.shape
    tm = TM_PROJ
    nt = L // tm
    n = B * nt
    cur = lambda t: jnp.minimum(t, n - 1)
    nxt = lambda t: jnp.minimum(t + 1, n - 1)
    lag = lambda t: jnp.maximum(t - 1, 0)
    first = lambda t: 0 * t
    row = lambda width, tile: pl.BlockSpec((None, tm, width), lambda t: (tile(t) // nt, tile(t) % nt, 0))
    const = lambda shape: pl.BlockSpec(shape, lambda t: (0,) * len(shape))
    slab = lambda m: pl.BlockSpec((m.shape[0] // n, m.shape[1]), lambda t: (cur(t), 0))
    outs = pl.pallas_call(
        functools.partial(_proj_ret_kernel, tiles_per_seq=nt, num_tiles=n, num_casts=len(later)),
        grid=(n + 1,),
        in_specs=[row(D, first), row(1, first), row(D, nxt), row(1, nxt), const((1, LANES)), const((1, D)),
                  const((D, IN_COLS)), const((1, RET_V_WIDTH))] + [slab(m) for m in later],
        out_specs=[pl.BlockSpec((None, LANE_TILES, tm // S5_TAPS, S5_TAPS * LANES),
                                lambda t: (cur(t) // nt, 0, cur(t) % nt, 0)),
                   row(S5_WIDTH, cur), row(RET_V_WIDTH, lag)] + [slab(m) for m in later],
        out_shape=[jax.ShapeDtypeStruct((B, LANE_TILES, L // S5_TAPS, S5_TAPS * LANES), BF16),
                   jax.ShapeDtypeStruct((B, L, S5_WIDTH), BF16),
                   jax.ShapeDtypeStruct((B, L, RET_V_WIDTH), BF16)]
        + [jax.ShapeDtypeStruct(m.shape, BF16) for m in later],
        scratch_shapes=[pltpu.VMEM((tm, COL_U), BF16),
                        pltpu.VMEM((tm, COL_U), BF16),
                        pltpu.VMEM((tm, D), BF16), pltpu.VMEM((tm, LANES), F32), pltpu.VMEM((tm, LANES), F32),
                        pltpu.VMEM((tm, D), BF16), pltpu.VMEM((tm, LANES), F32), pltpu.VMEM((tm, LANES), F32),
                        pltpu.VMEM((LANE_TILES, tm, LANES), F32),
                        pltpu.VMEM((RET_HEADS // 2, LANES, 2 * RET_DV), F32),
                        pltpu.VMEM((RET_HEADS // 2, CHUNK, 2 * CHUNK), F32),
                        pltpu.VMEM((RET_HEADS // 2, CHUNK, 2 * RET_DV), F32),
                        pltpu.VMEM((RET_HEADS // 2, CHUNK, LANES), F32)],
        compiler_params=pltpu.CompilerParams(
            dimension_semantics=("arbitrary",), vmem_limit_bytes=VMEM_LIMIT),
        name="proj_ret",
    )(x, pos3, x, pos3, inv, g, w, gn, *later)
    return outs[:3], outs[3:]


def _retention_constants(decay_ref, qw_ref, kw_ref):
    i = lax.broadcasted_iota(jnp.int32, (CHUNK, 2 * CHUNK), 0).astype(F32)
    c = lax.broadcasted_iota(jnp.int32, (CHUNK, 2 * CHUNK), 1)
    j = (c % CHUNK).astype(F32)
    diff = i - j
    lane = lax.broadcasted_iota(jnp.int32, (CHUNK, LANES), 1)
    ik = lax.broadcasted_iota(jnp.int32, (CHUNK, LANES), 0).astype(F32)
    for p in range(RET_HEADS // 2):
        lg = jnp.where(c < CHUNK, LOG_G[2 * p], LOG_G[2 * p + 1])
        decay_ref[p] = jnp.where(diff >= 0.0, jnp.exp(lg * jnp.maximum(diff, 0.0)), 0.0)
        qw_ref[p] = jnp.exp(lg * (i + 1.0))
        lgk = jnp.where(lane < RET_DK, LOG_G[2 * p], LOG_G[2 * p + 1])
        kw_ref[p] = jnp.exp(lgk * (CHUNK - 1.0 - ik))


def _retention_items(src_ref, gn_ref, o_ref, state_ref, decay_ref, qw_ref, kw_ref):
    pairs = RET_HEADS // 2
    low = lax.broadcasted_iota(jnp.int32, (1, LANES), 1) < RET_DK
    srow = lax.broadcasted_iota(jnp.int32, (LANES, 2 * RET_DV), 0)
    scol = lax.broadcasted_iota(jnp.int32, (LANES, 2 * RET_DV), 1)
    own_block = (srow < RET_DK) == (scol < RET_DV)
    scol1 = lax.broadcasted_iota(jnp.int32, (1, 2 * RET_DV), 1)
    zero_v = jnp.zeros((CHUNK, RET_DV), BF16)

    def pair_step(c, p):
        rows = slice(c * CHUNK, (c + 1) * CHUNK)
        qt = src_ref[rows, COL_Q + p * LANES:COL_Q + (p + 1) * LANES]
        kt = src_ref[rows, COL_K + p * LANES:COL_K + (p + 1) * LANES]
        vp = src_ref[rows, COL_V + 2 * p * RET_DV:COL_V + 2 * (p + 1) * RET_DV]
        state = state_ref[p]
        cross = _dot(qt, state.astype(BF16))
        kw = (kt.astype(F32) * kw_ref[p]).astype(BF16)
        upd = lax.dot_general(kw, vp, (((0,), (0,)), ((), ())), preferred_element_type=F32)
        chunk_decay = jnp.where(scol1 < RET_DV,
                                math.exp(LOG_G[2 * p] * CHUNK), math.exp(LOG_G[2 * p + 1] * CHUNK))
        state_ref[p] = state * chunk_decay + jnp.where(own_block, upd, 0.0)

        zero_k = jnp.zeros_like(kt)
        k_rows = jnp.concatenate([jnp.where(low, kt, zero_k), jnp.where(low, zero_k, kt)], axis=0)
        s = lax.dot_general(qt, k_rows, (((1,), (1,)), ((), ())), preferred_element_type=F32)
        pm = (s * decay_ref[p]).astype(BF16)
        v_diag = jnp.concatenate(
            [jnp.concatenate([vp[:, 0:RET_DV], zero_v], axis=1),
             jnp.concatenate([zero_v, vp[:, RET_DV:2 * RET_DV]], axis=1)], axis=0)
        o2 = _dot(pm, v_diag) + qw_ref[p] * cross
        for e in range(2):
            h = 2 * p + e
            cols = slice(h * RET_DV, (h + 1) * RET_DV)
            o = o2[:, e * RET_DV:(e + 1) * RET_DV]
            mu = jnp.mean(o, axis=-1, keepdims=True)
            oc = o - mu
            var = jnp.mean(oc * oc, axis=-1, keepdims=True)
            on = oc * lax.rsqrt(var + EPS) * gn_ref[:, cols]
            gate = src_ref[rows, COL_GR + h * RET_DV:COL_GR + (h + 1) * RET_DV].astype(F32)
            o_ref[rows, cols] = (on * gate).astype(BF16)

    return [functools.partial(pair_step, c, p)
            for c in range(src_ref.shape[0] // CHUNK) for p in range(pairs)]


def _s5_scan_kernel(u_ref, wb_ref, consts_ref, u2_ref, wck_ref, y_ref, *scratch, tiles_per_seq):
    st_refs = scratch[0:3]
    sb_ref, ys_ref, carry_ref = scratch[3:]
    n = pl.program_id(0)
    s = S5_TAPS
    nblk = u_ref.shape[0]
    half = TILE_STATES

    @pl.when(n == 0)
    def _init():
        for ref in scratch[0:3]:
            ref[...] = jnp.zeros_like(ref)

    @pl.when((n == 0) | ((n - 1) % tiles_per_seq == 0))
    def _reset():
        carry_ref[...] = jnp.zeros_like(carry_ref)

    ncol = 2 * LANES

    def increments(slot):
        st_ref = st_refs[slot]

        def piece(c):
            z = _dot(u_ref[...], wb_ref[:, c * ncol:(c + 1) * ncol])
            for i in range(ncol // LANES):
                st_ref[c * (ncol // LANES) + i] = z[:, i * LANES:(i + 1) * LANES]

        return [functools.partial(piece, c) for c in range(2 * half // ncol)]

    ntile = half // LANES

    def scan(slot, row_pieces):
        st_ref = st_refs[slot]
        run = S5_RUN
        span = run * SUBLANES
        row = lax.broadcasted_iota(jnp.int32, (SUBLANES, LANES), 0)
        last = SUBLANES - 1
        groups = nblk // span
        per = -(-groups // row_pieces)
        tiles_per_piece = S5_SCAN_LANES // LANES

        def tile_group(t, g, carry):
            lanes = slice(t * LANES, (t + 1) * LANES)
            const = lambda idx: (consts_ref[idx, 0, :, lanes], consts_ref[idx, 1, :, lanes])
            cr, ci = carry
            rows = [pl.ds(g * span + j, SUBLANES, stride=run) for j in range(run)]
            loc = [(st_ref[t, rows[0], :], st_ref[ntile + t, rows[0], :])]
            for j in range(1, run):
                pr, pi = _cmul(*const(0), *loc[-1])
                loc.append((pr + st_ref[t, rows[j], :], pi + st_ref[ntile + t, rows[j], :]))
            fr, fi = loc[-1]
            for idx, d in enumerate((1, 2, 4)):
                pr, pi = _cmul(*const(1 + idx), pltpu.roll(fr, d, 0), pltpu.roll(fi, d, 0))
                fr, fi = fr + pr, fi + pi
            pr, pi = _cmul(*const(4), cr, ci)
            fr, fi = fr + pr, fi + pi
            er = jnp.where(row == 0, cr, pltpu.roll(fr, 1, 0))
            ei = jnp.where(row == 0, ci, pltpu.roll(fi, 1, 0))
            st_ref[t, rows[0], :] = er
            st_ref[ntile + t, rows[0], :] = ei
            for j in range(1, run):
                er, ei = _cmul(*const(0), er, ei)
                st_ref[t, rows[j], :] = loc[j - 1][0] + er
                st_ref[ntile + t, rows[j], :] = loc[j - 1][1] + ei
            return (jnp.broadcast_to(fr[last:, :], (SUBLANES, LANES)),
                    jnp.broadcast_to(fi[last:, :], (SUBLANES, LANES)))

        def piece(q, i, carries):
            tiles = range(q * tiles_per_piece, (q + 1) * tiles_per_piece)
            if i == 0:
                for t in tiles:
                    carries[t] = (carry_ref[:, t * LANES:(t + 1) * LANES],
                                  carry_ref[:, half + t * LANES:half + (t + 1) * LANES])
            for g in range(i * per, min((i + 1) * per, groups)):
                for t in tiles:
                    carries[t] = tile_group(t, g, carries[t])
            if i == row_pieces - 1:
                for t in tiles:
                    carry_ref[:, t * LANES:(t + 1) * LANES] = carries[t][0]
                    carry_ref[:, half + t * LANES:half + (t + 1) * LANES] = carries[t][1]

        carries = {}
        return [functools.partial(piece, q, i, carries)
                for q in range(ntile // tiles_per_piece) for i in range(row_pieces)]

    def outputs(slot):
        taps_per_piece = ncol // LANES

        def cast():
            for t in range(2 * ntile):
                sb_ref[:, t * LANES:(t + 1) * LANES] = st_refs[slot][t].astype(BF16)

        def piece(c):
            lhs = jnp.concatenate([sb_ref[...], u2_ref[...]], axis=1)
            yall = _dot(lhs, wck_ref[:, c * ncol:(c + 1) * ncol])
            for i in range(taps_per_piece):
                tap = c * taps_per_piece + i
                ys_ref[pl.ds(tap, nblk, stride=s), :] = yall[:, i * LANES:(i + 1) * LANES]

        def finish():
            y_ref[...] = jax.nn.gelu(ys_ref[...]).astype(BF16)

        return [cast] + [functools.partial(piece, c) for c in range(s * LANES // ncol)] + [finish]

    for phase in range(3):
        @pl.when(n % 3 == phase)
        def _steps(phase=phase):
            mxu = increments(phase) + outputs((phase + 1) % 3)
            vpu = scan((phase + 2) % 3, 3)
            for i, piece in enumerate(mxu):
                piece()
                if i < len(vpu):
                    vpu[i]()
            for piece in vpu[len(mxu):]:
                piece()


def _s5_scan(u, wb, wck, consts):
    B = u.shape[0]
    s = S5_TAPS
    L = u.shape[2] * s
    tm = TM_S5
    nt = L // tm
    ntiles = LANE_TILES * B * nt
    tile_of = lambda n, lag: jnp.clip(n - lag, 0, ntiles - 1)
    lane_tile = lambda n, lag: tile_of(n, lag) // (B * nt)

    def tok(lag, rows, width):
        def index(n):
            i = tile_of(n, lag)
            return ((i // nt) % B, i // (B * nt), i % nt, 0)
        return pl.BlockSpec((None, None, rows, width), index)

    def par(lag, *shape):
        return pl.BlockSpec((None,) + shape, lambda n: (lane_tile(n, lag),) + (0,) * len(shape))

    nblk = tm // s
    return pl.pallas_call(
        functools.partial(_s5_scan_kernel, tiles_per_seq=nt),
        grid=(ntiles + 2,),
        in_specs=[tok(0, nblk, s * LANES), par(0, s * LANES, 2 * TILE_STATES),
                  par(1, 5, 2, SUBLANES, TILE_STATES), tok(2, nblk, s * LANES),
                  par(2, 2 * TILE_STATES + s * LANES, s * LANES)],
        out_specs=tok(2, tm, LANES),
        out_shape=jax.ShapeDtypeStruct((B, LANE_TILES, L, LANES), BF16),
        scratch_shapes=([pltpu.VMEM((2 * TILE_STATES // LANES, nblk, LANES), F32)] * 3
                        + [pltpu.VMEM((nblk, 2 * TILE_STATES), BF16), pltpu.VMEM((tm, LANES), F32),
                           pltpu.VMEM((SUBLANES, 2 * TILE_STATES), F32)]),
        compiler_params=pltpu.CompilerParams(
            dimension_semantics=("arbitrary",), vmem_limit_bytes=VMEM_LIMIT),
        name="s5_scan",
    )(u, wb, consts, u, wck)


def _out_xattn_kernel(x_ref, ret_ref, y_ref, gs_ref, gw_ref, gb_ref, wout_ref, g2_ref, wq_ref,
                      ka_ref, va_ref, wo_ref, gf_ref, o_ref):
    y = jnp.concatenate([y_ref[j] for j in range(LANE_TILES)], axis=1)
    z = _dot(y, gw_ref[...]) + gb_ref[...]
    ssm = (y.astype(F32) * _sigmoid(z) * gs_ref[...].astype(F32)).astype(BF16)
    x1 = (x_ref[...] + _dot(ret_ref[...], wout_ref[0:RET_V_WIDTH, :])
          + _dot(ssm, wout_ref[RET_V_WIDTH:RET_V_WIDTH + S5_WIDTH, :]))
    h2 = _rms(x1, g2_ref[...]).astype(BF16)
    qa = (_dot(h2, wq_ref[...]) * (XA_DH ** -0.5)).astype(BF16)
    heads = []
    for h in range(XA_HEADS):
        cols = slice(h * XA_DH, (h + 1) * XA_DH)
        s = lax.dot_general(qa[:, cols], ka_ref[:, cols], (((1,), (1,)), ((), ())),
                            preferred_element_type=F32)
        e = jnp.exp(s - jnp.max(s, axis=-1, keepdims=True))
        l = jnp.sum(e, axis=-1, keepdims=True)
        heads.append((_dot(e.astype(BF16), va_ref[:, cols]) * (1.0 / l)).astype(BF16))
    o = jnp.concatenate(heads, axis=-1)
    x2 = x1 + _dot(o, wo_ref[...])
    o_ref[...] = _rms(x2, gf_ref[...])


def _out_xattn(x, ret, y, gs, gw, gb, wout, g2, wq, ka, va, wo, gf):
    B, L, D = x.shape
    tm = TM_OUT
    row = pl.BlockSpec((None, tm, D), lambda b, t: (b, t, 0))
    const = lambda shape: pl.BlockSpec(shape, lambda b, t: (0,) * len(shape))
    mem = pl.BlockSpec((None, MEM_LEN, D), lambda b, t: (b, 0, 0))
    ytile = pl.BlockSpec((None, LANE_TILES, tm, LANES), lambda b, t: (b, 0, t, 0))
    return pl.pallas_call(
        _out_xattn_kernel,
        grid=(B, L // tm),
        in_specs=[row, row, ytile, row, const((D, D)), const((1, D)), const(wout.shape), const((1, D)),
                  const((D, D)), mem, mem, const((D, D)), const((1, D))],
        out_specs=row,
        out_shape=jax.ShapeDtypeStruct((B, L, D), F32),
        compiler_params=pltpu.CompilerParams(
            dimension_semantics=("arbitrary", "arbitrary"), vmem_limit_bytes=VMEM_LIMIT),
        name="out_xattn",
    )(x, ret, y, gs, gw, gb, wout, g2, wq, ka, va, wo, gf)


def _rope_inv_lanes():
    half = RET_DK // 2
    inv = ROPE_BASE ** (-np.arange(half, dtype=np.float64) / half)
    return jnp.asarray(np.tile(inv, LANES // half).reshape(1, LANES), F32)


def kernel(x, mem, positions, norm1_g, w_in, ret_gn_g, s5_a_re, s5_a_im, s5_log_dt, s5_b_re, s5_b_im, s5_c_re, s5_c_im, s5_d, s5_glu_w, s5_glu_b, w_out, norm2_g, norm_mem_g, xa_wq, xa_wk, xa_wv, xa_wo, norm_f_g):
    B, L, D = x.shape
    l = 0
    rowvec = lambda v: v.reshape(1, -1)

    wb, wck, consts, win = _s5_prep(s5_a_re[l], s5_a_im[l], s5_log_dt[l], s5_b_re[l], s5_b_im[l],
                                    s5_c_re[l], s5_c_im[l], s5_d[l], w_in[l])
    (u, g_s5, ret), (glu_w, wout, wq, wk, wv, wo) = _proj_ret(
        x, positions.reshape(B, L, 1), _rope_inv_lanes(), rowvec(norm1_g[l]), win, rowvec(ret_gn_g[l]),
        [s5_glu_w[l], w_out[l], xa_wq[l], xa_wk[l], xa_wv[l], xa_wo[l]])
    ka, va = _mem_kv(mem, rowvec(norm_mem_g[l]), wk, wv)
    y = _s5_scan(u, wb, wck, consts)
    return _out_xattn(x, ret, y, g_s5, glu_w, rowvec(s5_glu_b[l]), wout,
                      rowvec(norm2_g[l]), wq, ka, va, wo, rowvec(norm_f_g))
```

```python
import functools
import math

import numpy as np
import jax
import jax.numpy as jnp
from jax import lax
from jax.experimental import pallas as pl
from jax.experimental.pallas import tpu as pltpu

F32 = jnp.float32
BF16 = jnp.bfloat16

D_MODEL = 1024
MEM_LEN = 256
EPS = 1e-6
ROPE_BASE = 10000.0

RET_HEADS = 8
RET_QK_WIDTH = 512
RET_V_WIDTH = 1024
RET_DK = 64
RET_DV = 128
CHUNK = 128

S5_WIDTH = 1024
S5_GROUP = 16
S5_GROUPS = 64
S5_STATE = 64
S5_NSTATE = S5_GROUPS * S5_STATE

IN_COLS = 5120
COL_Q, COL_K, COL_V, COL_GR, COL_U, COL_GS = 0, 512, 1024, 2048, 3072, 4096

XA_HEADS = 4
XA_DH = 256

LANES = 128
SUBLANES = 8
VMEM_LIMIT = 48 * 1024 * 1024

LANE_TILES = S5_WIDTH // LANES
TILE_GROUPS = LANES // S5_GROUP
TILE_STATES = S5_NSTATE // LANE_TILES
S5_TAPS = 4
S5_SCAN_LANES = 2 * LANES
S5_RUN = 4

LOG_G = tuple(math.log1p(-(2.0 ** (-5.0 - h))) for h in range(RET_HEADS))

TM_PROJ = 512
TM_S5 = 4096
TM_OUT = 1024


def _rms(x, g):
    ms = jnp.mean(x * x, axis=-1, keepdims=True)
    return x * lax.rsqrt(ms + EPS) * g


def _sigmoid(z):
    return 1.0 / (1.0 + jnp.exp(-z))


def _silu(g):
    return g * _sigmoid(g)


def _dot(a, b):
    return jnp.dot(a, b, preferred_element_type=F32)


def _cmul(ar, ai, br, bi):
    return ar * br - ai * bi, ar * bi + ai * br


def _s5_discretise(ar, ai, ldt):
    dt = jnp.exp(ldt)
    mag = jnp.exp(ar * dt)
    p_re = mag * jnp.cos(ai * dt)
    p_im = mag * jnp.sin(ai * dt)
    den = ar * ar + ai * ai
    nr, ni = p_re - 1.0, p_im
    f_re = (nr * ar + ni * ai) / den
    f_im = (ni * ar - nr * ai) / den
    return p_re, p_im, f_re, f_im


def _powers(p_re, p_im, n):
    pw = [(jnp.ones_like(p_re), jnp.zeros_like(p_im))]
    for _ in range(n):
        pw.append(_cmul(pw[-1][0], pw[-1][1], p_re, p_im))
    return pw


def _block_diag(x):
    tiled = jnp.concatenate([x] * TILE_GROUPS, axis=1)
    r = lax.broadcasted_iota(jnp.int32, tiled.shape, 0)
    c = lax.broadcasted_iota(jnp.int32, tiled.shape, 1)
    return jnp.where(r // S5_GROUP == c // S5_STATE, tiled, 0.0)


def _s5_prep_kernel(a_ref, ag_ref, b_ref, c_ref, d_ref, win_ref, wb_ref, wck_ref, consts_ref, win16_ref):
    s = S5_TAPS
    half = TILE_STATES
    def per_row(v):
        return jnp.broadcast_to(v[:, None, :], (TILE_GROUPS, S5_GROUP, S5_STATE)).reshape(LANES, S5_STATE)

    g_re, g_im, gf_re, gf_im = _s5_discretise(ag_ref[0], ag_ref[1], ag_ref[2])
    gw = [(per_row(r), per_row(i)) for r, i in _powers(g_re, g_im, s)]
    bbr, bbi = _cmul(per_row(gf_re), per_row(gf_im), b_ref[0], b_ref[1])
    for k in range(s):
        wr, wi = _cmul(bbr, bbi, *gw[s - 1 - k])
        wb_ref[k * LANES:(k + 1) * LANES, 0:half] = _block_diag(wr).astype(BF16)
        wb_ref[k * LANES:(k + 1) * LANES, half:2 * half] = _block_diag(wi).astype(BF16)

    bb = jnp.concatenate([bbr, bbi], axis=1)
    r = lax.broadcasted_iota(jnp.int32, (LANES, LANES), 0)
    c = lax.broadcasted_iota(jnp.int32, (LANES, LANES), 1)
    same_group = r // S5_GROUP == c // S5_GROUP
    taps = []
    for d in range(s + 1):
        er, ei = _cmul(c_ref[0], c_ref[1], *gw[d])
        if d >= 1:
            cwt = jnp.concatenate([_block_diag(er), -_block_diag(ei)], axis=1)
            wck_ref[0:2 * half, (d - 1) * LANES:d * LANES] = cwt.T.astype(BF16)
        if d < s:
            t = lax.dot_general(bb, jnp.concatenate([er, -ei], axis=1), (((1,), (1,)), ((), ())),
                                preferred_element_type=F32, precision=lax.Precision.HIGHEST)
            if d == 0:
                t = t + jnp.where(r == c, d_ref[...], 0.0)
            taps.append(jnp.where(same_group, t, 0.0))
    zero = jnp.zeros((LANES, LANES), BF16)
    for m in range(s):
        for i in range(s):
            blk = taps[i - m].astype(BF16) if m <= i else zero
            wck_ref[2 * half + m * LANES:2 * half + (m + 1) * LANES, i * LANES:(i + 1) * LANES] = blk

    p_re, p_im, _, _ = _s5_discretise(a_ref[0:1, :], a_ref[1:2, :], a_ref[2:3, :])
    pw = _powers(p_re, p_im, s)
    shape = (SUBLANES, half)
    qr = _powers(pw[s][0], pw[s][1], S5_RUN)
    consts_ref[0, 0] = jnp.broadcast_to(qr[1][0], shape)
    consts_ref[0, 1] = jnp.broadcast_to(qr[1][1], shape)
    rp = _powers(qr[S5_RUN][0], qr[S5_RUN][1], SUBLANES)
    row = lax.broadcasted_iota(jnp.int32, shape, 0)
    for idx, d in enumerate((1, 2, 4)):
        consts_ref[1 + idx, 0] = jnp.where(row >= d, rp[d][0], 0.0)
        consts_ref[1 + idx, 1] = jnp.where(row >= d, rp[d][1], 0.0)
    cr = jnp.zeros(shape, F32)
    ci = jnp.zeros(shape, F32)
    for k in range(SUBLANES):
        cr = jnp.where(row == k, rp[k + 1][0], cr)
        ci = jnp.where(row == k, rp[k + 1][1], ci)
    consts_ref[4, 0] = cr
    consts_ref[4, 1] = ci

    win16_ref[...] = win_ref[...].astype(BF16)


def _s5_prep(a_re, a_im, log_dt, b_re, b_im, c_re, c_im, d, w_in):
    s = S5_TAPS
    ldt = jnp.broadcast_to(log_dt[:, None], (S5_GROUPS, S5_STATE))
    ag = jnp.stack([a_re, a_im, ldt])
    a3 = ag.reshape(3, S5_NSTATE)
    bt = jnp.stack([b_re, b_im]).transpose(0, 1, 3, 2).reshape(2, S5_WIDTH, S5_STATE)
    ct = jnp.stack([c_re, c_im]).reshape(2, S5_WIDTH, S5_STATE)
    tile = lambda *shape: pl.BlockSpec((None,) + shape, lambda j: (j,) + (0,) * len(shape))
    return pl.pallas_call(
        _s5_prep_kernel,
        grid=(LANE_TILES,),
        in_specs=[pl.BlockSpec((3, TILE_STATES), lambda j: (0, j)),
                  pl.BlockSpec((3, TILE_GROUPS, S5_STATE), lambda j: (0, j, 0)),
                  pl.BlockSpec((2, LANES, S5_STATE), lambda j: (0, j, 0)),
                  pl.BlockSpec((2, LANES, S5_STATE), lambda j: (0, j, 0)),
                  pl.BlockSpec((1, LANES), lambda j: (0, j)),
                  pl.BlockSpec((D_MODEL // LANE_TILES, IN_COLS), lambda j: (j, 0))],
        out_specs=(tile(s * LANES, 2 * TILE_STATES),
                   tile(2 * TILE_STATES + s * LANES, s * LANES),
                   tile(5, 2, SUBLANES, TILE_STATES),
                   pl.BlockSpec((D_MODEL // LANE_TILES, IN_COLS), lambda j: (j, 0))),
        out_shape=(jax.ShapeDtypeStruct((LANE_TILES, s * LANES, 2 * TILE_STATES), BF16),
                   jax.ShapeDtypeStruct((LANE_TILES, 2 * TILE_STATES + s * LANES, s * LANES), BF16),
                   jax.ShapeDtypeStruct((LANE_TILES, 5, 2, SUBLANES, TILE_STATES), F32),
                   jax.ShapeDtypeStruct((D_MODEL, IN_COLS), BF16)),
        compiler_params=pltpu.CompilerParams(
            dimension_semantics=("arbitrary",), vmem_limit_bytes=VMEM_LIMIT),
        name="s5_prep",
    )(a3, ag, bt, ct, d.reshape(1, S5_WIDTH), w_in)


def _mem_kv_kernel(mem_ref, g_ref, wk_ref, wv_ref, k_ref, v_ref):
    m = _rms(mem_ref[...], g_ref[...]).astype(BF16)
    k_ref[...] = _dot(m, wk_ref[...]).astype(BF16)
    v_ref[...] = _dot(m, wv_ref[...]).astype(BF16)


def _mem_kv(mem, g, wk, wv):
    B, M, D = mem.shape
    return pl.pallas_call(
        _mem_kv_kernel,
        grid=(B,),
        in_specs=[pl.BlockSpec((None, M, D), lambda b: (b, 0, 0)),
                  pl.BlockSpec((1, D), lambda b: (0, 0)),
                  pl.BlockSpec((D, D), lambda b: (0, 0)),
                  pl.BlockSpec((D, D), lambda b: (0, 0))],
        out_specs=(pl.BlockSpec((None, M, D), lambda b: (b, 0, 0)),
                   pl.BlockSpec((None, M, D), lambda b: (b, 0, 0))),
        out_shape=(jax.ShapeDtypeStruct((B, M, D), BF16),
                   jax.ShapeDtypeStruct((B, M, D), BF16)),
        compiler_params=pltpu.CompilerParams(
            dimension_semantics=("arbitrary",), vmem_limit_bytes=VMEM_LIMIT),
        name="mem_kv",
    )(mem, g, wk, wv)


def _token_prologue(x_ref, pos_ref, inv_ref, g_ref, h_ref, cos_ref, sin_ref):
    h_ref[...] = _rms(x_ref[...], g_ref[...]).astype(BF16)
    lane = lax.broadcasted_iota(jnp.int32, (1, LANES), 1)
    nfreq = RET_DK // 2
    group = lane // nfreq
    nblock = LANES // nfreq
    rows = x_ref.shape[0] // nblock
    pos = pos_ref[...].astype(F32)
    pos4 = jnp.zeros((rows, LANES), F32)
    for k in range(nblock):
        pos4 = jnp.where(group == k, pos[k * rows:(k + 1) * rows, :], pos4)
    ang4 = pos4 * inv_ref[...]

    def spread(t4):
        blocks = []
        for k in range(nblock):
            m = jnp.where(group == k, t4, 0.0)
            m = m + pltpu.roll(m, nfreq, 1)
            blocks.append(m + pltpu.roll(m, 2 * nfreq, 1))
        return jnp.concatenate(blocks, axis=0)

    first_half = (lane % RET_DK) < nfreq
    sin = spread(jnp.sin(ang4))
    cos_ref[...] = spread(jnp.cos(ang4))
    sin_ref[...] = jnp.where(first_half, -sin, sin)


def _proj_ret_kernel(x0_ref, pos0_ref, x_ref, pos_ref, inv_ref, g_ref, w_ref, gn_ref, *rest,
                     tiles_per_seq, num_tiles, num_casts):
    f32_refs, rest = rest[:num_casts], rest[num_casts:]
    (u_ref, gs_ref, ret_ref), rest = rest[:3], rest[3:]
    bf16_refs, rest = rest[:num_casts], rest[num_casts:]
    (qa_ref, qb_ref, ha_ref, ca_ref, sa_ref, hb_ref, cb_ref, sb_ref, us_ref,
     state_ref, decay_ref, qw_ref, kw_ref) = rest
    tiles = (qa_ref, qb_ref)
    norms = ((ha_ref, ca_ref, sa_ref), (hb_ref, cb_ref, sb_ref))
    t = pl.program_id(0)

    @pl.when(t == 0)
    def _init():
        _retention_constants(decay_ref, qw_ref, kw_ref)
        qb_ref[...] = jnp.zeros_like(qb_ref)
        _token_prologue(x0_ref, pos0_ref, inv_ref, g_ref, *norms[0])

    @pl.when((t == 0) | (t % tiles_per_seq == 1))
    def _reset():
        state_ref[...] = jnp.zeros_like(state_ref)

    def retention_of(parity):
        return _retention_items(tiles[parity], gn_ref, ret_ref, state_ref, decay_ref, qw_ref, kw_ref)

    @pl.when(t == num_tiles)
    def _drain():
        for r in retention_of((num_tiles - 1) % 2):
            r()

    def step(parity):
        cur_ref = tiles[parity]
        ret_items = retention_of(1 - parity)
        h_ref, cos_ref, sin_ref = norms[parity]
        hn_ref, cn_ref, sn_ref = norms[1 - parity]
        h = h_ref[...]
        cos = cos_ref[...]
        sin_signed = sin_ref[...]
        lane = lax.broadcasted_iota(jnp.int32, (1, LANES), 1)
        first_half = (lane % RET_DK) < (RET_DK // 2)

        def rope(p):
            partner = jnp.where(first_half,
                                pltpu.roll(p, LANES - RET_DK // 2, 1),
                                pltpu.roll(p, RET_DK // 2, 1))
            return p * cos + partner * sin_signed

        def proj_q():
            pq = _dot(h, w_ref[:, COL_Q:COL_K])
            for i in range(RET_QK_WIDTH // LANES):
                sl = slice(i * LANES, (i + 1) * LANES)
                cur_ref[:, COL_Q + i * LANES:COL_Q + (i + 1) * LANES] = rope(pq[:, sl]).astype(BF16)

        def proj_k():
            pk = _dot(h, w_ref[:, COL_K:COL_V])
            for i in range(RET_QK_WIDTH // LANES):
                sl = slice(i * LANES, (i + 1) * LANES)
                cur_ref[:, COL_K + i * LANES:COL_K + (i + 1) * LANES] = (
                    rope(pk[:, sl]) * (RET_DK ** -0.5)).astype(BF16)

        def proj_v():
            cur_ref[:, COL_V:COL_GR] = _dot(h, w_ref[:, COL_V:COL_GR]).astype(BF16)

        def proj_gate():
            cur_ref[:, COL_GR:COL_U] = _silu(_dot(h, w_ref[:, COL_GR:COL_U])).astype(BF16)

        def proj_u():
            pu = _dot(h, w_ref[:, COL_U:COL_GS])
            for j in range(LANE_TILES):
                us_ref[j] = pu[:, j * LANES:(j + 1) * LANES]
            for j in range(LANE_TILES):
                for k in range(S5_TAPS):
                    u_ref[j, :, k * LANES:(k + 1) * LANES] = us_ref[
                        j, pl.ds(k, us_ref.shape[1] // S5_TAPS, stride=S5_TAPS), :].astype(BF16)

        def proj_gs():
            gs_ref[...] = _silu(_dot(h, w_ref[:, COL_GS:IN_COLS])).astype(BF16)

        proj_items = [proj_q, proj_k, proj_v, proj_gate, proj_u, proj_gs]
        per = -(-len(ret_items) // len(proj_items))
        for i, item in enumerate(proj_items):
            for r in ret_items[i * per:(i + 1) * per]:
                r()
            item()
        _token_prologue(x_ref, pos_ref, inv_ref, g_ref, hn_ref, cn_ref, sn_ref)
        for src_ref, dst_ref in zip(f32_refs, bf16_refs):
            dst_ref[...] = src_ref[...].astype(BF16)

    for parity in range(2):
        pl.when((t < num_tiles) & (t % 2 == parity))(functools.partial(step, parity))


def _proj_ret(x, pos3, inv, g, w, gn, later):
    B, L, D = x.shape
    tm = TM_PROJ
    nt = L // tm
    n = B * nt
    cur = lambda t: jnp.minimum(t, n - 1)
    nxt = lambda t: jnp.minimum(t + 1, n - 1)
    lag = lambda t: jnp.maximum(t - 1, 0)
    first = lambda t: 0 * t
    row = lambda width, tile: pl.BlockSpec((None, tm, width), lambda t: (tile(t) // nt, tile(t) % nt, 0))
    const = lambda shape: pl.BlockSpec(shape, lambda t: (0,) * len(shape))
    slab = lambda m: pl.BlockSpec((m.shape[0] // n, m.shape[1]), lambda t: (cur(t), 0))
    outs = pl.pallas_call(
        functools.partial(_proj_ret_kernel, tiles_per_seq=nt, num_tiles=n, num_casts=len(later)),
        grid=(n + 1,),
        in_specs=[row(D, first), row(1, first), row(D, nxt), row(1, nxt), const((1, LANES)), const((1, D)),
                  const((D, IN_COLS)), const((1, RET_V_WIDTH))] + [slab(m) for m in later],
        out_specs=[pl.BlockSpec((None, LANE_TILES, tm // S5_TAPS, S5_TAPS * LANES),
                                lambda t: (cur(t) // nt, 0, cur(t) % nt, 0)),
                   row(S5_WIDTH, cur), row(RET_V_WIDTH, lag)] + [slab(m) for m in later],
        out_shape=[jax.ShapeDtypeStruct((B, LANE_TILES, L // S5_TAPS, S5_TAPS * LANES), BF16),
                   jax.ShapeDtypeStruct((B, L, S5_WIDTH), BF16),
                   jax.ShapeDtypeStruct((B, L, RET_V_WIDTH), BF16)]
        + [jax.ShapeDtypeStruct(m.shape, BF16) for m in later],
        scratch_shapes=[pltpu.VMEM((tm, COL_U), BF16),
                        pltpu.VMEM((tm, COL_U), BF16),
                        pltpu.VMEM((tm, D), BF16), pltpu.VMEM((tm, LANES), F32), pltpu.VMEM((tm, LANES), F32),
                        pltpu.VMEM((tm, D), BF16), pltpu.VMEM((tm, LANES), F32), pltpu.VMEM((tm, LANES), F32),
                        pltpu.VMEM((LANE_TILES, tm, LANES), F32),
                        pltpu.VMEM((RET_HEADS // 2, LANES, 2 * RET_DV), F32),
                        pltpu.VMEM((RET_HEADS // 2, CHUNK, 2 * CHUNK), F32),
                        pltpu.VMEM((RET_HEADS // 2, CHUNK, 2 * RET_DV), F32),
                        pltpu.VMEM((RET_HEADS // 2, CHUNK, LANES), F32)],
        compiler_params=pltpu.CompilerParams(
            dimension_semantics=("arbitrary",), vmem_limit_bytes=VMEM_LIMIT),
        name="proj_ret",
    )(x, pos3, x, pos3, inv, g, w, gn, *later)
    return outs[:3], outs[3:]


def _retention_constants(decay_ref, qw_ref, kw_ref):
    i = lax.broadcasted_iota(jnp.int32, (CHUNK, 2 * CHUNK), 0).astype(F32)
    c = lax.broadcasted_iota(jnp.int32, (CHUNK, 2 * CHUNK), 1)
    j = (c % CHUNK).astype(F32)
    diff = i - j
    lane = lax.broadcasted_iota(jnp.int32, (CHUNK, LANES), 1)
    ik = lax.broadcasted_iota(jnp.int32, (CHUNK, LANES), 0).astype(F32)
    for p in range(RET_HEADS // 2):
        lg = jnp.where(c < CHUNK, LOG_G[2 * p], LOG_G[2 * p + 1])
        decay_ref[p] = jnp.where(diff >= 0.0, jnp.exp(lg * jnp.maximum(diff, 0.0)), 0.0)
        qw_ref[p] = jnp.exp(lg * (i + 1.0))
        lgk = jnp.where(lane < RET_DK, LOG_G[2 * p], LOG_G[2 * p + 1])
        kw_ref[p] = jnp.exp(lgk * (CHUNK - 1.0 - ik))


def _retention_items(src_ref, gn_ref, o_ref, state_ref, decay_ref, qw_ref, kw_ref):
    pairs = RET_HEADS // 2
    low = lax.broadcasted_iota(jnp.int32, (1, LANES), 1) < RET_DK
    srow = lax.broadcasted_iota(jnp.int32, (LANES, 2 * RET_DV), 0)
    scol = lax.broadcasted_iota(jnp.int32, (LANES, 2 * RET_DV), 1)
    own_block = (srow < RET_DK) == (scol < RET_DV)
    scol1 = lax.broadcasted_iota(jnp.int32, (1, 2 * RET_DV), 1)
    zero_v = jnp.zeros((CHUNK, RET_DV), BF16)

    def pair_step(c, p):
        rows = slice(c * CHUNK, (c + 1) * CHUNK)
        qt = src_ref[rows, COL_Q + p * LANES:COL_Q + (p + 1) * LANES]
        kt = src_ref[rows, COL_K + p * LANES:COL_K + (p + 1) * LANES]
        vp = src_ref[rows, COL_V + 2 * p * RET_DV:COL_V + 2 * (p + 1) * RET_DV]
        state = state_ref[p]
        cross = _dot(qt, state.astype(BF16))
        kw = (kt.astype(F32) * kw_ref[p]).astype(BF16)
        upd = lax.dot_general(kw, vp, (((0,), (0,)), ((), ())), preferred_element_type=F32)
        chunk_decay = jnp.where(scol1 < RET_DV,
                                math.exp(LOG_G[2 * p] * CHUNK), math.exp(LOG_G[2 * p + 1] * CHUNK))
        state_ref[p] = state * chunk_decay + jnp.where(own_block, upd, 0.0)

        zero_k = jnp.zeros_like(kt)
        k_rows = jnp.concatenate([jnp.where(low, kt, zero_k), jnp.where(low, zero_k, kt)], axis=0)
        s = lax.dot_general(qt, k_rows, (((1,), (1,)), ((), ())), preferred_element_type=F32)
        pm = (s * decay_ref[p]).astype(BF16)
        v_diag = jnp.concatenate(
            [jnp.concatenate([vp[:, 0:RET_DV], zero_v], axis=1),
             jnp.concatenate([zero_v, vp[:, RET_DV:2 * RET_DV]], axis=1)], axis=0)
        o2 = _dot(pm, v_diag) + qw_ref[p] * cross
        for e in range(2):
            h = 2 * p + e
            cols = slice(h * RET_DV, (h + 1) * RET_DV)
            o = o2[:, e * RET_DV:(e + 1) * RET_DV]
            mu = jnp.mean(o, axis=-1, keepdims=True)
            oc = o - mu
            var = jnp.mean(oc * oc, axis=-1, keepdims=True)
            on = oc * lax.rsqrt(var + EPS) * gn_ref[:, cols]
            gate = src_ref[rows, COL_GR + h * RET_DV:COL_GR + (h + 1) * RET_DV].astype(F32)
            o_ref[rows, cols] = (on * gate).astype(BF16)

    return [functools.partial(pair_step, c, p)
            for c in range(src_ref.shape[0] // CHUNK) for p in range(pairs)]


def _s5_scan_kernel(u_ref, wb_ref, consts_ref, u2_ref, wck_ref, y_ref, *scratch, tiles_per_seq):
    st_refs = scratch[0:3]
    sb_ref, ys_ref, carry_ref = scratch[3:]
    n = pl.program_id(0)
    s = S5_TAPS
    nblk = u_ref.shape[0]
    half = TILE_STATES

    @pl.when(n == 0)
    def _init():
        for ref in scratch[0:3]:
            ref[...] = jnp.zeros_like(ref)

    @pl.when((n == 0) | ((n - 1) % tiles_per_seq == 0))
    def _reset():
        carry_ref[...] = jnp.zeros_like(carry_ref)

    ncol = 2 * LANES

    def increments(slot):
        st_ref = st_refs[slot]

        def piece(c):
            z = _dot(u_ref[...], wb_ref[:, c * ncol:(c + 1) * ncol])
            for i in range(ncol // LANES):
                st_ref[c * (ncol // LANES) + i] = z[:, i * LANES:(i + 1) * LANES]

        return [functools.partial(piece, c) for c in range(2 * half // ncol)]

    ntile = half // LANES

    def scan(slot, row_pieces):
        st_ref = st_refs[slot]
        run = S5_RUN
        span = run * SUBLANES
        row = lax.broadcasted_iota(jnp.int32, (SUBLANES, LANES), 0)
        last = SUBLANES - 1
        groups = nblk // span
        per = -(-groups // row_pieces)
        tiles_per_piece = S5_SCAN_LANES // LANES

        def tile_group(t, g, carry):
            lanes = slice(t * LANES, (t + 1) * LANES)
            const = lambda idx: (consts_ref[idx, 0, :, lanes], consts_ref[idx, 1, :, lanes])
            cr, ci = carry
            rows = [pl.ds(g * span + j, SUBLANES, stride=run) for j in range(run)]
            loc = [(st_ref[t, rows[0], :], st_ref[ntile + t, rows[0], :])]
            for j in range(1, run):
                pr, pi = _cmul(*const(0), *loc[-1])
                loc.append((pr + st_ref[t, rows[j], :], pi + st_ref[ntile + t, rows[j], :]))
            fr, fi = loc[-1]
            for idx, d in enumerate((1, 2, 4)):
                pr, pi = _cmul(*const(1 + idx), pltpu.roll(fr, d, 0), pltpu.roll(fi, d, 0))
                fr, fi = fr + pr, fi + pi
            pr, pi = _cmul(*const(4), cr, ci)
            fr, fi = fr + pr, fi + pi
            er = jnp.where(row == 0, cr, pltpu.roll(fr, 1, 0))
            ei = jnp.where(row == 0, ci, pltpu.roll(fi, 1, 0))
            st_ref[t, rows[0], :] = er
            st_ref[ntile + t, rows[0], :] = ei
            for j in range(1, run):
                er, ei = _cmul(*const(0), er, ei)
                st_ref[t, rows[j], :] = loc[j - 1][0] + er
                st_ref[ntile + t, rows[j], :] = loc[j - 1][1] + ei
            return (jnp.broadcast_to(fr[last:, :], (SUBLANES, LANES)),
                    jnp.broadcast_to(fi[last:, :], (SUBLANES, LANES)))

        def piece(q, i, carries):
            tiles = range(q * tiles_per_piece, (q + 1) * tiles_per_piece)
            if i == 0:
                for t in tiles:
                    carries[t] = (carry_ref[:, t * LANES:(t + 1) * LANES],
                                  carry_ref[:, half + t * LANES:half + (t + 1) * LANES])
            for g in range(i * per, min((i + 1) * per, groups)):
                for t in tiles:
                    carries[t] = tile_group(t, g, carries[t])
            if i == row_pieces - 1:
                for t in tiles:
                    carry_ref[:, t * LANES:(t + 1) * LANES] = carries[t][0]
                    carry_ref[:, half + t * LANES:half + (t + 1) * LANES] = carries[t][1]

        carries = {}
        return [functools.partial(piece, q, i, carries)
                for q in range(ntile // tiles_per_piece) for i in range(row_pieces)]

    def outputs(slot):
        taps_per_piece = ncol // LANES

        def cast():
            for t in range(2 * ntile):
                sb_ref[:, t * LANES:(t + 1) * LANES] = st_refs[slot][t].astype(BF16)

        def piece(c):
            lhs = jnp.concatenate([sb_ref[...], u2_ref[...]], axis=1)
            yall = _dot(lhs, wck_ref[:, c * ncol:(c + 1) * ncol])
            for i in range(taps_per_piece):
                tap = c * taps_per_piece + i
                ys_ref[pl.ds(tap, nblk, stride=s), :] = yall[:, i * LANES:(i + 1) * LANES]

        def finish():
            y_ref[...] = jax.nn.gelu(ys_ref[...]).astype(BF16)

        return [cast] + [functools.partial(piece, c) for c in range(s * LANES // ncol)] + [finish]

    for phase in range(3):
        @pl.when(n % 3 == phase)
        def _steps(phase=phase):
            mxu = increments(phase) + outputs((phase + 1) % 3)
            vpu = scan((phase + 2) % 3, 3)
            for i, piece in enumerate(mxu):
                piece()
                if i < len(vpu):
                    vpu[i]()
            for piece in vpu[len(mxu):]:
                piece()


def _s5_scan(u, wb, wck, consts):
    B = u.shape[0]
    s = S5_TAPS
    L = u.shape[2] * s
    tm = TM_S5
    nt = L // tm
    ntiles = LANE_TILES * B * nt
    tile_of = lambda n, lag: jnp.clip(n - lag, 0, ntiles - 1)
    lane_tile = lambda n, lag: tile_of(n, lag) // (B * nt)

    def tok(lag, rows, width):
        def index(n):
            i = tile_of(n, lag)
            return ((i // nt) % B, i // (B * nt), i % nt, 0)
        return pl.BlockSpec((None, None, rows, width), index)

    def par(lag, *shape):
        return pl.BlockSpec((None,) + shape, lambda n: (lane_tile(n, lag),) + (0,) * len(shape))

    nblk = tm // s
    return pl.pallas_call(
        functools.partial(_s5_scan_kernel, tiles_per_seq=nt),
        grid=(ntiles + 2,),
        in_specs=[tok(0, nblk, s * LANES), par(0, s * LANES, 2 * TILE_STATES),
                  par(1, 5, 2, SUBLANES, TILE_STATES), tok(2, nblk, s * LANES),
                  par(2, 2 * TILE_STATES + s * LANES, s * LANES)],
        out_specs=tok(2, tm, LANES),
        out_shape=jax.ShapeDtypeStruct((B, LANE_TILES, L, LANES), BF16),
        scratch_shapes=([pltpu.VMEM((2 * TILE_STATES // LANES, nblk, LANES), F32)] * 3
                        + [pltpu.VMEM((nblk, 2 * TILE_STATES), BF16), pltpu.VMEM((tm, LANES), F32),
                           pltpu.VMEM((SUBLANES, 2 * TILE_STATES), F32)]),
        compiler_params=pltpu.CompilerParams(
            dimension_semantics=("arbitrary",), vmem_limit_bytes=VMEM_LIMIT),
        name="s5_scan",
    )(u, wb, consts, u, wck)


def _out_xattn_kernel(x_ref, ret_ref, y_ref, gs_ref, gw_ref, gb_ref, wout_ref, g2_ref, wq_ref,
                      ka_ref, va_ref, wo_ref, gf_ref, o_ref):
    y = jnp.concatenate([y_ref[j] for j in range(LANE_TILES)], axis=1)
    z = _dot(y, gw_ref[...]) + gb_ref[...]
    ssm = (y.astype(F32) * _sigmoid(z) * gs_ref[...].astype(F32)).astype(BF16)
    x1 = (x_ref[...] + _dot(ret_ref[...], wout_ref[0:RET_V_WIDTH, :])
          + _dot(ssm, wout_ref[RET_V_WIDTH:RET_V_WIDTH + S5_WIDTH, :]))
    h2 = _rms(x1, g2_ref[...]).astype(BF16)
    qa = (_dot(h2, wq_ref[...]) * (XA_DH ** -0.5)).astype(BF16)
    heads = []
    for h in range(XA_HEADS):
        cols = slice(h * XA_DH, (h + 1) * XA_DH)
        s = lax.dot_general(qa[:, cols], ka_ref[:, cols], (((1,), (1,)), ((), ())),
                            preferred_element_type=F32)
        e = jnp.exp(s - jnp.max(s, axis=-1, keepdims=True))
        l = jnp.sum(e, axis=-1, keepdims=True)
        heads.append((_dot(e.astype(BF16), va_ref[:, cols]) * (1.0 / l)).astype(BF16))
    o = jnp.concatenate(heads, axis=-1)
    x2 = x1 + _dot(o, wo_ref[...])
    o_ref[...] = _rms(x2, gf_ref[...])


def _out_xattn(x, ret, y, gs, gw, gb, wout, g2, wq, ka, va, wo, gf):
    B, L, D = x.shape
    tm = TM_OUT
    row = pl.BlockSpec((None, tm, D), lambda b, t: (b, t, 0))
    const = lambda shape: pl.BlockSpec(shape, lambda b, t: (0,) * len(shape))
    mem = pl.BlockSpec((None, MEM_LEN, D), lambda b, t: (b, 0, 0))
    ytile = pl.BlockSpec((None, LANE_TILES, tm, LANES), lambda b, t: (b, 0, t, 0))
    return pl.pallas_call(
        _out_xattn_kernel,
        grid=(B, L // tm),
        in_specs=[row, row, ytile, row, const((D, D)), const((1, D)), const(wout.shape), const((1, D)),
                  const((D, D)), mem, mem, const((D, D)), const((1, D))],
        out_specs=row,
        out_shape=jax.ShapeDtypeStruct((B, L, D), F32),
        compiler_params=pltpu.CompilerParams(
            dimension_semantics=("arbitrary", "arbitrary"), vmem_limit_bytes=VMEM_LIMIT),
        name="out_xattn",
    )(x, ret, y, gs, gw, gb, wout, g2, wq, ka, va, wo, gf)


def _rope_inv_lanes():
    half = RET_DK // 2
    inv = ROPE_BASE ** (-np.arange(half, dtype=np.float64) / half)
    return jnp.asarray(np.tile(inv, LANES // half).reshape(1, LANES), F32)


def kernel(x, mem, positions, norm1_g, w_in, ret_gn_g, s5_a_re, s5_a_im, s5_log_dt, s5_b_re, s5_b_im, s5_c_re, s5_c_im, s5_d, s5_glu_w, s5_glu_b, w_out, norm2_g, norm_mem_g, xa_wq, xa_wk, xa_wv, xa_wo, norm_f_g):
    B, L, D = x.shape
    l = 0
    rowvec = lambda v: v.reshape(1, -1)

    wb, wck, consts, win = _s5_prep(s5_a_re[l], s5_a_im[l], s5_log_dt[l], s5_b_re[l], s5_b_im[l],
                                    s5_c_re[l], s5_c_im[l], s5_d[l], w_in[l])
    (u, g_s5, ret), (glu_w, wout, wq, wk, wv, wo) = _proj_ret(
        x, positions.reshape(B, L, 1), _rope_inv_lanes(), rowvec(norm1_g[l]), win, rowvec(ret_gn_g[l]),
        [s5_glu_w[l], w_out[l], xa_wq[l], xa_wk[l], xa_wv[l], xa_wo[l]])
    ka, va = _mem_kv(mem, rowvec(norm_mem_g[l]), wk, wv)
    y = _s5_scan(u, wb, wck, consts)
    return _out_xattn(x, ret, y, g_s5, glu_w, rowvec(s5_glu_b[l]), wout,
                      rowvec(norm2_g[l]), wq, ka, va, wo, rowvec(norm_f_g))
```

```python
import functools
import math

import numpy as np
import jax
import jax.numpy as jnp
from jax import lax
from jax.experimental import pallas as pl
from jax.experimental.pallas import tpu as pltpu

F32 = jnp.float32
BF16 = jnp.bfloat16

D_MODEL = 1024
MEM_LEN = 256
EPS = 1e-6
ROPE_BASE = 10000.0

RET_HEADS = 8
RET_QK_WIDTH = 512
RET_V_WIDTH = 1024
RET_DK = 64
RET_DV = 128
CHUNK = 128

S5_WIDTH = 1024
S5_GROUP = 16
S5_GROUPS = 64
S5_STATE = 64
S5_NSTATE = S5_GROUPS * S5_STATE

IN_COLS = 5120
COL_Q, COL_K, COL_V, COL_GR, COL_U, COL_GS = 0, 512, 1024, 2048, 3072, 4096

XA_HEADS = 4
XA_DH = 256

LANES = 128
SUBLANES = 8
VMEM_LIMIT = 48 * 1024 * 1024

LANE_TILES = S5_WIDTH // LANES
TILE_GROUPS = LANES // S5_GROUP
TILE_STATES = S5_NSTATE // LANE_TILES
S5_TAPS = 4
S5_SCAN_LANES = 2 * LANES
S5_RUN = 4

LOG_G = tuple(math.log1p(-(2.0 ** (-5.0 - h))) for h in range(RET_HEADS))

TM_PROJ = 512
TM_S5 = 4096
TM_OUT = 1024


def _rms(x, g):
    ms = jnp.mean(x * x, axis=-1, keepdims=True)
    return x * lax.rsqrt(ms + EPS) * g


def _sigmoid(z):
    return 1.0 / (1.0 + jnp.exp(-z))


def _silu(g):
    return g * _sigmoid(g)


def _dot(a, b):
    return jnp.dot(a, b, preferred_element_type=F32)


def _cmul(ar, ai, br, bi):
    return ar * br - ai * bi, ar * bi + ai * br


def _s5_discretise(ar, ai, ldt):
    dt = jnp.exp(ldt)
    mag = jnp.exp(ar * dt)
    p_re = mag * jnp.cos(ai * dt)
    p_im = mag * jnp.sin(ai * dt)
    den = ar * ar + ai * ai
    nr, ni = p_re - 1.0, p_im
    f_re = (nr * ar + ni * ai) / den
    f_im = (ni * ar - nr * ai) / den
    return p_re, p_im, f_re, f_im


def _powers(p_re, p_im, n):
    pw = [(jnp.ones_like(p_re), jnp.zeros_like(p_im))]
    for _ in range(n):
        pw.append(_cmul(pw[-1][0], pw[-1][1], p_re, p_im))
    return pw


def _block_diag(x):
    tiled = jnp.concatenate([x] * TILE_GROUPS, axis=1)
    r = lax.broadcasted_iota(jnp.int32, tiled.shape, 0)
    c = lax.broadcasted_iota(jnp.int32, tiled.shape, 1)
    return jnp.where(r // S5_GROUP == c // S5_STATE, tiled, 0.0)


def _s5_prep_kernel(are_ref, aim_ref, adt_ref, gre_ref, gim_ref, gdt_ref, b_ref, cre_ref, cim_ref, d_ref, win_ref,
                    wb_ref, wck_ref, consts_ref, win16_ref):
    s = S5_TAPS
    half = TILE_STATES
    def per_row(v):
        return jnp.broadcast_to(v[:, None, :], (TILE_GROUPS, S5_GROUP, S5_STATE)).reshape(LANES, S5_STATE)

    g_re, g_im, gf_re, gf_im = _s5_discretise(gre_ref[...], gim_ref[...], gdt_ref[...])
    gw = [(per_row(r), per_row(i)) for r, i in _powers(g_re, g_im, s)]
    bbr, bbi = _cmul(per_row(gf_re), per_row(gf_im), b_ref[0], b_ref[1])
    for k in range(s):
        wr, wi = _cmul(bbr, bbi, *gw[s - 1 - k])
        wb_ref[k * LANES:(k + 1) * LANES, 0:half] = _block_diag(wr).astype(BF16)
        wb_ref[k * LANES:(k + 1) * LANES, half:2 * half] = _block_diag(wi).astype(BF16)

    bb = jnp.concatenate([bbr, bbi], axis=1)
    r = lax.broadcasted_iota(jnp.int32, (LANES, LANES), 0)
    c = lax.broadcasted_iota(jnp.int32, (LANES, LANES), 1)
    same_group = r // S5_GROUP == c // S5_GROUP
    taps = []
    for d in range(s + 1):
        er, ei = _cmul(cre_ref[...], cim_ref[...], *gw[d])
        if d >= 1:
            cwt = jnp.concatenate([_block_diag(er), -_block_diag(ei)], axis=1)
            wck_ref[0:2 * half, (d - 1) * LANES:d * LANES] = cwt.T.astype(BF16)
        if d < s:
            t = lax.dot_general(bb, jnp.concatenate([er, -ei], axis=1), (((1,), (1,)), ((), ())),
                                preferred_element_type=F32, precision=lax.Precision.HIGHEST)
            if d == 0:
                t = t + jnp.where(r == c, d_ref[...], 0.0)
            taps.append(jnp.where(same_group, t, 0.0))
    zero = jnp.zeros((LANES, LANES), BF16)
    for m in range(s):
        for i in range(s):
            blk = taps[i - m].astype(BF16) if m <= i else zero
            wck_ref[2 * half + m * LANES:2 * half + (m + 1) * LANES, i * LANES:(i + 1) * LANES] = blk

    p_re, p_im, _, _ = _s5_discretise(are_ref[...], aim_ref[...], adt_ref[...])
    pw = _powers(p_re, p_im, s)
    shape = (SUBLANES, half)
    qr = _powers(pw[s][0], pw[s][1], S5_RUN)
    consts_ref[0, 0] = jnp.broadcast_to(qr[1][0], shape)
    consts_ref[0, 1] = jnp.broadcast_to(qr[1][1], shape)
    rp = _powers(qr[S5_RUN][0], qr[S5_RUN][1], SUBLANES)
    row = lax.broadcasted_iota(jnp.int32, shape, 0)
    for idx, d in enumerate((1, 2, 4)):
        consts_ref[1 + idx, 0] = jnp.where(row >= d, rp[d][0], 0.0)
        consts_ref[1 + idx, 1] = jnp.where(row >= d, rp[d][1], 0.0)
    cr = jnp.zeros(shape, F32)
    ci = jnp.zeros(shape, F32)
    for k in range(SUBLANES):
        cr = jnp.where(row == k, rp[k + 1][0], cr)
        ci = jnp.where(row == k, rp[k + 1][1], ci)
    consts_ref[4, 0] = cr
    consts_ref[4, 1] = ci

    win16_ref[...] = win_ref[...].astype(BF16)


def _s5_prep(a_re, a_im, log_dt, b_re, b_im, c_re, c_im, d, w_in):
    s = S5_TAPS
    ldt = jnp.broadcast_to(log_dt[:, None], (S5_GROUPS, S5_STATE))
    per_group = [a_re, a_im, ldt]
    per_state = [a.reshape(1, S5_NSTATE) for a in per_group]
    bt = jnp.stack([b_re, b_im]).transpose(0, 1, 3, 2).reshape(2, S5_WIDTH, S5_STATE)
    c_rows = [c.reshape(S5_WIDTH, S5_STATE) for c in (c_re, c_im)]
    tile = lambda *shape: pl.BlockSpec((None,) + shape, lambda j: (j,) + (0,) * len(shape))
    rows = pl.BlockSpec((LANES, S5_STATE), lambda j: (j, 0))
    return pl.pallas_call(
        _s5_prep_kernel,
        grid=(LANE_TILES,),
        in_specs=[pl.BlockSpec((1, TILE_STATES), lambda j: (0, j))] * 3
        + [pl.BlockSpec((TILE_GROUPS, S5_STATE), lambda j: (j, 0))] * 3
        + [pl.BlockSpec((2, LANES, S5_STATE), lambda j: (0, j, 0)), rows, rows,
                  pl.BlockSpec((1, LANES), lambda j: (0, j)),
                  pl.BlockSpec((D_MODEL // LANE_TILES, IN_COLS), lambda j: (j, 0))],
        out_specs=(tile(s * LANES, 2 * TILE_STATES),
                   tile(2 * TILE_STATES + s * LANES, s * LANES),
                   tile(5, 2, SUBLANES, TILE_STATES),
                   pl.BlockSpec((D_MODEL // LANE_TILES, IN_COLS), lambda j: (j, 0))),
        out_shape=(jax.ShapeDtypeStruct((LANE_TILES, s * LANES, 2 * TILE_STATES), BF16),
                   jax.ShapeDtypeStruct((LANE_TILES, 2 * TILE_STATES + s * LANES, s * LANES), BF16),
                   jax.ShapeDtypeStruct((LANE_TILES, 5, 2, SUBLANES, TILE_STATES), F32),
                   jax.ShapeDtypeStruct((D_MODEL, IN_COLS), BF16)),
        compiler_params=pltpu.CompilerParams(
            dimension_semantics=("arbitrary",), vmem_limit_bytes=VMEM_LIMIT),
        name="s5_prep",
    )(*per_state, *per_group, bt, *c_rows, d.reshape(1, S5_WIDTH), w_in)


def _mem_kv_kernel(mem_ref, g_ref, wk_ref, wv_ref, k_ref, v_ref):
    m = _rms(mem_ref[...], g_ref[...]).astype(BF16)
    k_ref[...] = _dot(m, wk_ref[...]).astype(BF16)
    v_ref[...] = _dot(m, wv_ref[...]).astype(BF16)


def _mem_kv(mem, g, wk, wv):
    B, M, D = mem.shape
    return pl.pallas_call(
        _mem_kv_kernel,
        grid=(B,),
        in_specs=[pl.BlockSpec((None, M, D), lambda b: (b, 0, 0)),
                  pl.BlockSpec((1, D), lambda b: (0, 0)),
                  pl.BlockSpec((D, D), lambda b: (0, 0)),
                  pl.BlockSpec((D, D), lambda b: (0, 0))],
        out_specs=(pl.BlockSpec((None, M, D), lambda b: (b, 0, 0)),
                   pl.BlockSpec((None, M, D), lambda b: (b, 0, 0))),
        out_shape=(jax.ShapeDtypeStruct((B, M, D), BF16),
                   jax.ShapeDtypeStruct((B, M, D), BF16)),
        compiler_params=pltpu.CompilerParams(
            dimension_semantics=("arbitrary",), vmem_limit_bytes=VMEM_LIMIT),
        name="mem_kv",
    )(mem, g, wk, wv)


def _token_prologue(x_ref, pos_ref, inv_ref, g_ref, h_ref, cos_ref, sin_ref):
    h_ref[...] = _rms(x_ref[...], g_ref[...]).astype(BF16)
    lane = lax.broadcasted_iota(jnp.int32, (1, LANES), 1)
    nfreq = RET_DK // 2
    group = lane // nfreq
    nblock = LANES // nfreq
    rows = x_ref.shape[0] // nblock
    pos = pos_ref[...].astype(F32)
    pos4 = jnp.zeros((rows, LANES), F32)
    for k in range(nblock):
        pos4 = jnp.where(group == k, pos[k * rows:(k + 1) * rows, :], pos4)
    ang4 = pos4 * inv_ref[...]

    def spread(t4):
        blocks = []
        for k in range(nblock):
            m = jnp.where(group == k, t4, 0.0)
            m = m + pltpu.roll(m, nfreq, 1)
            blocks.append(m + pltpu.roll(m, 2 * nfreq, 1))
        return jnp.concatenate(blocks, axis=0)

    first_half = (lane % RET_DK) < nfreq
    sin = spread(jnp.sin(ang4))
    cos_ref[...] = spread(jnp.cos(ang4))
    sin_ref[...] = jnp.where(first_half, -sin, sin)


def _proj_ret_kernel(x0_ref, pos0_ref, x_ref, pos_ref, inv_ref, g_ref, w_ref, gn_ref, *rest,
                     tiles_per_seq, num_tiles, num_casts):
    f32_refs, rest = rest[:num_casts], rest[num_casts:]
    (u_ref, gs_ref, ret_ref), rest = rest[:3], rest[3:]
    bf16_refs, rest = rest[:num_casts], rest[num_casts:]
    (qa_ref, qb_ref, ha_ref, ca_ref, sa_ref, hb_ref, cb_ref, sb_ref, us_ref,
     state_ref, decay_ref, qw_ref, kw_ref) = rest
    tiles = (qa_ref, qb_ref)
    norms = ((ha_ref, ca_ref, sa_ref), (hb_ref, cb_ref, sb_ref))
    t = pl.program_id(0)

    @pl.when(t == 0)
    def _init():
        _retention_constants(decay_ref, qw_ref, kw_ref)
        qb_ref[...] = jnp.zeros_like(qb_ref)
        _token_prologue(x0_ref, pos0_ref, inv_ref, g_ref, *norms[0])

    @pl.when((t == 0) | (t % tiles_per_seq == 1))
    def _reset():
        state_ref[...] = jnp.zeros_like(state_ref)

    def retention_of(parity):
        return _retention_items(tiles[parity], gn_ref, ret_ref, state_ref, decay_ref, qw_ref, kw_ref)

    @pl.when(t == num_tiles)
    def _drain():
        for r in retention_of((num_tiles - 1) % 2):
            r()

    def step(parity):
        cur_ref = tiles[parity]
        ret_items = retention_of(1 - parity)
        h_ref, cos_ref, sin_ref = norms[parity]
        hn_ref, cn_ref, sn_ref = norms[1 - parity]
        h = h_ref[...]
        cos = cos_ref[...]
        sin_signed = sin_ref[...]
        lane = lax.broadcasted_iota(jnp.int32, (1, LANES), 1)
        first_half = (lane % RET_DK) < (RET_DK // 2)

        def rope(p):
            partner = jnp.where(first_half,
                                pltpu.roll(p, LANES - RET_DK // 2, 1),
                                pltpu.roll(p, RET_DK // 2, 1))
            return p * cos + partner * sin_signed

        def proj_q():
            pq = _dot(h, w_ref[:, COL_Q:COL_K])
            for i in range(RET_QK_WIDTH // LANES):
                sl = slice(i * LANES, (i + 1) * LANES)
                cur_ref[:, COL_Q + i * LANES:COL_Q + (i + 1) * LANES] = rope(pq[:, sl]).astype(BF16)

        def proj_k():
            pk = _dot(h, w_ref[:, COL_K:COL_V])
            for i in range(RET_QK_WIDTH // LANES):
                sl = slice(i * LANES, (i + 1) * LANES)
                cur_ref[:, COL_K + i * LANES:COL_K + (i + 1) * LANES] = (
                    rope(pk[:, sl]) * (RET_DK ** -0.5)).astype(BF16)

        def proj_v():
            cur_ref[:, COL_V:COL_GR] = _dot(h, w_ref[:, COL_V:COL_GR]).astype(BF16)

        def proj_gate():
            cur_ref[:, COL_GR:COL_U] = _silu(_dot(h, w_ref[:, COL_GR:COL_U])).astype(BF16)

        def proj_u():
            pu = _dot(h, w_ref[:, COL_U:COL_GS])
            for j in range(LANE_TILES):
                us_ref[j] = pu[:, j * LANES:(j + 1) * LANES]
            for j in range(LANE_TILES):
                for k in range(S5_TAPS):
                    u_ref[j, :, k * LANES:(k + 1) * LANES] = us_ref[
                        j, pl.ds(k, us_ref.shape[1] // S5_TAPS, stride=S5_TAPS), :].astype(BF16)

        def proj_gs():
            gs_ref[...] = _silu(_dot(h, w_ref[:, COL_GS:IN_COLS])).astype(BF16)

        proj_items = [proj_q, proj_k, proj_v, proj_gate, proj_u, proj_gs]
        per = -(-len(ret_items) // len(proj_items))
        for i, item in enumerate(proj_items):
            for r in ret_items[i * per:(i + 1) * per]:
                r()
            item()
        _token_prologue(x_ref, pos_ref, inv_ref, g_ref, hn_ref, cn_ref, sn_ref)
        for src_ref, dst_ref in zip(f32_refs, bf16_refs):
            dst_ref[...] = src_ref[...].astype(BF16)

    for parity in range(2):
        pl.when((t < num_tiles) & (t % 2 == parity))(functools.partial(step, parity))


def _proj_ret(x, pos3, inv, g, w, gn, later):
    B, L, D = x.shape
    tm = TM_PROJ
    nt = L // tm
    n = B * nt
    cur = lambda t: jnp.minimum(t, n - 1)
    nxt = lambda t: jnp.minimum(t + 1, n - 1)
    lag = lambda t: jnp.maximum(t - 1, 0)
    first = lambda t: 0 * t
    row = lambda width, tile: pl.BlockSpec((None, tm, width), lambda t: (tile(t) // nt, tile(t) % nt, 0))
    const = lambda shape: pl.BlockSpec(shape, lambda t: (0,) * len(shape))
    slab = lambda m: pl.BlockSpec((m.shape[0] // n, m.shape[1]), lambda t: (cur(t), 0))
    outs = pl.pallas_call(
        functools.partial(_proj_ret_kernel, tiles_per_seq=nt, num_tiles=n, num_casts=len(later)),
        grid=(n + 1,),
        in_specs=[row(D, first), row(1, first), row(D, nxt), row(1, nxt), const((1, LANES)), const((1, D)),
                  const((D, IN_COLS)), const((1, RET_V_WIDTH))] + [slab(m) for m in later],
        out_specs=[pl.BlockSpec((None, LANE_TILES, tm // S5_TAPS, S5_TAPS * LANES),
                                lambda t: (cur(t) // nt, 0, cur(t) % nt, 0)),
                   row(S5_WIDTH, cur), row(RET_V_WIDTH, lag)] + [slab(m) for m in later],
        out_shape=[jax.ShapeDtypeStruct((B, LANE_TILES, L // S5_TAPS, S5_TAPS * LANES), BF16),
                   jax.ShapeDtypeStruct((B, L, S5_WIDTH), BF16),
                   jax.ShapeDtypeStruct((B, L, RET_V_WIDTH), BF16)]
        + [jax.ShapeDtypeStruct(m.shape, BF16) for m in later],
        scratch_shapes=[pltpu.VMEM((tm, COL_U), BF16),
                        pltpu.VMEM((tm, COL_U), BF16),
                        pltpu.VMEM((tm, D), BF16), pltpu.VMEM((tm, LANES), F32), pltpu.VMEM((tm, LANES), F32),
                        pltpu.VMEM((tm, D), BF16), pltpu.VMEM((tm, LANES), F32), pltpu.VMEM((tm, LANES), F32),
                        pltpu.VMEM((LANE_TILES, tm, LANES), F32),
                        pltpu.VMEM((RET_HEADS // 2, LANES, 2 * RET_DV), F32),
                        pltpu.VMEM((RET_HEADS // 2, CHUNK, 2 * CHUNK), F32),
                        pltpu.VMEM((RET_HEADS // 2, CHUNK, 2 * RET_DV), F32),
                        pltpu.VMEM((RET_HEADS // 2, CHUNK, LANES), F32)],
        compiler_params=pltpu.CompilerParams(
            dimension_semantics=("arbitrary",), vmem_limit_bytes=VMEM_LIMIT),
        name="proj_ret",
    )(x, pos3, x, pos3, inv, g, w, gn, *later)
    return outs[:3], outs[3:]


def _retention_constants(decay_ref, qw_ref, kw_ref):
    i = lax.broadcasted_iota(jnp.int32, (CHUNK, 2 * CHUNK), 0).astype(F32)
    c = lax.broadcasted_iota(jnp.int32, (CHUNK, 2 * CHUNK), 1)
    j = (c % CHUNK).astype(F32)
    diff = i - j
    lane = lax.broadcasted_iota(jnp.int32, (CHUNK, LANES), 1)
    ik = lax.broadcasted_iota(jnp.int32, (CHUNK, LANES), 0).astype(F32)
    for p in range(RET_HEADS // 2):
        lg = jnp.where(c < CHUNK, LOG_G[2 * p], LOG_G[2 * p + 1])
        decay_ref[p] = jnp.where(diff >= 0.0, jnp.exp(lg * jnp.maximum(diff, 0.0)), 0.0)
        qw_ref[p] = jnp.exp(lg * (i + 1.0))
        lgk = jnp.where(lane < RET_DK, LOG_G[2 * p], LOG_G[2 * p + 1])
        kw_ref[p] = jnp.exp(lgk * (CHUNK - 1.0 - ik))


def _retention_items(src_ref, gn_ref, o_ref, state_ref, decay_ref, qw_ref, kw_ref):
    pairs = RET_HEADS // 2
    low = lax.broadcasted_iota(jnp.int32, (1, LANES), 1) < RET_DK
    srow = lax.broadcasted_iota(jnp.int32, (LANES, 2 * RET_DV), 0)
    scol = lax.broadcasted_iota(jnp.int32, (LANES, 2 * RET_DV), 1)
    own_block = (srow < RET_DK) == (scol < RET_DV)
    scol1 = lax.broadcasted_iota(jnp.int32, (1, 2 * RET_DV), 1)
    zero_v = jnp.zeros((CHUNK, RET_DV), BF16)

    def pair_step(c, p):
        rows = slice(c * CHUNK, (c + 1) * CHUNK)
        qt = src_ref[rows, COL_Q + p * LANES:COL_Q + (p + 1) * LANES]
        kt = src_ref[rows, COL_K + p * LANES:COL_K + (p + 1) * LANES]
        vp = src_ref[rows, COL_V + 2 * p * RET_DV:COL_V + 2 * (p + 1) * RET_DV]
        state = state_ref[p]
        cross = _dot(qt, state.astype(BF16))
        kw = (kt.astype(F32) * kw_ref[p]).astype(BF16)
        upd = lax.dot_general(kw, vp, (((0,), (0,)), ((), ())), preferred_element_type=F32)
        chunk_decay = jnp.where(scol1 < RET_DV,
                                math.exp(LOG_G[2 * p] * CHUNK), math.exp(LOG_G[2 * p + 1] * CHUNK))
        state_ref[p] = state * chunk_decay + jnp.where(own_block, upd, 0.0)

        zero_k = jnp.zeros_like(kt)
        k_rows = jnp.concatenate([jnp.where(low, kt, zero_k), jnp.where(low, zero_k, kt)], axis=0)
        s = lax.dot_general(qt, k_rows, (((1,), (1,)), ((), ())), preferred_element_type=F32)
        pm = (s * decay_ref[p]).astype(BF16)
        v_diag = jnp.concatenate(
            [jnp.concatenate([vp[:, 0:RET_DV], zero_v], axis=1),
             jnp.concatenate([zero_v, vp[:, RET_DV:2 * RET_DV]], axis=1)], axis=0)
        o2 = _dot(pm, v_diag) + qw_ref[p] * cross
        for e in range(2):
            h = 2 * p + e
            cols = slice(h * RET_DV, (h + 1) * RET_DV)
            o = o2[:, e * RET_DV:(e + 1) * RET_DV]
            mu = jnp.mean(o, axis=-1, keepdims=True)
            oc = o - mu
            var = jnp.mean(oc * oc, axis=-1, keepdims=True)
            on = oc * lax.rsqrt(var + EPS) * gn_ref[:, cols]
            gate = src_ref[rows, COL_GR + h * RET_DV:COL_GR + (h + 1) * RET_DV].astype(F32)
            o_ref[rows, cols] = (on * gate).astype(BF16)

    return [functools.partial(pair_step, c, p)
            for c in range(src_ref.shape[0] // CHUNK) for p in range(pairs)]


def _s5_scan_kernel(u_ref, wb_ref, consts_ref, u2_ref, wck_ref, y_ref, *scratch, tiles_per_seq):
    st_refs = scratch[0:3]
    sb_ref, ys_ref, carry_ref = scratch[3:]
    n = pl.program_id(0)
    s = S5_TAPS
    nblk = u_ref.shape[0]
    half = TILE_STATES

    @pl.when(n == 0)
    def _init():
        for ref in scratch[0:3]:
            ref[...] = jnp.zeros_like(ref)

    @pl.when((n == 0) | ((n - 1) % tiles_per_seq == 0))
    def _reset():
        carry_ref[...] = jnp.zeros_like(carry_ref)

    ncol = 2 * LANES

    def increments(slot):
        st_ref = st_refs[slot]

        def piece(c):
            z = _dot(u_ref[...], wb_ref[:, c * ncol:(c + 1) * ncol])
            for i in range(ncol // LANES):
                st_ref[c * (ncol // LANES) + i] = z[:, i * LANES:(i + 1) * LANES]

        return [functools.partial(piece, c) for c in range(2 * half // ncol)]

    ntile = half // LANES

    def scan(slot, row_pieces):
        st_ref = st_refs[slot]
        run = S5_RUN
        span = run * SUBLANES
        row = lax.broadcasted_iota(jnp.int32, (SUBLANES, LANES), 0)
        last = SUBLANES - 1
        groups = nblk // span
        per = -(-groups // row_pieces)
        tiles_per_piece = S5_SCAN_LANES // LANES

        def tile_group(t, g, carry):
            lanes = slice(t * LANES, (t + 1) * LANES)
            const = lambda idx: (consts_ref[idx, 0, :, lanes], consts_ref[idx, 1, :, lanes])
            cr, ci = carry
            rows = [pl.ds(g * span + j, SUBLANES, stride=run) for j in range(run)]
            loc = [(st_ref[t, rows[0], :], st_ref[ntile + t, rows[0], :])]
            for j in range(1, run):
                pr, pi = _cmul(*const(0), *loc[-1])
                loc.append((pr + st_ref[t, rows[j], :], pi + st_ref[ntile + t, rows[j], :]))
            fr, fi = loc[-1]
            for idx, d in enumerate((1, 2, 4)):
                pr, pi = _cmul(*const(1 + idx), pltpu.roll(fr, d, 0), pltpu.roll(fi, d, 0))
                fr, fi = fr + pr, fi + pi
            pr, pi = _cmul(*const(4), cr, ci)
            fr, fi = fr + pr, fi + pi
            er = jnp.where(row == 0, cr, pltpu.roll(fr, 1, 0))
            ei = jnp.where(row == 0, ci, pltpu.roll(fi, 1, 0))
            st_ref[t, rows[0], :] = er
            st_ref[ntile + t, rows[0], :] = ei
            for j in range(1, run):
                er, ei = _cmul(*const(0), er, ei)
                st_ref[t, rows[j], :] = loc[j - 1][0] + er
                st_ref[ntile + t, rows[j], :] = loc[j - 1][1] + ei
            return (jnp.broadcast_to(fr[last:, :], (SUBLANES, LANES)),
                    jnp.broadcast_to(fi[last:, :], (SUBLANES, LANES)))

        def piece(q, i, carries):
            tiles = range(q * tiles_per_piece, (q + 1) * tiles_per_piece)
            if i == 0:
                for t in tiles:
                    carries[t] = (carry_ref[:, t * LANES:(t + 1) * LANES],
                                  carry_ref[:, half + t * LANES:half + (t + 1) * LANES])
            for g in range(i * per, min((i + 1) * per, groups)):
                for t in tiles:
                    carries[t] = tile_group(t, g, carries[t])
            if i == row_pieces - 1:
                for t in tiles:
                    carry_ref[:, t * LANES:(t + 1) * LANES] = carries[t][0]
                    carry_ref[:, half + t * LANES:half + (t + 1) * LANES] = carries[t][1]

        carries = {}
        return [functools.partial(piece, q, i, carries)
                for q in range(ntile // tiles_per_piece) for i in range(row_pieces)]

    def outputs(slot):
        taps_per_piece = ncol // LANES

        def cast():
            for t in range(2 * ntile):
                sb_ref[:, t * LANES:(t + 1) * LANES] = st_refs[slot][t].astype(BF16)

        def piece(c):
            lhs = jnp.concatenate([sb_ref[...], u2_ref[...]], axis=1)
            yall = _dot(lhs, wck_ref[:, c * ncol:(c + 1) * ncol])
            for i in range(taps_per_piece):
                tap = c * taps_per_piece + i
                ys_ref[pl.ds(tap, nblk, stride=s), :] = yall[:, i * LANES:(i + 1) * LANES]

        def finish():
            y_ref[...] = jax.nn.gelu(ys_ref[...]).astype(BF16)

        return [cast] + [functools.partial(piece, c) for c in range(s * LANES // ncol)] + [finish]

    for phase in range(3):
        @pl.when(n % 3 == phase)
        def _steps(phase=phase):
            mxu = increments(phase) + outputs((phase + 1) % 3)
            vpu = scan((phase + 2) % 3, 3)
            for i, piece in enumerate(mxu):
                piece()
                if i < len(vpu):
                    vpu[i]()
            for piece in vpu[len(mxu):]:
                piece()


def _s5_scan(u, wb, wck, consts):
    B = u.shape[0]
    s = S5_TAPS
    L = u.shape[2] * s
    tm = TM_S5
    nt = L // tm
    ntiles = LANE_TILES * B * nt
    tile_of = lambda n, lag: jnp.clip(n - lag, 0, ntiles - 1)
    lane_tile = lambda n, lag: tile_of(n, lag) // (B * nt)

    def tok(lag, rows, width):
        def index(n):
            i = tile_of(n, lag)
            return ((i // nt) % B, i // (B * nt), i % nt, 0)
        return pl.BlockSpec((None, None, rows, width), index)

    def par(lag, *shape):
        return pl.BlockSpec((None,) + shape, lambda n: (lane_tile(n, lag),) + (0,) * len(shape))

    nblk = tm // s
    return pl.pallas_call(
        functools.partial(_s5_scan_kernel, tiles_per_seq=nt),
        grid=(ntiles + 2,),
        in_specs=[tok(0, nblk, s * LANES), par(0, s * LANES, 2 * TILE_STATES),
                  par(1, 5, 2, SUBLANES, TILE_STATES), tok(2, nblk, s * LANES),
                  par(2, 2 * TILE_STATES + s * LANES, s * LANES)],
        out_specs=tok(2, tm, LANES),
        out_shape=jax.ShapeDtypeStruct((B, LANE_TILES, L, LANES), BF16),
        scratch_shapes=([pltpu.VMEM((2 * TILE_STATES // LANES, nblk, LANES), F32)] * 3
                        + [pltpu.VMEM((nblk, 2 * TILE_STATES), BF16), pltpu.VMEM((tm, LANES), F32),
                           pltpu.VMEM((SUBLANES, 2 * TILE_STATES), F32)]),
        compiler_params=pltpu.CompilerParams(
            dimension_semantics=("arbitrary",), vmem_limit_bytes=VMEM_LIMIT),
        name="s5_scan",
    )(u, wb, consts, u, wck)


def _out_xattn_kernel(x_ref, ret_ref, y_ref, gs_ref, gw_ref, gb_ref, wout_ref, g2_ref, wq_ref,
                      ka_ref, va_ref, wo_ref, gf_ref, o_ref):
    y = jnp.concatenate([y_ref[j] for j in range(LANE_TILES)], axis=1)
    z = _dot(y, gw_ref[...]) + gb_ref[...]
    ssm = (y.astype(F32) * _sigmoid(z) * gs_ref[...].astype(F32)).astype(BF16)
    x1 = (x_ref[...] + _dot(ret_ref[...], wout_ref[0:RET_V_WIDTH, :])
          + _dot(ssm, wout_ref[RET_V_WIDTH:RET_V_WIDTH + S5_WIDTH, :]))
    h2 = _rms(x1, g2_ref[...]).astype(BF16)
    qa = (_dot(h2, wq_ref[...]) * (XA_DH ** -0.5)).astype(BF16)
    heads = []
    for h in range(XA_HEADS):
        cols = slice(h * XA_DH, (h + 1) * XA_DH)
        s = lax.dot_general(qa[:, cols], ka_ref[:, cols], (((1,), (1,)), ((), ())),
                            preferred_element_type=F32)
        e = jnp.exp(s - jnp.max(s, axis=-1, keepdims=True))
        l = jnp.sum(e, axis=-1, keepdims=True)
        heads.append((_dot(e.astype(BF16), va_ref[:, cols]) * (1.0 / l)).astype(BF16))
    o = jnp.concatenate(heads, axis=-1)
    x2 = x1 + _dot(o, wo_ref[...])
    o_ref[...] = _rms(x2, gf_ref[...])


def _out_xattn(x, ret, y, gs, gw, gb, wout, g2, wq, ka, va, wo, gf):
    B, L, D = x.shape
    tm = TM_OUT
    row = pl.BlockSpec((None, tm, D), lambda b, t: (b, t, 0))
    const = lambda shape: pl.BlockSpec(shape, lambda b, t: (0,) * len(shape))
    mem = pl.BlockSpec((None, MEM_LEN, D), lambda b, t: (b, 0, 0))
    ytile = pl.BlockSpec((None, LANE_TILES, tm, LANES), lambda b, t: (b, 0, t, 0))
    return pl.pallas_call(
        _out_xattn_kernel,
        grid=(B, L // tm),
        in_specs=[row, row, ytile, row, const((D, D)), const((1, D)), const(wout.shape), const((1, D)),
                  const((D, D)), mem, mem, const((D, D)), const((1, D))],
        out_specs=row,
        out_shape=jax.ShapeDtypeStruct((B, L, D), F32),
        compiler_params=pltpu.CompilerParams(
            dimension_semantics=("arbitrary", "arbitrary"), vmem_limit_bytes=VMEM_LIMIT),
        name="out_xattn",
    )(x, ret, y, gs, gw, gb, wout, g2, wq, ka, va, wo, gf)


def _rope_inv_lanes():
    half = RET_DK // 2
    inv = ROPE_BASE ** (-np.arange(half, dtype=np.float64) / half)
    return jnp.asarray(np.tile(inv, LANES // half).reshape(1, LANES), F32)


def kernel(x, mem, positions, norm1_g, w_in, ret_gn_g, s5_a_re, s5_a_im, s5_log_dt, s5_b_re, s5_b_im, s5_c_re, s5_c_im, s5_d, s5_glu_w, s5_glu_b, w_out, norm2_g, norm_mem_g, xa_wq, xa_wk, xa_wv, xa_wo, norm_f_g):
    B, L, D = x.shape
    l = 0
    rowvec = lambda v: v.reshape(1, -1)

    wb, wck, consts, win = _s5_prep(s5_a_re[l], s5_a_im[l], s5_log_dt[l], s5_b_re[l], s5_b_im[l],
                                    s5_c_re[l], s5_c_im[l], s5_d[l], w_in[l])
    (u, g_s5, ret), (glu_w, wout, wq, wk, wv, wo) = _proj_ret(
        x, positions.reshape(B, L, 1), _rope_inv_lanes(), rowvec(norm1_g[l]), win, rowvec(ret_gn_g[l]),
        [s5_glu_w[l], w_out[l], xa_wq[l], xa_wk[l], xa_wv[l], xa_wo[l]])
    ka, va = _mem_kv(mem, rowvec(norm_mem_g[l]), wk, wv)
    y = _s5_scan(u, wb, wck, consts)
    return _out_xattn(x, ret, y, g_s5, glu_w, rowvec(s5_glu_b[l]), wout,
                      rowvec(norm2_g[l]), wq, ka, va, wo, rowvec(norm_f_g))
```

```python
import functools
import math

import numpy as np
import jax
import jax.numpy as jnp
from jax import lax
from jax.experimental import pallas as pl
from jax.experimental.pallas import tpu as pltpu

F32 = jnp.float32
BF16 = jnp.bfloat16

D_MODEL = 1024
MEM_LEN = 256
EPS = 1e-6
ROPE_BASE = 10000.0

RET_HEADS = 8
RET_QK_WIDTH = 512
RET_V_WIDTH = 1024
RET_DK = 64
RET_DV = 128
CHUNK = 128

S5_WIDTH = 1024
S5_GROUP = 16
S5_GROUPS = 64
S5_STATE = 64
S5_NSTATE = S5_GROUPS * S5_STATE

IN_COLS = 5120
COL_Q, COL_K, COL_V, COL_GR, COL_U, COL_GS = 0, 512, 1024, 2048, 3072, 4096

XA_HEADS = 4
XA_DH = 256

LANES = 128
SUBLANES = 8
VMEM_LIMIT = 48 * 1024 * 1024

LANE_TILES = S5_WIDTH // LANES
TILE_GROUPS = LANES // S5_GROUP
TILE_STATES = S5_NSTATE // LANE_TILES
S5_TAPS = 4
S5_SCAN_LANES = 2 * LANES
S5_RUN = 4

LOG_G = tuple(math.log1p(-(2.0 ** (-5.0 - h))) for h in range(RET_HEADS))

TM_PROJ = 512
TM_S5 = 4096
TM_OUT = 1024


def _rms(x, g):
    ms = jnp.mean(x * x, axis=-1, keepdims=True)
    return x * lax.rsqrt(ms + EPS) * g


def _sigmoid(z):
    return 1.0 / (1.0 + jnp.exp(-z))


def _silu(g):
    return g * _sigmoid(g)


def _dot(a, b):
    return jnp.dot(a, b, preferred_element_type=F32)


def _cmul(ar, ai, br, bi):
    return ar * br - ai * bi, ar * bi + ai * br


def _s5_discretise(ar, ai, ldt):
    dt = jnp.exp(ldt)
    mag = jnp.exp(ar * dt)
    p_re = mag * jnp.cos(ai * dt)
    p_im = mag * jnp.sin(ai * dt)
    den = ar * ar + ai * ai
    nr, ni = p_re - 1.0, p_im
    f_re = (nr * ar + ni * ai) / den
    f_im = (ni * ar - nr * ai) / den
    return p_re, p_im, f_re, f_im


def _powers(p_re, p_im, n):
    pw = [(jnp.ones_like(p_re), jnp.zeros_like(p_im))]
    for _ in range(n):
        pw.append(_cmul(pw[-1][0], pw[-1][1], p_re, p_im))
    return pw


def _block_diag(x):
    tiled = jnp.concatenate([x] * TILE_GROUPS, axis=1)
    r = lax.broadcasted_iota(jnp.int32, tiled.shape, 0)
    c = lax.broadcasted_iota(jnp.int32, tiled.shape, 1)
    return jnp.where(r // S5_GROUP == c // S5_STATE, tiled, 0.0)


def _s5_prep_kernel(a_ref, ag_ref, b_ref, c_ref, d_ref, win_ref, wb_ref, wck_ref, consts_ref, win16_ref):
    s = S5_TAPS
    half = TILE_STATES
    def per_row(v):
        return jnp.broadcast_to(v[:, None, :], (TILE_GROUPS, S5_GROUP, S5_STATE)).reshape(LANES, S5_STATE)

    g_re, g_im, gf_re, gf_im = _s5_discretise(ag_ref[0], ag_ref[1], ag_ref[2])
    gw = [(per_row(r), per_row(i)) for r, i in _powers(g_re, g_im, s)]
    bbr, bbi = _cmul(per_row(gf_re), per_row(gf_im), b_ref[0], b_ref[1])
    for k in range(s):
        wr, wi = _cmul(bbr, bbi, *gw[s - 1 - k])
        wb_ref[k * LANES:(k + 1) * LANES, 0:half] = _block_diag(wr).astype(BF16)
        wb_ref[k * LANES:(k + 1) * LANES, half:2 * half] = _block_diag(wi).astype(BF16)

    bb = jnp.concatenate([bbr, bbi], axis=1)
    r = lax.broadcasted_iota(jnp.int32, (LANES, LANES), 0)
    c = lax.broadcasted_iota(jnp.int32, (LANES, LANES), 1)
    same_group = r // S5_GROUP == c // S5_GROUP
    taps = []
    for d in range(s + 1):
        er, ei = _cmul(c_ref[0], c_ref[1], *gw[d])
        if d >= 1:
            cwt = jnp.concatenate([_block_diag(er), -_block_diag(ei)], axis=1)
            wck_ref[0:2 * half, (d - 1) * LANES:d * LANES] = cwt.T.astype(BF16)
        if d < s:
            t = lax.dot_general(bb, jnp.concatenate([er, -ei], axis=1), (((1,), (1,)), ((), ())),
                                preferred_element_type=F32, precision=lax.Precision.HIGHEST)
            if d == 0:
                t = t + jnp.where(r == c, d_ref[...], 0.0)
            taps.append(jnp.where(same_group, t, 0.0))
    zero = jnp.zeros((LANES, LANES), BF16)
    for m in range(s):
        for i in range(s):
            blk = taps[i - m].astype(BF16) if m <= i else zero
            wck_ref[2 * half + m * LANES:2 * half + (m + 1) * LANES, i * LANES:(i + 1) * LANES] = blk

    p_re, p_im, _, _ = _s5_discretise(a_ref[0:1, :], a_ref[1:2, :], a_ref[2:3, :])
    pw = _powers(p_re, p_im, s)
    shape = (SUBLANES, half)
    qr = _powers(pw[s][0], pw[s][1], S5_RUN)
    consts_ref[0, 0] = jnp.broadcast_to(qr[1][0], shape)
    consts_ref[0, 1] = jnp.broadcast_to(qr[1][1], shape)
    rp = _powers(qr[S5_RUN][0], qr[S5_RUN][1], SUBLANES)
    row = lax.broadcasted_iota(jnp.int32, shape, 0)
    for idx, d in enumerate((1, 2, 4)):
        consts_ref[1 + idx, 0] = jnp.where(row >= d, rp[d][0], 0.0)
        consts_ref[1 + idx, 1] = jnp.where(row >= d, rp[d][1], 0.0)
    cr = jnp.zeros(shape, F32)
    ci = jnp.zeros(shape, F32)
    for k in range(SUBLANES):
        cr = jnp.where(row == k, rp[k + 1][0], cr)
        ci = jnp.where(row == k, rp[k + 1][1], ci)
    consts_ref[4, 0] = cr
    consts_ref[4, 1] = ci

    col = lax.broadcasted_iota(jnp.int32, (1, IN_COLS), 1)
    key_scale = jnp.where((col >= COL_K) & (col < COL_V), RET_DK ** -0.5, 1.0)
    win16_ref[...] = (win_ref[...] * key_scale).astype(BF16)


def _s5_prep(a_re, a_im, log_dt, b_re, b_im, c_re, c_im, d, w_in):
    s = S5_TAPS
    ldt = jnp.broadcast_to(log_dt[:, None], (S5_GROUPS, S5_STATE))
    ag = jnp.stack([a_re, a_im, ldt])
    a3 = ag.reshape(3, S5_NSTATE)
    bt = jnp.stack([b_re, b_im]).transpose(0, 1, 3, 2).reshape(2, S5_WIDTH, S5_STATE)
    ct = jnp.stack([c_re, c_im]).reshape(2, S5_WIDTH, S5_STATE)
    tile = lambda *shape: pl.BlockSpec((None,) + shape, lambda j: (j,) + (0,) * len(shape))
    return pl.pallas_call(
        _s5_prep_kernel,
        grid=(LANE_TILES,),
        in_specs=[pl.BlockSpec((3, TILE_STATES), lambda j: (0, j)),
                  pl.BlockSpec((3, TILE_GROUPS, S5_STATE), lambda j: (0, j, 0)),
                  pl.BlockSpec((2, LANES, S5_STATE), lambda j: (0, j, 0)),
                  pl.BlockSpec((2, LANES, S5_STATE), lambda j: (0, j, 0)),
                  pl.BlockSpec((1, LANES), lambda j: (0, j)),
                  pl.BlockSpec((D_MODEL // LANE_TILES, IN_COLS), lambda j: (j, 0))],
        out_specs=(tile(s * LANES, 2 * TILE_STATES),
                   tile(2 * TILE_STATES + s * LANES, s * LANES),
                   tile(5, 2, SUBLANES, TILE_STATES),
                   pl.BlockSpec((D_MODEL // LANE_TILES, IN_COLS), lambda j: (j, 0))),
        out_shape=(jax.ShapeDtypeStruct((LANE_TILES, s * LANES, 2 * TILE_STATES), BF16),
                   jax.ShapeDtypeStruct((LANE_TILES, 2 * TILE_STATES + s * LANES, s * LANES), BF16),
                   jax.ShapeDtypeStruct((LANE_TILES, 5, 2, SUBLANES, TILE_STATES), F32),
                   jax.ShapeDtypeStruct((D_MODEL, IN_COLS), BF16)),
        compiler_params=pltpu.CompilerParams(
            dimension_semantics=("arbitrary",), vmem_limit_bytes=VMEM_LIMIT),
        name="s5_prep",
    )(a3, ag, bt, ct, d.reshape(1, S5_WIDTH), w_in)


def _mem_kv_kernel(mem_ref, g_ref, wk_ref, wv_ref, k_ref, v_ref):
    m = _rms(mem_ref[...], g_ref[...]).astype(BF16)
    k_ref[...] = _dot(m, wk_ref[...]).astype(BF16)
    v_ref[...] = _dot(m, wv_ref[...]).astype(BF16)


def _mem_kv(mem, g, wk, wv):
    B, M, D = mem.shape
    return pl.pallas_call(
        _mem_kv_kernel,
        grid=(B,),
        in_specs=[pl.BlockSpec((None, M, D), lambda b: (b, 0, 0)),
                  pl.BlockSpec((1, D), lambda b: (0, 0)),
                  pl.BlockSpec((D, D), lambda b: (0, 0)),
                  pl.BlockSpec((D, D), lambda b: (0, 0))],
        out_specs=(pl.BlockSpec((None, M, D), lambda b: (b, 0, 0)),
                   pl.BlockSpec((None, M, D), lambda b: (b, 0, 0))),
        out_shape=(jax.ShapeDtypeStruct((B, M, D), BF16),
                   jax.ShapeDtypeStruct((B, M, D), BF16)),
        compiler_params=pltpu.CompilerParams(
            dimension_semantics=("arbitrary",), vmem_limit_bytes=VMEM_LIMIT),
        name="mem_kv",
    )(mem, g, wk, wv)


def _token_prologue(x_ref, pos_ref, inv_ref, g_ref, h_ref, cos_ref, sin_ref):
    h_ref[...] = _rms(x_ref[...], g_ref[...]).astype(BF16)
    lane = lax.broadcasted_iota(jnp.int32, (1, LANES), 1)
    nfreq = RET_DK // 2
    group = lane // nfreq
    nblock = LANES // nfreq
    rows = x_ref.shape[0] // nblock
    pos = pos_ref[...].astype(F32)
    pos4 = jnp.zeros((rows, LANES), F32)
    for k in range(nblock):
        pos4 = jnp.where(group == k, pos[k * rows:(k + 1) * rows, :], pos4)
    ang4 = pos4 * inv_ref[...]

    def spread(t4):
        blocks = []
        for k in range(nblock):
            m = jnp.where(group == k, t4, 0.0)
            m = m + pltpu.roll(m, nfreq, 1)
            blocks.append(m + pltpu.roll(m, 2 * nfreq, 1))
        return jnp.concatenate(blocks, axis=0)

    first_half = (lane % RET_DK) < nfreq
    sin = spread(jnp.sin(ang4))
    cos_ref[...] = spread(jnp.cos(ang4))
    sin_ref[...] = jnp.where(first_half, -sin, sin)


def _proj_ret_kernel(x0_ref, pos0_ref, x_ref, pos_ref, inv_ref, g_ref, w_ref, gn_ref, *rest,
                     tiles_per_seq, num_tiles, num_casts, cast_scales):
    f32_refs, rest = rest[:num_casts], rest[num_casts:]
    (u_ref, gs_ref, ret_ref), rest = rest[:3], rest[3:]
    bf16_refs, rest = rest[:num_casts], rest[num_casts:]
    (qa_ref, qb_ref, ha_ref, ca_ref, sa_ref, hb_ref, cb_ref, sb_ref, us_ref,
     state_ref, decay_ref, qw_ref, kw_ref) = rest
    tiles = (qa_ref, qb_ref)
    norms = ((ha_ref, ca_ref, sa_ref), (hb_ref, cb_ref, sb_ref))
    t = pl.program_id(0)

    @pl.when(t == 0)
    def _init():
        _retention_constants(decay_ref, qw_ref, kw_ref)
        qb_ref[...] = jnp.zeros_like(qb_ref)
        _token_prologue(x0_ref, pos0_ref, inv_ref, g_ref, *norms[0])

    @pl.when((t == 0) | (t % tiles_per_seq == 1))
    def _reset():
        state_ref[...] = jnp.zeros_like(state_ref)

    def retention_of(parity):
        return _retention_items(tiles[parity], gn_ref, ret_ref, state_ref, decay_ref, qw_ref, kw_ref)

    @pl.when(t == num_tiles)
    def _drain():
        for r in retention_of((num_tiles - 1) % 2):
            r()

    def step(parity):
        cur_ref = tiles[parity]
        ret_items = retention_of(1 - parity)
        h_ref, cos_ref, sin_ref = norms[parity]
        hn_ref, cn_ref, sn_ref = norms[1 - parity]
        h = h_ref[...]
        cos = cos_ref[...]
        sin_signed = sin_ref[...]
        lane = lax.broadcasted_iota(jnp.int32, (1, LANES), 1)
        first_half = (lane % RET_DK) < (RET_DK // 2)

        def rope(p):
            partner = jnp.where(first_half,
                                pltpu.roll(p, LANES - RET_DK // 2, 1),
                                pltpu.roll(p, RET_DK // 2, 1))
            return p * cos + partner * sin_signed

        def proj_q():
            pq = _dot(h, w_ref[:, COL_Q:COL_K])
            for i in range(RET_QK_WIDTH // LANES):
                sl = slice(i * LANES, (i + 1) * LANES)
                cur_ref[:, COL_Q + i * LANES:COL_Q + (i + 1) * LANES] = rope(pq[:, sl]).astype(BF16)

        def proj_k():
            pk = _dot(h, w_ref[:, COL_K:COL_V])
            for i in range(RET_QK_WIDTH // LANES):
                sl = slice(i * LANES, (i + 1) * LANES)
                cur_ref[:, COL_K + i * LANES:COL_K + (i + 1) * LANES] = rope(pk[:, sl]).astype(BF16)

        def proj_v():
            cur_ref[:, COL_V:COL_GR] = _dot(h, w_ref[:, COL_V:COL_GR]).astype(BF16)

        def proj_gate():
            cur_ref[:, COL_GR:COL_U] = _silu(_dot(h, w_ref[:, COL_GR:COL_U])).astype(BF16)

        def proj_u():
            pu = _dot(h, w_ref[:, COL_U:COL_GS])
            for j in range(LANE_TILES):
                us_ref[j] = pu[:, j * LANES:(j + 1) * LANES]
            for j in range(LANE_TILES):
                for k in range(S5_TAPS):
                    u_ref[j, :, k * LANES:(k + 1) * LANES] = us_ref[
                        j, pl.ds(k, us_ref.shape[1] // S5_TAPS, stride=S5_TAPS), :].astype(BF16)

        def proj_gs():
            gs_ref[...] = _silu(_dot(h, w_ref[:, COL_GS:IN_COLS])).astype(BF16)

        proj_items = [proj_q, proj_k, proj_v, proj_gate, proj_u, proj_gs]
        per = -(-len(ret_items) // len(proj_items))
        for i, item in enumerate(proj_items):
            for r in ret_items[i * per:(i + 1) * per]:
                r()
            item()
        _token_prologue(x_ref, pos_ref, inv_ref, g_ref, hn_ref, cn_ref, sn_ref)
        for src_ref, dst_ref, scale in zip(f32_refs, bf16_refs, cast_scales):
            dst_ref[...] = (src_ref[...] * scale).astype(BF16)

    for parity in range(2):
        pl.when((t < num_tiles) & (t % 2 == parity))(functools.partial(step, parity))


def _proj_ret(x, pos3, inv, g, w, gn, later):
    cast_scales = tuple(scale for _, scale in later)
    later = [m for m, _ in later]
    B, L, D = x.shape
    tm = TM_PROJ
    nt = L // tm
    n = B * nt
    cur = lambda t: jnp.minimum(t, n - 1)
    nxt = lambda t: jnp.minimum(t + 1, n - 1)
    lag = lambda t: jnp.maximum(t - 1, 0)
    first = lambda t: 0 * t
    row = lambda width, tile: pl.BlockSpec((None, tm, width), lambda t: (tile(t) // nt, tile(t) % nt, 0))
    const = lambda shape: pl.BlockSpec(shape, lambda t: (0,) * len(shape))
    slab = lambda m: pl.BlockSpec((m.shape[0] // n, m.shape[1]), lambda t: (cur(t), 0))
    outs = pl.pallas_call(
        functools.partial(_proj_ret_kernel, tiles_per_seq=nt, num_tiles=n, num_casts=len(later),
                          cast_scales=cast_scales),
        grid=(n + 1,),
        in_specs=[row(D, first), row(1, first), row(D, nxt), row(1, nxt), const((1, LANES)), const((1, D)),
                  const((D, IN_COLS)), const((1, RET_V_WIDTH))] + [slab(m) for m in later],
        out_specs=[pl.BlockSpec((None, LANE_TILES, tm // S5_TAPS, S5_TAPS * LANES),
                                lambda t: (cur(t) // nt, 0, cur(t) % nt, 0)),
                   row(S5_WIDTH, cur), row(RET_V_WIDTH, lag)] + [slab(m) for m in later],
        out_shape=[jax.ShapeDtypeStruct((B, LANE_TILES, L // S5_TAPS, S5_TAPS * LANES), BF16),
                   jax.ShapeDtypeStruct((B, L, S5_WIDTH), BF16),
                   jax.ShapeDtypeStruct((B, L, RET_V_WIDTH), BF16)]
        + [jax.ShapeDtypeStruct(m.shape, BF16) for m in later],
        scratch_shapes=[pltpu.VMEM((tm, COL_U), BF16),
                        pltpu.VMEM((tm, COL_U), BF16),
                        pltpu.VMEM((tm, D), BF16), pltpu.VMEM((tm, LANES), F32), pltpu.VMEM((tm, LANES), F32),
                        pltpu.VMEM((tm, D), BF16), pltpu.VMEM((tm, LANES), F32), pltpu.VMEM((tm, LANES), F32),
                        pltpu.VMEM((LANE_TILES, tm, LANES), F32),
                        pltpu.VMEM((RET_HEADS // 2, LANES, 2 * RET_DV), F32),
                        pltpu.VMEM((RET_HEADS // 2, CHUNK, 2 * CHUNK), F32),
                        pltpu.VMEM((RET_HEADS // 2, CHUNK, 2 * RET_DV), F32),
                        pltpu.VMEM((RET_HEADS // 2, CHUNK, LANES), F32)],
        compiler_params=pltpu.CompilerParams(
            dimension_semantics=("arbitrary",), vmem_limit_bytes=VMEM_LIMIT),
        name="proj_ret",
    )(x, pos3, x, pos3, inv, g, w, gn, *later)
    return outs[:3], outs[3:]


def _retention_constants(decay_ref, qw_ref, kw_ref):
    i = lax.broadcasted_iota(jnp.int32, (CHUNK, 2 * CHUNK), 0).astype(F32)
    c = lax.broadcasted_iota(jnp.int32, (CHUNK, 2 * CHUNK), 1)
    j = (c % CHUNK).astype(F32)
    diff = i - j
    lane = lax.broadcasted_iota(jnp.int32, (CHUNK, LANES), 1)
    ik = lax.broadcasted_iota(jnp.int32, (CHUNK, LANES), 0).astype(F32)
    for p in range(RET_HEADS // 2):
        lg = jnp.where(c < CHUNK, LOG_G[2 * p], LOG_G[2 * p + 1])
        decay_ref[p] = jnp.where(diff >= 0.0, jnp.exp(lg * jnp.maximum(diff, 0.0)), 0.0)
        qw_ref[p] = jnp.exp(lg * (i + 1.0))
        lgk = jnp.where(lane < RET_DK, LOG_G[2 * p], LOG_G[2 * p + 1])
        kw_ref[p] = jnp.exp(lgk * (CHUNK - 1.0 - ik))


def _retention_items(src_ref, gn_ref, o_ref, state_ref, decay_ref, qw_ref, kw_ref):
    pairs = RET_HEADS // 2
    low = lax.broadcasted_iota(jnp.int32, (1, LANES), 1) < RET_DK
    srow = lax.broadcasted_iota(jnp.int32, (LANES, 2 * RET_DV), 0)
    scol = lax.broadcasted_iota(jnp.int32, (LANES, 2 * RET_DV), 1)
    own_block = (srow < RET_DK) == (scol < RET_DV)
    scol1 = lax.broadcasted_iota(jnp.int32, (1, 2 * RET_DV), 1)
    zero_v = jnp.zeros((CHUNK, RET_DV), BF16)

    def pair_step(c, p):
        rows = slice(c * CHUNK, (c + 1) * CHUNK)
        qt = src_ref[rows, COL_Q + p * LANES:COL_Q + (p + 1) * LANES]
        kt = src_ref[rows, COL_K + p * LANES:COL_K + (p + 1) * LANES]
        vp = src_ref[rows, COL_V + 2 * p * RET_DV:COL_V + 2 * (p + 1) * RET_DV]
        state = state_ref[p]
        cross = _dot(qt, state.astype(BF16))
        kw = (kt.astype(F32) * kw_ref[p]).astype(BF16)
        upd = lax.dot_general(kw, vp, (((0,), (0,)), ((), ())), preferred_element_type=F32)
        chunk_decay = jnp.where(scol1 < RET_DV,
                                math.exp(LOG_G[2 * p] * CHUNK), math.exp(LOG_G[2 * p + 1] * CHUNK))
        state_ref[p] = state * chunk_decay + jnp.where(own_block, upd, 0.0)

        zero_k = jnp.zeros_like(kt)
        k_rows = jnp.concatenate([jnp.where(low, kt, zero_k), jnp.where(low, zero_k, kt)], axis=0)
        s = lax.dot_general(qt, k_rows, (((1,), (1,)), ((), ())), preferred_element_type=F32)
        pm = (s * decay_ref[p]).astype(BF16)
        v_diag = jnp.concatenate(
            [jnp.concatenate([vp[:, 0:RET_DV], zero_v], axis=1),
             jnp.concatenate([zero_v, vp[:, RET_DV:2 * RET_DV]], axis=1)], axis=0)
        o2 = _dot(pm, v_diag) + qw_ref[p] * cross
        for e in range(2):
            h = 2 * p + e
            cols = slice(h * RET_DV, (h + 1) * RET_DV)
            o = o2[:, e * RET_DV:(e + 1) * RET_DV]
            mu = jnp.mean(o, axis=-1, keepdims=True)
            oc = o - mu
            var = jnp.mean(oc * oc, axis=-1, keepdims=True)
            on = oc * lax.rsqrt(var + EPS) * gn_ref[:, cols]
            gate = src_ref[rows, COL_GR + h * RET_DV:COL_GR + (h + 1) * RET_DV].astype(F32)
            o_ref[rows, cols] = (on * gate).astype(BF16)

    return [functools.partial(pair_step, c, p)
            for c in range(src_ref.shape[0] // CHUNK) for p in range(pairs)]


def _s5_scan_kernel(u_ref, wb_ref, consts_ref, u2_ref, wck_ref, y_ref, *scratch, tiles_per_seq):
    st_refs = scratch[0:3]
    sb_ref, ys_ref, carry_ref = scratch[3:]
    n = pl.program_id(0)
    s = S5_TAPS
    nblk = u_ref.shape[0]
    half = TILE_STATES

    @pl.when(n == 0)
    def _init():
        for ref in scratch[0:3]:
            ref[...] = jnp.zeros_like(ref)

    @pl.when((n == 0) | ((n - 1) % tiles_per_seq == 0))
    def _reset():
        carry_ref[...] = jnp.zeros_like(carry_ref)

    ncol = 2 * LANES

    def increments(slot):
        st_ref = st_refs[slot]

        def piece(c):
            z = _dot(u_ref[...], wb_ref[:, c * ncol:(c + 1) * ncol])
            for i in range(ncol // LANES):
                st_ref[c * (ncol // LANES) + i] = z[:, i * LANES:(i + 1) * LANES]

        return [functools.partial(piece, c) for c in range(2 * half // ncol)]

    ntile = half // LANES

    def scan(slot, row_pieces):
        st_ref = st_refs[slot]
        run = S5_RUN
        span = run * SUBLANES
        row = lax.broadcasted_iota(jnp.int32, (SUBLANES, LANES), 0)
        last = SUBLANES - 1
        groups = nblk // span
        per = -(-groups // row_pieces)
        tiles_per_piece = S5_SCAN_LANES // LANES

        def tile_group(t, g, carry):
            lanes = slice(t * LANES, (t + 1) * LANES)
            const = lambda idx: (consts_ref[idx, 0, :, lanes], consts_ref[idx, 1, :, lanes])
            cr, ci = carry
            rows = [pl.ds(g * span + j, SUBLANES, stride=run) for j in range(run)]
            loc = [(st_ref[t, rows[0], :], st_ref[ntile + t, rows[0], :])]
            for j in range(1, run):
                pr, pi = _cmul(*const(0), *loc[-1])
                loc.append((pr + st_ref[t, rows[j], :], pi + st_ref[ntile + t, rows[j], :]))
            fr, fi = loc[-1]
            for idx, d in enumerate((1, 2, 4)):
                pr, pi = _cmul(*const(1 + idx), pltpu.roll(fr, d, 0), pltpu.roll(fi, d, 0))
                fr, fi = fr + pr, fi + pi
            pr, pi = _cmul(*const(4), cr, ci)
            fr, fi = fr + pr, fi + pi
            er = jnp.where(row == 0, cr, pltpu.roll(fr, 1, 0))
            ei = jnp.where(row == 0, ci, pltpu.roll(fi, 1, 0))
            st_ref[t, rows[0], :] = er
            st_ref[ntile + t, rows[0], :] = ei
            for j in range(1, run):
                er, ei = _cmul(*const(0), er, ei)
                st_ref[t, rows[j], :] = loc[j - 1][0] + er
                st_ref[ntile + t, rows[j], :] = loc[j - 1][1] + ei
            return (jnp.broadcast_to(fr[last:, :], (SUBLANES, LANES)),
                    jnp.broadcast_to(fi[last:, :], (SUBLANES, LANES)))

        def piece(q, i, carries):
            tiles = range(q * tiles_per_piece, (q + 1) * tiles_per_piece)
            if i == 0:
                for t in tiles:
                    carries[t] = (carry_ref[:, t * LANES:(t + 1) * LANES],
                                  carry_ref[:, half + t * LANES:half + (t + 1) * LANES])
            for g in range(i * per, min((i + 1) * per, groups)):
                for t in tiles:
                    carries[t] = tile_group(t, g, carries[t])
            if i == row_pieces - 1:
                for t in tiles:
                    carry_ref[:, t * LANES:(t + 1) * LANES] = carries[t][0]
                    carry_ref[:, half + t * LANES:half + (t + 1) * LANES] = carries[t][1]

        carries = {}
        return [functools.partial(piece, q, i, carries)
                for q in range(ntile // tiles_per_piece) for i in range(row_pieces)]

    def outputs(slot):
        taps_per_piece = ncol // LANES

        def cast():
            for t in range(2 * ntile):
                sb_ref[:, t * LANES:(t + 1) * LANES] = st_refs[slot][t].astype(BF16)

        def piece(c):
            lhs = jnp.concatenate([sb_ref[...], u2_ref[...]], axis=1)
            yall = _dot(lhs, wck_ref[:, c * ncol:(c + 1) * ncol])
            for i in range(taps_per_piece):
                tap = c * taps_per_piece + i
                ys_ref[pl.ds(tap, nblk, stride=s), :] = yall[:, i * LANES:(i + 1) * LANES]

        def finish():
            y_ref[...] = jax.nn.gelu(ys_ref[...]).astype(BF16)

        return [cast] + [functools.partial(piece, c) for c in range(s * LANES // ncol)] + [finish]

    for phase in range(3):
        @pl.when(n % 3 == phase)
        def _steps(phase=phase):
            mxu = increments(phase) + outputs((phase + 1) % 3)
            vpu = scan((phase + 2) % 3, 3)
            for i, piece in enumerate(mxu):
                piece()
                if i < len(vpu):
                    vpu[i]()
            for piece in vpu[len(mxu):]:
                piece()


def _s5_scan(u, wb, wck, consts):
    B = u.shape[0]
    s = S5_TAPS
    L = u.shape[2] * s
    tm = TM_S5
    nt = L // tm
    ntiles = LANE_TILES * B * nt
    tile_of = lambda n, lag: jnp.clip(n - lag, 0, ntiles - 1)
    lane_tile = lambda n, lag: tile_of(n, lag) // (B * nt)

    def tok(lag, rows, width):
        def index(n):
            i = tile_of(n, lag)
            return ((i // nt) % B, i // (B * nt), i % nt, 0)
        return pl.BlockSpec((None, None, rows, width), index)

    def par(lag, *shape):
        return pl.BlockSpec((None,) + shape, lambda n: (lane_tile(n, lag),) + (0,) * len(shape))

    nblk = tm // s
    return pl.pallas_call(
        functools.partial(_s5_scan_kernel, tiles_per_seq=nt),
        grid=(ntiles + 2,),
        in_specs=[tok(0, nblk, s * LANES), par(0, s * LANES, 2 * TILE_STATES),
                  par(1, 5, 2, SUBLANES, TILE_STATES), tok(2, nblk, s * LANES),
                  par(2, 2 * TILE_STATES + s * LANES, s * LANES)],
        out_specs=tok(2, tm, LANES),
        out_shape=jax.ShapeDtypeStruct((B, LANE_TILES, L, LANES), BF16),
        scratch_shapes=([pltpu.VMEM((2 * TILE_STATES // LANES, nblk, LANES), F32)] * 3
                        + [pltpu.VMEM((nblk, 2 * TILE_STATES), BF16), pltpu.VMEM((tm, LANES), F32),
                           pltpu.VMEM((SUBLANES, 2 * TILE_STATES), F32)]),
        compiler_params=pltpu.CompilerParams(
            dimension_semantics=("arbitrary",), vmem_limit_bytes=VMEM_LIMIT),
        name="s5_scan",
    )(u, wb, consts, u, wck)


def _out_xattn_kernel(x_ref, ret_ref, y_ref, gs_ref, gw_ref, gb_ref, wout_ref, g2_ref, wq_ref,
                      ka_ref, va_ref, wo_ref, gf_ref, o_ref):
    y = jnp.concatenate([y_ref[j] for j in range(LANE_TILES)], axis=1)
    z = _dot(y, gw_ref[...]) + gb_ref[...]
    ssm = (y.astype(F32) * _sigmoid(z) * gs_ref[...].astype(F32)).astype(BF16)
    x1 = (x_ref[...] + _dot(ret_ref[...], wout_ref[0:RET_V_WIDTH, :])
          + _dot(ssm, wout_ref[RET_V_WIDTH:RET_V_WIDTH + S5_WIDTH, :]))
    h2 = _rms(x1, g2_ref[...]).astype(BF16)
    qa = _dot(h2, wq_ref[...]).astype(BF16)
    heads = []
    for h in range(XA_HEADS):
        cols = slice(h * XA_DH, (h + 1) * XA_DH)
        s = lax.dot_general(qa[:, cols], ka_ref[:, cols], (((1,), (1,)), ((), ())),
                            preferred_element_type=F32)
        e = jnp.exp(s - jnp.max(s, axis=-1, keepdims=True))
        l = jnp.sum(e, axis=-1, keepdims=True)
        heads.append((_dot(e.astype(BF16), va_ref[:, cols]) * (1.0 / l)).astype(BF16))
    o = jnp.concatenate(heads, axis=-1)
    x2 = x1 + _dot(o, wo_ref[...])
    o_ref[...] = _rms(x2, gf_ref[...])


def _out_xattn(x, ret, y, gs, gw, gb, wout, g2, wq, ka, va, wo, gf):
    B, L, D = x.shape
    tm = TM_OUT
    row = pl.BlockSpec((None, tm, D), lambda b, t: (b, t, 0))
    const = lambda shape: pl.BlockSpec(shape, lambda b, t: (0,) * len(shape))
    mem = pl.BlockSpec((None, MEM_LEN, D), lambda b, t: (b, 0, 0))
    ytile = pl.BlockSpec((None, LANE_TILES, tm, LANES), lambda b, t: (b, 0, t, 0))
    return pl.pallas_call(
        _out_xattn_kernel,
        grid=(B, L // tm),
        in_specs=[row, row, ytile, row, const((D, D)), const((1, D)), const(wout.shape), const((1, D)),
                  const((D, D)), mem, mem, const((D, D)), const((1, D))],
        out_specs=row,
        out_shape=jax.ShapeDtypeStruct((B, L, D), F32),
        compiler_params=pltpu.CompilerParams(
            dimension_semantics=("arbitrary", "arbitrary"), vmem_limit_bytes=VMEM_LIMIT),
        name="out_xattn",
    )(x, ret, y, gs, gw, gb, wout, g2, wq, ka, va, wo, gf)


def _rope_inv_lanes():
    half = RET_DK // 2
    inv = ROPE_BASE ** (-np.arange(half, dtype=np.float64) / half)
    return jnp.asarray(np.tile(inv, LANES // half).reshape(1, LANES), F32)


def kernel(x, mem, positions, norm1_g, w_in, ret_gn_g, s5_a_re, s5_a_im, s5_log_dt, s5_b_re, s5_b_im, s5_c_re, s5_c_im, s5_d, s5_glu_w, s5_glu_b, w_out, norm2_g, norm_mem_g, xa_wq, xa_wk, xa_wv, xa_wo, norm_f_g):
    B, L, D = x.shape
    l = 0
    rowvec = lambda v: v.reshape(1, -1)

    wb, wck, consts, win = _s5_prep(s5_a_re[l], s5_a_im[l], s5_log_dt[l], s5_b_re[l], s5_b_im[l],
                                    s5_c_re[l], s5_c_im[l], s5_d[l], w_in[l])
    (u, g_s5, ret), (glu_w, wout, wq, wk, wv, wo) = _proj_ret(
        x, positions.reshape(B, L, 1), _rope_inv_lanes(), rowvec(norm1_g[l]), win, rowvec(ret_gn_g[l]),
        [(s5_glu_w[l], 1.0), (w_out[l], 1.0), (xa_wq[l], XA_DH ** -0.5), (xa_wk[l], 1.0), (xa_wv[l], 1.0),
         (xa_wo[l], 1.0)])
    ka, va = _mem_kv(mem, rowvec(norm_mem_g[l]), wk, wv)
    y = _s5_scan(u, wb, wck, consts)
    return _out_xattn(x, ret, y, g_s5, glu_w, rowvec(s5_glu_b[l]), wout,
                      rowvec(norm2_g[l]), wq, ka, va, wo, rowvec(norm_f_g))
```

```python
import functools
import math

import numpy as np
import jax
import jax.numpy as jnp
from jax import lax
from jax.experimental import pallas as pl
from jax.experimental.pallas import tpu as pltpu

F32 = jnp.float32
BF16 = jnp.bfloat16

D_MODEL = 1024
MEM_LEN = 256
EPS = 1e-6
ROPE_BASE = 10000.0

RET_HEADS = 8
RET_QK_WIDTH = 512
RET_V_WIDTH = 1024
RET_DK = 64
RET_DV = 128
CHUNK = 128

S5_WIDTH = 1024
S5_GROUP = 16
S5_GROUPS = 64
S5_STATE = 64
S5_NSTATE = S5_GROUPS * S5_STATE

IN_COLS = 5120
COL_Q, COL_K, COL_V, COL_GR, COL_U, COL_GS = 0, 512, 1024, 2048, 3072, 4096

XA_HEADS = 4
XA_DH = 256

LANES = 128
SUBLANES = 8
VMEM_LIMIT = 48 * 1024 * 1024

LANE_TILES = S5_WIDTH // LANES
TILE_GROUPS = LANES // S5_GROUP
TILE_STATES = S5_NSTATE // LANE_TILES
S5_TAPS = 4
S5_SCAN_LANES = 2 * LANES
S5_RUN = 4

LOG_G = tuple(math.log1p(-(2.0 ** (-5.0 - h))) for h in range(RET_HEADS))

TM_PROJ = 512
TM_S5 = 4096
TM_OUT = 1024


def _rms(x, g):
    ms = jnp.mean(x * x, axis=-1, keepdims=True)
    return x * lax.rsqrt(ms + EPS) * g


def _sigmoid(z):
    return 1.0 / (1.0 + jnp.exp(-z))


def _silu(g):
    return g * _sigmoid(g)


def _dot(a, b):
    return jnp.dot(a, b, preferred_element_type=F32)


def _cmul(ar, ai, br, bi):
    return ar * br - ai * bi, ar * bi + ai * br


def _s5_discretise(ar, ai, ldt):
    dt = jnp.exp(ldt)
    mag = jnp.exp(ar * dt)
    p_re = mag * jnp.cos(ai * dt)
    p_im = mag * jnp.sin(ai * dt)
    den = ar * ar + ai * ai
    nr, ni = p_re - 1.0, p_im
    f_re = (nr * ar + ni * ai) / den
    f_im = (ni * ar - nr * ai) / den
    return p_re, p_im, f_re, f_im


def _powers(p_re, p_im, n):
    pw = [(jnp.ones_like(p_re), jnp.zeros_like(p_im))]
    for _ in range(n):
        pw.append(_cmul(pw[-1][0], pw[-1][1], p_re, p_im))
    return pw


def _block_diag(x):
    tiled = jnp.concatenate([x] * TILE_GROUPS, axis=1)
    r = lax.broadcasted_iota(jnp.int32, tiled.shape, 0)
    c = lax.broadcasted_iota(jnp.int32, tiled.shape, 1)
    return jnp.where(r // S5_GROUP == c // S5_STATE, tiled, 0.0)


def _s5_prep_kernel(a_ref, ag_ref, b_ref, c_ref, d_ref, win_ref, wb_ref, wck_ref, consts_ref, win16_ref):
    s = S5_TAPS
    half = TILE_STATES
    def per_row(v):
        return jnp.broadcast_to(v[:, None, :], (TILE_GROUPS, S5_GROUP, S5_STATE)).reshape(LANES, S5_STATE)

    g_re, g_im, gf_re, gf_im = _s5_discretise(ag_ref[0], ag_ref[1], ag_ref[2])
    gw = [(per_row(r), per_row(i)) for r, i in _powers(g_re, g_im, s)]
    bbr, bbi = _cmul(per_row(gf_re), per_row(gf_im), b_ref[0], b_ref[1])
    for k in range(s):
        wr, wi = _cmul(bbr, bbi, *gw[s - 1 - k])
        wb_ref[k * LANES:(k + 1) * LANES, 0:half] = _block_diag(wr).astype(BF16)
        wb_ref[k * LANES:(k + 1) * LANES, half:2 * half] = _block_diag(wi).astype(BF16)

    bb = jnp.concatenate([bbr, bbi], axis=1)
    r = lax.broadcasted_iota(jnp.int32, (LANES, LANES), 0)
    c = lax.broadcasted_iota(jnp.int32, (LANES, LANES), 1)
    same_group = r // S5_GROUP == c // S5_GROUP
    taps = []
    for d in range(s + 1):
        er, ei = _cmul(c_ref[0], c_ref[1], *gw[d])
        if d >= 1:
            cwt = jnp.concatenate([_block_diag(er), -_block_diag(ei)], axis=1)
            wck_ref[0:2 * half, (d - 1) * LANES:d * LANES] = cwt.T.astype(BF16)
        if d < s:
            t = lax.dot_general(bb, jnp.concatenate([er, -ei], axis=1), (((1,), (1,)), ((), ())),
                                preferred_element_type=F32, precision=lax.Precision.HIGHEST)
            if d == 0:
                t = t + jnp.where(r == c, d_ref[...], 0.0)
            taps.append(jnp.where(same_group, t, 0.0))
    zero = jnp.zeros((LANES, LANES), BF16)
    for m in range(s):
        for i in range(s):
            blk = taps[i - m].astype(BF16) if m <= i else zero
            wck_ref[2 * half + m * LANES:2 * half + (m + 1) * LANES, i * LANES:(i + 1) * LANES] = blk

    p_re, p_im, _, _ = _s5_discretise(a_ref[0:1, :], a_ref[1:2, :], a_ref[2:3, :])
    pw = _powers(p_re, p_im, s)
    shape = (SUBLANES, half)
    qr = _powers(pw[s][0], pw[s][1], S5_RUN)
    consts_ref[0, 0] = jnp.broadcast_to(qr[1][0], shape)
    consts_ref[0, 1] = jnp.broadcast_to(qr[1][1], shape)
    rp = _powers(qr[S5_RUN][0], qr[S5_RUN][1], SUBLANES)
    row = lax.broadcasted_iota(jnp.int32, shape, 0)
    for idx, d in enumerate((1, 2, 4)):
        consts_ref[1 + idx, 0] = jnp.where(row >= d, rp[d][0], 0.0)
        consts_ref[1 + idx, 1] = jnp.where(row >= d, rp[d][1], 0.0)
    cr = jnp.zeros(shape, F32)
    ci = jnp.zeros(shape, F32)
    for k in range(SUBLANES):
        cr = jnp.where(row == k, rp[k + 1][0], cr)
        ci = jnp.where(row == k, rp[k + 1][1], ci)
    consts_ref[4, 0] = cr
    consts_ref[4, 1] = ci

    win16_ref[...] = win_ref[...].astype(BF16)


def _s5_prep(a_re, a_im, log_dt, b_re, b_im, c_re, c_im, d, w_in):
    s = S5_TAPS
    ldt = jnp.broadcast_to(log_dt[:, None], (S5_GROUPS, S5_STATE))
    ag = jnp.stack([a_re, a_im, ldt])
    a3 = ag.reshape(3, S5_NSTATE)
    bt = jnp.stack([b_re, b_im]).transpose(0, 1, 3, 2).reshape(2, S5_WIDTH, S5_STATE)
    ct = jnp.stack([c_re, c_im]).reshape(2, S5_WIDTH, S5_STATE)
    tile = lambda *shape: pl.BlockSpec((None,) + shape, lambda j: (j,) + (0,) * len(shape))
    return pl.pallas_call(
        _s5_prep_kernel,
        grid=(LANE_TILES,),
        in_specs=[pl.BlockSpec((3, TILE_STATES), lambda j: (0, j)),
                  pl.BlockSpec((3, TILE_GROUPS, S5_STATE), lambda j: (0, j, 0)),
                  pl.BlockSpec((2, LANES, S5_STATE), lambda j: (0, j, 0)),
                  pl.BlockSpec((2, LANES, S5_STATE), lambda j: (0, j, 0)),
                  pl.BlockSpec((1, LANES), lambda j: (0, j)),
                  pl.BlockSpec((D_MODEL // LANE_TILES, IN_COLS), lambda j: (j, 0))],
        out_specs=(tile(s * LANES, 2 * TILE_STATES),
                   tile(2 * TILE_STATES + s * LANES, s * LANES),
                   tile(5, 2, SUBLANES, TILE_STATES),
                   pl.BlockSpec((D_MODEL // LANE_TILES, IN_COLS), lambda j: (j, 0))),
        out_shape=(jax.ShapeDtypeStruct((LANE_TILES, s * LANES, 2 * TILE_STATES), BF16),
                   jax.ShapeDtypeStruct((LANE_TILES, 2 * TILE_STATES + s * LANES, s * LANES), BF16),
                   jax.ShapeDtypeStruct((LANE_TILES, 5, 2, SUBLANES, TILE_STATES), F32),
                   jax.ShapeDtypeStruct((D_MODEL, IN_COLS), BF16)),
        compiler_params=pltpu.CompilerParams(
            dimension_semantics=("arbitrary",), vmem_limit_bytes=VMEM_LIMIT,
            allow_input_fusion=(True, True, True, True, True, False)),
        name="s5_prep",
    )(a3, ag, bt, ct, d.reshape(1, S5_WIDTH), w_in)


def _mem_kv_kernel(mem_ref, g_ref, wk_ref, wv_ref, k_ref, v_ref):
    m = _rms(mem_ref[...], g_ref[...]).astype(BF16)
    k_ref[...] = _dot(m, wk_ref[...]).astype(BF16)
    v_ref[...] = _dot(m, wv_ref[...]).astype(BF16)


def _mem_kv(mem, g, wk, wv):
    B, M, D = mem.shape
    return pl.pallas_call(
        _mem_kv_kernel,
        grid=(B,),
        in_specs=[pl.BlockSpec((None, M, D), lambda b: (b, 0, 0)),
                  pl.BlockSpec((1, D), lambda b: (0, 0)),
                  pl.BlockSpec((D, D), lambda b: (0, 0)),
                  pl.BlockSpec((D, D), lambda b: (0, 0))],
        out_specs=(pl.BlockSpec((None, M, D), lambda b: (b, 0, 0)),
                   pl.BlockSpec((None, M, D), lambda b: (b, 0, 0))),
        out_shape=(jax.ShapeDtypeStruct((B, M, D), BF16),
                   jax.ShapeDtypeStruct((B, M, D), BF16)),
        compiler_params=pltpu.CompilerParams(
            dimension_semantics=("arbitrary",), vmem_limit_bytes=VMEM_LIMIT),
        name="mem_kv",
    )(mem, g, wk, wv)


def _token_prologue(x_ref, pos_ref, inv_ref, g_ref, h_ref, cos_ref, sin_ref):
    h_ref[...] = _rms(x_ref[...], g_ref[...]).astype(BF16)
    lane = lax.broadcasted_iota(jnp.int32, (1, LANES), 1)
    nfreq = RET_DK // 2
    group = lane // nfreq
    nblock = LANES // nfreq
    rows = x_ref.shape[0] // nblock
    pos = pos_ref[...].astype(F32)
    pos4 = jnp.zeros((rows, LANES), F32)
    for k in range(nblock):
        pos4 = jnp.where(group == k, pos[k * rows:(k + 1) * rows, :], pos4)
    ang4 = pos4 * inv_ref[...]

    def spread(t4):
        blocks = []
        for k in range(nblock):
            m = jnp.where(group == k, t4, 0.0)
            m = m + pltpu.roll(m, nfreq, 1)
            blocks.append(m + pltpu.roll(m, 2 * nfreq, 1))
        return jnp.concatenate(blocks, axis=0)

    first_half = (lane % RET_DK) < nfreq
    sin = spread(jnp.sin(ang4))
    cos_ref[...] = spread(jnp.cos(ang4))
    sin_ref[...] = jnp.where(first_half, -sin, sin)


def _proj_ret_kernel(x0_ref, pos0_ref, x_ref, pos_ref, inv_ref, g_ref, w_ref, gn_ref, *rest,
                     tiles_per_seq, num_tiles, num_casts):
    f32_refs, rest = rest[:num_casts], rest[num_casts:]
    (u_ref, gs_ref, ret_ref), rest = rest[:3], rest[3:]
    bf16_refs, rest = rest[:num_casts], rest[num_casts:]
    (qa_ref, qb_ref, ha_ref, ca_ref, sa_ref, hb_ref, cb_ref, sb_ref, us_ref,
     state_ref, decay_ref, qw_ref, kw_ref) = rest
    tiles = (qa_ref, qb_ref)
    norms = ((ha_ref, ca_ref, sa_ref), (hb_ref, cb_ref, sb_ref))
    t = pl.program_id(0)

    @pl.when(t == 0)
    def _init():
        _retention_constants(decay_ref, qw_ref, kw_ref)
        qb_ref[...] = jnp.zeros_like(qb_ref)
        _token_prologue(x0_ref, pos0_ref, inv_ref, g_ref, *norms[0])

    @pl.when((t == 0) | (t % tiles_per_seq == 1))
    def _reset():
        state_ref[...] = jnp.zeros_like(state_ref)

    def retention_of(parity):
        return _retention_items(tiles[parity], gn_ref, ret_ref, state_ref, decay_ref, qw_ref, kw_ref)

    @pl.when(t == num_tiles)
    def _drain():
        for r in retention_of((num_tiles - 1) % 2):
            r()

    def step(parity):
        cur_ref = tiles[parity]
        ret_items = retention_of(1 - parity)
        h_ref, cos_ref, sin_ref = norms[parity]
        hn_ref, cn_ref, sn_ref = norms[1 - parity]
        h = h_ref[...]
        cos = cos_ref[...]
        sin_signed = sin_ref[...]
        lane = lax.broadcasted_iota(jnp.int32, (1, LANES), 1)
        first_half = (lane % RET_DK) < (RET_DK // 2)

        def rope(p):
            partner = jnp.where(first_half,
                                pltpu.roll(p, LANES - RET_DK // 2, 1),
                                pltpu.roll(p, RET_DK // 2, 1))
            return p * cos + partner * sin_signed

        def proj_q():
            pq = _dot(h, w_ref[:, COL_Q:COL_K])
            for i in range(RET_QK_WIDTH // LANES):
                sl = slice(i * LANES, (i + 1) * LANES)
                cur_ref[:, COL_Q + i * LANES:COL_Q + (i + 1) * LANES] = rope(pq[:, sl]).astype(BF16)

        def proj_k():
            pk = _dot(h, w_ref[:, COL_K:COL_V])
            for i in range(RET_QK_WIDTH // LANES):
                sl = slice(i * LANES, (i + 1) * LANES)
                cur_ref[:, COL_K + i * LANES:COL_K + (i + 1) * LANES] = (
                    rope(pk[:, sl]) * (RET_DK ** -0.5)).astype(BF16)

        def proj_v():
            cur_ref[:, COL_V:COL_GR] = _dot(h, w_ref[:, COL_V:COL_GR]).astype(BF16)

        def proj_gate():
            cur_ref[:, COL_GR:COL_U] = _silu(_dot(h, w_ref[:, COL_GR:COL_U])).astype(BF16)

        def proj_u():
            pu = _dot(h, w_ref[:, COL_U:COL_GS])
            for j in range(LANE_TILES):
                us_ref[j] = pu[:, j * LANES:(j + 1) * LANES]
            for j in range(LANE_TILES):
                for k in range(S5_TAPS):
                    u_ref[j, :, k * LANES:(k + 1) * LANES] = us_ref[
                        j, pl.ds(k, us_ref.shape[1] // S5_TAPS, stride=S5_TAPS), :].astype(BF16)

        def proj_gs():
            gs_ref[...] = _silu(_dot(h, w_ref[:, COL_GS:IN_COLS])).astype(BF16)

        proj_items = [proj_q, proj_k, proj_v, proj_gate, proj_u, proj_gs]
        per = -(-len(ret_items) // len(proj_items))
        for i, item in enumerate(proj_items):
            for r in ret_items[i * per:(i + 1) * per]:
                r()
            item()
        _token_prologue(x_ref, pos_ref, inv_ref, g_ref, hn_ref, cn_ref, sn_ref)
        for src_ref, dst_ref in zip(f32_refs, bf16_refs):
            dst_ref[...] = src_ref[...].astype(BF16)

    for parity in range(2):
        pl.when((t < num_tiles) & (t % 2 == parity))(functools.partial(step, parity))


def _proj_ret(x, pos3, inv, g, w, gn, later):
    B, L, D = x.shape
    tm = TM_PROJ
    nt = L // tm
    n = B * nt
    cur = lambda t: jnp.minimum(t, n - 1)
    nxt = lambda t: jnp.minimum(t + 1, n - 1)
    lag = lambda t: jnp.maximum(t - 1, 0)
    first = lambda t: 0 * t
    row = lambda width, tile: pl.BlockSpec((None, tm, width), lambda t: (tile(t) // nt, tile(t) % nt, 0))
    const = lambda shape: pl.BlockSpec(shape, lambda t: (0,) * len(shape))
    slab = lambda m: pl.BlockSpec((m.shape[0] // n, m.shape[1]), lambda t: (cur(t), 0))
    outs = pl.pallas_call(
        functools.partial(_proj_ret_kernel, tiles_per_seq=nt, num_tiles=n, num_casts=len(later)),
        grid=(n + 1,),
        in_specs=[row(D, first), row(1, first), row(D, nxt), row(1, nxt), const((1, LANES)), const((1, D)),
                  const((D, IN_COLS)), const((1, RET_V_WIDTH))] + [slab(m) for m in later],
        out_specs=[pl.BlockSpec((None, LANE_TILES, tm // S5_TAPS, S5_TAPS * LANES),
                                lambda t: (cur(t) // nt, 0, cur(t) % nt, 0)),
                   row(S5_WIDTH, cur), row(RET_V_WIDTH, lag)] + [slab(m) for m in later],
        out_shape=[jax.ShapeDtypeStruct((B, LANE_TILES, L // S5_TAPS, S5_TAPS * LANES), BF16),
                   jax.ShapeDtypeStruct((B, L, S5_WIDTH), BF16),
                   jax.ShapeDtypeStruct((B, L, RET_V_WIDTH), BF16)]
        + [jax.ShapeDtypeStruct(m.shape, BF16) for m in later],
        scratch_shapes=[pltpu.VMEM((tm, COL_U), BF16),
                        pltpu.VMEM((tm, COL_U), BF16),
                        pltpu.VMEM((tm, D), BF16), pltpu.VMEM((tm, LANES), F32), pltpu.VMEM((tm, LANES), F32),
                        pltpu.VMEM((tm, D), BF16), pltpu.VMEM((tm, LANES), F32), pltpu.VMEM((tm, LANES), F32),
                        pltpu.VMEM((LANE_TILES, tm, LANES), F32),
                        pltpu.VMEM((RET_HEADS // 2, LANES, 2 * RET_DV), F32),
                        pltpu.VMEM((RET_HEADS // 2, CHUNK, 2 * CHUNK), F32),
                        pltpu.VMEM((RET_HEADS // 2, CHUNK, 2 * RET_DV), F32),
                        pltpu.VMEM((RET_HEADS // 2, CHUNK, LANES), F32)],
        compiler_params=pltpu.CompilerParams(
            dimension_semantics=("arbitrary",), vmem_limit_bytes=VMEM_LIMIT),
        name="proj_ret",
    )(x, pos3, x, pos3, inv, g, w, gn, *later)
    return outs[:3], outs[3:]


def _retention_constants(decay_ref, qw_ref, kw_ref):
    i = lax.broadcasted_iota(jnp.int32, (CHUNK, 2 * CHUNK), 0).astype(F32)
    c = lax.broadcasted_iota(jnp.int32, (CHUNK, 2 * CHUNK), 1)
    j = (c % CHUNK).astype(F32)
    diff = i - j
    lane = lax.broadcasted_iota(jnp.int32, (CHUNK, LANES), 1)
    ik = lax.broadcasted_iota(jnp.int32, (CHUNK, LANES), 0).astype(F32)
    for p in range(RET_HEADS // 2):
        lg = jnp.where(c < CHUNK, LOG_G[2 * p], LOG_G[2 * p + 1])
        decay_ref[p] = jnp.where(diff >= 0.0, jnp.exp(lg * jnp.maximum(diff, 0.0)), 0.0)
        qw_ref[p] = jnp.exp(lg * (i + 1.0))
        lgk = jnp.where(lane < RET_DK, LOG_G[2 * p], LOG_G[2 * p + 1])
        kw_ref[p] = jnp.exp(lgk * (CHUNK - 1.0 - ik))


def _retention_items(src_ref, gn_ref, o_ref, state_ref, decay_ref, qw_ref, kw_ref):
    pairs = RET_HEADS // 2
    low = lax.broadcasted_iota(jnp.int32, (1, LANES), 1) < RET_DK
    srow = lax.broadcasted_iota(jnp.int32, (LANES, 2 * RET_DV), 0)
    scol = lax.broadcasted_iota(jnp.int32, (LANES, 2 * RET_DV), 1)
    own_block = (srow < RET_DK) == (scol < RET_DV)
    scol1 = lax.broadcasted_iota(jnp.int32, (1, 2 * RET_DV), 1)
    zero_v = jnp.zeros((CHUNK, RET_DV), BF16)

    def pair_step(c, p):
        rows = slice(c * CHUNK, (c + 1) * CHUNK)
        qt = src_ref[rows, COL_Q + p * LANES:COL_Q + (p + 1) * LANES]
        kt = src_ref[rows, COL_K + p * LANES:COL_K + (p + 1) * LANES]
        vp = src_ref[rows, COL_V + 2 * p * RET_DV:COL_V + 2 * (p + 1) * RET_DV]
        state = state_ref[p]
        cross = _dot(qt, state.astype(BF16))
        kw = (kt.astype(F32) * kw_ref[p]).astype(BF16)
        upd = lax.dot_general(kw, vp, (((0,), (0,)), ((), ())), preferred_element_type=F32)
        chunk_decay = jnp.where(scol1 < RET_DV,
                                math.exp(LOG_G[2 * p] * CHUNK), math.exp(LOG_G[2 * p + 1] * CHUNK))
        state_ref[p] = state * chunk_decay + jnp.where(own_block, upd, 0.0)

        zero_k = jnp.zeros_like(kt)
        k_rows = jnp.concatenate([jnp.where(low, kt, zero_k), jnp.where(low, zero_k, kt)], axis=0)
        s = lax.dot_general(qt, k_rows, (((1,), (1,)), ((), ())), preferred_element_type=F32)
        pm = (s * decay_ref[p]).astype(BF16)
        v_diag = jnp.concatenate(
            [jnp.concatenate([vp[:, 0:RET_DV], zero_v], axis=1),
             jnp.concatenate([zero_v, vp[:, RET_DV:2 * RET_DV]], axis=1)], axis=0)
        o2 = _dot(pm, v_diag) + qw_ref[p] * cross
        for e in range(2):
            h = 2 * p + e
            cols = slice(h * RET_DV, (h + 1) * RET_DV)
            o = o2[:, e * RET_DV:(e + 1) * RET_DV]
            mu = jnp.mean(o, axis=-1, keepdims=True)
            oc = o - mu
            var = jnp.mean(oc * oc, axis=-1, keepdims=True)
            on = oc * lax.rsqrt(var + EPS) * gn_ref[:, cols]
            gate = src_ref[rows, COL_GR + h * RET_DV:COL_GR + (h + 1) * RET_DV].astype(F32)
            o_ref[rows, cols] = (on * gate).astype(BF16)

    return [functools.partial(pair_step, c, p)
            for c in range(src_ref.shape[0] // CHUNK) for p in range(pairs)]


def _s5_scan_kernel(u_ref, wb_ref, consts_ref, u2_ref, wck_ref, y_ref, *scratch, tiles_per_seq):
    st_refs = scratch[0:3]
    sb_ref, ys_ref, carry_ref = scratch[3:]
    n = pl.program_id(0)
    s = S5_TAPS
    nblk = u_ref.shape[0]
    half = TILE_STATES

    @pl.when(n == 0)
    def _init():
        for ref in scratch[0:3]:
            ref[...] = jnp.zeros_like(ref)

    @pl.when((n == 0) | ((n - 1) % tiles_per_seq == 0))
    def _reset():
        carry_ref[...] = jnp.zeros_like(carry_ref)

    ncol = 2 * LANES

    def increments(slot):
        st_ref = st_refs[slot]

        def piece(c):
            z = _dot(u_ref[...], wb_ref[:, c * ncol:(c + 1) * ncol])
            for i in range(ncol // LANES):
                st_ref[c * (ncol // LANES) + i] = z[:, i * LANES:(i + 1) * LANES]

        return [functools.partial(piece, c) for c in range(2 * half // ncol)]

    ntile = half // LANES

    def scan(slot, row_pieces):
        st_ref = st_refs[slot]
        run = S5_RUN
        span = run * SUBLANES
        row = lax.broadcasted_iota(jnp.int32, (SUBLANES, LANES), 0)
        last = SUBLANES - 1
        groups = nblk // span
        per = -(-groups // row_pieces)
        tiles_per_piece = S5_SCAN_LANES // LANES

        def tile_group(t, g, carry):
            lanes = slice(t * LANES, (t + 1) * LANES)
            const = lambda idx: (consts_ref[idx, 0, :, lanes], consts_ref[idx, 1, :, lanes])
            cr, ci = carry
            rows = [pl.ds(g * span + j, SUBLANES, stride=run) for j in range(run)]
            loc = [(st_ref[t, rows[0], :], st_ref[ntile + t, rows[0], :])]
            for j in range(1, run):
                pr, pi = _cmul(*const(0), *loc[-1])
                loc.append((pr + st_ref[t, rows[j], :], pi + st_ref[ntile + t, rows[j], :]))
            fr, fi = loc[-1]
            for idx, d in enumerate((1, 2, 4)):
                pr, pi = _cmul(*const(1 + idx), pltpu.roll(fr, d, 0), pltpu.roll(fi, d, 0))
                fr, fi = fr + pr, fi + pi
            pr, pi = _cmul(*const(4), cr, ci)
            fr, fi = fr + pr, fi + pi
            er = jnp.where(row == 0, cr, pltpu.roll(fr, 1, 0))
            ei = jnp.where(row == 0, ci, pltpu.roll(fi, 1, 0))
            st_ref[t, rows[0], :] = er
            st_ref[ntile + t, rows[0], :] = ei
            for j in range(1, run):
                er, ei = _cmul(*const(0), er, ei)
                st_ref[t, rows[j], :] = loc[j - 1][0] + er
                st_ref[ntile + t, rows[j], :] = loc[j - 1][1] + ei
            return (jnp.broadcast_to(fr[last:, :], (SUBLANES, LANES)),
                    jnp.broadcast_to(fi[last:, :], (SUBLANES, LANES)))

        def piece(q, i, carries):
            tiles = range(q * tiles_per_piece, (q + 1) * tiles_per_piece)
            if i == 0:
                for t in tiles:
                    carries[t] = (carry_ref[:, t * LANES:(t + 1) * LANES],
                                  carry_ref[:, half + t * LANES:half + (t + 1) * LANES])
            for g in range(i * per, min((i + 1) * per, groups)):
                for t in tiles:
                    carries[t] = tile_group(t, g, carries[t])
            if i == row_pieces - 1:
                for t in tiles:
                    carry_ref[:, t * LANES:(t + 1) * LANES] = carries[t][0]
                    carry_ref[:, half + t * LANES:half + (t + 1) * LANES] = carries[t][1]

        carries = {}
        return [functools.partial(piece, q, i, carries)
                for q in range(ntile // tiles_per_piece) for i in range(row_pieces)]

    def outputs(slot):
        taps_per_piece = ncol // LANES

        def cast():
            for t in range(2 * ntile):
                sb_ref[:, t * LANES:(t + 1) * LANES] = st_refs[slot][t].astype(BF16)

        def piece(c):
            lhs = jnp.concatenate([sb_ref[...], u2_ref[...]], axis=1)
            yall = _dot(lhs, wck_ref[:, c * ncol:(c + 1) * ncol])
            for i in range(taps_per_piece):
                tap = c * taps_per_piece + i
                ys_ref[pl.ds(tap, nblk, stride=s), :] = yall[:, i * LANES:(i + 1) * LANES]

        def finish():
            y_ref[...] = jax.nn.gelu(ys_ref[...]).astype(BF16)

        return [cast] + [functools.partial(piece, c) for c in range(s * LANES // ncol)] + [finish]

    for phase in range(3):
        @pl.when(n % 3 == phase)
        def _steps(phase=phase):
            mxu = increments(phase) + outputs((phase + 1) % 3)
            vpu = scan((phase + 2) % 3, 3)
            for i, piece in enumerate(mxu):
                piece()
                if i < len(vpu):
                    vpu[i]()
            for piece in vpu[len(mxu):]:
                piece()


def _s5_scan(u, wb, wck, consts):
    B = u.shape[0]
    s = S5_TAPS
    L = u.shape[2] * s
    tm = TM_S5
    nt = L // tm
    ntiles = LANE_TILES * B * nt
    tile_of = lambda n, lag: jnp.clip(n - lag, 0, ntiles - 1)
    lane_tile = lambda n, lag: tile_of(n, lag) // (B * nt)

    def tok(lag, rows, width):
        def index(n):
            i = tile_of(n, lag)
            return ((i // nt) % B, i // (B * nt), i % nt, 0)
        return pl.BlockSpec((None, None, rows, width), index)

    def par(lag, *shape):
        return pl.BlockSpec((None,) + shape, lambda n: (lane_tile(n, lag),) + (0,) * len(shape))

    nblk = tm // s
    return pl.pallas_call(
        functools.partial(_s5_scan_kernel, tiles_per_seq=nt),
        grid=(ntiles + 2,),
        in_specs=[tok(0, nblk, s * LANES), par(0, s * LANES, 2 * TILE_STATES),
                  par(1, 5, 2, SUBLANES, TILE_STATES), tok(2, nblk, s * LANES),
                  par(2, 2 * TILE_STATES + s * LANES, s * LANES)],
        out_specs=tok(2, tm, LANES),
        out_shape=jax.ShapeDtypeStruct((B, LANE_TILES, L, LANES), BF16),
        scratch_shapes=([pltpu.VMEM((2 * TILE_STATES // LANES, nblk, LANES), F32)] * 3
                        + [pltpu.VMEM((nblk, 2 * TILE_STATES), BF16), pltpu.VMEM((tm, LANES), F32),
                           pltpu.VMEM((SUBLANES, 2 * TILE_STATES), F32)]),
        compiler_params=pltpu.CompilerParams(
            dimension_semantics=("arbitrary",), vmem_limit_bytes=VMEM_LIMIT),
        name="s5_scan",
    )(u, wb, consts, u, wck)


def _out_xattn_kernel(x_ref, ret_ref, y_ref, gs_ref, gw_ref, gb_ref, wout_ref, g2_ref, wq_ref,
                      ka_ref, va_ref, wo_ref, gf_ref, o_ref):
    y = jnp.concatenate([y_ref[j] for j in range(LANE_TILES)], axis=1)
    z = _dot(y, gw_ref[...]) + gb_ref[...]
    ssm = (y.astype(F32) * _sigmoid(z) * gs_ref[...].astype(F32)).astype(BF16)
    x1 = (x_ref[...] + _dot(ret_ref[...], wout_ref[0:RET_V_WIDTH, :])
          + _dot(ssm, wout_ref[RET_V_WIDTH:RET_V_WIDTH + S5_WIDTH, :]))
    h2 = _rms(x1, g2_ref[...]).astype(BF16)
    qa = (_dot(h2, wq_ref[...]) * (XA_DH ** -0.5)).astype(BF16)
    heads = []
    for h in range(XA_HEADS):
        cols = slice(h * XA_DH, (h + 1) * XA_DH)
        s = lax.dot_general(qa[:, cols], ka_ref[:, cols], (((1,), (1,)), ((), ())),
                            preferred_element_type=F32)
        e = jnp.exp(s - jnp.max(s, axis=-1, keepdims=True))
        l = jnp.sum(e, axis=-1, keepdims=True)
        heads.append((_dot(e.astype(BF16), va_ref[:, cols]) * (1.0 / l)).astype(BF16))
    o = jnp.concatenate(heads, axis=-1)
    x2 = x1 + _dot(o, wo_ref[...])
    o_ref[...] = _rms(x2, gf_ref[...])


def _out_xattn(x, ret, y, gs, gw, gb, wout, g2, wq, ka, va, wo, gf):
    B, L, D = x.shape
    tm = TM_OUT
    row = pl.BlockSpec((None, tm, D), lambda b, t: (b, t, 0))
    const = lambda shape: pl.BlockSpec(shape, lambda b, t: (0,) * len(shape))
    mem = pl.BlockSpec((None, MEM_LEN, D), lambda b, t: (b, 0, 0))
    ytile = pl.BlockSpec((None, LANE_TILES, tm, LANES), lambda b, t: (b, 0, t, 0))
    return pl.pallas_call(
        _out_xattn_kernel,
        grid=(B, L // tm),
        in_specs=[row, row, ytile, row, const((D, D)), const((1, D)), const(wout.shape), const((1, D)),
                  const((D, D)), mem, mem, const((D, D)), const((1, D))],
        out_specs=row,
        out_shape=jax.ShapeDtypeStruct((B, L, D), F32),
        compiler_params=pltpu.CompilerParams(
            dimension_semantics=("arbitrary", "arbitrary"), vmem_limit_bytes=VMEM_LIMIT),
        name="out_xattn",
    )(x, ret, y, gs, gw, gb, wout, g2, wq, ka, va, wo, gf)


def _rope_inv_lanes():
    half = RET_DK // 2
    inv = ROPE_BASE ** (-np.arange(half, dtype=np.float64) / half)
    return jnp.asarray(np.tile(inv, LANES // half).reshape(1, LANES), F32)


def kernel(x, mem, positions, norm1_g, w_in, ret_gn_g, s5_a_re, s5_a_im, s5_log_dt, s5_b_re, s5_b_im, s5_c_re, s5_c_im, s5_d, s5_glu_w, s5_glu_b, w_out, norm2_g, norm_mem_g, xa_wq, xa_wk, xa_wv, xa_wo, norm_f_g):
    B, L, D = x.shape
    l = 0
    rowvec = lambda v: v.reshape(1, -1)

    wb, wck, consts, win = _s5_prep(s5_a_re[l], s5_a_im[l], s5_log_dt[l], s5_b_re[l], s5_b_im[l],
                                    s5_c_re[l], s5_c_im[l], s5_d[l], w_in[l])
    (u, g_s5, ret), (glu_w, wout, wq, wk, wv, wo) = _proj_ret(
        x, positions.reshape(B, L, 1), _rope_inv_lanes(), rowvec(norm1_g[l]), win, rowvec(ret_gn_g[l]),
        [s5_glu_w[l], w_out[l], xa_wq[l], xa_wk[l], xa_wv[l], xa_wo[l]])
    ka, va = _mem_kv(mem, rowvec(norm_mem_g[l]), wk, wv)
    y = _s5_scan(u, wb, wck, consts)
    return _out_xattn(x, ret, y, g_s5, glu_w, rowvec(s5_glu_b[l]), wout,
                      rowvec(norm2_g[l]), wq, ka, va, wo, rowvec(norm_f_g))
```
